```python
import math
import jax, jax.numpy as jnp
from jax import lax
import numpy as np

D_MODEL = 1024
BATCH = 8
SEQ = 4096
DEPTH = 4

N_MIXERS = 2
SB_HEADS = 16
SB_HEAD_DIM = D_MODEL // SB_HEADS
Q_BLOCK = 128
HG_EXPAND = 128
HG_HEADS = D_MODEL // HG_EXPAND
HG_KEY_DIM = HG_EXPAND
HG_VAL_DIM = D_MODEL // HG_HEADS
HG_CHUNK = 64
D_FF = 4 * D_MODEL
N_SB = (DEPTH + N_MIXERS - 1) // N_MIXERS
N_HG = DEPTH // N_MIXERS
EPS = 1e-6

kernel_name = "stick_breaking_hgrn2_hybrid"


def rmsnorm(x, gain):
    xf = x.astype(jnp.float32)
    y = xf * lax.rsqrt(jnp.mean(xf * xf, axis=-1, keepdims=True) + EPS)
    return (y * gain.astype(jnp.float32)).astype(x.dtype)


def stick_breaking_attention(q, k, v):
    seq = q.shape[2]
    scale = 1.0 / math.sqrt(q.shape[-1])
    outs = []
    for t0 in range(0, seq, Q_BLOCK):
        t1 = t0 + Q_BLOCK
        z = jnp.einsum('bhqd,bhkd->bhqk', q[:, :, t0:t1], k[:, :, :t1]).astype(jnp.float32) * scale
        causal = jnp.arange(t1)[None, :] < (t0 + jnp.arange(Q_BLOCK))[:, None]
        log_stay = jnp.where(causal, jax.nn.log_sigmoid(-z), 0.0)
        log_between = lax.cumsum(log_stay, axis=3, reverse=True) - log_stay
        weights = jnp.where(causal, jnp.exp(jax.nn.log_sigmoid(z) + log_between), 0.0)
        outs.append(jnp.einsum('bhqk,bhkd->bhqd', weights.astype(v.dtype), v[:, :, :t1]))
    return jnp.concatenate(outs, axis=2)


def stick_breaking_mixer(h, w_qkv, q_gain, k_gain, w_o):
    bsz, seq, _ = h.shape
    qkv = h @ w_qkv
    q, k, v = jnp.split(qkv, 3, axis=-1)
    q = rmsnorm(q.reshape(bsz, seq, SB_HEADS, SB_HEAD_DIM), q_gain)
    k = rmsnorm(k.reshape(bsz, seq, SB_HEADS, SB_HEAD_DIM), k_gain)
    v = v.reshape(bsz, seq, SB_HEADS, SB_HEAD_DIM)
    q, k, v = (jnp.transpose(a, (0, 2, 1, 3)) for a in (q, k, v))
    o = stick_breaking_attention(q, k, v)
    o = jnp.transpose(o, (0, 2, 1, 3)).reshape(bsz, seq, D_MODEL)
    return o @ w_o


def hgrn2_chunk_scan(q, k, v, log_f):
    bsz, nh, seq, dk = q.shape
    dv = v.shape[-1]
    n_chunks = seq // HG_CHUNK

    def to_chunks(a):
        return jnp.moveaxis(a.astype(jnp.float32).reshape(bsz, nh, n_chunks, HG_CHUNK, a.shape[-1]), 2, 0)

    incl = jnp.tril(jnp.ones((HG_CHUNK, HG_CHUNK), dtype=bool))

    def step(state, inp):
        qc, kc, vc, lfc = inp
        b = jnp.cumsum(lfc, axis=2)
        inter = jnp.einsum('bhck,bhkv->bhcv', qc * jnp.exp(b), state)
        diff = b[:, :, :, None, :] - b[:, :, None, :, :]
        decay = jnp.where(incl[:, :, None], jnp.exp(jnp.minimum(diff, 0.0)), 0.0)
        scores = jnp.einsum('bhtk,bhsk,bhtsk->bhts', qc, kc, decay)
        intra = jnp.einsum('bhts,bhsv->bhtv', scores, vc)
        b_last = b[:, :, -1:, :]
        new_state = (jnp.exp(b_last[:, :, 0, :])[..., None] * state
                     + jnp.einsum('bhsk,bhsv->bhkv', kc * jnp.exp(b_last - b), vc))
        return new_state, inter + intra

    init = jnp.zeros((bsz, nh, dk, dv), jnp.float32)
    _, ys = lax.scan(step, init, (to_chunks(q), to_chunks(k), to_chunks(v), to_chunks(log_f)))
    return jnp.moveaxis(ys, 0, 2).reshape(bsz, nh, seq, dv)


def hgrn2_mixer(h, w_in, lower_bound, norm_gain, w_o):
    bsz, seq, _ = h.shape
    proj = h @ w_in
    q, f, i, g = jnp.split(proj, 4, axis=-1)
    q = jax.nn.silu(q)
    lb = lower_bound.astype(jnp.float32)
    forget = lb + (1.0 - lb) * jax.nn.sigmoid(f.astype(jnp.float32))
    log_f = jnp.log(forget)
    k = -jnp.expm1(log_f)

    def heads(a, d):
        return jnp.transpose(a.reshape(bsz, seq, HG_HEADS, d), (0, 2, 1, 3))

    o = hgrn2_chunk_scan(heads(q, HG_KEY_DIM), heads(k, HG_KEY_DIM),
                         heads(i, HG_VAL_DIM), heads(log_f, HG_KEY_DIM))
    o = jnp.transpose(o, (0, 2, 1, 3)).astype(h.dtype)
    o = rmsnorm(o, norm_gain).reshape(bsz, seq, D_MODEL)
    o = o * jax.nn.sigmoid(g)
    return o @ w_o


def squared_relu_mlp(h, w1, w2):
    a = jax.nn.relu(h @ w1)
    return (a * a) @ w2


def _fwd_setup_inputs(seed: int = 0) -> dict:
    key = jax.random.key(seed)
    ks = jax.random.split(key, 12)
    d_in = D_MODEL ** -0.5
    res = (2 * DEPTH) ** -0.5
    nrm = jax.random.normal
    return {
        "x": nrm(ks[0], (BATCH, SEQ, D_MODEL), jnp.float32),
        "norm_gains": 1.0 + 0.02 * nrm(ks[1], (DEPTH, 2, D_MODEL), jnp.float32),
        "sb_w_qkv": nrm(ks[2], (N_SB, D_MODEL, 3 * D_MODEL), jnp.float32) * d_in,
        "sb_q_gain": 1.0 + 0.02 * nrm(ks[3], (N_SB, SB_HEAD_DIM), jnp.float32),
        "sb_k_gain": 1.0 + 0.02 * nrm(ks[4], (N_SB, SB_HEAD_DIM), jnp.float32),
        "sb_w_o": nrm(ks[5], (N_SB, D_MODEL, D_MODEL), jnp.float32) * d_in * res,
        "hg_w_in": nrm(ks[6], (N_HG, D_MODEL, 4 * D_MODEL), jnp.float32) * d_in,
        "hg_lb_logits": 0.5 * nrm(ks[7], (N_HG, D_MODEL), jnp.float32),
        "hg_norm_gain": 1.0 + 0.02 * nrm(ks[8], (N_HG, HG_VAL_DIM), jnp.float32),
        "hg_w_o": nrm(ks[9], (N_HG, D_MODEL, D_MODEL), jnp.float32) * d_in * res,
        "mlp_w1": nrm(ks[10], (DEPTH, D_MODEL, D_FF), jnp.float32) * d_in,
        "mlp_w2": nrm(ks[11], (DEPTH, D_FF, D_MODEL), jnp.float32) * (D_FF ** -0.5) * res,
    }


def _fwd_reference(x, norm_gains, sb_w_qkv, sb_q_gain, sb_k_gain, sb_w_o,
              hg_w_in, hg_lb_logits, hg_norm_gain, hg_w_o, mlp_w1, mlp_w2):
    p = jax.nn.softmax(hg_lb_logits.astype(jnp.float32), axis=0)
    lower_bounds = jnp.cumsum(p, axis=0) - p[0:1]
    for layer in range(DEPTH):
        j = layer // N_MIXERS
        h = rmsnorm(x, norm_gains[layer, 0])
        if layer % N_MIXERS == 0:
            x = x + stick_breaking_mixer(h, sb_w_qkv[j], sb_q_gain[j], sb_k_gain[j], sb_w_o[j])
        else:
            x = x + hgrn2_mixer(h, hg_w_in[j], lower_bounds[j], hg_norm_gain[j], hg_w_o[j])
        h = rmsnorm(x, norm_gains[layer, 1])
        x = x + squared_relu_mlp(h, mlp_w1[layer], mlp_w2[layer])
    return x


import jax as _jax
import jax.numpy as _jnp

TWIN_FORMAT = 'train_step'
FWD_PARAMS = ['x', 'norm_gains', 'sb_w_qkv', 'sb_q_gain', 'sb_k_gain', 'sb_w_o', 'hg_w_in', 'hg_lb_logits', 'hg_norm_gain', 'hg_w_o', 'mlp_w1', 'mlp_w2']
TWIN_WEIGHTS = ['norm_gains', 'sb_w_qkv', 'sb_q_gain', 'sb_k_gain', 'sb_w_o', 'hg_w_in', 'hg_lb_logits', 'hg_norm_gain', 'hg_w_o', 'mlp_w1', 'mlp_w2']
TWIN_DIFF_INPUT = 'x'
TWIN_INPUTS = ['x', 'norm_gains', 'sb_w_qkv', 'sb_q_gain', 'sb_k_gain', 'sb_w_o', 'hg_w_in', 'hg_lb_logits', 'hg_norm_gain', 'hg_w_o', 'mlp_w1', 'mlp_w2', 'loss_target', 'm_norm_gains', 'm_sb_w_qkv', 'm_sb_q_gain', 'm_sb_k_gain', 'm_sb_w_o', 'm_hg_w_in', 'm_hg_lb_logits', 'm_hg_norm_gain', 'm_hg_w_o', 'm_mlp_w1', 'm_mlp_w2', 'v_norm_gains', 'v_sb_w_qkv', 'v_sb_q_gain', 'v_sb_k_gain', 'v_sb_w_o', 'v_hg_w_in', 'v_hg_lb_logits', 'v_hg_norm_gain', 'v_hg_w_o', 'v_mlp_w1', 'v_mlp_w2']
TWIN_OUTPUTS = ['loss', 'grad_x', 'grad_norm_gains', 'grad_sb_w_qkv', 'grad_sb_q_gain', 'grad_sb_k_gain', 'grad_sb_w_o', 'grad_hg_w_in', 'grad_hg_lb_logits', 'grad_hg_norm_gain', 'grad_hg_w_o', 'grad_mlp_w1', 'grad_mlp_w2', 'delta_norm_gains', 'delta_sb_w_qkv', 'delta_sb_q_gain', 'delta_sb_k_gain', 'delta_sb_w_o', 'delta_hg_w_in', 'delta_hg_lb_logits', 'delta_hg_norm_gain', 'delta_hg_w_o', 'delta_mlp_w1', 'delta_mlp_w2', 'new_m_norm_gains', 'new_m_sb_w_qkv', 'new_m_sb_q_gain', 'new_m_sb_k_gain', 'new_m_sb_w_o', 'new_m_hg_w_in', 'new_m_hg_lb_logits', 'new_m_hg_norm_gain', 'new_m_hg_w_o', 'new_m_mlp_w1', 'new_m_mlp_w2', 'new_v_norm_gains', 'new_v_sb_w_qkv', 'new_v_sb_q_gain', 'new_v_sb_k_gain', 'new_v_sb_w_o', 'new_v_hg_w_in', 'new_v_hg_lb_logits', 'new_v_hg_norm_gain', 'new_v_hg_w_o', 'new_v_mlp_w1', 'new_v_mlp_w2']
TWIN_LEAF_KINDS = {'loss': 'loss', 'grad_x': 'grad_x', 'grad_norm_gains': 'grad_w', 'grad_sb_w_qkv': 'grad_w', 'grad_sb_q_gain': 'grad_w', 'grad_sb_k_gain': 'grad_w', 'grad_sb_w_o': 'grad_w', 'grad_hg_w_in': 'grad_w', 'grad_hg_lb_logits': 'grad_w', 'grad_hg_norm_gain': 'grad_w', 'grad_hg_w_o': 'grad_w', 'grad_mlp_w1': 'grad_w', 'grad_mlp_w2': 'grad_w', 'delta_norm_gains': 'delta_w', 'delta_sb_w_qkv': 'delta_w', 'delta_sb_q_gain': 'delta_w', 'delta_sb_k_gain': 'delta_w', 'delta_sb_w_o': 'delta_w', 'delta_hg_w_in': 'delta_w', 'delta_hg_lb_logits': 'delta_w', 'delta_hg_norm_gain': 'delta_w', 'delta_hg_w_o': 'delta_w', 'delta_mlp_w1': 'delta_w', 'delta_mlp_w2': 'delta_w', 'new_m_norm_gains': 'new_m', 'new_m_sb_w_qkv': 'new_m', 'new_m_sb_q_gain': 'new_m', 'new_m_sb_k_gain': 'new_m', 'new_m_sb_w_o': 'new_m', 'new_m_hg_w_in': 'new_m', 'new_m_hg_lb_logits': 'new_m', 'new_m_hg_norm_gain': 'new_m', 'new_m_hg_w_o': 'new_m', 'new_m_mlp_w1': 'new_m', 'new_m_mlp_w2': 'new_m', 'new_v_norm_gains': 'new_v', 'new_v_sb_w_qkv': 'new_v', 'new_v_sb_q_gain': 'new_v', 'new_v_sb_k_gain': 'new_v', 'new_v_sb_w_o': 'new_v', 'new_v_hg_w_in': 'new_v', 'new_v_hg_lb_logits': 'new_v', 'new_v_hg_norm_gain': 'new_v', 'new_v_hg_w_o': 'new_v', 'new_v_mlp_w1': 'new_v', 'new_v_mlp_w2': 'new_v'}


def _forward(args):
    return _fwd_reference(*[args[k] for k in FWD_PARAMS])


def _output_shape():
    out = _jax.eval_shape(lambda: _forward(_fwd_setup_inputs(0)))
    return out.shape, out.dtype

N_MICROBATCH = 1
ADAM_LR = 0.001
ADAM_B1 = 0.9
ADAM_B2 = 0.999
ADAM_EPS = 1e-08
ADAM_WD = 0.01
ADAM_STEP = 10
PER_EXAMPLE_BATCH_AXIS = {'x': 0, 'loss_target': 0}
SHARED_INPUTS = []
_WEIGHT_DTYPES = {'norm_gains': _jnp.float32, 'sb_w_qkv': _jnp.float32, 'sb_q_gain': _jnp.float32, 'sb_k_gain': _jnp.float32, 'sb_w_o': _jnp.float32, 'hg_w_in': _jnp.float32, 'hg_lb_logits': _jnp.float32, 'hg_norm_gain': _jnp.float32, 'hg_w_o': _jnp.float32, 'mlp_w1': _jnp.float32, 'mlp_w2': _jnp.float32}
MOMENT_SCALE = {'norm_gains': 8.531149e+00, 'sb_w_qkv': 3.794470e-01, 'sb_q_gain': 3.641715e+00, 'sb_k_gain': 3.647119e+00, 'sb_w_o': 1.832367e+00, 'hg_w_in': 4.965777e-01, 'hg_lb_logits': 1.671726e-02, 'hg_norm_gain': 9.841363e+00, 'hg_w_o': 2.860019e+00, 'mlp_w1': 4.312553e-01, 'mlp_w2': 7.396206e+00}


def _to_microbatches(a, axis):
    t = _jnp.moveaxis(a, axis, 0)
    t = t.reshape((N_MICROBATCH, t.shape[0] // N_MICROBATCH) + t.shape[1:])
    return _jnp.moveaxis(t, 1, axis + 1)


def setup_inputs(seed: int = 0) -> dict:
    inp = _fwd_setup_inputs(seed)
    key = _jax.random.fold_in(_jax.random.key(seed), 7919)
    shape, _ = _output_shape()
    out = dict(inp)
    out["loss_target"] = _jax.random.normal(_jax.random.fold_in(key, 0), shape, _jnp.float32)
    for i, name in enumerate(TWIN_WEIGHTS):
        w = inp[name].astype(_jnp.float32)
        if MOMENT_SCALE is None:
            s = _jnp.sqrt(_jnp.mean(_jnp.square(w)) + 1e-30)
        else:
            s = MOMENT_SCALE[name]
        km, kv = _jax.random.split(_jax.random.fold_in(key, i + 1))
        out[name] = w
        out["m_" + name] = s * _jax.random.normal(km, w.shape, _jnp.float32)
        out["v_" + name] = (s * s) * _jax.random.uniform(kv, w.shape, _jnp.float32, 0.5, 1.5)
    if N_MICROBATCH > 1:
        for name, axis in PER_EXAMPLE_BATCH_AXIS.items():
            out[name] = _to_microbatches(out[name], axis)
    return {'x': out['x'], 'norm_gains': out['norm_gains'], 'sb_w_qkv': out['sb_w_qkv'], 'sb_q_gain': out['sb_q_gain'], 'sb_k_gain': out['sb_k_gain'], 'sb_w_o': out['sb_w_o'], 'hg_w_in': out['hg_w_in'], 'hg_lb_logits': out['hg_lb_logits'], 'hg_norm_gain': out['hg_norm_gain'], 'hg_w_o': out['hg_w_o'], 'mlp_w1': out['mlp_w1'], 'mlp_w2': out['mlp_w2'], 'loss_target': out['loss_target'], 'm_norm_gains': out['m_norm_gains'], 'm_sb_w_qkv': out['m_sb_w_qkv'], 'm_sb_q_gain': out['m_sb_q_gain'], 'm_sb_k_gain': out['m_sb_k_gain'], 'm_sb_w_o': out['m_sb_w_o'], 'm_hg_w_in': out['m_hg_w_in'], 'm_hg_lb_logits': out['m_hg_lb_logits'], 'm_hg_norm_gain': out['m_hg_norm_gain'], 'm_hg_w_o': out['m_hg_w_o'], 'm_mlp_w1': out['m_mlp_w1'], 'm_mlp_w2': out['m_mlp_w2'], 'v_norm_gains': out['v_norm_gains'], 'v_sb_w_qkv': out['v_sb_w_qkv'], 'v_sb_q_gain': out['v_sb_q_gain'], 'v_sb_k_gain': out['v_sb_k_gain'], 'v_sb_w_o': out['v_sb_w_o'], 'v_hg_w_in': out['v_hg_w_in'], 'v_hg_lb_logits': out['v_hg_lb_logits'], 'v_hg_norm_gain': out['v_hg_norm_gain'], 'v_hg_w_o': out['v_hg_w_o'], 'v_mlp_w1': out['v_mlp_w1'], 'v_mlp_w2': out['v_mlp_w2']}


def _loss(weights, diff, rest, loss_target):
    with _jax.named_scope("forward"):
        args = {**rest, TWIN_DIFF_INPUT: diff, **{k: w.astype(_WEIGHT_DTYPES[k]) for k, w in weights.items()}}
        y = _forward(args)
    with _jax.named_scope("loss_head"):
        err = _jnp.square(y.astype(_jnp.float32) - loss_target)
        return 0.5 * _jnp.sum(_jnp.mean(err, axis=-1)) if err.ndim else 0.5 * err


def _adamw(w, g, m, v):
    m = ADAM_B1 * m + (1.0 - ADAM_B1) * g
    v = ADAM_B2 * v + (1.0 - ADAM_B2) * _jnp.square(g)
    m_hat = m / (1.0 - ADAM_B1 ** ADAM_STEP)
    v_hat = v / (1.0 - ADAM_B2 ** ADAM_STEP)
    delta = -ADAM_LR * (m_hat / (_jnp.sqrt(v_hat) + ADAM_EPS) + ADAM_WD * w)
    return delta, m, v


def reference(x, norm_gains, sb_w_qkv, sb_q_gain, sb_k_gain, sb_w_o, hg_w_in, hg_lb_logits, hg_norm_gain, hg_w_o, mlp_w1, mlp_w2, loss_target, m_norm_gains, m_sb_w_qkv, m_sb_q_gain, m_sb_k_gain, m_sb_w_o, m_hg_w_in, m_hg_lb_logits, m_hg_norm_gain, m_hg_w_o, m_mlp_w1, m_mlp_w2, v_norm_gains, v_sb_w_qkv, v_sb_q_gain, v_sb_k_gain, v_sb_w_o, v_hg_w_in, v_hg_lb_logits, v_hg_norm_gain, v_hg_w_o, v_mlp_w1, v_mlp_w2):
    given = dict(x=x, norm_gains=norm_gains, sb_w_qkv=sb_w_qkv, sb_q_gain=sb_q_gain, sb_k_gain=sb_k_gain, sb_w_o=sb_w_o, hg_w_in=hg_w_in, hg_lb_logits=hg_lb_logits, hg_norm_gain=hg_norm_gain, hg_w_o=hg_w_o, mlp_w1=mlp_w1, mlp_w2=mlp_w2, loss_target=loss_target, m_norm_gains=m_norm_gains, m_sb_w_qkv=m_sb_w_qkv, m_sb_q_gain=m_sb_q_gain, m_sb_k_gain=m_sb_k_gain, m_sb_w_o=m_sb_w_o, m_hg_w_in=m_hg_w_in, m_hg_lb_logits=m_hg_lb_logits, m_hg_norm_gain=m_hg_norm_gain, m_hg_w_o=m_hg_w_o, m_mlp_w1=m_mlp_w1, m_mlp_w2=m_mlp_w2, v_norm_gains=v_norm_gains, v_sb_w_qkv=v_sb_w_qkv, v_sb_q_gain=v_sb_q_gain, v_sb_k_gain=v_sb_k_gain, v_sb_w_o=v_sb_w_o, v_hg_w_in=v_hg_w_in, v_hg_lb_logits=v_hg_lb_logits, v_hg_norm_gain=v_hg_norm_gain, v_hg_w_o=v_hg_w_o, v_mlp_w1=v_mlp_w1, v_mlp_w2=v_mlp_w2)
    weights = {n: given[n] for n in TWIN_WEIGHTS}
    shared = {n: given[n] for n in SHARED_INPUTS}
    per_example = {n: given[n] for n in ['x']}
    grad_fn = _jax.value_and_grad(_loss, argnums=(0, 1))

    def one_microbatch(ex, loss_target):
        ex = dict(ex)
        diff = ex.pop(TWIN_DIFF_INPUT)
        return grad_fn(weights, diff, {**shared, **ex}, loss_target)

    if N_MICROBATCH == 1:
        loss, (grad_w, grad_x) = one_microbatch(per_example, given["loss_target"])
    else:
        def body(carry, xs):
            loss_sum, grad_sum = carry
            l_k, (gw_k, gx_k) = one_microbatch(xs[0], xs[1])
            with _jax.named_scope("update"):
                return (loss_sum + l_k, _jax.tree.map(_jnp.add, grad_sum, gw_k)), gx_k

        init = (_jnp.zeros((), _jnp.float32), _jax.tree.map(_jnp.zeros_like, weights))
        (loss, grad_w), grad_x = _jax.lax.scan(body, init, (per_example, given["loss_target"]))
    with _jax.named_scope("update"):
        delta_w, new_m, new_v = {}, {}, {}
        for n in TWIN_WEIGHTS:
            delta_w[n], new_m[n], new_v[n] = _adamw(weights[n], grad_w[n], given["m_" + n], given["v_" + n])
    return (loss, grad_x, *[grad_w[n] for n in TWIN_WEIGHTS], *[delta_w[n] for n in TWIN_WEIGHTS],
            *[new_m[n] for n in TWIN_WEIGHTS], *[new_v[n] for n in TWIN_WEIGHTS])
```

```python
import functools
import math

import numpy as np
import jax
import jax.numpy as jnp
from jax import lax
from jax.experimental import pallas as pl
from jax.experimental.pallas import tpu as pltpu

F32 = jnp.float32
BF16 = jnp.bfloat16
EPS = 1e-6
SB_HEAD_DIM = 64
HG_DIM = 128
LANES = 128
N_DEV = 8
AXES = ("x", "y", "c")
VMEM_LIMIT_BYTES = 48 * 1024 * 1024
ATT_BLOCK = 128
HG_CHUNK = 64
ADAM_LR, ADAM_B1, ADAM_B2, ADAM_EPS, ADAM_WD, ADAM_STEP = 0.001, 0.9, 0.999, 1e-08, 0.01, 10


def _call(body, **kw):
    return pl.pallas_call(body, **kw)


def _sds(shape, dtype):
    return jax.ShapeDtypeStruct(tuple(shape), dtype)


def _cparams(*sem):
    return pltpu.CompilerParams(dimension_semantics=sem or None, vmem_limit_bytes=VMEM_LIMIT_BYTES)


def _split_bf16(x):
    hi = x.astype(BF16)
    lo = (x - hi.astype(F32)).astype(BF16)
    return hi, lo


def _dot(a, b, dims):
    return lax.dot_general(a, b, (dims, ((), ())), preferred_element_type=F32)


NN = ((1,), (0,))
NT = ((1,), (1,))
TN = ((0,), (0,))


def _dot2(x, m, dims):
    hi, lo = _split_bf16(x)
    return _dot(hi, m, dims) + _dot(lo, m, dims)


def _mdot2(m, x, dims):
    hi, lo = _split_bf16(x)
    return _dot(m, hi, dims) + _dot(m, lo, dims)


def _matmul(kind, a, b, name, out_dtypes, epilogue=None, extras=(), tm=512, tn=512):
    if kind == "nn":
        (m, k), n = a.shape, b.shape[1]
    elif kind == "nt":
        (m, k), n = a.shape, b.shape[0]
    else:
        (k, m), n = a.shape, b.shape[1]
    tm, tn = min(tm, m), min(tn, n)
    assert m % tm == 0 and n % tn == 0, (name, a.shape, b.shape)
    a_spec = pl.BlockSpec((k, tm), lambda i, j: (0, i)) if kind == "tn" else pl.BlockSpec((tm, k), lambda i, j: (i, 0))
    b_spec = pl.BlockSpec((tn, k), lambda i, j: (j, 0)) if kind == "nt" else pl.BlockSpec((k, tn), lambda i, j: (0, j))
    o_spec = pl.BlockSpec((tm, tn), lambda i, j: (i, j))
    dims = {"nn": NN, "nt": NT, "tn": TN}[kind]
    n_ex = len(extras)

    def body(*refs):
        a_ref, b_ref = refs[:2]
        ex, outs = refs[2:2 + n_ex], refs[2 + n_ex:]
        acc = _dot(a_ref[...].astype(BF16), b_ref[...].astype(BF16), dims)
        res = epilogue(acc, *[e[...] for e in ex]) if epilogue is not None else (acc,)
        for o_ref, r in zip(outs, res):
            o_ref[...] = r.astype(o_ref.dtype)

    out = _call(
        body, name=name, grid=(m // tm, n // tn),
        in_specs=[a_spec, b_spec] + [o_spec] * n_ex,
        out_specs=[o_spec] * len(out_dtypes),
        out_shape=[_sds((m, n), dt) for dt in out_dtypes],
        compiler_params=_cparams("parallel", "parallel"),
    )(a, b, *extras)
    return out if len(out_dtypes) > 1 else out[0]


def _ep_add(acc, res):
    return (acc + res,)


def _ep_relu2(acc):
    r = jnp.maximum(acc, 0.0)
    return acc, r * r


def _ep_relu2_bwd(acc, a):
    return (acc * (2.0 * jnp.maximum(a, 0.0)),)


def _rmsnorm(x, g, name):
    s, d = x.shape
    tm = min(s, 512)

    def body(x_ref, g_ref, h_ref):
        xv = x_ref[...]
        r = lax.rsqrt(jnp.mean(xv * xv, axis=-1, keepdims=True) + EPS)
        h_ref[...] = (xv * r * g_ref[...]).astype(BF16)

    return _call(
        body, name=name, grid=(s // tm,),
        in_specs=[pl.BlockSpec((tm, d), lambda i: (i, 0)), pl.BlockSpec((1, d), lambda i: (0, 0))],
        out_specs=pl.BlockSpec((tm, d), lambda i: (i, 0)),
        out_shape=_sds((s, d), BF16), compiler_params=_cparams("parallel"),
    )(x, g)


def _rmsnorm_bwd(x, g, dh, dres, name):
    s, d = x.shape
    tm = min(s, 512)

    def body(x_ref, g_ref, dh_ref, dres_ref, dx_ref, dxb_ref, dg_ref):
        xv, dhv = x_ref[...], dh_ref[...]
        r = lax.rsqrt(jnp.mean(xv * xv, axis=-1, keepdims=True) + EPS)
        xr = xv * r
        t = dhv * g_ref[...]
        dx = dres_ref[...] + r * (t - xr * jnp.mean(t * xr, axis=-1, keepdims=True))
        dx_ref[...] = dx
        dxb_ref[...] = dx.astype(BF16)

        @pl.when(pl.program_id(0) == 0)
        def _():
            dg_ref[...] = jnp.zeros_like(dg_ref)

        dg_ref[...] += jnp.sum(dhv * xr, axis=0, keepdims=True)

    row = pl.BlockSpec((tm, d), lambda i: (i, 0))
    vec = pl.BlockSpec((1, d), lambda i: (0, 0))
    return _call(
        body, name=name, grid=(s // tm,), in_specs=[row, vec, row, row], out_specs=[row, row, vec],
        out_shape=[_sds((s, d), F32), _sds((s, d), BF16), _sds((1, d), F32)],
        compiler_params=_cparams("arbitrary"),
    )(x, g, dh, dres)


def _loss_head(y, target, name):
    s, d = y.shape
    tm = min(s, 512)

    def body(y_ref, t_ref, loss_ref, dy_ref, dyb_ref):
        err = y_ref[...] - t_ref[...]
        dy = err * (1.0 / d)
        dy_ref[...] = dy
        dyb_ref[...] = dy.astype(BF16)

        @pl.when(pl.program_id(0) == 0)
        def _():
            loss_ref[...] = jnp.zeros_like(loss_ref)

        part = 0.5 * jnp.sum(jnp.mean(err * err, axis=-1, keepdims=True), axis=0, keepdims=True)
        loss_ref[...] += part

    row = pl.BlockSpec((tm, d), lambda i: (i, 0))
    return _call(
        body, name=name, grid=(s // tm,), in_specs=[row, row],
        out_specs=[pl.BlockSpec((8, LANES), lambda i: (0, 0)), row, row],
        out_shape=[_sds((8, LANES), F32), _sds((s, d), F32), _sds((s, d), BF16)],
        compiler_params=_cparams("arbitrary"),
    )(y, target)


def _head_lane_mask():
    lane = lax.broadcasted_iota(jnp.int32, (1, LANES), 1)
    return lane < SB_HEAD_DIM


def _pair_rms(xv, first):
    x2 = xv * xv
    s0 = jnp.sum(jnp.where(first, x2, 0.0), axis=-1, keepdims=True)
    s1 = jnp.sum(jnp.where(first, 0.0, x2), axis=-1, keepdims=True)
    inv = 1.0 / SB_HEAD_DIM
    return jnp.where(first, lax.rsqrt(s0 * inv + EPS), lax.rsqrt(s1 * inv + EPS))


def _pair_mean(t, first):
    s0 = jnp.sum(jnp.where(first, t, 0.0), axis=-1, keepdims=True)
    s1 = jnp.sum(jnp.where(first, 0.0, t), axis=-1, keepdims=True)
    return jnp.where(first, s0, s1) * (1.0 / SB_HEAD_DIM)


def _qk_gain_table(q_gain, k_gain):
    return jnp.stack([jnp.tile(q_gain, 2), jnp.tile(k_gain, 2), jnp.ones((LANES,), F32)])[:, None, :]


def _qknorm(qkv, gains, name):
    s, d3 = qkv.shape
    pairs = d3 // 3 // LANES
    tm = min(s, 512)

    def body(x_ref, g_ref, o_ref):
        xv = x_ref[...]
        first = _head_lane_mask()
        normed = xv * _pair_rms(xv, first) * g_ref[0]
        o_ref[...] = jnp.where(pl.program_id(0) < 2 * pairs, normed, xv).astype(BF16)

    tile = pl.BlockSpec((tm, LANES), lambda c, i: (i, c))
    return _call(
        body, name=name, grid=(3 * pairs, s // tm),
        in_specs=[tile, pl.BlockSpec((1, 1, LANES), lambda c, i: (c // pairs, 0, 0))], out_specs=tile,
        out_shape=_sds((s, d3), BF16), compiler_params=_cparams("parallel", "parallel"),
    )(qkv, gains)


def _qknorm_bwd(qkv, dqkv, gains, name):
    s, d3 = qkv.shape
    pairs = d3 // 3 // LANES
    tm = min(s, 512)

    def body(x_ref, d_ref, g_ref, dx_ref, dg_ref):
        c, i = pl.program_id(0), pl.program_id(1)
        xv, dv = x_ref[...], d_ref[...]
        first = _head_lane_mask()
        r = _pair_rms(xv, first)
        xr = xv * r
        t = dv * g_ref[0]
        dx = r * (t - xr * _pair_mean(t * xr, first))
        dx_ref[...] = jnp.where(c < 2 * pairs, dx, dv).astype(BF16)

        @pl.when((c % pairs == 0) & (i == 0))
        def _():
            dg_ref[...] = jnp.zeros_like(dg_ref)

        dg_ref[0] += jnp.sum(dv * xr, axis=0, keepdims=True)

    tile = pl.BlockSpec((tm, LANES), lambda c, i: (i, c))
    vec = pl.BlockSpec((1, 1, LANES), lambda c, i: (c // pairs, 0, 0))
    return _call(
        body, name=name, grid=(3 * pairs, s // tm), in_specs=[tile, tile, vec], out_specs=[tile, vec],
        out_shape=[_sds((s, d3), BF16), _sds((3, 1, LANES), F32)],
        compiler_params=_cparams("arbitrary", "arbitrary"),
    )(qkv, dqkv, gains)


def _softplus_parts(z):
    sp = jnp.maximum(z, 0.0) + jnp.log1p(jnp.exp(-jnp.abs(z)))
    return sp, z - sp


def _att_tile_masks(i, j, tb):
    row = i * tb + lax.broadcasted_iota(jnp.int32, (tb, tb), 0)
    col = j * tb + lax.broadcasted_iota(jnp.int32, (tb, tb), 1)
    return col < row


def _tri(tb, strict_upper_of_rows):
    r = lax.broadcasted_iota(jnp.int32, (tb, tb), 0)
    c = lax.broadcasted_iota(jnp.int32, (tb, tb), 1)
    return jnp.where(r > c if strict_upper_of_rows else r >= c, 1.0, 0.0).astype(BF16)


def _sb_attention_fwd(qkvn, name):
    s, d3 = qkvn.shape
    d = d3 // 3
    pairs, tb = d // LANES, min(ATT_BLOCK, s)
    scale = 1.0 / math.sqrt(SB_HEAD_DIM)

    def body(q_ref, k_ref, v_ref, o_ref):
        i = pl.program_id(1)
        first = _head_lane_mask()
        later = _tri(tb, True)
        q2 = q_ref[...]
        outs = []
        for head in range(2):
            mine = first if head == 0 else jnp.logical_not(first)
            qa = jnp.where(mine, q2, jnp.zeros_like(q2))

            def step(jj, carry, qa=qa):
                run, acc = carry
                j = i - jj
                rows = pl.ds(pl.multiple_of(j * tb, tb), tb)
                z = _dot(qa, k_ref[rows, :], NT) * scale
                causal = _att_tile_masks(i, j, tb)
                sp, logsig = _softplus_parts(z)
                stay = jnp.where(causal, -sp, 0.0)
                between = _dot2(stay, later, NN) + run
                w = jnp.where(causal, jnp.exp(logsig + between), 0.0)
                acc = acc + _dot(w.astype(BF16), v_ref[rows, :], NN)
                return run + jnp.sum(stay, axis=-1, keepdims=True), acc

            _, acc = lax.fori_loop(0, i + 1, step, (jnp.zeros((tb, 1), F32), jnp.zeros((tb, LANES), F32)))
            outs.append(acc)
        o_ref[...] = jnp.where(first, outs[0], outs[1])

    return _call(
        body, name=name, grid=(pairs, s // tb),
        in_specs=[pl.BlockSpec((tb, LANES), lambda p, i: (i, p)),
                  pl.BlockSpec((s, LANES), lambda p, i: (0, pairs + p)),
                  pl.BlockSpec((s, LANES), lambda p, i: (0, 2 * pairs + p))],
        out_specs=pl.BlockSpec((tb, LANES), lambda p, i: (i, p)),
        out_shape=_sds((s, d), F32), compiler_params=_cparams("parallel", "parallel"),
    )(qkvn, qkvn, qkvn)


def _sb_attention_bwd(qkvn, do, name):
    s, d3 = qkvn.shape
    d = d3 // 3
    pairs, tb = d // LANES, min(ATT_BLOCK, s)
    nb = s // tb
    scale = 1.0 / math.sqrt(SB_HEAD_DIM)

    def body(q_ref, k_ref, v_ref, do_ref, dq_ref, dk_ref, dv_ref, run_ref):
        i = pl.program_id(1)

        @pl.when(i == 0)
        def _():
            dk_ref[...] = jnp.zeros_like(dk_ref)
            dv_ref[...] = jnp.zeros_like(dv_ref)

        first = _head_lane_mask()
        later = _tri(tb, True)
        q2, do2 = q_ref[...], do_ref[...].astype(BF16)
        dqs = []
        for head in range(2):
            mine = first if head == 0 else jnp.logical_not(first)
            qa = jnp.where(mine, q2, jnp.zeros_like(q2))
            doa = jnp.where(mine, do2, jnp.zeros_like(do2))

            def sweep1(jj, run, qa=qa, head=head):
                j = i - jj
                rows = pl.ds(pl.multiple_of(j * tb, tb), tb)
                run_ref[head * nb + j] = run
                z = _dot(qa, k_ref[rows, :], NT) * scale
                sp, _ = _softplus_parts(z)
                stay = jnp.where(_att_tile_masks(i, j, tb), -sp, 0.0)
                return run + jnp.sum(stay, axis=-1, keepdims=True)

            lax.fori_loop(0, i + 1, sweep1, jnp.zeros((tb, 1), F32))

            def sweep2(j, carry, qa=qa, doa=doa, mine=mine, head=head):
                gsum, dq = carry
                rows = pl.ds(pl.multiple_of(j * tb, tb), tb)
                ka = jnp.where(mine, k_ref[rows, :], jnp.zeros((tb, LANES), BF16))
                va = jnp.where(mine, v_ref[rows, :], jnp.zeros((tb, LANES), BF16))
                z = _dot(qa, ka, NT) * scale
                causal = _att_tile_masks(i, j, tb)
                sp, logsig = _softplus_parts(z)
                stay = jnp.where(causal, -sp, 0.0)
                between = _dot2(stay, later, NN) + run_ref[head * nb + j]
                w = jnp.where(causal, jnp.exp(logsig + between), 0.0)
                g = w * _dot(doa, va, NT)
                before = _dot2(g, later, NT) + gsum
                sig = jnp.exp(logsig)
                dz = jnp.where(causal, g * (1.0 - sig) - before * sig, 0.0) * scale
                dzb = dz.astype(BF16)
                dq = dq + _dot(dzb, ka, NN)
                dk_ref[rows, :] += _dot(dzb, qa, TN)
                dv_ref[rows, :] += _dot(w.astype(BF16), doa, TN)
                return gsum + jnp.sum(g, axis=-1, keepdims=True), dq

            _, dq = lax.fori_loop(0, i + 1, sweep2, (jnp.zeros((tb, 1), F32), jnp.zeros((tb, LANES), F32)))
            dqs.append(dq)
        dq_ref[...] = dqs[0] + dqs[1]

    q_spec = pl.BlockSpec((tb, LANES), lambda p, i: (i, p))
    return _call(
        body, name=name, grid=(pairs, nb),
        in_specs=[q_spec,
                  pl.BlockSpec((s, LANES), lambda p, i: (0, pairs + p)),
                  pl.BlockSpec((s, LANES), lambda p, i: (0, 2 * pairs + p)),
                  q_spec],
        out_specs=[q_spec, pl.BlockSpec((s, LANES), lambda p, i: (0, p)), pl.BlockSpec((s, LANES), lambda p, i: (0, p))],
        out_shape=[_sds((s, d), F32)] * 3,
        scratch_shapes=[pltpu.VMEM((2 * nb, tb, 1), F32)],
        compiler_params=_cparams("parallel", "arbitrary"),
    )(qkvn, qkvn, qkvn, do)


def _hg_tables(c):
    t = np.arange(c)[:, None]
    j = np.arange(c)[None, :]
    sums = [j <= t, j > t]
    masks = []
    m = c // 2
    while m >= 1:
        pos, base = t % (2 * m), t - t % (2 * m)
        sums.append((pos >= m) & (j >= base + m) & (j <= t))
        sums.append((pos < m) & (j > t) & (j <= base + m - 1))
        masks.append((t // (2 * m) == j // (2 * m)) & (t % (2 * m) >= m) & (j % (2 * m) < m))
        m //= 2
    return (jnp.asarray(np.concatenate(sums, 0), BF16), jnp.asarray(np.stack(masks), F32), len(masks))


def _hg_gates(qr, fr, lb):
    sq = jax.nn.sigmoid(qr)
    sg = jax.nn.sigmoid(fr)
    forget = lb + (1.0 - lb) * sg
    return sq, qr * sq, sg, forget, jnp.log(forget), (1.0 - lb) * (1.0 - sg)


def _hg_scores(q, k, x, masks_ref, c, levels):
    eye = (lax.broadcasted_iota(jnp.int32, (c, c), 0) == lax.broadcasted_iota(jnp.int32, (c, c), 1)).astype(F32)
    scores = eye * jnp.sum(q * k, axis=-1, keepdims=True)
    ops = []
    for l in range(levels):
        qm = (q * x[(2 + 2 * l) * c:(3 + 2 * l) * c]).astype(BF16)
        km = (k * x[(3 + 2 * l) * c:(4 + 2 * l) * c]).astype(BF16)
        scores = scores + masks_ref[l] * _dot(qm, km, NT)
        ops.append((qm, km))
    return scores, eye, ops


def _hg_specs(c, heads, reverse, n_chunks):
    def chunk(ci):
        return n_chunks - 1 - ci if reverse else ci
    proj = [pl.BlockSpec((c, HG_DIM), functools.partial(lambda h, ci, part: (chunk(ci), part * heads + h), part=p))
            for p in range(4)]
    tile = pl.BlockSpec((c, HG_DIM), lambda h, ci: (chunk(ci), h))
    lb = pl.BlockSpec((1, HG_DIM), lambda h, ci: (0, h))
    gain = pl.BlockSpec((1, HG_DIM), lambda h, ci: (0, 0))
    state = pl.BlockSpec((1, 1, HG_DIM, HG_DIM), lambda h, ci: (h, chunk(ci), 0, 0))
    return proj, tile, lb, gain, state


def _hgrn2_fwd(proj, lb, gain, name):
    s, d4 = proj.shape
    d = d4 // 4
    heads, c = d // HG_DIM, min(HG_CHUNK, s)
    n_chunks = s // c
    sums, masks, levels = _hg_tables(c)

    def body(qr_ref, fr_ref, ir_ref, gr_ref, lb_ref, gain_ref, sums_ref, masks_ref, og_ref, o_ref, states_ref, st_ref):
        @pl.when(pl.program_id(1) == 0)
        def _():
            st_ref[...] = jnp.zeros_like(st_ref)

        _, q, _, _, lf, k = _hg_gates(qr_ref[...], fr_ref[...], lb_ref[...])
        x = jnp.exp(_mdot2(sums_ref[...], lf, NN))
        st = st_ref[...]
        states_ref[0, 0] = st
        vb = ir_ref[...].astype(BF16)
        scores, _, _ = _hg_scores(q, k, x, masks_ref, c, levels)
        qh = (q * x[0:c]).astype(BF16)
        kh = (k * x[c:2 * c]).astype(BF16)
        o = _dot(qh, st.astype(BF16), NT) + _dot(scores.astype(BF16), vb, NN)
        st_ref[...] = st * x[c - 1:c] + _dot(vb, kh, TN)
        o_ref[...] = o
        r = lax.rsqrt(jnp.mean(o * o, axis=-1, keepdims=True) + EPS)
        og_ref[...] = (o * r * gain_ref[...] * jax.nn.sigmoid(gr_ref[...])).astype(BF16)

    pspecs, tile, lbs, gs, state = _hg_specs(c, heads, False, n_chunks)
    const = [pl.BlockSpec(sums.shape, lambda h, ci: (0, 0)), pl.BlockSpec(masks.shape, lambda h, ci: (0, 0, 0))]
    return _call(
        body, name=name, grid=(heads, n_chunks), in_specs=pspecs + [lbs, gs] + const,
        out_specs=[tile, tile, state],
        out_shape=[_sds((s, d), BF16), _sds((s, d), F32), _sds((heads, n_chunks, HG_DIM, HG_DIM), F32)],
        scratch_shapes=[pltpu.VMEM((HG_DIM, HG_DIM), F32)],
        compiler_params=_cparams("parallel", "arbitrary"),
    )(proj, proj, proj, proj, lb, gain, sums, masks)


def _hgrn2_bwd(proj, lb, gain, o, states, dog, name):
    s, d4 = proj.shape
    d = d4 // 4
    heads, c = d // HG_DIM, min(HG_CHUNK, s)
    n_chunks = s // c
    sums, masks, levels = _hg_tables(c)

    def body(qr_ref, fr_ref, ir_ref, gr_ref, lb_ref, gain_ref, sums_ref, masks_ref, o_ref, states_ref, dog_ref,
             dq_ref, df_ref, di_ref, dg_ref, dlb_ref, dgain_ref, dst_ref):
        @pl.when(pl.program_id(1) == 0)
        def _():
            dst_ref[...] = jnp.zeros_like(dst_ref)
            dlb_ref[...] = jnp.zeros_like(dlb_ref)
            dgain_ref[...] = jnp.zeros_like(dgain_ref)

        qr, lbv, gainv = qr_ref[...], lb_ref[...], gain_ref[...]
        sq, q, sg, forget, lf, k = _hg_gates(qr, fr_ref[...], lbv)
        x = jnp.exp(_mdot2(sums_ref[...], lf, NN))
        st, dst = states_ref[0, 0], dst_ref[...]
        e_end = x[c - 1:c]

        ov, gate = o_ref[...], jax.nn.sigmoid(gr_ref[...])
        r = lax.rsqrt(jnp.mean(ov * ov, axis=-1, keepdims=True) + EPS)
        orr = ov * r
        dogv = dog_ref[...]
        dg_ref[...] = (dogv * orr * gainv * gate * (1.0 - gate)).astype(BF16)
        don = dogv * gate
        dgain_ref[0] += jnp.sum(don * orr, axis=0, keepdims=True)
        t = don * gainv
        dob = (r * (t - orr * jnp.mean(t * orr, axis=-1, keepdims=True))).astype(BF16)

        vb = ir_ref[...].astype(BF16)
        scores, eye, ops = _hg_scores(q, k, x, masks_ref, c, levels)
        qh_f, kh_f = q * x[0:c], k * x[c:2 * c]
        qh, kh = qh_f.astype(BF16), kh_f.astype(BF16)
        dstb = dst.astype(BF16)
        dscores = _dot(dob, vb, NT)
        di_ref[...] = (_dot(scores.astype(BF16), dob, TN) + _dot(kh, dstb, NT)).astype(BF16)
        dqh = _dot(dob, st.astype(BF16), NN)
        dkh = _dot(vb, dstb, NN)
        decay_grad = e_end * jnp.sum(dst * st, axis=0, keepdims=True)
        dst_ref[...] = dst * e_end + _dot(dob, qh, TN)

        ddiag = jnp.sum(eye * dscores, axis=-1, keepdims=True)
        dq = dqh * x[0:c] + ddiag * k
        dk = dkh * x[c:2 * c] + ddiag * q
        dexp = [dqh * qh_f, dkh * kh_f]
        for l, (qm, km) in enumerate(ops):
            dsm = (masks_ref[l] * dscores).astype(BF16)
            dqm, dkm = _dot(dsm, km, NN), _dot(dsm, qm, TN)
            xq, xk = x[(2 + 2 * l) * c:(3 + 2 * l) * c], x[(3 + 2 * l) * c:(4 + 2 * l) * c]
            dq = dq + dqm * xq
            dk = dk + dkm * xk
            dexp += [dqm * (q * xq), dkm * (k * xk)]
        dlf = _mdot2(sums_ref[...], jnp.concatenate(dexp, axis=0), TN) + decay_grad
        dforget = dlf / forget - dk
        dlb_ref[...] += jnp.sum(dforget * (1.0 - sg), axis=0, keepdims=True)
        df_ref[...] = (dforget * (1.0 - lbv) * sg * (1.0 - sg)).astype(BF16)
        dq_ref[...] = (dq * sq * (1.0 + qr * (1.0 - sq))).astype(BF16)

    pspecs, tile, lbs, gs, state = _hg_specs(c, heads, True, n_chunks)
    const = [pl.BlockSpec(sums.shape, lambda h, ci: (0, 0)), pl.BlockSpec(masks.shape, lambda h, ci: (0, 0, 0))]
    return _call(
        body, name=name, grid=(heads, n_chunks), in_specs=pspecs + [lbs, gs] + const + [tile, state, tile],
        out_specs=[tile, tile, tile, tile, lbs, pl.BlockSpec((1, 1, HG_DIM), lambda h, ci: (h, 0, 0))],
        out_shape=[_sds((s, d), BF16)] * 4 + [_sds((1, d), F32), _sds((heads, 1, HG_DIM), F32)],
        scratch_shapes=[pltpu.VMEM((HG_DIM, HG_DIM), F32)],
        compiler_params=_cparams("parallel", "arbitrary"),
    )(proj, proj, proj, proj, lb, gain, sums, masks, o, states, dog)


def _lower_bounds(logits, name):
    n, d = logits.shape

    def body(l_ref, lb_ref):
        lv = l_ref[...]
        e = jnp.exp(lv - jnp.max(lv, axis=0, keepdims=True))
        p = e / jnp.sum(e, axis=0, keepdims=True)
        run = jnp.zeros((1, d), F32)
        for j in range(n):
            if j > 0:
                run = run + p[j:j + 1]
            lb_ref[j:j + 1, :] = run

    return _call(body, name=name, out_shape=_sds((n, d), F32))(logits)


def _lower_bounds_bwd(logits, dlb_parts, name):
    n, d = logits.shape

    def body(l_ref, dlb_ref, dl_ref):
        lv, dv = l_ref[...], dlb_ref[0]
        for dev in range(1, N_DEV):
            dv = dv + dlb_ref[dev]
        e = jnp.exp(lv - jnp.max(lv, axis=0, keepdims=True))
        p = e / jnp.sum(e, axis=0, keepdims=True)
        run = jnp.zeros((1, d), F32)
        dps = [None] * n
        for j in range(n - 1, 0, -1):
            run = run + dv[j:j + 1]
            dps[j] = run
        dps[0] = jnp.zeros((1, d), F32)
        inner = jnp.zeros((1, d), F32)
        for j in range(n):
            inner = inner + p[j:j + 1] * dps[j]
        for j in range(n):
            dl_ref[j:j + 1, :] = p[j:j + 1] * (dps[j] - inner)

    return _call(body, name=name, out_shape=_sds((n, d), F32))(logits, dlb_parts)


_ANY = pl.BlockSpec(memory_space=pl.ANY)
_MESH = pl.DeviceIdType.MESH


def _all_gather(x, name):
    def body(x_ref, out_ref, send_sems, recv_sems, local_sem):
        mx, my, mc = lax.axis_index("x"), lax.axis_index("y"), lax.axis_index("c")
        me, sibling = (mx, my, mc), (mx, my, 1 - mc)
        chips = [(1 - mx, my), (mx, 1 - my), (1 - mx, 1 - my)]

        def slot(px, py, pc):
            return out_ref.at[4 * px + 2 * py + pc]

        def copy(k, block, to, src=None):
            return pltpu.make_async_remote_copy(
                src_ref=slot(*block) if src is None else src, dst_ref=slot(*block),
                send_sem=send_sems.at[k], recv_sem=recv_sems.at[k], device_id=to, device_id_type=_MESH)

        mine = pltpu.make_async_copy(x_ref, slot(*me), local_sem)
        mine.start()
        first = [copy(0, me, sibling, src=x_ref)]
        first += [copy(1 + j, me, (*chip, mc), src=x_ref) for j, chip in enumerate(chips)]
        for cp in first:
            cp.start()
        passed = [copy(4 + j, (*chip, mc), sibling) for j, chip in enumerate(chips)]
        for j, chip in enumerate(chips):
            copy(1 + j, (*chip, mc), me).wait_recv()
            passed[j].start()
        copy(0, sibling, me).wait_recv()
        for j, chip in enumerate(chips):
            copy(4 + j, (*chip, 1 - mc), me).wait_recv()
        for cp in first + passed:
            cp.wait_send()
        mine.wait()

    return _call(
        body, name=name, out_shape=_sds((N_DEV,) + x.shape, x.dtype), in_specs=[_ANY], out_specs=_ANY,
        scratch_shapes=[pltpu.SemaphoreType.DMA((7,)), pltpu.SemaphoreType.DMA((7,)), pltpu.SemaphoreType.DMA],
    )(x)


def _exchange(g8, name):
    def body(g_ref, out_ref, send_sems, recv_sems, local_sem):
        mx, my, mc = lax.axis_index("x"), lax.axis_index("y"), lax.axis_index("c")
        me = 4 * mx + 2 * my + mc
        mine = pltpu.make_async_copy(g_ref.at[me], out_ref.at[me], local_sem)
        mine.start()
        copies = []
        for k in range(1, N_DEV):
            px, py, pc = mx ^ (k >> 2), my ^ ((k >> 1) & 1), mc ^ (k & 1)
            peer = 4 * px + 2 * py + pc
            cp = pltpu.make_async_remote_copy(
                src_ref=g_ref.at[peer], dst_ref=out_ref.at[me], send_sem=send_sems.at[k - 1],
                recv_sem=recv_sems.at[k - 1], device_id=(px, py, pc), device_id_type=_MESH)
            cp.start()
            arrival = pltpu.make_async_remote_copy(
                src_ref=g_ref.at[peer], dst_ref=out_ref.at[peer], send_sem=send_sems.at[k - 1],
                recv_sem=recv_sems.at[k - 1], device_id=(px, py, pc), device_id_type=_MESH)
            copies.append((cp, arrival))
        for _, arrival in copies:
            arrival.wait_recv()
        for cp, _ in copies:
            cp.wait_send()
        mine.wait()

    return _call(
        body, name=name, out_shape=_sds(g8.shape, g8.dtype), in_specs=[_ANY], out_specs=_ANY,
        scratch_shapes=[pltpu.SemaphoreType.DMA((7,)), pltpu.SemaphoreType.DMA((7,)), pltpu.SemaphoreType.DMA],
    )(g8)


def _adamw(parts, w, m, v, name):
    _, r, c = parts.shape
    tr = r if r <= 256 else 256
    assert r % tr == 0, (name, parts.shape)

    def body(p_ref, w_ref, m_ref, v_ref, g_ref, d_ref, nm_ref, nv_ref):
        g = p_ref[0]
        for dev in range(1, N_DEV):
            g = g + p_ref[dev]
        nm = ADAM_B1 * m_ref[...] + (1.0 - ADAM_B1) * g
        nv = ADAM_B2 * v_ref[...] + (1.0 - ADAM_B2) * (g * g)
        m_hat = nm / (1.0 - ADAM_B1 ** ADAM_STEP)
        v_hat = nv / (1.0 - ADAM_B2 ** ADAM_STEP)
        g_ref[...] = g
        nm_ref[...] = nm
        nv_ref[...] = nv
        d_ref[...] = -ADAM_LR * (m_hat / (jnp.sqrt(v_hat) + ADAM_EPS) + ADAM_WD * w_ref[...])

    tile = pl.BlockSpec((tr, c), lambda i: (i, 0))
    return _call(
        body, name=name, grid=(r // tr,),
        in_specs=[pl.BlockSpec((N_DEV, tr, c), lambda i: (0, i, 0)), tile, tile, tile], out_specs=[tile] * 4,
        out_shape=[_sds((r, c), F32)] * 4, compiler_params=_cparams("parallel"),
    )(parts, w, m, v)


def _gather_cols(w, name):
    l, k, n = w.shape
    g = _all_gather(w.astype(BF16).reshape(l * k, n), name).reshape(N_DEV, l, k, n)
    return [jnp.transpose(g[:, i], (1, 0, 2)).reshape(k, N_DEV * n) for i in range(l)]


def _gather_rows(w, name):
    l, k, n = w.shape
    g = _all_gather(w.astype(BF16).reshape(l * k, n), name).reshape(N_DEV, l, k, n)
    return [g[:, i].reshape(N_DEV * k, n) for i in range(l)]


def _parts_cols(grads):
    k, n8 = grads[0].shape
    g = jnp.stack(grads).reshape(len(grads), k, N_DEV, n8 // N_DEV)
    return jnp.transpose(g, (2, 0, 1, 3)).reshape(N_DEV, len(grads) * k, n8 // N_DEV)


def _parts_rows(grads):
    k8, n = grads[0].shape
    g = jnp.stack(grads).reshape(len(grads), N_DEV, k8 // N_DEV, n)
    return jnp.transpose(g, (1, 0, 2, 3)).reshape(N_DEV, len(grads) * (k8 // N_DEV), n)


def _pad_rows(a, rows):
    return jnp.concatenate([a, jnp.zeros((rows - a.shape[0], a.shape[1]), a.dtype)], axis=0)


def kernel(x, norm_gains, sb_w_qkv, sb_q_gain, sb_k_gain, sb_w_o, hg_w_in, hg_lb_logits, hg_norm_gain, hg_w_o, mlp_w1, mlp_w2, loss_target, m_norm_gains, m_sb_w_qkv, m_sb_q_gain, m_sb_k_gain, m_sb_w_o, m_hg_w_in, m_hg_lb_logits, m_hg_norm_gain, m_hg_w_o, m_mlp_w1, m_mlp_w2, v_norm_gains, v_sb_w_qkv, v_sb_q_gain, v_sb_k_gain, v_sb_w_o, v_hg_w_in, v_hg_lb_logits, v_hg_norm_gain, v_hg_w_o, v_mlp_w1, v_mlp_w2):
    depth, _, d_loc = norm_gains.shape
    n_sb, n_hg = sb_w_qkv.shape[0], hg_w_in.shape[0]
    xs = x[0]
    target = loss_target[0]
    s, d = xs.shape
    me = 4 * lax.axis_index("x") + 2 * lax.axis_index("y") + lax.axis_index("c")

    w_qkv = _gather_cols(sb_w_qkv, "gather_w_qkv")
    w_o = _gather_rows(sb_w_o, "gather_sb_w_o")
    w_in = _gather_cols(hg_w_in, "gather_hg_w_in")
    w_ho = _gather_rows(hg_w_o, "gather_hg_w_o")
    w_1 = _gather_cols(mlp_w1, "gather_mlp_w1")
    w_2 = _gather_rows(mlp_w2, "gather_mlp_w2")
    n_gain_rows = 2 * depth
    small_rows = -(-(n_gain_rows + n_hg) // 8) * 8
    small = _pad_rows(jnp.concatenate([norm_gains.reshape(n_gain_rows, d_loc), hg_lb_logits], axis=0), small_rows)
    small = _all_gather(small, "gather_small")
    gains_full = jnp.transpose(small[:, :n_gain_rows], (1, 0, 2)).reshape(depth, 2, 1, d)
    logits_full = jnp.transpose(small[:, n_gain_rows:n_gain_rows + n_hg], (1, 0, 2)).reshape(n_hg, d)
    lower = _lower_bounds(logits_full, "lower_bounds")

    saved = []
    cur = xs
    for layer in range(depth):
        j = layer // 2
        h = _rmsnorm(cur, gains_full[layer, 0], f"norm_mix_{layer}")
        if layer % 2 == 0:
            qkv = _matmul("nn", h, w_qkv[j], f"qkv_{layer}", [F32])
            qk_gains = _qk_gain_table(sb_q_gain[j], sb_k_gain[j])
            qkvn = _qknorm(qkv, qk_gains, f"qknorm_{layer}")
            o = _sb_attention_fwd(qkvn, f"sb_fwd_{layer}")
            mix = (qkv, qk_gains, qkvn, o)
            x1 = _matmul("nn", o, w_o[j], f"sb_out_{layer}", [F32], _ep_add, [cur])
        else:
            proj = _matmul("nn", h, w_in[j], f"hg_in_{layer}", [F32], tn=1024)
            og, o, states = _hgrn2_fwd(proj, lower[j:j + 1], hg_norm_gain[j:j + 1], f"hg_fwd_{layer}")
            mix = (proj, og, o, states)
            x1 = _matmul("nn", og, w_ho[j], f"hg_out_{layer}", [F32], _ep_add, [cur])
        h2 = _rmsnorm(x1, gains_full[layer, 1], f"norm_mlp_{layer}")
        a, u = _matmul("nn", h2, w_1[layer], f"mlp_up_{layer}", [F32, BF16], _ep_relu2)
        x2 = _matmul("nn", u, w_2[layer], f"mlp_down_{layer}", [F32], _ep_add, [x1])
        saved.append((cur, h, mix, x1, h2, a, u))
        cur = x2

    loss_tile, dx, dxb = _loss_head(cur, target, "loss_head")
    loss = lax.psum(loss_tile[0, 0], AXES)

    d_gains = [[None, None] for _ in range(depth)]
    d_w1, d_w2 = [None] * depth, [None] * depth
    d_wqkv, d_wo, d_qk = [None] * n_sb, [None] * n_sb, [None] * n_sb
    d_win, d_who, d_lb, d_hgain = [None] * n_hg, [None] * n_hg, [None] * n_hg, [None] * n_hg
    for layer in reversed(range(depth)):
        j = layer // 2
        x0, h, mix, x1, h2, a, u = saved[layer]
        d_w2[layer] = _matmul("tn", u, dxb, f"d_mlp_w2_{layer}", [F32])
        da = _matmul("nt", dxb, w_2[layer], f"d_mlp_act_{layer}", [BF16], _ep_relu2_bwd, [a])
        d_w1[layer] = _matmul("tn", h2, da, f"d_mlp_w1_{layer}", [F32])
        dh2 = _matmul("nt", da, w_1[layer], f"d_mlp_in_{layer}", [F32])
        dx, dxb, d_gains[layer][1] = _rmsnorm_bwd(x1, gains_full[layer, 1], dh2, dx, f"d_norm_mlp_{layer}")
        if layer % 2 == 0:
            qkv, qk_gains, qkvn, o = mix
            d_wo[j] = _matmul("tn", o, dxb, f"d_sb_w_o_{layer}", [F32])
            do = _matmul("nt", dxb, w_o[j], f"d_sb_o_{layer}", [F32])
            dq, dk, dv = _sb_attention_bwd(qkvn, do, f"sb_bwd_{layer}")
            dqkv, d_qk[j] = _qknorm_bwd(qkv, jnp.concatenate([dq, dk, dv], axis=1), qk_gains, f"d_qknorm_{layer}")
            d_wqkv[j] = _matmul("tn", h, dqkv, f"d_sb_w_qkv_{layer}", [F32])
            dh = _matmul("nt", dqkv, w_qkv[j], f"d_sb_in_{layer}", [F32])
        else:
            proj, og, o, states = mix
            d_who[j] = _matmul("tn", og, dxb, f"d_hg_w_o_{layer}", [F32])
            dog = _matmul("nt", dxb, w_ho[j], f"d_hg_o_{layer}", [F32])
            dq, df, di, dg, d_lb[j], d_hgain[j] = _hgrn2_bwd(
                proj, lower[j:j + 1], hg_norm_gain[j:j + 1], o, states, dog, f"hg_bwd_{layer}")
            dproj = jnp.concatenate([dq, df, di, dg], axis=1)
            d_win[j] = _matmul("tn", h, dproj, f"d_hg_w_in_{layer}", [F32])
            dh = _matmul("nt", dproj, w_in[j], f"d_hg_in_{layer}", [F32])
        dx, dxb, d_gains[layer][0] = _rmsnorm_bwd(x0, gains_full[layer, 0], dh, dx, f"d_norm_mix_{layer}")
    grad_x = dx[None]

    def update(parts, w, m, v, name):
        shape = w.shape
        flat = (shape[0] * shape[1], shape[2])
        got = _exchange(parts, "exchange_" + name)
        return [r.reshape(shape) for r in _adamw(got, w.reshape(flat), m.reshape(flat), v.reshape(flat), "adamw_" + name)]

    big = {
        "sb_w_qkv": update(_parts_cols(d_wqkv), sb_w_qkv, m_sb_w_qkv, v_sb_w_qkv, "sb_w_qkv"),
        "sb_w_o": update(_parts_rows(d_wo), sb_w_o, m_sb_w_o, v_sb_w_o, "sb_w_o"),
        "hg_w_in": update(_parts_cols(d_win), hg_w_in, m_hg_w_in, v_hg_w_in, "hg_w_in"),
        "hg_w_o": update(_parts_rows(d_who), hg_w_o, m_hg_w_o, v_hg_w_o, "hg_w_o"),
        "mlp_w1": update(_parts_cols(d_w1), mlp_w1, m_mlp_w1, v_mlp_w1, "mlp_w1"),
        "mlp_w2": update(_parts_rows(d_w2), mlp_w2, m_mlp_w2, v_mlp_w2, "mlp_w2"),
    }

    d_gain_rows = jnp.concatenate([d_gains[l][t] for l in range(depth) for t in range(2)], axis=0)
    d_lb_rows = jnp.concatenate(d_lb, axis=0)
    def fold(t):
        return t[:, :SB_HEAD_DIM] + t[:, SB_HEAD_DIM:]
    d_qg = jnp.concatenate([fold(d_qk[i][0]) for i in range(n_sb)], axis=0)
    d_kg = jnp.concatenate([fold(d_qk[i][1]) for i in range(n_sb)], axis=0)
    d_hg = jnp.concatenate([jnp.sum(d_hgain[i], axis=0) for i in range(n_hg)], axis=0)
    per_row = d // LANES
    packed = jnp.concatenate([
        d_gain_rows.reshape(n_gain_rows * per_row, LANES), d_lb_rows.reshape(n_hg * per_row, LANES),
        jnp.concatenate([d_qg, d_kg], axis=1), d_hg], axis=0)
    n_packed = packed.shape[0]
    packed = _pad_rows(packed, -(-n_packed // 8) * 8)
    everyone = _all_gather(packed, "gather_small_grads")
    o_lb = n_gain_rows * per_row
    o_qk = o_lb + n_hg * per_row
    o_hg = o_qk + n_sb

    def mine_of(rows, count):
        return lax.dynamic_slice_in_dim(rows.reshape(N_DEV, count, per_row, LANES), me, 1, axis=2)[:, :, 0]

    d_logits_full = _lower_bounds_bwd(logits_full, everyone[:, o_lb:o_qk].reshape(N_DEV, n_hg, d), "lower_bounds_bwd")
    d_logits_mine = lax.dynamic_slice_in_dim(d_logits_full.reshape(n_hg, per_row, LANES), me, 1, axis=1)[:, 0]
    zeros7 = jnp.zeros((N_DEV - 1, n_hg, LANES), F32)
    small_parts = jnp.concatenate([
        mine_of(everyone[:, :o_lb], n_gain_rows),
        jnp.concatenate([d_logits_mine[None], zeros7], axis=0),
        everyone[:, o_qk:o_hg], everyone[:, o_hg:o_hg + n_hg]], axis=1)
    rows_small = small_parts.shape[1]
    pad_to = -(-rows_small // 8) * 8
    small_parts = jnp.concatenate([small_parts, jnp.zeros((N_DEV, pad_to - rows_small, LANES), F32)], axis=1)

    def pack_small(ng, qg, kg, lbl, hgn):
        return _pad_rows(jnp.concatenate([
            ng.reshape(n_gain_rows, d_loc), lbl, jnp.concatenate([qg, kg], axis=1), hgn], axis=0), pad_to)

    res = _adamw(small_parts,
                 pack_small(norm_gains, sb_q_gain, sb_k_gain, hg_lb_logits, hg_norm_gain),
                 pack_small(m_norm_gains, m_sb_q_gain, m_sb_k_gain, m_hg_lb_logits, m_hg_norm_gain),
                 pack_small(v_norm_gains, v_sb_q_gain, v_sb_k_gain, v_hg_lb_logits, v_hg_norm_gain), "adamw_small")

    def unpack_small(t):
        o1 = n_gain_rows
        o2 = o1 + n_hg
        o3 = o2 + n_sb
        return {"norm_gains": t[:o1].reshape(depth, 2, d_loc), "hg_lb_logits": t[o1:o2],
                "sb_q_gain": t[o2:o3, :SB_HEAD_DIM], "sb_k_gain": t[o2:o3, SB_HEAD_DIM:],
                "hg_norm_gain": t[o3:o3 + n_hg]}

    small_out = [unpack_small(t) for t in res]
    order = ["norm_gains", "sb_w_qkv", "sb_q_gain", "sb_k_gain", "sb_w_o", "hg_w_in", "hg_lb_logits",
             "hg_norm_gain", "hg_w_o", "mlp_w1", "mlp_w2"]
    outs = [loss, grad_x]
    for kind in range(4):
        outs += [big[n][kind] if n in big else small_out[kind][n] for n in order]
    return tuple(outs)
```

```python
import functools
import math

import numpy as np
import jax
import jax.numpy as jnp
from jax import lax
from jax.experimental import pallas as pl
from jax.experimental.pallas import tpu as pltpu

F32 = jnp.float32
BF16 = jnp.bfloat16
EPS = 1e-6
SB_HEAD_DIM = 64
HG_DIM = 128
LANES = 128
N_DEV = 8
AXES = ("x", "y", "c")
VMEM_LIMIT_BYTES = 48 * 1024 * 1024
SB_SCALE = 1.0 / math.sqrt(SB_HEAD_DIM)
ATT_BLOCK = 256
HG_CHUNK = 64
ADAM_LR, ADAM_B1, ADAM_B2, ADAM_EPS, ADAM_WD, ADAM_STEP = 0.001, 0.9, 0.999, 1e-08, 0.01, 10


def _call(body, **kw):
    return pl.pallas_call(body, **kw)


def _sds(shape, dtype):
    return jax.ShapeDtypeStruct(tuple(shape), dtype)


def _cparams(*sem):
    return pltpu.CompilerParams(dimension_semantics=sem or None, vmem_limit_bytes=VMEM_LIMIT_BYTES)


def _split_bf16(x):
    hi = x.astype(BF16)
    lo = (x - hi.astype(F32)).astype(BF16)
    return hi, lo


def _dot(a, b, dims):
    return lax.dot_general(a, b, (dims, ((), ())), preferred_element_type=F32)


NN = ((1,), (0,))
NT = ((1,), (1,))
TN = ((0,), (0,))


def _dot2(x, m, dims):
    hi, lo = _split_bf16(x)
    return _dot(hi, m, dims) + _dot(lo, m, dims)


def _mdot2(m, x, dims):
    hi, lo = _split_bf16(x)
    return _dot(m, hi, dims) + _dot(m, lo, dims)


def _matmul(kind, a, b, name, out_dtypes, epilogue=None, extras=(), tm=512, tn=512):
    if kind == "nn":
        (m, k), n = a.shape, b.shape[1]
    elif kind == "nt":
        (m, k), n = a.shape, b.shape[0]
    else:
        (k, m), n = a.shape, b.shape[1]
    tm, tn = min(tm, m), min(tn, n)
    assert m % tm == 0 and n % tn == 0, (name, a.shape, b.shape)
    a_spec = pl.BlockSpec((k, tm), lambda i, j: (0, i)) if kind == "tn" else pl.BlockSpec((tm, k), lambda i, j: (i, 0))
    b_spec = pl.BlockSpec((tn, k), lambda i, j: (j, 0)) if kind == "nt" else pl.BlockSpec((k, tn), lambda i, j: (0, j))
    o_spec = pl.BlockSpec((tm, tn), lambda i, j: (i, j))
    dims = {"nn": NN, "nt": NT, "tn": TN}[kind]
    n_ex = len(extras)

    def body(*refs):
        a_ref, b_ref = refs[:2]
        ex, outs = refs[2:2 + n_ex], refs[2 + n_ex:]
        acc = _dot(a_ref[...].astype(BF16), b_ref[...].astype(BF16), dims)
        res = epilogue(acc, *[e[...] for e in ex]) if epilogue is not None else (acc,)
        for o_ref, r in zip(outs, res):
            o_ref[...] = r.astype(o_ref.dtype)

    out = _call(
        body, name=name, grid=(m // tm, n // tn),
        in_specs=[a_spec, b_spec] + [o_spec] * n_ex,
        out_specs=[o_spec] * len(out_dtypes),
        out_shape=[_sds((m, n), dt) for dt in out_dtypes],
        compiler_params=_cparams("parallel", "parallel"),
    )(a, b, *extras)
    return out if len(out_dtypes) > 1 else out[0]


def _ep_add(acc, res):
    return (acc + res,)


def _ep_relu2(acc):
    r = jnp.maximum(acc, 0.0)
    return acc, r * r


def _ep_relu2_bwd(acc, a):
    return (acc * (2.0 * jnp.maximum(a, 0.0)),)


def _rmsnorm(x, g, name):
    s, d = x.shape
    tm = min(s, 512)

    def body(x_ref, g_ref, h_ref):
        xv = x_ref[...]
        r = lax.rsqrt(jnp.mean(xv * xv, axis=-1, keepdims=True) + EPS)
        h_ref[...] = (xv * r * g_ref[...]).astype(BF16)

    return _call(
        body, name=name, grid=(s // tm,),
        in_specs=[pl.BlockSpec((tm, d), lambda i: (i, 0)), pl.BlockSpec((1, d), lambda i: (0, 0))],
        out_specs=pl.BlockSpec((tm, d), lambda i: (i, 0)),
        out_shape=_sds((s, d), BF16), compiler_params=_cparams("parallel"),
    )(x, g)


def _rmsnorm_bwd(x, g, dh, dres, name):
    s, d = x.shape
    tm = min(s, 512)

    def body(x_ref, g_ref, dh_ref, dres_ref, dx_ref, dxb_ref, dg_ref):
        xv, dhv = x_ref[...], dh_ref[...]
        r = lax.rsqrt(jnp.mean(xv * xv, axis=-1, keepdims=True) + EPS)
        xr = xv * r
        t = dhv * g_ref[...]
        dx = dres_ref[...] + r * (t - xr * jnp.mean(t * xr, axis=-1, keepdims=True))
        dx_ref[...] = dx
        dxb_ref[...] = dx.astype(BF16)

        @pl.when(pl.program_id(0) == 0)
        def _():
            dg_ref[...] = jnp.zeros_like(dg_ref)

        dg_ref[...] += jnp.sum(dhv * xr, axis=0, keepdims=True)

    row = pl.BlockSpec((tm, d), lambda i: (i, 0))
    vec = pl.BlockSpec((1, d), lambda i: (0, 0))
    return _call(
        body, name=name, grid=(s // tm,), in_specs=[row, vec, row, row], out_specs=[row, row, vec],
        out_shape=[_sds((s, d), F32), _sds((s, d), BF16), _sds((1, d), F32)],
        compiler_params=_cparams("arbitrary"),
    )(x, g, dh, dres)


def _loss_head(y, target, name):
    s, d = y.shape
    tm = min(s, 512)

    def body(y_ref, t_ref, loss_ref, dy_ref, dyb_ref):
        err = y_ref[...] - t_ref[...]
        dy = err * (1.0 / d)
        dy_ref[...] = dy
        dyb_ref[...] = dy.astype(BF16)

        @pl.when(pl.program_id(0) == 0)
        def _():
            loss_ref[...] = jnp.zeros_like(loss_ref)

        part = 0.5 * jnp.sum(jnp.mean(err * err, axis=-1, keepdims=True), axis=0, keepdims=True)
        loss_ref[...] += part

    row = pl.BlockSpec((tm, d), lambda i: (i, 0))
    return _call(
        body, name=name, grid=(s // tm,), in_specs=[row, row],
        out_specs=[pl.BlockSpec((8, LANES), lambda i: (0, 0)), row, row],
        out_shape=[_sds((8, LANES), F32), _sds((s, d), F32), _sds((s, d), BF16)],
        compiler_params=_cparams("arbitrary"),
    )(y, target)


def _head_lane_mask():
    lane = lax.broadcasted_iota(jnp.int32, (1, LANES), 1)
    return lane < SB_HEAD_DIM


def _pair_rms(xv, first):
    x2 = xv * xv
    s0 = jnp.sum(jnp.where(first, x2, 0.0), axis=-1, keepdims=True)
    s1 = jnp.sum(jnp.where(first, 0.0, x2), axis=-1, keepdims=True)
    inv = 1.0 / SB_HEAD_DIM
    return jnp.where(first, lax.rsqrt(s0 * inv + EPS), lax.rsqrt(s1 * inv + EPS))


def _pair_mean(t, first):
    s0 = jnp.sum(jnp.where(first, t, 0.0), axis=-1, keepdims=True)
    s1 = jnp.sum(jnp.where(first, 0.0, t), axis=-1, keepdims=True)
    return jnp.where(first, s0, s1) * (1.0 / SB_HEAD_DIM)


def _qk_gain_table(q_gain, k_gain):
    return jnp.stack([jnp.tile(q_gain, 2) * SB_SCALE, jnp.tile(k_gain, 2), jnp.ones((LANES,), F32)])[:, None, :]


def _qknorm(qkv, gains, name):
    s, d3 = qkv.shape
    pairs = d3 // 3 // LANES
    tm = min(s, 512)

    def body(x_ref, g_ref, o_ref):
        xv = x_ref[...]
        first = _head_lane_mask()
        normed = xv * _pair_rms(xv, first) * g_ref[0]
        o_ref[...] = jnp.where(pl.program_id(0) < 2 * pairs, normed, xv).astype(BF16)

    tile = pl.BlockSpec((tm, LANES), lambda c, i: (i, c))
    return _call(
        body, name=name, grid=(3 * pairs, s // tm),
        in_specs=[tile, pl.BlockSpec((1, 1, LANES), lambda c, i: (c // pairs, 0, 0))], out_specs=tile,
        out_shape=_sds((s, d3), BF16), compiler_params=_cparams("parallel", "parallel"),
    )(qkv, gains)


def _qknorm_bwd(qkv, dqkv, gains, name):
    s, d3 = qkv.shape
    pairs = d3 // 3 // LANES
    tm = min(s, 512)

    def body(x_ref, d_ref, g_ref, dx_ref, dg_ref):
        c, i = pl.program_id(0), pl.program_id(1)
        xv, dv = x_ref[...], d_ref[...]
        first = _head_lane_mask()
        r = _pair_rms(xv, first)
        xr = xv * r
        t = dv * g_ref[0]
        dx = r * (t - xr * _pair_mean(t * xr, first))
        dx_ref[...] = jnp.where(c < 2 * pairs, dx, dv).astype(BF16)

        @pl.when((c % pairs == 0) & (i == 0))
        def _():
            dg_ref[...] = jnp.zeros_like(dg_ref)

        dg_ref[0] += jnp.sum(dv * xr, axis=0, keepdims=True)

    tile = pl.BlockSpec((tm, LANES), lambda c, i: (i, c))
    vec = pl.BlockSpec((1, 1, LANES), lambda c, i: (c // pairs, 0, 0))
    return _call(
        body, name=name, grid=(3 * pairs, s // tm), in_specs=[tile, tile, vec], out_specs=[tile, vec],
        out_shape=[_sds((s, d3), BF16), _sds((3, 1, LANES), F32)],
        compiler_params=_cparams("arbitrary", "arbitrary"),
    )(qkv, dqkv, gains)


def _softplus_parts(z):
    sp = jnp.maximum(z, 0.0) + jnp.log(1.0 + jnp.exp(-jnp.abs(z)))
    return sp, z - sp


def _diag_causal(tb):
    return lax.broadcasted_iota(jnp.int32, (tb, tb), 1) < lax.broadcasted_iota(jnp.int32, (tb, tb), 0)


def _later_keys(tb):
    r = lax.broadcasted_iota(jnp.int32, (tb, tb), 0)
    c = lax.broadcasted_iota(jnp.int32, (tb, tb), 1)
    return jnp.where(r > c, 1.0, 0.0).astype(BF16)


def _sb_scores(qa, kj, causal):
    sp, logsig = _softplus_parts(_dot(qa, kj, NT))
    return (-sp if causal is None else jnp.where(causal, -sp, 0.0)), logsig


def _sb_weights(stay, logsig, run, later, causal):
    w = jnp.exp(logsig + _dot2(stay, later, NN) + run)
    return w if causal is None else jnp.where(causal, w, 0.0)


def _sb_attention_fwd(qkvn, name):
    s, d3 = qkvn.shape
    d = d3 // 3
    pairs, tb = d // LANES, min(ATT_BLOCK, s)
    nb = s // tb
    assert nb <= LANES

    def body(q_ref, k_ref, v_ref, o_ref, runs_ref):
        i = pl.program_id(1)
        first = _head_lane_mask()
        lane = lax.broadcasted_iota(jnp.int32, (1, LANES), 1)
        later = _later_keys(tb)
        q2 = q_ref[...]
        qs = (jnp.where(first, q2, jnp.zeros_like(q2)), jnp.where(first, jnp.zeros_like(q2), q2))

        def tiles(js, carry, causal):
            kvs = []
            for j in js:
                rows = pl.ds(pl.multiple_of(j * tb, tb), tb)
                kvs.append((k_ref[rows, :], v_ref[rows, :]))
            scores = [[_sb_scores(qs[h], kj, causal) for h in range(2)] for kj, _ in kvs]
            run = [carry[0][0], carry[1][0]]
            acc = [carry[0][1], carry[1][1]]
            runs = [carry[0][2], carry[1][2]]
            weights = []
            for t, j in enumerate(js):
                weights.append([_sb_weights(*scores[t][h], run[h], later, causal) for h in range(2)])
                for h in range(2):
                    runs[h] = jnp.where(lane == j, run[h], runs[h])
                    run[h] = run[h] + jnp.sum(scores[t][h][0], axis=-1, keepdims=True)
            for t, (_, vj) in enumerate(kvs):
                for h in range(2):
                    acc[h] = acc[h] + _dot(weights[t][h].astype(BF16), vj, NN)
            return tuple((run[h], acc[h], runs[h]) for h in range(2))

        zero = (jnp.zeros((tb, 1), F32), jnp.zeros((tb, LANES), F32), jnp.zeros((tb, LANES), F32))
        carry = tiles([i], (zero, zero), _diag_causal(tb))
        carry = lax.fori_loop(0, i // 2, lambda jj, c: tiles([i - 1 - 2 * jj, i - 2 - 2 * jj], c, None), carry)
        carry = lax.cond(i % 2 == 1, lambda c: tiles([0], c, None), lambda c: c, carry)
        o_ref[...] = jnp.where(first, carry[0][1], carry[1][1])
        runs_ref[0] = carry[0][2]
        runs_ref[1] = carry[1][2]

    return _call(
        body, name=name, grid=(pairs, nb),
        in_specs=[pl.BlockSpec((tb, LANES), lambda p, i: (i, p)),
                  pl.BlockSpec((s, LANES), lambda p, i: (0, pairs + p)),
                  pl.BlockSpec((s, LANES), lambda p, i: (0, 2 * pairs + p))],
        out_specs=[pl.BlockSpec((tb, LANES), lambda p, i: (i, p)),
                   pl.BlockSpec((2, tb, LANES), lambda p, i: (p * nb + i, 0, 0))],
        out_shape=[_sds((s, d), F32), _sds((pairs * nb * 2, tb, LANES), F32)],
        compiler_params=_cparams("parallel", "parallel"),
    )(qkvn, qkvn, qkvn)


def _sb_attention_bwd(qkvn, do, runs, name):
    s, d3 = qkvn.shape
    d = d3 // 3
    pairs, tb = d // LANES, min(ATT_BLOCK, s)
    nb = s // tb

    def body(q_ref, k_ref, v_ref, do_ref, runs_ref, dq_ref, dk_ref, dv_ref):
        i = pl.program_id(1)

        @pl.when(i == 0)
        def _():
            dk_ref[...] = jnp.zeros_like(dk_ref)
            dv_ref[...] = jnp.zeros_like(dv_ref)

        first = _head_lane_mask()
        lane = lax.broadcasted_iota(jnp.int32, (1, LANES), 1)
        later = _later_keys(tb)
        q2, do2 = q_ref[...], do_ref[...].astype(BF16)
        zq = jnp.zeros_like(q2)
        qs = (jnp.where(first, q2, zq), jnp.where(first, zq, q2))
        dos = (jnp.where(first, do2, zq), jnp.where(first, zq, do2))

        def tiles(js, carry, causal):
            rows, kv = [], []
            for j in js:
                r = pl.ds(pl.multiple_of(j * tb, tb), tb)
                kj, vj = k_ref[r, :], v_ref[r, :]
                zk = jnp.zeros_like(kj)
                rows.append(r)
                kv.append([(jnp.where(first, kj, zk), jnp.where(first, vj, zk)),
                           (jnp.where(first, zk, kj), jnp.where(first, zk, vj))])
            pairs_th = [(t, h) for t in range(len(js)) for h in range(2)]
            scores = {(t, h): _sb_scores(qs[h], kv[t][h][0], causal) for t, h in pairs_th}
            w, g = {}, {}
            for t, h in pairs_th:
                run = jnp.sum(jnp.where(lane == js[t], runs_ref[h], 0.0), axis=-1, keepdims=True)
                w[t, h] = _sb_weights(*scores[t, h], run, later, causal)
                g[t, h] = w[t, h] * _dot(dos[h], kv[t][h][1], NT)
            gsum = [carry[0], carry[1]]
            dz = {}
            for t, h in pairs_th:
                before = _dot2(g[t, h], later, NT) + gsum[h]
                gsum[h] = gsum[h] + jnp.sum(g[t, h], axis=-1, keepdims=True)
                sig = jnp.exp(scores[t, h][1])
                d = g[t, h] * (1.0 - sig) - before * sig
                dz[t, h] = (d if causal is None else jnp.where(causal, d, 0.0)).astype(BF16)
            dq = carry[2]
            for t, h in pairs_th:
                dq = dq + _dot(dz[t, h], kv[t][h][0], NN)
            for t in range(len(js)):
                dk_ref[rows[t], :] += _dot(dz[t, 0], qs[0], TN) + _dot(dz[t, 1], qs[1], TN)
                dv_ref[rows[t], :] += _dot(w[t, 0].astype(BF16), dos[0], TN) + _dot(w[t, 1].astype(BF16), dos[1], TN)
            return gsum[0], gsum[1], dq

        carry = (jnp.zeros((tb, 1), F32), jnp.zeros((tb, 1), F32), jnp.zeros((tb, LANES), F32))
        carry = lax.fori_loop(0, i // 2, lambda jj, c: tiles([2 * jj, 2 * jj + 1], c, None), carry)
        carry = lax.cond(i % 2 == 1, lambda c: tiles([i - 1], c, None), lambda c: c, carry)
        carry = tiles([i], carry, _diag_causal(tb))
        dq_ref[...] = carry[2]

    q_spec = pl.BlockSpec((tb, LANES), lambda p, i: (i, p))
    return _call(
        body, name=name, grid=(pairs, nb),
        in_specs=[q_spec,
                  pl.BlockSpec((s, LANES), lambda p, i: (0, pairs + p)),
                  pl.BlockSpec((s, LANES), lambda p, i: (0, 2 * pairs + p)),
                  q_spec,
                  pl.BlockSpec((2, tb, LANES), lambda p, i: (p * nb + i, 0, 0))],
        out_specs=[q_spec, pl.BlockSpec((s, LANES), lambda p, i: (0, p)), pl.BlockSpec((s, LANES), lambda p, i: (0, p))],
        out_shape=[_sds((s, d), F32)] * 3,
        compiler_params=_cparams("parallel", "arbitrary"),
    )(qkvn, qkvn, qkvn, do, runs)


def _hg_tables(c):
    t = np.arange(c)[:, None]
    j = np.arange(c)[None, :]
    sums = [j <= t, j > t]
    masks = []
    m = c // 2
    while m >= 1:
        pos, base = t % (2 * m), t - t % (2 * m)
        sums.append((pos >= m) & (j >= base + m) & (j <= t))
        sums.append((pos < m) & (j > t) & (j <= base + m - 1))
        masks.append((t // (2 * m) == j // (2 * m)) & (t % (2 * m) >= m) & (j % (2 * m) < m))
        m //= 2
    return (jnp.asarray(np.concatenate(sums, 0), BF16), jnp.asarray(np.stack(masks), F32), len(masks))


def _hg_gates(qr, fr, lb):
    sq = jax.nn.sigmoid(qr)
    sg = jax.nn.sigmoid(fr)
    forget = lb + (1.0 - lb) * sg
    return sq, qr * sq, sg, forget, jnp.log(forget), (1.0 - lb) * (1.0 - sg)


def _hg_scores(q, k, x, masks_ref, c, levels):
    eye = (lax.broadcasted_iota(jnp.int32, (c, c), 0) == lax.broadcasted_iota(jnp.int32, (c, c), 1)).astype(F32)
    scores = eye * jnp.sum(q * k, axis=-1, keepdims=True)
    ops = []
    for l in range(levels):
        qm = (q * x[(2 + 2 * l) * c:(3 + 2 * l) * c]).astype(BF16)
        km = (k * x[(3 + 2 * l) * c:(4 + 2 * l) * c]).astype(BF16)
        scores = scores + masks_ref[l] * _dot(qm, km, NT)
        ops.append((qm, km))
    return scores, eye, ops


def _hg_specs(c, heads, reverse, n_chunks):
    def chunk(ci):
        return n_chunks - 1 - ci if reverse else ci
    proj = [pl.BlockSpec((c, HG_DIM), functools.partial(lambda h, ci, part: (chunk(ci), part * heads + h), part=p))
            for p in range(4)]
    tile = pl.BlockSpec((c, HG_DIM), lambda h, ci: (chunk(ci), h))
    lb = pl.BlockSpec((1, HG_DIM), lambda h, ci: (0, h))
    gain = pl.BlockSpec((1, HG_DIM), lambda h, ci: (0, 0))
    state = pl.BlockSpec((1, 1, HG_DIM, HG_DIM), lambda h, ci: (h, chunk(ci), 0, 0))
    return proj, tile, lb, gain, state


def _hgrn2_fwd(proj, lb, gain, name):
    s, d4 = proj.shape
    d = d4 // 4
    heads, c = d // HG_DIM, min(HG_CHUNK, s)
    n_chunks = s // c
    sums, masks, levels = _hg_tables(c)

    def body(qr_ref, fr_ref, ir_ref, gr_ref, lb_ref, gain_ref, sums_ref, masks_ref, og_ref, o_ref, states_ref, st_ref):
        @pl.when(pl.program_id(1) == 0)
        def _():
            st_ref[...] = jnp.zeros_like(st_ref)

        _, q, _, _, lf, k = _hg_gates(qr_ref[...], fr_ref[...], lb_ref[...])
        x = jnp.exp(_mdot2(sums_ref[...], lf, NN))
        st = st_ref[...]
        states_ref[0, 0] = st
        vb = ir_ref[...].astype(BF16)
        scores, _, _ = _hg_scores(q, k, x, masks_ref, c, levels)
        qh = (q * x[0:c]).astype(BF16)
        kh = (k * x[c:2 * c]).astype(BF16)
        o = _dot(qh, st.astype(BF16), NT) + _dot(scores.astype(BF16), vb, NN)
        st_ref[...] = st * x[c - 1:c] + _dot(vb, kh, TN)
        o_ref[...] = o
        r = lax.rsqrt(jnp.mean(o * o, axis=-1, keepdims=True) + EPS)
        og_ref[...] = (o * r * gain_ref[...] * jax.nn.sigmoid(gr_ref[...])).astype(BF16)

    pspecs, tile, lbs, gs, state = _hg_specs(c, heads, False, n_chunks)
    const = [pl.BlockSpec(sums.shape, lambda h, ci: (0, 0)), pl.BlockSpec(masks.shape, lambda h, ci: (0, 0, 0))]
    return _call(
        body, name=name, grid=(heads, n_chunks), in_specs=pspecs + [lbs, gs] + const,
        out_specs=[tile, tile, state],
        out_shape=[_sds((s, d), BF16), _sds((s, d), F32), _sds((heads, n_chunks, HG_DIM, HG_DIM), F32)],
        scratch_shapes=[pltpu.VMEM((HG_DIM, HG_DIM), F32)],
        compiler_params=_cparams("parallel", "arbitrary"),
    )(proj, proj, proj, proj, lb, gain, sums, masks)


def _hgrn2_bwd(proj, lb, gain, o, states, dog, name):
    s, d4 = proj.shape
    d = d4 // 4
    heads, c = d // HG_DIM, min(HG_CHUNK, s)
    n_chunks = s // c
    sums, masks, levels = _hg_tables(c)

    def body(qr_ref, fr_ref, ir_ref, gr_ref, lb_ref, gain_ref, sums_ref, masks_ref, o_ref, states_ref, dog_ref,
             dq_ref, df_ref, di_ref, dg_ref, dlb_ref, dgain_ref, dst_ref):
        @pl.when(pl.program_id(1) == 0)
        def _():
            dst_ref[...] = jnp.zeros_like(dst_ref)
            dlb_ref[...] = jnp.zeros_like(dlb_ref)
            dgain_ref[...] = jnp.zeros_like(dgain_ref)

        qr, lbv, gainv = qr_ref[...], lb_ref[...], gain_ref[...]
        sq, q, sg, forget, lf, k = _hg_gates(qr, fr_ref[...], lbv)
        x = jnp.exp(_mdot2(sums_ref[...], lf, NN))
        st, dst = states_ref[0, 0], dst_ref[...]
        e_end = x[c - 1:c]

        ov, gate = o_ref[...], jax.nn.sigmoid(gr_ref[...])
        r = lax.rsqrt(jnp.mean(ov * ov, axis=-1, keepdims=True) + EPS)
        orr = ov * r
        dogv = dog_ref[...]
        dg_ref[...] = (dogv * orr * gainv * gate * (1.0 - gate)).astype(BF16)
        don = dogv * gate
        dgain_ref[0] += jnp.sum(don * orr, axis=0, keepdims=True)
        t = don * gainv
        dob = (r * (t - orr * jnp.mean(t * orr, axis=-1, keepdims=True))).astype(BF16)

        vb = ir_ref[...].astype(BF16)
        scores, eye, ops = _hg_scores(q, k, x, masks_ref, c, levels)
        qh_f, kh_f = q * x[0:c], k * x[c:2 * c]
        qh, kh = qh_f.astype(BF16), kh_f.astype(BF16)
        dstb = dst.astype(BF16)
        dscores = _dot(dob, vb, NT)
        di_ref[...] = (_dot(scores.astype(BF16), dob, TN) + _dot(kh, dstb, NT)).astype(BF16)
        dqh = _dot(dob, st.astype(BF16), NN)
        dkh = _dot(vb, dstb, NN)
        decay_grad = e_end * jnp.sum(dst * st, axis=0, keepdims=True)
        dst_ref[...] = dst * e_end + _dot(dob, qh, TN)

        ddiag = jnp.sum(eye * dscores, axis=-1, keepdims=True)
        dq = dqh * x[0:c] + ddiag * k
        dk = dkh * x[c:2 * c] + ddiag * q
        dexp = [dqh * qh_f, dkh * kh_f]
        for l, (qm, km) in enumerate(ops):
            dsm = (masks_ref[l] * dscores).astype(BF16)
            dqm, dkm = _dot(dsm, km, NN), _dot(dsm, qm, TN)
            xq, xk = x[(2 + 2 * l) * c:(3 + 2 * l) * c], x[(3 + 2 * l) * c:(4 + 2 * l) * c]
            dq = dq + dqm * xq
            dk = dk + dkm * xk
            dexp += [dqm * (q * xq), dkm * (k * xk)]
        dlf = _mdot2(sums_ref[...], jnp.concatenate(dexp, axis=0), TN) + decay_grad
        dforget = dlf / forget - dk
        dlb_ref[...] += jnp.sum(dforget * (1.0 - sg), axis=0, keepdims=True)
        df_ref[...] = (dforget * (1.0 - lbv) * sg * (1.0 - sg)).astype(BF16)
        dq_ref[...] = (dq * sq * (1.0 + qr * (1.0 - sq))).astype(BF16)

    pspecs, tile, lbs, gs, state = _hg_specs(c, heads, True, n_chunks)
    const = [pl.BlockSpec(sums.shape, lambda h, ci: (0, 0)), pl.BlockSpec(masks.shape, lambda h, ci: (0, 0, 0))]
    return _call(
        body, name=name, grid=(heads, n_chunks), in_specs=pspecs + [lbs, gs] + const + [tile, state, tile],
        out_specs=[tile, tile, tile, tile, lbs, pl.BlockSpec((1, 1, HG_DIM), lambda h, ci: (h, 0, 0))],
        out_shape=[_sds((s, d), BF16)] * 4 + [_sds((1, d), F32), _sds((heads, 1, HG_DIM), F32)],
        scratch_shapes=[pltpu.VMEM((HG_DIM, HG_DIM), F32)],
        compiler_params=_cparams("parallel", "arbitrary"),
    )(proj, proj, proj, proj, lb, gain, sums, masks, o, states, dog)


def _lower_bounds(logits, name):
    n, d = logits.shape

    def body(l_ref, lb_ref):
        lv = l_ref[...]
        e = jnp.exp(lv - jnp.max(lv, axis=0, keepdims=True))
        p = e / jnp.sum(e, axis=0, keepdims=True)
        run = jnp.zeros((1, d), F32)
        for j in range(n):
            if j > 0:
                run = run + p[j:j + 1]
            lb_ref[j:j + 1, :] = run

    return _call(body, name=name, out_shape=_sds((n, d), F32))(logits)


def _lower_bounds_bwd(logits, dlb_parts, name):
    n, d = logits.shape

    def body(l_ref, dlb_ref, dl_ref):
        lv, dv = l_ref[...], dlb_ref[0]
        for dev in range(1, N_DEV):
            dv = dv + dlb_ref[dev]
        e = jnp.exp(lv - jnp.max(lv, axis=0, keepdims=True))
        p = e / jnp.sum(e, axis=0, keepdims=True)
        run = jnp.zeros((1, d), F32)
        dps = [None] * n
        for j in range(n - 1, 0, -1):
            run = run + dv[j:j + 1]
            dps[j] = run
        dps[0] = jnp.zeros((1, d), F32)
        inner = jnp.zeros((1, d), F32)
        for j in range(n):
            inner = inner + p[j:j + 1] * dps[j]
        for j in range(n):
            dl_ref[j:j + 1, :] = p[j:j + 1] * (dps[j] - inner)

    return _call(body, name=name, out_shape=_sds((n, d), F32))(logits, dlb_parts)


_ANY = pl.BlockSpec(memory_space=pl.ANY)
_MESH = pl.DeviceIdType.MESH


def _all_gather(x, name):
    def body(x_ref, out_ref, send_sems, recv_sems, local_sem):
        mx, my, mc = lax.axis_index("x"), lax.axis_index("y"), lax.axis_index("c")
        me, sibling = (mx, my, mc), (mx, my, 1 - mc)
        chips = [(1 - mx, my), (mx, 1 - my), (1 - mx, 1 - my)]

        def slot(px, py, pc):
            return out_ref.at[4 * px + 2 * py + pc]

        def copy(k, block, to, src=None):
            return pltpu.make_async_remote_copy(
                src_ref=slot(*block) if src is None else src, dst_ref=slot(*block),
                send_sem=send_sems.at[k], recv_sem=recv_sems.at[k], device_id=to, device_id_type=_MESH)

        mine = pltpu.make_async_copy(x_ref, slot(*me), local_sem)
        mine.start()
        first = [copy(0, me, sibling, src=x_ref)]
        first += [copy(1 + j, me, (*chip, mc), src=x_ref) for j, chip in enumerate(chips)]
        for cp in first:
            cp.start()
        passed = [copy(4 + j, (*chip, mc), sibling) for j, chip in enumerate(chips)]
        for j, chip in enumerate(chips):
            copy(1 + j, (*chip, mc), me).wait_recv()
            passed[j].start()
        copy(0, sibling, me).wait_recv()
        for j, chip in enumerate(chips):
            copy(4 + j, (*chip, 1 - mc), me).wait_recv()
        for cp in first + passed:
            cp.wait_send()
        mine.wait()

    return _call(
        body, name=name, out_shape=_sds((N_DEV,) + x.shape, x.dtype), in_specs=[_ANY], out_specs=_ANY,
        scratch_shapes=[pltpu.SemaphoreType.DMA((7,)), pltpu.SemaphoreType.DMA((7,)), pltpu.SemaphoreType.DMA],
    )(x)


def _exchange(g8, name):
    def body(g_ref, out_ref, send_sems, recv_sems, local_sem):
        mx, my, mc = lax.axis_index("x"), lax.axis_index("y"), lax.axis_index("c")
        me = 4 * mx + 2 * my + mc
        mine = pltpu.make_async_copy(g_ref.at[me], out_ref.at[me], local_sem)
        mine.start()
        copies = []
        for k in range(1, N_DEV):
            px, py, pc = mx ^ (k >> 2), my ^ ((k >> 1) & 1), mc ^ (k & 1)
            peer = 4 * px + 2 * py + pc
            cp = pltpu.make_async_remote_copy(
                src_ref=g_ref.at[peer], dst_ref=out_ref.at[me], send_sem=send_sems.at[k - 1],
                recv_sem=recv_sems.at[k - 1], device_id=(px, py, pc), device_id_type=_MESH)
            cp.start()
            arrival = pltpu.make_async_remote_copy(
                src_ref=g_ref.at[peer], dst_ref=out_ref.at[peer], send_sem=send_sems.at[k - 1],
                recv_sem=recv_sems.at[k - 1], device_id=(px, py, pc), device_id_type=_MESH)
            copies.append((cp, arrival))
        for _, arrival in copies:
            arrival.wait_recv()
        for cp, _ in copies:
            cp.wait_send()
        mine.wait()

    return _call(
        body, name=name, out_shape=_sds(g8.shape, g8.dtype), in_specs=[_ANY], out_specs=_ANY,
        scratch_shapes=[pltpu.SemaphoreType.DMA((7,)), pltpu.SemaphoreType.DMA((7,)), pltpu.SemaphoreType.DMA],
    )(g8)


def _adamw(parts, w, m, v, name):
    _, r, c = parts.shape
    tr = r if r <= 256 else 256
    assert r % tr == 0, (name, parts.shape)

    def body(p_ref, w_ref, m_ref, v_ref, g_ref, d_ref, nm_ref, nv_ref):
        g = p_ref[0].astype(F32)
        for dev in range(1, N_DEV):
            g = g + p_ref[dev].astype(F32)
        nm = ADAM_B1 * m_ref[...] + (1.0 - ADAM_B1) * g
        nv = ADAM_B2 * v_ref[...] + (1.0 - ADAM_B2) * (g * g)
        m_hat = nm / (1.0 - ADAM_B1 ** ADAM_STEP)
        v_hat = nv / (1.0 - ADAM_B2 ** ADAM_STEP)
        g_ref[...] = g
        nm_ref[...] = nm
        nv_ref[...] = nv
        d_ref[...] = -ADAM_LR * (m_hat / (jnp.sqrt(v_hat) + ADAM_EPS) + ADAM_WD * w_ref[...])

    tile = pl.BlockSpec((tr, c), lambda i: (i, 0))
    return _call(
        body, name=name, grid=(r // tr,),
        in_specs=[pl.BlockSpec((N_DEV, tr, c), lambda i: (0, i, 0)), tile, tile, tile], out_specs=[tile] * 4,
        out_shape=[_sds((r, c), F32)] * 4, compiler_params=_cparams("parallel"),
    )(parts, w, m, v)


def _gather_cols(w, name):
    l, k, n = w.shape
    g = _all_gather(w.astype(BF16).reshape(l * k, n), name).reshape(N_DEV, l, k, n)
    return [jnp.transpose(g[:, i], (1, 0, 2)).reshape(k, N_DEV * n) for i in range(l)]


def _gather_rows(w, name):
    l, k, n = w.shape
    g = _all_gather(w.astype(BF16).reshape(l * k, n), name).reshape(N_DEV, l, k, n)
    return [g[:, i].reshape(N_DEV * k, n) for i in range(l)]


def _parts_cols(grads):
    k, n8 = grads[0].shape
    g = jnp.stack(grads).reshape(len(grads), k, N_DEV, n8 // N_DEV)
    return jnp.transpose(g, (2, 0, 1, 3)).reshape(N_DEV, len(grads) * k, n8 // N_DEV)


def _parts_rows(grads):
    k8, n = grads[0].shape
    g = jnp.stack(grads).reshape(len(grads), N_DEV, k8 // N_DEV, n)
    return jnp.transpose(g, (1, 0, 2, 3)).reshape(N_DEV, len(grads) * (k8 // N_DEV), n)


def _pad_rows(a, rows):
    return jnp.concatenate([a, jnp.zeros((rows - a.shape[0], a.shape[1]), a.dtype)], axis=0)


def kernel(x, norm_gains, sb_w_qkv, sb_q_gain, sb_k_gain, sb_w_o, hg_w_in, hg_lb_logits, hg_norm_gain, hg_w_o, mlp_w1, mlp_w2, loss_target, m_norm_gains, m_sb_w_qkv, m_sb_q_gain, m_sb_k_gain, m_sb_w_o, m_hg_w_in, m_hg_lb_logits, m_hg_norm_gain, m_hg_w_o, m_mlp_w1, m_mlp_w2, v_norm_gains, v_sb_w_qkv, v_sb_q_gain, v_sb_k_gain, v_sb_w_o, v_hg_w_in, v_hg_lb_logits, v_hg_norm_gain, v_hg_w_o, v_mlp_w1, v_mlp_w2):
    depth, _, d_loc = norm_gains.shape
    n_sb, n_hg = sb_w_qkv.shape[0], hg_w_in.shape[0]
    xs = x[0]
    target = loss_target[0]
    s, d = xs.shape
    me = 4 * lax.axis_index("x") + 2 * lax.axis_index("y") + lax.axis_index("c")

    w_qkv = _gather_cols(sb_w_qkv, "gather_w_qkv")
    w_o = _gather_rows(sb_w_o, "gather_sb_w_o")
    w_in = _gather_cols(hg_w_in, "gather_hg_w_in")
    w_ho = _gather_rows(hg_w_o, "gather_hg_w_o")
    w_1 = _gather_cols(mlp_w1, "gather_mlp_w1")
    w_2 = _gather_rows(mlp_w2, "gather_mlp_w2")
    n_gain_rows = 2 * depth
    small_rows = -(-(n_gain_rows + n_hg) // 8) * 8
    small = _pad_rows(jnp.concatenate([norm_gains.reshape(n_gain_rows, d_loc), hg_lb_logits], axis=0), small_rows)
    small = _all_gather(small, "gather_small")
    gains_full = jnp.transpose(small[:, :n_gain_rows], (1, 0, 2)).reshape(depth, 2, 1, d)
    logits_full = jnp.transpose(small[:, n_gain_rows:n_gain_rows + n_hg], (1, 0, 2)).reshape(n_hg, d)
    lower = _lower_bounds(logits_full, "lower_bounds")

    saved = []
    cur = xs
    for layer in range(depth):
        j = layer // 2
        h = _rmsnorm(cur, gains_full[layer, 0], f"norm_mix_{layer}")
        if layer % 2 == 0:
            qkv = _matmul("nn", h, w_qkv[j], f"qkv_{layer}", [F32])
            qk_gains = _qk_gain_table(sb_q_gain[j], sb_k_gain[j])
            qkvn = _qknorm(qkv, qk_gains, f"qknorm_{layer}")
            o, runs = _sb_attention_fwd(qkvn, f"sb_fwd_{layer}")
            mix = (qkv, qk_gains, qkvn, o, runs)
            x1 = _matmul("nn", o, w_o[j], f"sb_out_{layer}", [F32], _ep_add, [cur])
        else:
            proj = _matmul("nn", h, w_in[j], f"hg_in_{layer}", [F32], tn=1024)
            og, o, states = _hgrn2_fwd(proj, lower[j:j + 1], hg_norm_gain[j:j + 1], f"hg_fwd_{layer}")
            mix = (proj, og, o, states)
            x1 = _matmul("nn", og, w_ho[j], f"hg_out_{layer}", [F32], _ep_add, [cur])
        h2 = _rmsnorm(x1, gains_full[layer, 1], f"norm_mlp_{layer}")
        a, u = _matmul("nn", h2, w_1[layer], f"mlp_up_{layer}", [F32, BF16], _ep_relu2)
        x2 = _matmul("nn", u, w_2[layer], f"mlp_down_{layer}", [F32], _ep_add, [x1])
        saved.append((cur, h, mix, x1, h2, a, u))
        cur = x2

    loss_tile, dx, dxb = _loss_head(cur, target, "loss_head")
    loss = lax.psum(loss_tile[0, 0], AXES)

    d_gains = [[None, None] for _ in range(depth)]
    d_w1, d_w2 = [None] * depth, [None] * depth
    d_wqkv, d_wo, d_qk = [None] * n_sb, [None] * n_sb, [None] * n_sb
    d_win, d_who, d_lb, d_hgain = [None] * n_hg, [None] * n_hg, [None] * n_hg, [None] * n_hg
    for layer in reversed(range(depth)):
        j = layer // 2
        x0, h, mix, x1, h2, a, u = saved[layer]
        d_w2[layer] = _matmul("tn", u, dxb, f"d_mlp_w2_{layer}", [BF16])
        da = _matmul("nt", dxb, w_2[layer], f"d_mlp_act_{layer}", [BF16], _ep_relu2_bwd, [a])
        d_w1[layer] = _matmul("tn", h2, da, f"d_mlp_w1_{layer}", [BF16])
        dh2 = _matmul("nt", da, w_1[layer], f"d_mlp_in_{layer}", [F32])
        dx, dxb, d_gains[layer][1] = _rmsnorm_bwd(x1, gains_full[layer, 1], dh2, dx, f"d_norm_mlp_{layer}")
        if layer % 2 == 0:
            qkv, qk_gains, qkvn, o, runs = mix
            d_wo[j] = _matmul("tn", o, dxb, f"d_sb_w_o_{layer}", [BF16])
            do = _matmul("nt", dxb, w_o[j], f"d_sb_o_{layer}", [F32])
            dq, dk, dv = _sb_attention_bwd(qkvn, do, runs, f"sb_bwd_{layer}")
            dqkv, d_qk[j] = _qknorm_bwd(qkv, jnp.concatenate([dq, dk, dv], axis=1), qk_gains, f"d_qknorm_{layer}")
            d_wqkv[j] = _matmul("tn", h, dqkv, f"d_sb_w_qkv_{layer}", [BF16])
            dh = _matmul("nt", dqkv, w_qkv[j], f"d_sb_in_{layer}", [F32])
        else:
            proj, og, o, states = mix
            d_who[j] = _matmul("tn", og, dxb, f"d_hg_w_o_{layer}", [BF16])
            dog = _matmul("nt", dxb, w_ho[j], f"d_hg_o_{layer}", [F32])
            dq, df, di, dg, d_lb[j], d_hgain[j] = _hgrn2_bwd(
                proj, lower[j:j + 1], hg_norm_gain[j:j + 1], o, states, dog, f"hg_bwd_{layer}")
            dproj = jnp.concatenate([dq, df, di, dg], axis=1)
            d_win[j] = _matmul("tn", h, dproj, f"d_hg_w_in_{layer}", [BF16])
            dh = _matmul("nt", dproj, w_in[j], f"d_hg_in_{layer}", [F32])
        dx, dxb, d_gains[layer][0] = _rmsnorm_bwd(x0, gains_full[layer, 0], dh, dx, f"d_norm_mix_{layer}")
    grad_x = dx[None]

    def update(parts, w, m, v, name):
        shape = w.shape
        flat = (shape[0] * shape[1], shape[2])
        got = _exchange(parts.astype(BF16), "exchange_" + name)
        return [r.reshape(shape) for r in _adamw(got, w.reshape(flat), m.reshape(flat), v.reshape(flat), "adamw_" + name)]

    big = {
        "sb_w_qkv": update(_parts_cols(d_wqkv), sb_w_qkv, m_sb_w_qkv, v_sb_w_qkv, "sb_w_qkv"),
        "sb_w_o": update(_parts_rows(d_wo), sb_w_o, m_sb_w_o, v_sb_w_o, "sb_w_o"),
        "hg_w_in": update(_parts_cols(d_win), hg_w_in, m_hg_w_in, v_hg_w_in, "hg_w_in"),
        "hg_w_o": update(_parts_rows(d_who), hg_w_o, m_hg_w_o, v_hg_w_o, "hg_w_o"),
        "mlp_w1": update(_parts_cols(d_w1), mlp_w1, m_mlp_w1, v_mlp_w1, "mlp_w1"),
        "mlp_w2": update(_parts_rows(d_w2), mlp_w2, m_mlp_w2, v_mlp_w2, "mlp_w2"),
    }

    d_gain_rows = jnp.concatenate([d_gains[l][t] for l in range(depth) for t in range(2)], axis=0)
    d_lb_rows = jnp.concatenate(d_lb, axis=0)
    def fold(t):
        return t[:, :SB_HEAD_DIM] + t[:, SB_HEAD_DIM:]
    d_qg = jnp.concatenate([fold(d_qk[i][0]) for i in range(n_sb)], axis=0) * SB_SCALE
    d_kg = jnp.concatenate([fold(d_qk[i][1]) for i in range(n_sb)], axis=0)
    d_hg = jnp.concatenate([jnp.sum(d_hgain[i], axis=0) for i in range(n_hg)], axis=0)
    per_row = d // LANES
    packed = jnp.concatenate([
        d_gain_rows.reshape(n_gain_rows * per_row, LANES), d_lb_rows.reshape(n_hg * per_row, LANES),
        jnp.concatenate([d_qg, d_kg], axis=1), d_hg], axis=0)
    n_packed = packed.shape[0]
    packed = _pad_rows(packed, -(-n_packed // 8) * 8)
    everyone = _all_gather(packed, "gather_small_grads")
    o_lb = n_gain_rows * per_row
    o_qk = o_lb + n_hg * per_row
    o_hg = o_qk + n_sb

    def mine_of(rows, count):
        return lax.dynamic_slice_in_dim(rows.reshape(N_DEV, count, per_row, LANES), me, 1, axis=2)[:, :, 0]

    d_logits_full = _lower_bounds_bwd(logits_full, everyone[:, o_lb:o_qk].reshape(N_DEV, n_hg, d), "lower_bounds_bwd")
    d_logits_mine = lax.dynamic_slice_in_dim(d_logits_full.reshape(n_hg, per_row, LANES), me, 1, axis=1)[:, 0]
    zeros7 = jnp.zeros((N_DEV - 1, n_hg, LANES), F32)
    small_parts = jnp.concatenate([
        mine_of(everyone[:, :o_lb], n_gain_rows),
        jnp.concatenate([d_logits_mine[None], zeros7], axis=0),
        everyone[:, o_qk:o_hg], everyone[:, o_hg:o_hg + n_hg]], axis=1)
    rows_small = small_parts.shape[1]
    pad_to = -(-rows_small // 8) * 8
    small_parts = jnp.concatenate([small_parts, jnp.zeros((N_DEV, pad_to - rows_small, LANES), F32)], axis=1)

    def pack_small(ng, qg, kg, lbl, hgn):
        return _pad_rows(jnp.concatenate([
            ng.reshape(n_gain_rows, d_loc), lbl, jnp.concatenate([qg, kg], axis=1), hgn], axis=0), pad_to)

    res = _adamw(small_parts,
                 pack_small(norm_gains, sb_q_gain, sb_k_gain, hg_lb_logits, hg_norm_gain),
                 pack_small(m_norm_gains, m_sb_q_gain, m_sb_k_gain, m_hg_lb_logits, m_hg_norm_gain),
                 pack_small(v_norm_gains, v_sb_q_gain, v_sb_k_gain, v_hg_lb_logits, v_hg_norm_gain), "adamw_small")

    def unpack_small(t):
        o1 = n_gain_rows
        o2 = o1 + n_hg
        o3 = o2 + n_sb
        return {"norm_gains": t[:o1].reshape(depth, 2, d_loc), "hg_lb_logits": t[o1:o2],
                "sb_q_gain": t[o2:o3, :SB_HEAD_DIM], "sb_k_gain": t[o2:o3, SB_HEAD_DIM:],
                "hg_norm_gain": t[o3:o3 + n_hg]}

    small_out = [unpack_small(t) for t in res]
    order = ["norm_gains", "sb_w_qkv", "sb_q_gain", "sb_k_gain", "sb_w_o", "hg_w_in", "hg_lb_logits",
             "hg_norm_gain", "hg_w_o", "mlp_w1", "mlp_w2"]
    outs = [loss, grad_x]
    for kind in range(4):
        outs += [big[n][kind] if n in big else small_out[kind][n] for n in order]
    return tuple(outs)
```

```python
import functools
import math

import numpy as np
import jax
import jax.numpy as jnp
from jax import lax
from jax.experimental import pallas as pl
from jax.experimental.pallas import tpu as pltpu

F32 = jnp.float32
BF16 = jnp.bfloat16
EPS = 1e-6
SB_HEAD_DIM = 64
HG_DIM = 128
LANES = 128
N_DEV = 8
AXES = ("x", "y", "c")
VMEM_LIMIT_BYTES = 48 * 1024 * 1024
MATMUL_VMEM_BUDGET = 40 * 1024 * 1024
SB_SCALE = 1.0 / math.sqrt(SB_HEAD_DIM)
ATT_BLOCK = 256
HG_CHUNK = 64
HG_STEP_CHUNKS = 4
HG_STEP_HEADS = 2
ADAM_LR, ADAM_B1, ADAM_B2, ADAM_EPS, ADAM_WD, ADAM_STEP = 0.001, 0.9, 0.999, 1e-08, 0.01, 10


def _call(body, **kw):
    return pl.pallas_call(body, **kw)


def _sds(shape, dtype):
    return jax.ShapeDtypeStruct(tuple(shape), dtype)


def _cparams(*sem):
    return pltpu.CompilerParams(dimension_semantics=sem or None, vmem_limit_bytes=VMEM_LIMIT_BYTES)


def _split_bf16(x):
    hi = x.astype(BF16)
    lo = (x - hi.astype(F32)).astype(BF16)
    return hi, lo


def _dot(a, b, dims):
    return lax.dot_general(a, b, (dims, ((), ())), preferred_element_type=F32)


NN = ((1,), (0,))
NT = ((1,), (1,))
TN = ((0,), (0,))


def _dot2(x, m, dims):
    hi, lo = _split_bf16(x)
    return _dot(hi, m, dims) + _dot(lo, m, dims)


def _mdot2(m, x, dims):
    hi, lo = _split_bf16(x)
    return _dot(m, hi, dims) + _dot(m, lo, dims)


def _matmul_tiles(m, n, k, a_dtype, b_dtype, io_dtypes):
    tm, tn = min(m, 1024), min(n, 1024)

    def need(tm, tn):
        blocks = tm * k * jnp.dtype(a_dtype).itemsize + tn * k * jnp.dtype(b_dtype).itemsize
        blocks += sum(tm * tn * jnp.dtype(dt).itemsize for dt in io_dtypes)
        return 2 * blocks + 2 * tm * tn * 4

    while need(tm, tn) > MATMUL_VMEM_BUDGET:
        if tm >= tn and tm > 256:
            tm //= 2
        else:
            tn //= 2
    return tm, tn


def _matmul(kind, a, b, name, out_dtypes, epilogue=None, extras=()):
    if kind == "nn":
        (m, k), n = a.shape, b.shape[1]
    elif kind == "nt":
        (m, k), n = a.shape, b.shape[0]
    else:
        (k, m), n = a.shape, b.shape[1]
    tm, tn = _matmul_tiles(m, n, k, a.dtype, b.dtype, list(out_dtypes) + [e.dtype for e in extras])
    assert m % tm == 0 and n % tn == 0, (name, a.shape, b.shape)
    a_spec = pl.BlockSpec((k, tm), lambda i, j: (0, i)) if kind == "tn" else pl.BlockSpec((tm, k), lambda i, j: (i, 0))
    b_spec = pl.BlockSpec((tn, k), lambda i, j: (j, 0)) if kind == "nt" else pl.BlockSpec((k, tn), lambda i, j: (0, j))
    o_spec = pl.BlockSpec((tm, tn), lambda i, j: (i, j))
    dims = {"nn": NN, "nt": NT, "tn": TN}[kind]
    n_ex = len(extras)

    def body(*refs):
        a_ref, b_ref = refs[:2]
        ex, outs = refs[2:2 + n_ex], refs[2 + n_ex:]
        acc = _dot(a_ref[...].astype(BF16), b_ref[...].astype(BF16), dims)
        res = epilogue(acc, *[e[...] for e in ex]) if epilogue is not None else (acc,)
        for o_ref, r in zip(outs, res):
            o_ref[...] = r.astype(o_ref.dtype)

    out = _call(
        body, name=name, grid=(m // tm, n // tn),
        in_specs=[a_spec, b_spec] + [o_spec] * n_ex,
        out_specs=[o_spec] * len(out_dtypes),
        out_shape=[_sds((m, n), dt) for dt in out_dtypes],
        compiler_params=_cparams("parallel", "parallel"),
    )(a, b, *extras)
    return out if len(out_dtypes) > 1 else out[0]


def _ep_add(acc, res):
    return (acc + res,)


def _ep_relu2(acc):
    r = jnp.maximum(acc, 0.0)
    return acc, r * r


def _ep_relu2_bwd(acc, a):
    return (acc * (2.0 * jnp.maximum(a, 0.0)),)


def _rmsnorm(x, g, name):
    s, d = x.shape
    tm = min(s, 512)

    def body(x_ref, g_ref, h_ref):
        xv = x_ref[...]
        r = lax.rsqrt(jnp.mean(xv * xv, axis=-1, keepdims=True) + EPS)
        h_ref[...] = (xv * r * g_ref[...]).astype(BF16)

    return _call(
        body, name=name, grid=(s // tm,),
        in_specs=[pl.BlockSpec((tm, d), lambda i: (i, 0)), pl.BlockSpec((1, d), lambda i: (0, 0))],
        out_specs=pl.BlockSpec((tm, d), lambda i: (i, 0)),
        out_shape=_sds((s, d), BF16), compiler_params=_cparams("parallel"),
    )(x, g)


def _rmsnorm_bwd(x, g, dh, dres, name):
    s, d = x.shape
    tm = min(s, 512)

    def body(x_ref, g_ref, dh_ref, dres_ref, dx_ref, dxb_ref, dg_ref):
        xv, dhv = x_ref[...], dh_ref[...]
        r = lax.rsqrt(jnp.mean(xv * xv, axis=-1, keepdims=True) + EPS)
        xr = xv * r
        t = dhv * g_ref[...]
        dx = dres_ref[...] + r * (t - xr * jnp.mean(t * xr, axis=-1, keepdims=True))
        dx_ref[...] = dx
        dxb_ref[...] = dx.astype(BF16)

        @pl.when(pl.program_id(0) == 0)
        def _():
            dg_ref[...] = jnp.zeros_like(dg_ref)

        dg_ref[...] += jnp.sum(dhv * xr, axis=0, keepdims=True)

    row = pl.BlockSpec((tm, d), lambda i: (i, 0))
    vec = pl.BlockSpec((1, d), lambda i: (0, 0))
    return _call(
        body, name=name, grid=(s // tm,), in_specs=[row, vec, row, row], out_specs=[row, row, vec],
        out_shape=[_sds((s, d), F32), _sds((s, d), BF16), _sds((1, d), F32)],
        compiler_params=_cparams("arbitrary"),
    )(x, g, dh, dres)


def _loss_head(y, target, name):
    s, d = y.shape
    tm = min(s, 512)

    def body(y_ref, t_ref, loss_ref, dy_ref, dyb_ref):
        err = y_ref[...] - t_ref[...]
        dy = err * (1.0 / d)
        dy_ref[...] = dy
        dyb_ref[...] = dy.astype(BF16)

        @pl.when(pl.program_id(0) == 0)
        def _():
            loss_ref[...] = jnp.zeros_like(loss_ref)

        part = 0.5 * jnp.sum(jnp.mean(err * err, axis=-1, keepdims=True), axis=0, keepdims=True)
        loss_ref[...] += part

    row = pl.BlockSpec((tm, d), lambda i: (i, 0))
    return _call(
        body, name=name, grid=(s // tm,), in_specs=[row, row],
        out_specs=[pl.BlockSpec((8, LANES), lambda i: (0, 0)), row, row],
        out_shape=[_sds((8, LANES), F32), _sds((s, d), F32), _sds((s, d), BF16)],
        compiler_params=_cparams("arbitrary"),
    )(y, target)


def _head_lane_mask():
    lane = lax.broadcasted_iota(jnp.int32, (1, LANES), 1)
    return lane < SB_HEAD_DIM


def _pair_rms(xv, first):
    x2 = xv * xv
    s0 = jnp.sum(jnp.where(first, x2, 0.0), axis=-1, keepdims=True)
    s1 = jnp.sum(jnp.where(first, 0.0, x2), axis=-1, keepdims=True)
    inv = 1.0 / SB_HEAD_DIM
    return jnp.where(first, lax.rsqrt(s0 * inv + EPS), lax.rsqrt(s1 * inv + EPS))


def _pair_mean(t, first):
    s0 = jnp.sum(jnp.where(first, t, 0.0), axis=-1, keepdims=True)
    s1 = jnp.sum(jnp.where(first, 0.0, t), axis=-1, keepdims=True)
    return jnp.where(first, s0, s1) * (1.0 / SB_HEAD_DIM)


def _qk_gain_table(q_gain, k_gain, d):
    reps = d // SB_HEAD_DIM
    return jnp.stack([jnp.tile(q_gain, reps) * SB_SCALE, jnp.tile(k_gain, reps), jnp.ones((d,), F32)])[:, None, :]


QKNORM_ROWS = 256


def _qknorm(qkv, gains, name):
    s, d3 = qkv.shape
    d = d3 // 3
    tm = min(s, QKNORM_ROWS)

    def body(x_ref, g_ref, o_ref):
        first = _head_lane_mask()
        is_v = pl.program_id(0) == 2
        for c in range(d // LANES):
            cols = slice(c * LANES, (c + 1) * LANES)
            xv = x_ref[:, cols]
            normed = xv * _pair_rms(xv, first) * g_ref[0, :, cols]
            o_ref[:, cols] = jnp.where(is_v, xv, normed).astype(BF16)

    tile = pl.BlockSpec((tm, d), lambda c, i: (i, c))
    return _call(
        body, name=name, grid=(3, s // tm),
        in_specs=[tile, pl.BlockSpec((1, 1, d), lambda c, i: (c, 0, 0))], out_specs=tile,
        out_shape=_sds((s, d3), BF16), compiler_params=_cparams("parallel", "parallel"),
    )(qkv, gains)


def _qknorm_bwd(qkv, dq, dk, dv, gains, name):
    s, d3 = qkv.shape
    d = d3 // 3
    tm = min(s, QKNORM_ROWS)

    def body(x_ref, dq_ref, dk_ref, dv_ref, g_ref, dx_ref, dg_ref):
        c, i = pl.program_id(0), pl.program_id(1)

        @pl.when(i == 0)
        def _():
            dg_ref[...] = jnp.zeros_like(dg_ref)

        first = _head_lane_mask()
        for col in range(d // LANES):
            cols = slice(col * LANES, (col + 1) * LANES)
            xv = x_ref[:, cols]
            dy = jnp.where(c == 0, dq_ref[:, cols], jnp.where(c == 1, dk_ref[:, cols], dv_ref[:, cols]))
            r = _pair_rms(xv, first)
            xr = xv * r
            t = dy * g_ref[0, :, cols]
            dx = r * (t - xr * _pair_mean(t * xr, first))
            dx_ref[:, cols] = jnp.where(c == 2, dy, dx).astype(BF16)
            dg_ref[0, :, cols] += jnp.sum(dy * xr, axis=0, keepdims=True)

    tile = pl.BlockSpec((tm, d), lambda c, i: (i, c))
    vec = pl.BlockSpec((1, 1, d), lambda c, i: (c, 0, 0))

    def part(kind):
        return pl.BlockSpec((tm, d), lambda c, i: (jnp.where(c == kind, i, 0), 0))

    return _call(
        body, name=name, grid=(3, s // tm), in_specs=[tile, part(0), part(1), part(2), vec], out_specs=[tile, vec],
        out_shape=[_sds((s, d3), BF16), _sds((3, 1, d), F32)],
        compiler_params=_cparams("arbitrary", "arbitrary"),
    )(qkv, dq, dk, dv, gains)


def _softplus_parts(z):
    sp = jnp.maximum(z, 0.0) + jnp.log(1.0 + jnp.exp(-jnp.abs(z)))
    return sp, z - sp


def _diag_causal(tb):
    return lax.broadcasted_iota(jnp.int32, (tb, tb), 1) < lax.broadcasted_iota(jnp.int32, (tb, tb), 0)


def _later_keys(tb):
    r = lax.broadcasted_iota(jnp.int32, (tb, tb), 0)
    c = lax.broadcasted_iota(jnp.int32, (tb, tb), 1)
    return jnp.where(r > c, 1.0, 0.0).astype(BF16)


def _sb_scores(qa, kj, causal):
    sp, logsig = _softplus_parts(_dot(qa, kj, NT))
    return (-sp if causal is None else jnp.where(causal, -sp, 0.0)), logsig


def _sb_weights(stay, logsig, run, later, causal):
    w = jnp.exp(logsig + _dot2(stay, later, NN) + run)
    return w if causal is None else jnp.where(causal, w, 0.0)


def _sb_attention_fwd(qkvn, name):
    s, d3 = qkvn.shape
    d = d3 // 3
    pairs, tb = d // LANES, min(ATT_BLOCK, s)
    nb = s // tb
    assert nb <= LANES

    def body(q_ref, k_ref, v_ref, o_ref, runs_ref):
        i = pl.program_id(1)
        first = _head_lane_mask()
        lane = lax.broadcasted_iota(jnp.int32, (1, LANES), 1)
        later = _later_keys(tb)
        q2 = q_ref[...]
        qs = (jnp.where(first, q2, jnp.zeros_like(q2)), jnp.where(first, jnp.zeros_like(q2), q2))

        def tiles(js, carry, causal):
            kvs = []
            for j in js:
                rows = pl.ds(pl.multiple_of(j * tb, tb), tb)
                kvs.append((k_ref[rows, :], v_ref[rows, :]))
            scores = [[_sb_scores(qs[h], kj, causal) for h in range(2)] for kj, _ in kvs]
            run = [carry[0][0], carry[1][0]]
            acc = [carry[0][1], carry[1][1]]
            runs = [carry[0][2], carry[1][2]]
            weights = []
            for t, j in enumerate(js):
                weights.append([_sb_weights(*scores[t][h], run[h], later, causal) for h in range(2)])
                for h in range(2):
                    runs[h] = jnp.where(lane == j, run[h], runs[h])
                    run[h] = run[h] + jnp.sum(scores[t][h][0], axis=-1, keepdims=True)
            for t, (_, vj) in enumerate(kvs):
                for h in range(2):
                    acc[h] = acc[h] + _dot(weights[t][h].astype(BF16), vj, NN)
            return tuple((run[h], acc[h], runs[h]) for h in range(2))

        zero = (jnp.zeros((tb, 1), F32), jnp.zeros((tb, LANES), F32), jnp.zeros((tb, LANES), F32))
        carry = tiles([i], (zero, zero), _diag_causal(tb))
        carry = lax.fori_loop(0, i // 2, lambda jj, c: tiles([i - 1 - 2 * jj, i - 2 - 2 * jj], c, None), carry)
        carry = lax.cond(i % 2 == 1, lambda c: tiles([0], c, None), lambda c: c, carry)
        o_ref[...] = jnp.where(first, carry[0][1], carry[1][1])
        runs_ref[0] = carry[0][2]
        runs_ref[1] = carry[1][2]

    return _call(
        body, name=name, grid=(pairs, nb),
        in_specs=[pl.BlockSpec((tb, LANES), lambda p, i: (i, p)),
                  pl.BlockSpec((s, LANES), lambda p, i: (0, pairs + p)),
                  pl.BlockSpec((s, LANES), lambda p, i: (0, 2 * pairs + p))],
        out_specs=[pl.BlockSpec((tb, LANES), lambda p, i: (i, p)),
                   pl.BlockSpec((2, tb, LANES), lambda p, i: (p * nb + i, 0, 0))],
        out_shape=[_sds((s, d), F32), _sds((pairs * nb * 2, tb, LANES), F32)],
        compiler_params=_cparams("parallel", "parallel"),
    )(qkvn, qkvn, qkvn)


def _sb_attention_bwd(qkvn, do, runs, name):
    s, d3 = qkvn.shape
    d = d3 // 3
    pairs, tb = d // LANES, min(ATT_BLOCK, s)
    nb = s // tb

    def body(q_ref, k_ref, v_ref, do_ref, runs_ref, dq_ref, dk_ref, dv_ref):
        i = pl.program_id(1)

        @pl.when(i == 0)
        def _():
            dk_ref[...] = jnp.zeros_like(dk_ref)
            dv_ref[...] = jnp.zeros_like(dv_ref)

        first = _head_lane_mask()
        lane = lax.broadcasted_iota(jnp.int32, (1, LANES), 1)
        later = _later_keys(tb)
        q2, do2 = q_ref[...], do_ref[...].astype(BF16)
        zq = jnp.zeros_like(q2)
        qs = (jnp.where(first, q2, zq), jnp.where(first, zq, q2))
        dos = (jnp.where(first, do2, zq), jnp.where(first, zq, do2))

        def tiles(js, carry, causal):
            rows, kv = [], []
            for j in js:
                r = pl.ds(pl.multiple_of(j * tb, tb), tb)
                kj, vj = k_ref[r, :], v_ref[r, :]
                zk = jnp.zeros_like(kj)
                rows.append(r)
                kv.append([(jnp.where(first, kj, zk), jnp.where(first, vj, zk)),
                           (jnp.where(first, zk, kj), jnp.where(first, zk, vj))])
            pairs_th = [(t, h) for t in range(len(js)) for h in range(2)]
            scores = {(t, h): _sb_scores(qs[h], kv[t][h][0], causal) for t, h in pairs_th}
            w, g = {}, {}
            for t, h in pairs_th:
                run = jnp.sum(jnp.where(lane == js[t], runs_ref[h], 0.0), axis=-1, keepdims=True)
                w[t, h] = _sb_weights(*scores[t, h], run, later, causal)
                g[t, h] = w[t, h] * _dot(dos[h], kv[t][h][1], NT)
            gsum = [carry[0], carry[1]]
            dz = {}
            for t, h in pairs_th:
                before = _dot2(g[t, h], later, NT) + gsum[h]
                gsum[h] = gsum[h] + jnp.sum(g[t, h], axis=-1, keepdims=True)
                sig = jnp.exp(scores[t, h][1])
                d = g[t, h] * (1.0 - sig) - before * sig
                dz[t, h] = (d if causal is None else jnp.where(causal, d, 0.0)).astype(BF16)
            dq = carry[2]
            for t, h in pairs_th:
                dq = dq + _dot(dz[t, h], kv[t][h][0], NN)
            for t in range(len(js)):
                dk_ref[rows[t], :] += _dot(dz[t, 0], qs[0], TN) + _dot(dz[t, 1], qs[1], TN)
                dv_ref[rows[t], :] += _dot(w[t, 0].astype(BF16), dos[0], TN) + _dot(w[t, 1].astype(BF16), dos[1], TN)
            return gsum[0], gsum[1], dq

        carry = (jnp.zeros((tb, 1), F32), jnp.zeros((tb, 1), F32), jnp.zeros((tb, LANES), F32))
        carry = lax.fori_loop(0, i // 2, lambda jj, c: tiles([2 * jj, 2 * jj + 1], c, None), carry)
        carry = lax.cond(i % 2 == 1, lambda c: tiles([i - 1], c, None), lambda c: c, carry)
        carry = tiles([i], carry, _diag_causal(tb))
        dq_ref[...] = carry[2]

    q_spec = pl.BlockSpec((tb, LANES), lambda p, i: (i, p))
    return _call(
        body, name=name, grid=(pairs, nb),
        in_specs=[q_spec,
                  pl.BlockSpec((s, LANES), lambda p, i: (0, pairs + p)),
                  pl.BlockSpec((s, LANES), lambda p, i: (0, 2 * pairs + p)),
                  q_spec,
                  pl.BlockSpec((2, tb, LANES), lambda p, i: (p * nb + i, 0, 0))],
        out_specs=[q_spec, pl.BlockSpec((s, LANES), lambda p, i: (0, p)), pl.BlockSpec((s, LANES), lambda p, i: (0, p))],
        out_shape=[_sds((s, d), F32)] * 3,
        compiler_params=_cparams("parallel", "arbitrary"),
    )(qkvn, qkvn, qkvn, do, runs)


def _hg_tables(c):
    t = np.arange(c)[:, None]
    j = np.arange(c)[None, :]
    sums = [j <= t, j > t]
    masks = []
    m = c // 2
    while m >= 1:
        pos, base = t % (2 * m), t - t % (2 * m)
        sums.append((pos >= m) & (j >= base + m) & (j <= t))
        sums.append((pos < m) & (j > t) & (j <= base + m - 1))
        masks.append((t // (2 * m) == j // (2 * m)) & (t % (2 * m) >= m) & (j % (2 * m) < m))
        m //= 2
    return (jnp.asarray(np.concatenate(sums, 0), BF16), jnp.asarray(np.stack(masks), F32), len(masks))


def _hg_gates(qr, fr, lb):
    sq = jax.nn.sigmoid(qr)
    sg = jax.nn.sigmoid(fr)
    forget = lb + (1.0 - lb) * sg
    return sq, qr * sq, sg, forget, jnp.log(forget), (1.0 - lb) * (1.0 - sg)


def _hg_scores(q, k, x, masks_ref, c, levels):
    eye = (lax.broadcasted_iota(jnp.int32, (c, c), 0) == lax.broadcasted_iota(jnp.int32, (c, c), 1)).astype(F32)
    scores = eye * jnp.sum(q * k, axis=-1, keepdims=True)
    ops = []
    for l in range(levels):
        qm = (q * x[(2 + 2 * l) * c:(3 + 2 * l) * c]).astype(BF16)
        km = (k * x[(3 + 2 * l) * c:(4 + 2 * l) * c]).astype(BF16)
        scores = scores + masks_ref[l] * _dot(qm, km, NT)
        ops.append((qm, km))
    return scores, eye, ops


def _hg_layout(s, d):
    heads, c = d // HG_DIM, min(HG_CHUNK, s)
    nsub = min(HG_STEP_CHUNKS, s // c)
    hp = HG_STEP_HEADS if heads % HG_STEP_HEADS == 0 else 1
    return heads, c, nsub, hp


def _hg_specs(s, d, reverse):
    heads, c, nsub, hp = _hg_layout(s, d)
    rows, width, groups, n_steps = c * nsub, hp * HG_DIM, heads // hp, s // (c * nsub)

    def step(si):
        return n_steps - 1 - si if reverse else si
    proj = [pl.BlockSpec((rows, width), functools.partial(lambda g, si, part: (step(si), part * groups + g), part=p))
            for p in range(4)]
    tile = pl.BlockSpec((rows, width), lambda g, si: (step(si), g))
    lb = pl.BlockSpec((1, width), lambda g, si: (0, g))
    gain = pl.BlockSpec((1, HG_DIM), lambda g, si: (0, 0))
    state = pl.BlockSpec((hp, nsub, HG_DIM, HG_DIM), lambda g, si: (g, step(si), 0, 0))
    return proj, tile, lb, gain, state, (groups, n_steps)


def _hgrn2_fwd(proj, lb, gain, name):
    s, d4 = proj.shape
    d = d4 // 4
    heads, c, nsub, hp = _hg_layout(s, d)
    sums, masks, levels = _hg_tables(c)

    def body(qr_ref, fr_ref, ir_ref, gr_ref, lb_ref, gain_ref, sums_ref, masks_ref, og_ref, o_ref, states_ref, st_ref):
        @pl.when(pl.program_id(1) == 0)
        def _():
            st_ref[...] = jnp.zeros_like(st_ref)

        lbv, gainv = lb_ref[...], gain_ref[...]
        units = [(ci, hh) for ci in range(nsub) for hh in range(hp)]

        def lanes(hh):
            return slice(hh * HG_DIM, (hh + 1) * HG_DIM)

        pre = []
        for ci in range(nsub):
            rows = slice(ci * c, (ci + 1) * c)
            _, q, _, _, lf, k = _hg_gates(qr_ref[rows, :], fr_ref[rows, :], lbv)
            pre.append((q, k, jnp.exp(_mdot2(sums_ref[...], lf, NN))))
        scores, qh, vb, update = {}, {}, {}, {}
        for ci, hh in units:
            q, k, x = (a[:, lanes(hh)] for a in pre[ci])
            scores[ci, hh] = _hg_scores(q, k, x, masks_ref, c, levels)[0].astype(BF16)
            qh[ci, hh] = (q * x[0:c]).astype(BF16)
            vb[ci, hh] = ir_ref[ci * c:(ci + 1) * c, lanes(hh)].astype(BF16)
            update[ci, hh] = _dot(vb[ci, hh], (k * x[c:2 * c]).astype(BF16), TN)
        intra = {u: _dot(scores[u], vb[u], NN) for u in units}
        for hh in range(hp):
            st = st_ref[hh]
            for ci in range(nsub):
                rows = slice(ci * c, (ci + 1) * c)
                states_ref[hh, ci] = st
                o = _dot(qh[ci, hh], st.astype(BF16), NT) + intra[ci, hh]
                st = st * pre[ci][2][c - 1:c, lanes(hh)] + update[ci, hh]
                o_ref[rows, lanes(hh)] = o
                r = lax.rsqrt(jnp.mean(o * o, axis=-1, keepdims=True) + EPS)
                og_ref[rows, lanes(hh)] = (o * r * gainv * jax.nn.sigmoid(gr_ref[rows, lanes(hh)])).astype(BF16)
            st_ref[hh] = st

    pspecs, tile, lbs, gs, state, grid = _hg_specs(s, d, False)
    const = [pl.BlockSpec(sums.shape, lambda g, si: (0, 0)), pl.BlockSpec(masks.shape, lambda g, si: (0, 0, 0))]
    return _call(
        body, name=name, grid=grid, in_specs=pspecs + [lbs, gs] + const,
        out_specs=[tile, tile, state],
        out_shape=[_sds((s, d), BF16), _sds((s, d), F32), _sds((heads, s // c, HG_DIM, HG_DIM), F32)],
        scratch_shapes=[pltpu.VMEM((hp, HG_DIM, HG_DIM), F32)],
        compiler_params=_cparams("parallel", "arbitrary"),
    )(proj, proj, proj, proj, lb, gain, sums, masks)


def _hgrn2_bwd(proj, lb, gain, o, states, dog, name):
    s, d4 = proj.shape
    d = d4 // 4
    heads, c, nsub, hp = _hg_layout(s, d)
    sums, masks, levels = _hg_tables(c)

    def body(qr_ref, fr_ref, ir_ref, gr_ref, lb_ref, gain_ref, sums_ref, masks_ref, o_ref, states_ref, dog_ref,
             dq_ref, df_ref, di_ref, dg_ref, dlb_ref, dgain_ref, dst_ref):
        @pl.when(pl.program_id(1) == 0)
        def _():
            dst_ref[...] = jnp.zeros_like(dst_ref)
            dlb_ref[...] = jnp.zeros_like(dlb_ref)
            dgain_ref[...] = jnp.zeros_like(dgain_ref)

        lbv, gainv = lb_ref[...], gain_ref[...]
        units = [(ci, hh) for ci in range(nsub) for hh in range(hp)]

        def lanes(hh):
            return slice(hh * HG_DIM, (hh + 1) * HG_DIM)

        def rows_of(ci):
            return slice(ci * c, (ci + 1) * c)

        pre = []
        for ci in range(nsub):
            qr = qr_ref[rows_of(ci), :]
            sq, q, sg, forget, lf, k = _hg_gates(qr, fr_ref[rows_of(ci), :], lbv)
            pre.append(dict(qr=qr, sq=sq, q=q, sg=sg, forget=forget, k=k, x=jnp.exp(_mdot2(sums_ref[...], lf, NN))))

        dob, vb, sc, qh_f, kh_f, feed = {}, {}, {}, {}, {}, {}
        dgain = [jnp.zeros((1, HG_DIM), F32) for _ in range(hp)]
        for ci, hh in units:
            rows, ln = rows_of(ci), lanes(hh)
            ov, gate = o_ref[rows, ln], jax.nn.sigmoid(gr_ref[rows, ln])
            r = lax.rsqrt(jnp.mean(ov * ov, axis=-1, keepdims=True) + EPS)
            orr = ov * r
            dogv = dog_ref[rows, ln]
            dg_ref[rows, ln] = (dogv * orr * gainv * gate * (1.0 - gate)).astype(BF16)
            don = dogv * gate
            dgain[hh] = dgain[hh] + jnp.sum(don * orr, axis=0, keepdims=True)
            t = don * gainv
            dob[ci, hh] = (r * (t - orr * jnp.mean(t * orr, axis=-1, keepdims=True))).astype(BF16)
            q, k, x = (pre[ci][n][:, ln] for n in ("q", "k", "x"))
            vb[ci, hh] = ir_ref[rows, ln].astype(BF16)
            sc[ci, hh] = _hg_scores(q, k, x, masks_ref, c, levels)
            qh_f[ci, hh], kh_f[ci, hh] = q * x[0:c], k * x[c:2 * c]
            feed[ci, hh] = _dot(dob[ci, hh], qh_f[ci, hh].astype(BF16), TN)

        dsts = {}
        for hh in range(hp):
            dst = dst_ref[hh]
            for ci in reversed(range(nsub)):
                dsts[ci, hh] = dst
                dst = dst * pre[ci]["x"][c - 1:c, lanes(hh)] + feed[ci, hh]
            dst_ref[hh] = dst

        dlb = [jnp.zeros((1, HG_DIM), F32) for _ in range(hp)]
        for ci, hh in units:
            rows, ln = rows_of(ci), lanes(hh)
            p = {n: v[:, ln] for n, v in pre[ci].items()}
            q, k, x = p["q"], p["k"], p["x"]
            scores, eye, ops = sc[ci, hh]
            st, dst = states_ref[hh, ci], dsts[ci, hh]
            dstb = dst.astype(BF16)
            dscores = _dot(dob[ci, hh], vb[ci, hh], NT)
            di_ref[rows, ln] = (_dot(scores.astype(BF16), dob[ci, hh], TN)
                                + _dot(kh_f[ci, hh].astype(BF16), dstb, NT)).astype(BF16)
            dqh = _dot(dob[ci, hh], st.astype(BF16), NN)
            dkh = _dot(vb[ci, hh], dstb, NN)
            decay_grad = x[c - 1:c] * jnp.sum(dst * st, axis=0, keepdims=True)
            ddiag = jnp.sum(eye * dscores, axis=-1, keepdims=True)
            dq = dqh * x[0:c] + ddiag * k
            dk = dkh * x[c:2 * c] + ddiag * q
            dexp = [dqh * qh_f[ci, hh], dkh * kh_f[ci, hh]]
            for l, (qm, km) in enumerate(ops):
                dsm = (masks_ref[l] * dscores).astype(BF16)
                dqm, dkm = _dot(dsm, km, NN), _dot(dsm, qm, TN)
                xq, xk = x[(2 + 2 * l) * c:(3 + 2 * l) * c], x[(3 + 2 * l) * c:(4 + 2 * l) * c]
                dq = dq + dqm * xq
                dk = dk + dkm * xk
                dexp += [dqm * (q * xq), dkm * (k * xk)]
            dlf = _mdot2(sums_ref[...], jnp.concatenate(dexp, axis=0), TN) + decay_grad
            dforget = dlf / p["forget"] - dk
            dlb[hh] = dlb[hh] + jnp.sum(dforget * (1.0 - p["sg"]), axis=0, keepdims=True)
            df_ref[rows, ln] = (dforget * (1.0 - lbv[:, ln]) * p["sg"] * (1.0 - p["sg"])).astype(BF16)
            dq_ref[rows, ln] = (dq * p["sq"] * (1.0 + p["qr"] * (1.0 - p["sq"]))).astype(BF16)
        for hh in range(hp):
            dlb_ref[:, lanes(hh)] += dlb[hh]
            dgain_ref[hh] += dgain[hh]

    pspecs, tile, lbs, gs, state, grid = _hg_specs(s, d, True)
    const = [pl.BlockSpec(sums.shape, lambda g, si: (0, 0)), pl.BlockSpec(masks.shape, lambda g, si: (0, 0, 0))]
    return _call(
        body, name=name, grid=grid, in_specs=pspecs + [lbs, gs] + const + [tile, state, tile],
        out_specs=[tile, tile, tile, tile, lbs, pl.BlockSpec((hp, 1, HG_DIM), lambda g, si: (g, 0, 0))],
        out_shape=[_sds((s, d), BF16)] * 4 + [_sds((1, d), F32), _sds((heads, 1, HG_DIM), F32)],
        scratch_shapes=[pltpu.VMEM((hp, HG_DIM, HG_DIM), F32)],
        compiler_params=_cparams("parallel", "arbitrary"),
    )(proj, proj, proj, proj, lb, gain, sums, masks, o, states, dog)


def _lower_bounds(logits, name):
    n, d = logits.shape

    def body(l_ref, lb_ref):
        lv = l_ref[...]
        e = jnp.exp(lv - jnp.max(lv, axis=0, keepdims=True))
        p = e / jnp.sum(e, axis=0, keepdims=True)
        run = jnp.zeros((1, d), F32)
        for j in range(n):
            if j > 0:
                run = run + p[j:j + 1]
            lb_ref[j:j + 1, :] = run

    return _call(body, name=name, out_shape=_sds((n, d), F32))(logits)


def _lower_bounds_bwd(logits, dlb_parts, name):
    n, d = logits.shape

    def body(l_ref, dlb_ref, dl_ref):
        lv, dv = l_ref[...], dlb_ref[0]
        for dev in range(1, N_DEV):
            dv = dv + dlb_ref[dev]
        e = jnp.exp(lv - jnp.max(lv, axis=0, keepdims=True))
        p = e / jnp.sum(e, axis=0, keepdims=True)
        run = jnp.zeros((1, d), F32)
        dps = [None] * n
        for j in range(n - 1, 0, -1):
            run = run + dv[j:j + 1]
            dps[j] = run
        dps[0] = jnp.zeros((1, d), F32)
        inner = jnp.zeros((1, d), F32)
        for j in range(n):
            inner = inner + p[j:j + 1] * dps[j]
        for j in range(n):
            dl_ref[j:j + 1, :] = p[j:j + 1] * (dps[j] - inner)

    return _call(body, name=name, out_shape=_sds((n, d), F32))(logits, dlb_parts)


_ANY = pl.BlockSpec(memory_space=pl.ANY)
_MESH = pl.DeviceIdType.MESH


def _all_gather(x, name):
    def body(x_ref, out_ref, send_sems, recv_sems, local_sem):
        mx, my, mc = lax.axis_index("x"), lax.axis_index("y"), lax.axis_index("c")
        me, sibling = (mx, my, mc), (mx, my, 1 - mc)
        chips = [(1 - mx, my), (mx, 1 - my), (1 - mx, 1 - my)]

        def slot(px, py, pc):
            return out_ref.at[4 * px + 2 * py + pc]

        def copy(k, block, to, src=None):
            return pltpu.make_async_remote_copy(
                src_ref=slot(*block) if src is None else src, dst_ref=slot(*block),
                send_sem=send_sems.at[k], recv_sem=recv_sems.at[k], device_id=to, device_id_type=_MESH)

        mine = pltpu.make_async_copy(x_ref, slot(*me), local_sem)
        mine.start()
        first = [copy(0, me, sibling, src=x_ref)]
        first += [copy(1 + j, me, (*chip, mc), src=x_ref) for j, chip in enumerate(chips)]
        for cp in first:
            cp.start()
        passed = [copy(4 + j, (*chip, mc), sibling) for j, chip in enumerate(chips)]
        for j, chip in enumerate(chips):
            copy(1 + j, (*chip, mc), me).wait_recv()
            passed[j].start()
        copy(0, sibling, me).wait_recv()
        for j, chip in enumerate(chips):
            copy(4 + j, (*chip, 1 - mc), me).wait_recv()
        for cp in first + passed:
            cp.wait_send()
        mine.wait()

    return _call(
        body, name=name, out_shape=_sds((N_DEV,) + x.shape, x.dtype), in_specs=[_ANY], out_specs=_ANY,
        scratch_shapes=[pltpu.SemaphoreType.DMA((7,)), pltpu.SemaphoreType.DMA((7,)), pltpu.SemaphoreType.DMA],
    )(x)


def _exchange(g8, name):
    def body(g_ref, out_ref, send_sems, recv_sems, local_sem):
        mx, my, mc = lax.axis_index("x"), lax.axis_index("y"), lax.axis_index("c")
        me = 4 * mx + 2 * my + mc
        mine = pltpu.make_async_copy(g_ref.at[me], out_ref.at[me], local_sem)
        mine.start()
        copies = []
        for k in range(1, N_DEV):
            px, py, pc = mx ^ (k >> 2), my ^ ((k >> 1) & 1), mc ^ (k & 1)
            peer = 4 * px + 2 * py + pc
            cp = pltpu.make_async_remote_copy(
                src_ref=g_ref.at[peer], dst_ref=out_ref.at[me], send_sem=send_sems.at[k - 1],
                recv_sem=recv_sems.at[k - 1], device_id=(px, py, pc), device_id_type=_MESH)
            cp.start()
            arrival = pltpu.make_async_remote_copy(
                src_ref=g_ref.at[peer], dst_ref=out_ref.at[peer], send_sem=send_sems.at[k - 1],
                recv_sem=recv_sems.at[k - 1], device_id=(px, py, pc), device_id_type=_MESH)
            copies.append((cp, arrival))
        for _, arrival in copies:
            arrival.wait_recv()
        for cp, _ in copies:
            cp.wait_send()
        mine.wait()

    return _call(
        body, name=name, out_shape=_sds(g8.shape, g8.dtype), in_specs=[_ANY], out_specs=_ANY,
        scratch_shapes=[pltpu.SemaphoreType.DMA((7,)), pltpu.SemaphoreType.DMA((7,)), pltpu.SemaphoreType.DMA],
    )(g8)


def _adamw(parts, w, m, v, name):
    _, r, c = parts.shape
    tr = r if r <= 256 else 256
    assert r % tr == 0, (name, parts.shape)

    def body(p_ref, w_ref, m_ref, v_ref, g_ref, d_ref, nm_ref, nv_ref):
        g = p_ref[0].astype(F32)
        for dev in range(1, N_DEV):
            g = g + p_ref[dev].astype(F32)
        nm = ADAM_B1 * m_ref[...] + (1.0 - ADAM_B1) * g
        nv = ADAM_B2 * v_ref[...] + (1.0 - ADAM_B2) * (g * g)
        m_hat = nm / (1.0 - ADAM_B1 ** ADAM_STEP)
        v_hat = nv / (1.0 - ADAM_B2 ** ADAM_STEP)
        g_ref[...] = g
        nm_ref[...] = nm
        nv_ref[...] = nv
        d_ref[...] = -ADAM_LR * (m_hat / (jnp.sqrt(v_hat) + ADAM_EPS) + ADAM_WD * w_ref[...])

    tile = pl.BlockSpec((tr, c), lambda i: (i, 0))
    return _call(
        body, name=name, grid=(r // tr,),
        in_specs=[pl.BlockSpec((N_DEV, tr, c), lambda i: (0, i, 0)), tile, tile, tile], out_specs=[tile] * 4,
        out_shape=[_sds((r, c), F32)] * 4, compiler_params=_cparams("parallel"),
    )(parts, w, m, v)


def _gather_cols(w, name):
    l, k, n = w.shape
    g = _all_gather(w.astype(BF16).reshape(l * k, n), name).reshape(N_DEV, l, k, n)
    return [jnp.transpose(g[:, i], (1, 0, 2)).reshape(k, N_DEV * n) for i in range(l)]


def _gather_rows(w, name):
    l, k, n = w.shape
    g = _all_gather(w.astype(BF16).reshape(l * k, n), name).reshape(N_DEV, l, k, n)
    return [g[:, i].reshape(N_DEV * k, n) for i in range(l)]


def _parts_cols(grads):
    k, n8 = grads[0].shape
    g = jnp.stack(grads).reshape(len(grads), k, N_DEV, n8 // N_DEV)
    return jnp.transpose(g, (2, 0, 1, 3)).reshape(N_DEV, len(grads) * k, n8 // N_DEV)


def _parts_rows(grads):
    k8, n = grads[0].shape
    g = jnp.stack(grads).reshape(len(grads), N_DEV, k8 // N_DEV, n)
    return jnp.transpose(g, (1, 0, 2, 3)).reshape(N_DEV, len(grads) * (k8 // N_DEV), n)


def _pad_rows(a, rows):
    return jnp.concatenate([a, jnp.zeros((rows - a.shape[0], a.shape[1]), a.dtype)], axis=0)


def kernel(x, norm_gains, sb_w_qkv, sb_q_gain, sb_k_gain, sb_w_o, hg_w_in, hg_lb_logits, hg_norm_gain, hg_w_o, mlp_w1, mlp_w2, loss_target, m_norm_gains, m_sb_w_qkv, m_sb_q_gain, m_sb_k_gain, m_sb_w_o, m_hg_w_in, m_hg_lb_logits, m_hg_norm_gain, m_hg_w_o, m_mlp_w1, m_mlp_w2, v_norm_gains, v_sb_w_qkv, v_sb_q_gain, v_sb_k_gain, v_sb_w_o, v_hg_w_in, v_hg_lb_logits, v_hg_norm_gain, v_hg_w_o, v_mlp_w1, v_mlp_w2):
    depth, _, d_loc = norm_gains.shape
    n_sb, n_hg = sb_w_qkv.shape[0], hg_w_in.shape[0]
    xs = x[0]
    target = loss_target[0]
    s, d = xs.shape
    me = 4 * lax.axis_index("x") + 2 * lax.axis_index("y") + lax.axis_index("c")

    w_qkv = _gather_cols(sb_w_qkv, "gather_w_qkv")
    w_o = _gather_rows(sb_w_o, "gather_sb_w_o")
    w_in = _gather_cols(hg_w_in, "gather_hg_w_in")
    w_ho = _gather_rows(hg_w_o, "gather_hg_w_o")
    w_1 = _gather_cols(mlp_w1, "gather_mlp_w1")
    w_2 = _gather_rows(mlp_w2, "gather_mlp_w2")
    n_gain_rows = 2 * depth
    small_rows = -(-(n_gain_rows + n_hg) // 8) * 8
    small = _pad_rows(jnp.concatenate([norm_gains.reshape(n_gain_rows, d_loc), hg_lb_logits], axis=0), small_rows)
    small = _all_gather(small, "gather_small")
    gains_full = jnp.transpose(small[:, :n_gain_rows], (1, 0, 2)).reshape(depth, 2, 1, d)
    logits_full = jnp.transpose(small[:, n_gain_rows:n_gain_rows + n_hg], (1, 0, 2)).reshape(n_hg, d)
    lower = _lower_bounds(logits_full, "lower_bounds")

    saved = []
    cur = xs
    for layer in range(depth):
        j = layer // 2
        h = _rmsnorm(cur, gains_full[layer, 0], f"norm_mix_{layer}")
        if layer % 2 == 0:
            qkv = _matmul("nn", h, w_qkv[j], f"qkv_{layer}", [F32])
            qk_gains = _qk_gain_table(sb_q_gain[j], sb_k_gain[j], d)
            qkvn = _qknorm(qkv, qk_gains, f"qknorm_{layer}")
            o, runs = _sb_attention_fwd(qkvn, f"sb_fwd_{layer}")
            mix = (qkv, qk_gains, qkvn, o, runs)
            x1 = _matmul("nn", o, w_o[j], f"sb_out_{layer}", [F32], _ep_add, [cur])
        else:
            proj = _matmul("nn", h, w_in[j], f"hg_in_{layer}", [F32])
            og, o, states = _hgrn2_fwd(proj, lower[j:j + 1], hg_norm_gain[j:j + 1], f"hg_fwd_{layer}")
            mix = (proj, og, o, states)
            x1 = _matmul("nn", og, w_ho[j], f"hg_out_{layer}", [F32], _ep_add, [cur])
        h2 = _rmsnorm(x1, gains_full[layer, 1], f"norm_mlp_{layer}")
        a, u = _matmul("nn", h2, w_1[layer], f"mlp_up_{layer}", [F32, BF16], _ep_relu2)
        x2 = _matmul("nn", u, w_2[layer], f"mlp_down_{layer}", [F32], _ep_add, [x1])
        saved.append((cur, h, mix, x1, h2, a, u))
        cur = x2

    loss_tile, dx, dxb = _loss_head(cur, target, "loss_head")
    loss = lax.psum(loss_tile[0, 0], AXES)

    d_gains = [[None, None] for _ in range(depth)]
    d_w1, d_w2 = [None] * depth, [None] * depth
    d_wqkv, d_wo, d_qk = [None] * n_sb, [None] * n_sb, [None] * n_sb
    d_win, d_who, d_lb, d_hgain = [None] * n_hg, [None] * n_hg, [None] * n_hg, [None] * n_hg
    for layer in reversed(range(depth)):
        j = layer // 2
        x0, h, mix, x1, h2, a, u = saved[layer]
        d_w2[layer] = _matmul("tn", u, dxb, f"d_mlp_w2_{layer}", [BF16])
        da = _matmul("nt", dxb, w_2[layer], f"d_mlp_act_{layer}", [BF16], _ep_relu2_bwd, [a])
        d_w1[layer] = _matmul("tn", h2, da, f"d_mlp_w1_{layer}", [BF16])
        dh2 = _matmul("nt", da, w_1[layer], f"d_mlp_in_{layer}", [F32])
        dx, dxb, d_gains[layer][1] = _rmsnorm_bwd(x1, gains_full[layer, 1], dh2, dx, f"d_norm_mlp_{layer}")
        if layer % 2 == 0:
            qkv, qk_gains, qkvn, o, runs = mix
            d_wo[j] = _matmul("tn", o, dxb, f"d_sb_w_o_{layer}", [BF16])
            do = _matmul("nt", dxb, w_o[j], f"d_sb_o_{layer}", [F32])
            dq, dk, dv = _sb_attention_bwd(qkvn, do, runs, f"sb_bwd_{layer}")
            dqkv, d_qk[j] = _qknorm_bwd(qkv, dq, dk, dv, qk_gains, f"d_qknorm_{layer}")
            d_wqkv[j] = _matmul("tn", h, dqkv, f"d_sb_w_qkv_{layer}", [BF16])
            dh = _matmul("nt", dqkv, w_qkv[j], f"d_sb_in_{layer}", [F32])
        else:
            proj, og, o, states = mix
            d_who[j] = _matmul("tn", og, dxb, f"d_hg_w_o_{layer}", [BF16])
            dog = _matmul("nt", dxb, w_ho[j], f"d_hg_o_{layer}", [F32])
            dq, df, di, dg, d_lb[j], d_hgain[j] = _hgrn2_bwd(
                proj, lower[j:j + 1], hg_norm_gain[j:j + 1], o, states, dog, f"hg_bwd_{layer}")
            dproj = jnp.concatenate([dq, df, di, dg], axis=1)
            d_win[j] = _matmul("tn", h, dproj, f"d_hg_w_in_{layer}", [BF16])
            dh = _matmul("nt", dproj, w_in[j], f"d_hg_in_{layer}", [F32])
        dx, dxb, d_gains[layer][0] = _rmsnorm_bwd(x0, gains_full[layer, 0], dh, dx, f"d_norm_mix_{layer}")
    grad_x = dx[None]

    def update(parts, w, m, v, name):
        shape = w.shape
        flat = (shape[0] * shape[1], shape[2])
        got = _exchange(parts.astype(BF16), "exchange_" + name)
        return [r.reshape(shape) for r in _adamw(got, w.reshape(flat), m.reshape(flat), v.reshape(flat), "adamw_" + name)]

    big = {
        "sb_w_qkv": update(_parts_cols(d_wqkv), sb_w_qkv, m_sb_w_qkv, v_sb_w_qkv, "sb_w_qkv"),
        "sb_w_o": update(_parts_rows(d_wo), sb_w_o, m_sb_w_o, v_sb_w_o, "sb_w_o"),
        "hg_w_in": update(_parts_cols(d_win), hg_w_in, m_hg_w_in, v_hg_w_in, "hg_w_in"),
        "hg_w_o": update(_parts_rows(d_who), hg_w_o, m_hg_w_o, v_hg_w_o, "hg_w_o"),
        "mlp_w1": update(_parts_cols(d_w1), mlp_w1, m_mlp_w1, v_mlp_w1, "mlp_w1"),
        "mlp_w2": update(_parts_rows(d_w2), mlp_w2, m_mlp_w2, v_mlp_w2, "mlp_w2"),
    }

    d_gain_rows = jnp.concatenate([d_gains[l][t] for l in range(depth) for t in range(2)], axis=0)
    d_lb_rows = jnp.concatenate(d_lb, axis=0)
    def fold(t):
        return jnp.sum(t.reshape(d // SB_HEAD_DIM, SB_HEAD_DIM), axis=0, keepdims=True)
    d_qg = jnp.concatenate([fold(d_qk[i][0]) for i in range(n_sb)], axis=0) * SB_SCALE
    d_kg = jnp.concatenate([fold(d_qk[i][1]) for i in range(n_sb)], axis=0)
    d_hg = jnp.concatenate([jnp.sum(d_hgain[i], axis=0) for i in range(n_hg)], axis=0)
    per_row = d // LANES
    packed = jnp.concatenate([
        d_gain_rows.reshape(n_gain_rows * per_row, LANES), d_lb_rows.reshape(n_hg * per_row, LANES),
        jnp.concatenate([d_qg, d_kg], axis=1), d_hg], axis=0)
    n_packed = packed.shape[0]
    packed = _pad_rows(packed, -(-n_packed // 8) * 8)
    everyone = _all_gather(packed, "gather_small_grads")
    o_lb = n_gain_rows * per_row
    o_qk = o_lb + n_hg * per_row
    o_hg = o_qk + n_sb

    def mine_of(rows, count):
        return lax.dynamic_slice_in_dim(rows.reshape(N_DEV, count, per_row, LANES), me, 1, axis=2)[:, :, 0]

    d_logits_full = _lower_bounds_bwd(logits_full, everyone[:, o_lb:o_qk].reshape(N_DEV, n_hg, d), "lower_bounds_bwd")
    d_logits_mine = lax.dynamic_slice_in_dim(d_logits_full.reshape(n_hg, per_row, LANES), me, 1, axis=1)[:, 0]
    zeros7 = jnp.zeros((N_DEV - 1, n_hg, LANES), F32)
    small_parts = jnp.concatenate([
        mine_of(everyone[:, :o_lb], n_gain_rows),
        jnp.concatenate([d_logits_mine[None], zeros7], axis=0),
        everyone[:, o_qk:o_hg], everyone[:, o_hg:o_hg + n_hg]], axis=1)
    rows_small = small_parts.shape[1]
    pad_to = -(-rows_small // 8) * 8
    small_parts = jnp.concatenate([small_parts, jnp.zeros((N_DEV, pad_to - rows_small, LANES), F32)], axis=1)

    def pack_small(ng, qg, kg, lbl, hgn):
        return _pad_rows(jnp.concatenate([
            ng.reshape(n_gain_rows, d_loc), lbl, jnp.concatenate([qg, kg], axis=1), hgn], axis=0), pad_to)

    res = _adamw(small_parts,
                 pack_small(norm_gains, sb_q_gain, sb_k_gain, hg_lb_logits, hg_norm_gain),
                 pack_small(m_norm_gains, m_sb_q_gain, m_sb_k_gain, m_hg_lb_logits, m_hg_norm_gain),
                 pack_small(v_norm_gains, v_sb_q_gain, v_sb_k_gain, v_hg_lb_logits, v_hg_norm_gain), "adamw_small")

    def unpack_small(t):
        o1 = n_gain_rows
        o2 = o1 + n_hg
        o3 = o2 + n_sb
        return {"norm_gains": t[:o1].reshape(depth, 2, d_loc), "hg_lb_logits": t[o1:o2],
                "sb_q_gain": t[o2:o3, :SB_HEAD_DIM], "sb_k_gain": t[o2:o3, SB_HEAD_DIM:],
                "hg_norm_gain": t[o3:o3 + n_hg]}

    small_out = [unpack_small(t) for t in res]
    order = ["norm_gains", "sb_w_qkv", "sb_q_gain", "sb_k_gain", "sb_w_o", "hg_w_in", "hg_lb_logits",
             "hg_norm_gain", "hg_w_o", "mlp_w1", "mlp_w2"]
    outs = [loss, grad_x]
    for kind in range(4):
        outs += [big[n][kind] if n in big else small_out[kind][n] for n in order]
    return tuple(outs)
```

```python
import functools
import math

import numpy as np
import jax
import jax.numpy as jnp
from jax import lax
from jax.experimental import pallas as pl
from jax.experimental.pallas import tpu as pltpu

F32 = jnp.float32
BF16 = jnp.bfloat16
EPS = 1e-6
SB_HEAD_DIM = 64
HG_DIM = 128
LANES = 128
N_DEV = 8
AXES = ("x", "y", "c")
VMEM_LIMIT_BYTES = 48 * 1024 * 1024
MATMUL_VMEM_BUDGET = 40 * 1024 * 1024
SB_SCALE = 1.0 / math.sqrt(SB_HEAD_DIM)
ATT_BLOCK = 256
HG_CHUNK = 64
HG_STEP_CHUNKS = 4
HG_STEP_HEADS = 2
ADAM_LR, ADAM_B1, ADAM_B2, ADAM_EPS, ADAM_WD, ADAM_STEP = 0.001, 0.9, 0.999, 1e-08, 0.01, 10


def _call(body, **kw):
    return pl.pallas_call(body, **kw)


def _sds(shape, dtype):
    return jax.ShapeDtypeStruct(tuple(shape), dtype)


def _cparams(*sem):
    return pltpu.CompilerParams(dimension_semantics=sem or None, vmem_limit_bytes=VMEM_LIMIT_BYTES)


def _split_bf16(x):
    hi = x.astype(BF16)
    lo = (x - hi.astype(F32)).astype(BF16)
    return hi, lo


def _dot(a, b, dims):
    return lax.dot_general(a, b, (dims, ((), ())), preferred_element_type=F32)


NN = ((1,), (0,))
NT = ((1,), (1,))
TN = ((0,), (0,))


def _dot2(x, m, dims):
    hi, lo = _split_bf16(x)
    return _dot(hi, m, dims) + _dot(lo, m, dims)


def _mdot2(m, x, dims):
    hi, lo = _split_bf16(x)
    return _dot(m, hi, dims) + _dot(m, lo, dims)


def _matmul_tiles(m, n, k, a_dtype, b_dtype, io_dtypes):
    tm, tn = min(m, 1024), min(n, 1024)

    def need(tm, tn):
        blocks = tm * k * jnp.dtype(a_dtype).itemsize + tn * k * jnp.dtype(b_dtype).itemsize
        blocks += sum(tm * tn * jnp.dtype(dt).itemsize for dt in io_dtypes)
        return 2 * blocks + 2 * tm * tn * 4

    while need(tm, tn) > MATMUL_VMEM_BUDGET:
        if tm >= tn and tm > 256:
            tm //= 2
        else:
            tn //= 2
    return tm, tn


def _matmul(kind, a, b, name, out_dtypes, epilogue=None, extras=()):
    if kind == "nn":
        (m, k), n = a.shape, b.shape[1]
    elif kind == "nt":
        (m, k), n = a.shape, b.shape[0]
    else:
        (k, m), n = a.shape, b.shape[1]
    tm, tn = _matmul_tiles(m, n, k, a.dtype, b.dtype, list(out_dtypes) + [e.dtype for e in extras])
    assert m % tm == 0 and n % tn == 0, (name, a.shape, b.shape)
    a_spec = pl.BlockSpec((k, tm), lambda i, j: (0, i)) if kind == "tn" else pl.BlockSpec((tm, k), lambda i, j: (i, 0))
    b_spec = pl.BlockSpec((tn, k), lambda i, j: (j, 0)) if kind == "nt" else pl.BlockSpec((k, tn), lambda i, j: (0, j))
    o_spec = pl.BlockSpec((tm, tn), lambda i, j: (i, j))
    dims = {"nn": NN, "nt": NT, "tn": TN}[kind]
    n_ex = len(extras)

    def body(*refs):
        a_ref, b_ref = refs[:2]
        ex, outs = refs[2:2 + n_ex], refs[2 + n_ex:]
        acc = _dot(a_ref[...].astype(BF16), b_ref[...].astype(BF16), dims)
        res = epilogue(acc, *[e[...] for e in ex]) if epilogue is not None else (acc,)
        for o_ref, r in zip(outs, res):
            o_ref[...] = r.astype(o_ref.dtype)

    out = _call(
        body, name=name, grid=(m // tm, n // tn),
        in_specs=[a_spec, b_spec] + [o_spec] * n_ex,
        out_specs=[o_spec] * len(out_dtypes),
        out_shape=[_sds((m, n), dt) for dt in out_dtypes],
        compiler_params=_cparams("parallel", "parallel"),
    )(a, b, *extras)
    return out if len(out_dtypes) > 1 else out[0]


def _ep_add(acc, res):
    return (acc + res,)


def _ep_relu2(acc):
    r = jnp.maximum(acc, 0.0)
    return acc, r * r


def _ep_relu2_bwd(acc, a):
    return (acc * (2.0 * jnp.maximum(a, 0.0)),)


def _rmsnorm(x, g, name):
    s, d = x.shape
    tm = min(s, 512)

    def body(x_ref, g_ref, h_ref):
        xv = x_ref[...]
        r = lax.rsqrt(jnp.mean(xv * xv, axis=-1, keepdims=True) + EPS)
        h_ref[...] = (xv * r * g_ref[...]).astype(BF16)

    return _call(
        body, name=name, grid=(s // tm,),
        in_specs=[pl.BlockSpec((tm, d), lambda i: (i, 0)), pl.BlockSpec((1, d), lambda i: (0, 0))],
        out_specs=pl.BlockSpec((tm, d), lambda i: (i, 0)),
        out_shape=_sds((s, d), BF16), compiler_params=_cparams("parallel"),
    )(x, g)


def _rmsnorm_bwd(x, g, dh, dres, name):
    s, d = x.shape
    tm = min(s, 512)

    def body(x_ref, g_ref, dh_ref, dres_ref, dx_ref, dxb_ref, dg_ref):
        xv, dhv = x_ref[...], dh_ref[...]
        r = lax.rsqrt(jnp.mean(xv * xv, axis=-1, keepdims=True) + EPS)
        xr = xv * r
        t = dhv * g_ref[...]
        dx = dres_ref[...] + r * (t - xr * jnp.mean(t * xr, axis=-1, keepdims=True))
        dx_ref[...] = dx
        dxb_ref[...] = dx.astype(BF16)

        @pl.when(pl.program_id(0) == 0)
        def _():
            dg_ref[...] = jnp.zeros_like(dg_ref)

        dg_ref[...] += jnp.sum(dhv * xr, axis=0, keepdims=True)

    row = pl.BlockSpec((tm, d), lambda i: (i, 0))
    vec = pl.BlockSpec((1, d), lambda i: (0, 0))
    return _call(
        body, name=name, grid=(s // tm,), in_specs=[row, vec, row, row], out_specs=[row, row, vec],
        out_shape=[_sds((s, d), F32), _sds((s, d), BF16), _sds((1, d), F32)],
        compiler_params=_cparams("arbitrary"),
    )(x, g, dh, dres)


def _loss_head(y, target, name):
    s, d = y.shape
    tm = min(s, 512)

    def body(y_ref, t_ref, loss_ref, dy_ref, dyb_ref):
        err = y_ref[...] - t_ref[...]
        dy = err * (1.0 / d)
        dy_ref[...] = dy
        dyb_ref[...] = dy.astype(BF16)

        @pl.when(pl.program_id(0) == 0)
        def _():
            loss_ref[...] = jnp.zeros_like(loss_ref)

        part = 0.5 * jnp.sum(jnp.mean(err * err, axis=-1, keepdims=True), axis=0, keepdims=True)
        loss_ref[...] += part

    row = pl.BlockSpec((tm, d), lambda i: (i, 0))
    return _call(
        body, name=name, grid=(s // tm,), in_specs=[row, row],
        out_specs=[pl.BlockSpec((8, LANES), lambda i: (0, 0)), row, row],
        out_shape=[_sds((8, LANES), F32), _sds((s, d), F32), _sds((s, d), BF16)],
        compiler_params=_cparams("arbitrary"),
    )(y, target)


def _head_lane_mask():
    lane = lax.broadcasted_iota(jnp.int32, (1, LANES), 1)
    return lane < SB_HEAD_DIM


def _pair_rms(xv, first):
    x2 = xv * xv
    s0 = jnp.sum(jnp.where(first, x2, 0.0), axis=-1, keepdims=True)
    s1 = jnp.sum(jnp.where(first, 0.0, x2), axis=-1, keepdims=True)
    inv = 1.0 / SB_HEAD_DIM
    return jnp.where(first, lax.rsqrt(s0 * inv + EPS), lax.rsqrt(s1 * inv + EPS))


def _pair_mean(t, first):
    s0 = jnp.sum(jnp.where(first, t, 0.0), axis=-1, keepdims=True)
    s1 = jnp.sum(jnp.where(first, 0.0, t), axis=-1, keepdims=True)
    return jnp.where(first, s0, s1) * (1.0 / SB_HEAD_DIM)


def _qk_gain_table(q_gain, k_gain, d):
    reps = d // SB_HEAD_DIM
    return jnp.stack([jnp.tile(q_gain, reps) * SB_SCALE, jnp.tile(k_gain, reps), jnp.ones((d,), F32)])[:, None, :]


QKNORM_ROWS = 256


def _qknorm(qkv, gains, name):
    s, d3 = qkv.shape
    d = d3 // 3
    tm = min(s, QKNORM_ROWS)

    def body(x_ref, g_ref, o_ref):
        first = _head_lane_mask()
        is_v = pl.program_id(0) == 2
        for c in range(d // LANES):
            cols = slice(c * LANES, (c + 1) * LANES)
            xv = x_ref[:, cols]
            normed = xv * _pair_rms(xv, first) * g_ref[0, :, cols]
            o_ref[:, cols] = jnp.where(is_v, xv, normed).astype(BF16)

    tile = pl.BlockSpec((tm, d), lambda c, i: (i, c))
    return _call(
        body, name=name, grid=(3, s // tm),
        in_specs=[tile, pl.BlockSpec((1, 1, d), lambda c, i: (c, 0, 0))], out_specs=tile,
        out_shape=_sds((s, d3), BF16), compiler_params=_cparams("parallel", "parallel"),
    )(qkv, gains)


def _qknorm_bwd(qkv, dq, dk, dv, gains, name):
    s, d3 = qkv.shape
    d = d3 // 3
    tm = min(s, QKNORM_ROWS)

    def body(x_ref, dq_ref, dk_ref, dv_ref, g_ref, dx_ref, dg_ref):
        c, i = pl.program_id(0), pl.program_id(1)

        @pl.when(i == 0)
        def _():
            dg_ref[...] = jnp.zeros_like(dg_ref)

        first = _head_lane_mask()
        for col in range(d // LANES):
            cols = slice(col * LANES, (col + 1) * LANES)
            xv = x_ref[:, cols]
            dy = jnp.where(c == 0, dq_ref[:, cols], jnp.where(c == 1, dk_ref[:, cols], dv_ref[:, cols]))
            r = _pair_rms(xv, first)
            xr = xv * r
            t = dy * g_ref[0, :, cols]
            dx = r * (t - xr * _pair_mean(t * xr, first))
            dx_ref[:, cols] = jnp.where(c == 2, dy, dx).astype(BF16)
            dg_ref[0, :, cols] += jnp.sum(dy * xr, axis=0, keepdims=True)

    tile = pl.BlockSpec((tm, d), lambda c, i: (i, c))
    vec = pl.BlockSpec((1, 1, d), lambda c, i: (c, 0, 0))

    def part(kind):
        return pl.BlockSpec((tm, d), lambda c, i: (jnp.where(c == kind, i, 0), 0))

    return _call(
        body, name=name, grid=(3, s // tm), in_specs=[tile, part(0), part(1), part(2), vec], out_specs=[tile, vec],
        out_shape=[_sds((s, d3), BF16), _sds((3, 1, d), F32)],
        compiler_params=_cparams("arbitrary", "arbitrary"),
    )(qkv, dq, dk, dv, gains)


def _softplus_parts(z):
    sp = jnp.maximum(z, 0.0) + jnp.log(1.0 + jnp.exp(-jnp.abs(z)))
    return sp, z - sp


def _diag_causal(tb):
    return lax.broadcasted_iota(jnp.int32, (tb, tb), 1) < lax.broadcasted_iota(jnp.int32, (tb, tb), 0)


def _later_keys(tb):
    r = lax.broadcasted_iota(jnp.int32, (tb, tb), 0)
    c = lax.broadcasted_iota(jnp.int32, (tb, tb), 1)
    return jnp.where(r > c, 1.0, 0.0).astype(BF16)


def _sb_scores(qa, kj, causal):
    sp, logsig = _softplus_parts(_dot(qa, kj, NT))
    return (-sp if causal is None else jnp.where(causal, -sp, 0.0)), logsig


def _sb_weights(stay, logsig, run, later, causal):
    w = jnp.exp(logsig + _dot2(stay, later, NN) + run)
    return w if causal is None else jnp.where(causal, w, 0.0)


def _sb_attention_fwd(qkvn, name, riders=()):
    s, d3 = qkvn.shape
    d = d3 // 3
    pairs, tb = d // LANES, min(ATT_BLOCK, s)
    nb = s // tb
    assert nb <= LANES
    n_r = len(riders)
    ride_shapes, ride_scratch = _rider_shapes("gather", riders)

    def body(*refs):
        q_ref, k_ref, v_ref = refs[:3]
        o_ref, runs_ref = refs[3 + n_r:5 + n_r]
        ride_end = _ride("gather", refs[3:3 + n_r], refs[5 + n_r:5 + 2 * n_r], refs[5 + 2 * n_r:], (pairs, nb))
        i = pl.program_id(1)
        first = _head_lane_mask()
        lane = lax.broadcasted_iota(jnp.int32, (1, LANES), 1)
        later = _later_keys(tb)
        q2 = q_ref[...]
        qs = (jnp.where(first, q2, jnp.zeros_like(q2)), jnp.where(first, jnp.zeros_like(q2), q2))

        def tiles(js, carry, causal):
            kvs = []
            for j in js:
                rows = pl.ds(pl.multiple_of(j * tb, tb), tb)
                kvs.append((k_ref[rows, :], v_ref[rows, :]))
            scores = [[_sb_scores(qs[h], kj, causal) for h in range(2)] for kj, _ in kvs]
            run = [carry[0][0], carry[1][0]]
            acc = [carry[0][1], carry[1][1]]
            runs = [carry[0][2], carry[1][2]]
            weights = []
            for t, j in enumerate(js):
                weights.append([_sb_weights(*scores[t][h], run[h], later, causal) for h in range(2)])
                for h in range(2):
                    runs[h] = jnp.where(lane == j, run[h], runs[h])
                    run[h] = run[h] + jnp.sum(scores[t][h][0], axis=-1, keepdims=True)
            for t, (_, vj) in enumerate(kvs):
                for h in range(2):
                    acc[h] = acc[h] + _dot(weights[t][h].astype(BF16), vj, NN)
            return tuple((run[h], acc[h], runs[h]) for h in range(2))

        zero = (jnp.zeros((tb, 1), F32), jnp.zeros((tb, LANES), F32), jnp.zeros((tb, LANES), F32))
        carry = tiles([i], (zero, zero), _diag_causal(tb))
        carry = lax.fori_loop(0, i // 2, lambda jj, c: tiles([i - 1 - 2 * jj, i - 2 - 2 * jj], c, None), carry)
        carry = lax.cond(i % 2 == 1, lambda c: tiles([0], c, None), lambda c: c, carry)
        o_ref[...] = jnp.where(first, carry[0][1], carry[1][1])
        runs_ref[0] = carry[0][2]
        runs_ref[1] = carry[1][2]
        ride_end()

    out = _call(
        body, name=name, grid=(pairs, nb),
        in_specs=[pl.BlockSpec((tb, LANES), lambda p, i: (i, p)),
                  pl.BlockSpec((s, LANES), lambda p, i: (0, pairs + p)),
                  pl.BlockSpec((s, LANES), lambda p, i: (0, 2 * pairs + p))] + [_ANY] * n_r,
        out_specs=[pl.BlockSpec((tb, LANES), lambda p, i: (i, p)),
                   pl.BlockSpec((2, tb, LANES), lambda p, i: (p * nb + i, 0, 0))] + [_ANY] * n_r,
        out_shape=[_sds((s, d), F32), _sds((pairs * nb * 2, tb, LANES), F32)] + ride_shapes,
        scratch_shapes=ride_scratch,
        compiler_params=_cparams("arbitrary", "arbitrary"),
    )(qkvn, qkvn, qkvn, *riders)
    return out[0], out[1], list(out[2:])


def _sb_attention_bwd(qkvn, do, runs, name, riders=()):
    s, d3 = qkvn.shape
    d = d3 // 3
    pairs, tb = d // LANES, min(ATT_BLOCK, s)
    nb = s // tb
    n_r = len(riders)
    ride_shapes, ride_scratch = _rider_shapes("exchange", riders)

    def body(*refs):
        q_ref, k_ref, v_ref, do_ref, runs_ref = refs[:5]
        dq_ref, dk_ref, dv_ref = refs[5 + n_r:8 + n_r]
        ride_end = _ride("exchange", refs[5:5 + n_r], refs[8 + n_r:8 + 2 * n_r], refs[8 + 2 * n_r:], (pairs, nb))
        i = pl.program_id(1)

        @pl.when(i == 0)
        def _():
            dk_ref[...] = jnp.zeros_like(dk_ref)
            dv_ref[...] = jnp.zeros_like(dv_ref)

        first = _head_lane_mask()
        lane = lax.broadcasted_iota(jnp.int32, (1, LANES), 1)
        later = _later_keys(tb)
        q2, do2 = q_ref[...], do_ref[...].astype(BF16)
        zq = jnp.zeros_like(q2)
        qs = (jnp.where(first, q2, zq), jnp.where(first, zq, q2))
        dos = (jnp.where(first, do2, zq), jnp.where(first, zq, do2))

        def tiles(js, carry, causal):
            rows, kv = [], []
            for j in js:
                r = pl.ds(pl.multiple_of(j * tb, tb), tb)
                kj, vj = k_ref[r, :], v_ref[r, :]
                zk = jnp.zeros_like(kj)
                rows.append(r)
                kv.append([(jnp.where(first, kj, zk), jnp.where(first, vj, zk)),
                           (jnp.where(first, zk, kj), jnp.where(first, zk, vj))])
            pairs_th = [(t, h) for t in range(len(js)) for h in range(2)]
            scores = {(t, h): _sb_scores(qs[h], kv[t][h][0], causal) for t, h in pairs_th}
            w, g = {}, {}
            for t, h in pairs_th:
                run = jnp.sum(jnp.where(lane == js[t], runs_ref[h], 0.0), axis=-1, keepdims=True)
                w[t, h] = _sb_weights(*scores[t, h], run, later, causal)
                g[t, h] = w[t, h] * _dot(dos[h], kv[t][h][1], NT)
            gsum = [carry[0], carry[1]]
            dz = {}
            for t, h in pairs_th:
                before = _dot(g[t, h].astype(BF16), later, NT) + gsum[h]
                gsum[h] = gsum[h] + jnp.sum(g[t, h], axis=-1, keepdims=True)
                sig = jnp.exp(scores[t, h][1])
                d = g[t, h] * (1.0 - sig) - before * sig
                dz[t, h] = (d if causal is None else jnp.where(causal, d, 0.0)).astype(BF16)
            dq = carry[2]
            for t, h in pairs_th:
                dq = dq + _dot(dz[t, h], kv[t][h][0], NN)
            for t in range(len(js)):
                dk_ref[rows[t], :] += _dot(dz[t, 0], qs[0], TN) + _dot(dz[t, 1], qs[1], TN)
                dv_ref[rows[t], :] += _dot(w[t, 0].astype(BF16), dos[0], TN) + _dot(w[t, 1].astype(BF16), dos[1], TN)
            return gsum[0], gsum[1], dq

        carry = (jnp.zeros((tb, 1), F32), jnp.zeros((tb, 1), F32), jnp.zeros((tb, LANES), F32))
        carry = lax.fori_loop(0, i // 2, lambda jj, c: tiles([2 * jj, 2 * jj + 1], c, None), carry)
        carry = lax.cond(i % 2 == 1, lambda c: tiles([i - 1], c, None), lambda c: c, carry)
        carry = tiles([i], carry, _diag_causal(tb))
        dq_ref[...] = carry[2]
        ride_end()

    q_spec = pl.BlockSpec((tb, LANES), lambda p, i: (i, p))
    out = _call(
        body, name=name, grid=(pairs, nb),
        in_specs=[q_spec,
                  pl.BlockSpec((s, LANES), lambda p, i: (0, pairs + p)),
                  pl.BlockSpec((s, LANES), lambda p, i: (0, 2 * pairs + p)),
                  q_spec,
                  pl.BlockSpec((2, tb, LANES), lambda p, i: (p * nb + i, 0, 0))] + [_ANY] * n_r,
        out_specs=[q_spec, pl.BlockSpec((s, LANES), lambda p, i: (0, p)),
                   pl.BlockSpec((s, LANES), lambda p, i: (0, p))] + [_ANY] * n_r,
        out_shape=[_sds((s, d), F32)] * 3 + ride_shapes,
        scratch_shapes=ride_scratch,
        compiler_params=_cparams("arbitrary", "arbitrary"),
    )(qkvn, qkvn, qkvn, do, runs, *riders)
    return out[0], out[1], out[2], list(out[3:])


def _hg_tables(c):
    t = np.arange(c)[:, None]
    j = np.arange(c)[None, :]
    sums = [j <= t, j > t]
    masks = []
    m = c // 2
    while m >= 1:
        pos, base = t % (2 * m), t - t % (2 * m)
        sums.append((pos >= m) & (j >= base + m) & (j <= t))
        sums.append((pos < m) & (j > t) & (j <= base + m - 1))
        masks.append((t // (2 * m) == j // (2 * m)) & (t % (2 * m) >= m) & (j % (2 * m) < m))
        m //= 2
    return (jnp.asarray(np.concatenate(sums, 0), BF16), jnp.asarray(np.stack(masks), F32), len(masks))


def _hg_gates(qr, fr, lb):
    sq = jax.nn.sigmoid(qr)
    sg = jax.nn.sigmoid(fr)
    forget = lb + (1.0 - lb) * sg
    return sq, qr * sq, sg, forget, jnp.log(forget), (1.0 - lb) * (1.0 - sg)


def _hg_scores(q, k, x, masks_ref, c, levels):
    eye = (lax.broadcasted_iota(jnp.int32, (c, c), 0) == lax.broadcasted_iota(jnp.int32, (c, c), 1)).astype(F32)
    scores = eye * jnp.sum(q * k, axis=-1, keepdims=True)
    ops = []
    for l in range(levels):
        qm = (q * x[(2 + 2 * l) * c:(3 + 2 * l) * c]).astype(BF16)
        km = (k * x[(3 + 2 * l) * c:(4 + 2 * l) * c]).astype(BF16)
        scores = scores + masks_ref[l] * _dot(qm, km, NT)
        ops.append((qm, km))
    return scores, eye, ops


def _hg_layout(s, d):
    heads, c = d // HG_DIM, min(HG_CHUNK, s)
    nsub = min(HG_STEP_CHUNKS, s // c)
    hp = HG_STEP_HEADS if heads % HG_STEP_HEADS == 0 else 1
    return heads, c, nsub, hp


def _hg_specs(s, d, reverse):
    heads, c, nsub, hp = _hg_layout(s, d)
    rows, width, groups, n_steps = c * nsub, hp * HG_DIM, heads // hp, s // (c * nsub)

    def step(si):
        return n_steps - 1 - si if reverse else si
    proj = [pl.BlockSpec((rows, width), functools.partial(lambda g, si, part: (step(si), part * groups + g), part=p))
            for p in range(4)]
    tile = pl.BlockSpec((rows, width), lambda g, si: (step(si), g))
    lb = pl.BlockSpec((1, width), lambda g, si: (0, g))
    gain = pl.BlockSpec((1, HG_DIM), lambda g, si: (0, 0))
    state = pl.BlockSpec((hp, nsub, HG_DIM, HG_DIM), lambda g, si: (g, step(si), 0, 0))
    return proj, tile, lb, gain, state, (groups, n_steps)


def _hgrn2_fwd(proj, lb, gain, name):
    s, d4 = proj.shape
    d = d4 // 4
    heads, c, nsub, hp = _hg_layout(s, d)
    sums, masks, levels = _hg_tables(c)

    def body(qr_ref, fr_ref, ir_ref, gr_ref, lb_ref, gain_ref, sums_ref, masks_ref, og_ref, o_ref, states_ref, st_ref):
        @pl.when(pl.program_id(1) == 0)
        def _():
            st_ref[...] = jnp.zeros_like(st_ref)

        lbv, gainv = lb_ref[...], gain_ref[...]
        units = [(ci, hh) for ci in range(nsub) for hh in range(hp)]

        def lanes(hh):
            return slice(hh * HG_DIM, (hh + 1) * HG_DIM)

        pre = []
        for ci in range(nsub):
            rows = slice(ci * c, (ci + 1) * c)
            _, q, _, _, lf, k = _hg_gates(qr_ref[rows, :], fr_ref[rows, :], lbv)
            pre.append((q, k, jnp.exp(_mdot2(sums_ref[...], lf, NN))))
        scores, qh, vb, update = {}, {}, {}, {}
        for ci, hh in units:
            q, k, x = (a[:, lanes(hh)] for a in pre[ci])
            scores[ci, hh] = _hg_scores(q, k, x, masks_ref, c, levels)[0].astype(BF16)
            qh[ci, hh] = (q * x[0:c]).astype(BF16)
            vb[ci, hh] = ir_ref[ci * c:(ci + 1) * c, lanes(hh)].astype(BF16)
            update[ci, hh] = _dot(vb[ci, hh], (k * x[c:2 * c]).astype(BF16), TN)
        intra = {u: _dot(scores[u], vb[u], NN) for u in units}
        for hh in range(hp):
            st = st_ref[hh]
            for ci in range(nsub):
                rows = slice(ci * c, (ci + 1) * c)
                states_ref[hh, ci] = st
                o = _dot(qh[ci, hh], st.astype(BF16), NT) + intra[ci, hh]
                st = st * pre[ci][2][c - 1:c, lanes(hh)] + update[ci, hh]
                o_ref[rows, lanes(hh)] = o
                r = lax.rsqrt(jnp.mean(o * o, axis=-1, keepdims=True) + EPS)
                og_ref[rows, lanes(hh)] = (o * r * gainv * jax.nn.sigmoid(gr_ref[rows, lanes(hh)])).astype(BF16)
            st_ref[hh] = st

    pspecs, tile, lbs, gs, state, grid = _hg_specs(s, d, False)
    const = [pl.BlockSpec(sums.shape, lambda g, si: (0, 0)), pl.BlockSpec(masks.shape, lambda g, si: (0, 0, 0))]
    return _call(
        body, name=name, grid=grid, in_specs=pspecs + [lbs, gs] + const,
        out_specs=[tile, tile, state],
        out_shape=[_sds((s, d), BF16), _sds((s, d), F32), _sds((heads, s // c, HG_DIM, HG_DIM), F32)],
        scratch_shapes=[pltpu.VMEM((hp, HG_DIM, HG_DIM), F32)],
        compiler_params=_cparams("parallel", "arbitrary"),
    )(proj, proj, proj, proj, lb, gain, sums, masks)


def _hgrn2_bwd(proj, lb, gain, o, states, dog, name):
    s, d4 = proj.shape
    d = d4 // 4
    heads, c, nsub, hp = _hg_layout(s, d)
    sums, masks, levels = _hg_tables(c)

    def body(qr_ref, fr_ref, ir_ref, gr_ref, lb_ref, gain_ref, sums_ref, masks_ref, o_ref, states_ref, dog_ref,
             dq_ref, df_ref, di_ref, dg_ref, dlb_ref, dgain_ref, dst_ref):
        @pl.when(pl.program_id(1) == 0)
        def _():
            dst_ref[...] = jnp.zeros_like(dst_ref)
            dlb_ref[...] = jnp.zeros_like(dlb_ref)
            dgain_ref[...] = jnp.zeros_like(dgain_ref)

        lbv, gainv = lb_ref[...], gain_ref[...]
        units = [(ci, hh) for ci in range(nsub) for hh in range(hp)]

        def lanes(hh):
            return slice(hh * HG_DIM, (hh + 1) * HG_DIM)

        def rows_of(ci):
            return slice(ci * c, (ci + 1) * c)

        pre = []
        for ci in range(nsub):
            qr = qr_ref[rows_of(ci), :]
            sq, q, sg, forget, lf, k = _hg_gates(qr, fr_ref[rows_of(ci), :], lbv)
            pre.append(dict(qr=qr, sq=sq, q=q, sg=sg, forget=forget, k=k, x=jnp.exp(_mdot2(sums_ref[...], lf, NN))))

        dob, vb, sc, qh_f, kh_f, feed = {}, {}, {}, {}, {}, {}
        dgain = [jnp.zeros((1, HG_DIM), F32) for _ in range(hp)]
        for ci, hh in units:
            rows, ln = rows_of(ci), lanes(hh)
            ov, gate = o_ref[rows, ln], jax.nn.sigmoid(gr_ref[rows, ln])
            r = lax.rsqrt(jnp.mean(ov * ov, axis=-1, keepdims=True) + EPS)
            orr = ov * r
            dogv = dog_ref[rows, ln]
            dg_ref[rows, ln] = (dogv * orr * gainv * gate * (1.0 - gate)).astype(BF16)
            don = dogv * gate
            dgain[hh] = dgain[hh] + jnp.sum(don * orr, axis=0, keepdims=True)
            t = don * gainv
            dob[ci, hh] = (r * (t - orr * jnp.mean(t * orr, axis=-1, keepdims=True))).astype(BF16)
            q, k, x = (pre[ci][n][:, ln] for n in ("q", "k", "x"))
            vb[ci, hh] = ir_ref[rows, ln].astype(BF16)
            sc[ci, hh] = _hg_scores(q, k, x, masks_ref, c, levels)
            qh_f[ci, hh], kh_f[ci, hh] = q * x[0:c], k * x[c:2 * c]
            feed[ci, hh] = _dot(dob[ci, hh], qh_f[ci, hh].astype(BF16), TN)

        dsts = {}
        for hh in range(hp):
            dst = dst_ref[hh]
            for ci in reversed(range(nsub)):
                dsts[ci, hh] = dst
                dst = dst * pre[ci]["x"][c - 1:c, lanes(hh)] + feed[ci, hh]
            dst_ref[hh] = dst

        dlb = [jnp.zeros((1, HG_DIM), F32) for _ in range(hp)]
        for ci, hh in units:
            rows, ln = rows_of(ci), lanes(hh)
            p = {n: v[:, ln] for n, v in pre[ci].items()}
            q, k, x = p["q"], p["k"], p["x"]
            scores, eye, ops = sc[ci, hh]
            st, dst = states_ref[hh, ci], dsts[ci, hh]
            dstb = dst.astype(BF16)
            dscores = _dot(dob[ci, hh], vb[ci, hh], NT)
            di_ref[rows, ln] = (_dot(scores.astype(BF16), dob[ci, hh], TN)
                                + _dot(kh_f[ci, hh].astype(BF16), dstb, NT)).astype(BF16)
            dqh = _dot(dob[ci, hh], st.astype(BF16), NN)
            dkh = _dot(vb[ci, hh], dstb, NN)
            decay_grad = x[c - 1:c] * jnp.sum(dst * st, axis=0, keepdims=True)
            ddiag = jnp.sum(eye * dscores, axis=-1, keepdims=True)
            dq = dqh * x[0:c] + ddiag * k
            dk = dkh * x[c:2 * c] + ddiag * q
            dexp = [dqh * qh_f[ci, hh], dkh * kh_f[ci, hh]]
            for l, (qm, km) in enumerate(ops):
                dsm = (masks_ref[l] * dscores).astype(BF16)
                dqm, dkm = _dot(dsm, km, NN), _dot(dsm, qm, TN)
                xq, xk = x[(2 + 2 * l) * c:(3 + 2 * l) * c], x[(3 + 2 * l) * c:(4 + 2 * l) * c]
                dq = dq + dqm * xq
                dk = dk + dkm * xk
                dexp += [dqm * (q * xq), dkm * (k * xk)]
            dlf = _mdot2(sums_ref[...], jnp.concatenate(dexp, axis=0), TN) + decay_grad
            dforget = dlf / p["forget"] - dk
            dlb[hh] = dlb[hh] + jnp.sum(dforget * (1.0 - p["sg"]), axis=0, keepdims=True)
            df_ref[rows, ln] = (dforget * (1.0 - lbv[:, ln]) * p["sg"] * (1.0 - p["sg"])).astype(BF16)
            dq_ref[rows, ln] = (dq * p["sq"] * (1.0 + p["qr"] * (1.0 - p["sq"]))).astype(BF16)
        for hh in range(hp):
            dlb_ref[:, lanes(hh)] += dlb[hh]
            dgain_ref[hh] += dgain[hh]

    pspecs, tile, lbs, gs, state, grid = _hg_specs(s, d, True)
    const = [pl.BlockSpec(sums.shape, lambda g, si: (0, 0)), pl.BlockSpec(masks.shape, lambda g, si: (0, 0, 0))]
    return _call(
        body, name=name, grid=grid, in_specs=pspecs + [lbs, gs] + const + [tile, state, tile],
        out_specs=[tile, tile, tile, tile, lbs, pl.BlockSpec((hp, 1, HG_DIM), lambda g, si: (g, 0, 0))],
        out_shape=[_sds((s, d), BF16)] * 4 + [_sds((1, d), F32), _sds((heads, 1, HG_DIM), F32)],
        scratch_shapes=[pltpu.VMEM((hp, HG_DIM, HG_DIM), F32)],
        compiler_params=_cparams("parallel", "arbitrary"),
    )(proj, proj, proj, proj, lb, gain, sums, masks, o, states, dog)


def _lower_bounds(logits, name):
    n, d = logits.shape

    def body(l_ref, lb_ref):
        lv = l_ref[...]
        e = jnp.exp(lv - jnp.max(lv, axis=0, keepdims=True))
        p = e / jnp.sum(e, axis=0, keepdims=True)
        run = jnp.zeros((1, d), F32)
        for j in range(n):
            if j > 0:
                run = run + p[j:j + 1]
            lb_ref[j:j + 1, :] = run

    return _call(body, name=name, out_shape=_sds((n, d), F32))(logits)


def _lower_bounds_bwd(logits, dlb_parts, name):
    n, d = logits.shape

    def body(l_ref, dlb_ref, dl_ref):
        lv, dv = l_ref[...], dlb_ref[0]
        for dev in range(1, N_DEV):
            dv = dv + dlb_ref[dev]
        e = jnp.exp(lv - jnp.max(lv, axis=0, keepdims=True))
        p = e / jnp.sum(e, axis=0, keepdims=True)
        run = jnp.zeros((1, d), F32)
        dps = [None] * n
        for j in range(n - 1, 0, -1):
            run = run + dv[j:j + 1]
            dps[j] = run
        dps[0] = jnp.zeros((1, d), F32)
        inner = jnp.zeros((1, d), F32)
        for j in range(n):
            inner = inner + p[j:j + 1] * dps[j]
        for j in range(n):
            dl_ref[j:j + 1, :] = p[j:j + 1] * (dps[j] - inner)

    return _call(body, name=name, out_shape=_sds((n, d), F32))(logits, dlb_parts)


_ANY = pl.BlockSpec(memory_space=pl.ANY)
_MESH = pl.DeviceIdType.MESH


def _gather_stages(x_ref, out_ref, send_sems, recv_sems, local_sem):
    mx, my, mc = lax.axis_index("x"), lax.axis_index("y"), lax.axis_index("c")
    me, sibling = (mx, my, mc), (mx, my, 1 - mc)
    chips = [(1 - mx, my), (mx, 1 - my), (1 - mx, 1 - my)]

    def slot(px, py, pc):
        return out_ref.at[4 * px + 2 * py + pc]

    def copy(k, block, to, src=None):
        return pltpu.make_async_remote_copy(
            src_ref=slot(*block) if src is None else src, dst_ref=slot(*block),
            send_sem=send_sems.at[k], recv_sem=recv_sems.at[k], device_id=to, device_id_type=_MESH)

    mine = pltpu.make_async_copy(x_ref, slot(*me), local_sem)
    first = [copy(0, me, sibling, src=x_ref)] + [copy(1 + j, me, (*chip, mc), src=x_ref) for j, chip in enumerate(chips)]
    passed = [copy(4 + j, (*chip, mc), sibling) for j, chip in enumerate(chips)]

    def start():
        mine.start()
        for cp in first:
            cp.start()

    def middle():
        for j, chip in enumerate(chips):
            copy(1 + j, (*chip, mc), me).wait_recv()
            passed[j].start()

    def finish():
        copy(0, sibling, me).wait_recv()
        for j, chip in enumerate(chips):
            copy(4 + j, (*chip, 1 - mc), me).wait_recv()
        for cp in first + passed:
            cp.wait_send()
        mine.wait()

    return start, middle, finish


def _exchange_stages(g_ref, out_ref, send_sems, recv_sems, local_sem):
    mx, my, mc = lax.axis_index("x"), lax.axis_index("y"), lax.axis_index("c")
    me = 4 * mx + 2 * my + mc
    mine = pltpu.make_async_copy(g_ref.at[me], out_ref.at[me], local_sem)
    copies = []
    for k in range(1, N_DEV):
        px, py, pc = mx ^ (k >> 2), my ^ ((k >> 1) & 1), mc ^ (k & 1)
        peer = 4 * px + 2 * py + pc
        send = pltpu.make_async_remote_copy(
            src_ref=g_ref.at[peer], dst_ref=out_ref.at[me], send_sem=send_sems.at[k - 1],
            recv_sem=recv_sems.at[k - 1], device_id=(px, py, pc), device_id_type=_MESH)
        arrival = pltpu.make_async_remote_copy(
            src_ref=g_ref.at[peer], dst_ref=out_ref.at[peer], send_sem=send_sems.at[k - 1],
            recv_sem=recv_sems.at[k - 1], device_id=(px, py, pc), device_id_type=_MESH)
        copies.append((send, arrival))

    def start():
        mine.start()
        for send, _ in copies:
            send.start()

    def finish():
        for _, arrival in copies:
            arrival.wait_recv()
        for send, _ in copies:
            send.wait_send()
        mine.wait()

    return start, lambda: None, finish


_STAGES = {"gather": _gather_stages, "exchange": _exchange_stages}
SEMS_PER_TRANSFER = 3


def _rider_shapes(kind, arrays):
    outs = [_sds((N_DEV,) + a.shape if kind == "gather" else a.shape, a.dtype) for a in arrays]
    scratch = []
    for _ in arrays:
        scratch += [pltpu.SemaphoreType.DMA((7,)), pltpu.SemaphoreType.DMA((7,)), pltpu.SemaphoreType.DMA]
    return outs, scratch


def _ride(kind, in_refs, out_refs, sems, grid):
    stages = [_STAGES[kind](a, o, *sems[SEMS_PER_TRANSFER * n:SEMS_PER_TRANSFER * (n + 1)])
              for n, (a, o) in enumerate(zip(in_refs, out_refs))]
    if not stages:
        return lambda: None
    p, i = pl.program_id(0), pl.program_id(1)

    def run(stage):
        for st in stages:
            st[stage]()

    pl.when((p == 0) & (i == 0))(lambda: run(0))
    pl.when((p == grid[0] // 2) & (i == 0))(lambda: run(1))
    return lambda: pl.when((p == grid[0] - 1) & (i == grid[1] - 1))(lambda: run(2))


def _transfer(kind, arrays, name):
    outs, scratch = _rider_shapes(kind, arrays)
    n = len(arrays)

    def body(*refs):
        stages = [_STAGES[kind](refs[t], refs[n + t], *refs[2 * n + SEMS_PER_TRANSFER * t:2 * n + SEMS_PER_TRANSFER * (t + 1)])
                  for t in range(n)]
        for stage in range(3):
            for st in stages:
                st[stage]()

    return _call(body, name=name, out_shape=outs, in_specs=[_ANY] * n, out_specs=[_ANY] * n, scratch_shapes=scratch)(*arrays)


def _all_gather(x, name):
    return _transfer("gather", [x], name)[0]


def _adamw(parts, w, m, v, name):
    n_l = len(parts)
    _, r, c = parts[0].shape
    tr = r if r <= 256 else 256
    assert r % tr == 0 and w.shape == (n_l * r, c), (name, parts[0].shape, w.shape)
    steps = r // tr

    def body(*refs):
        p_refs, (w_ref, m_ref, v_ref) = refs[:n_l], refs[n_l:n_l + 3]
        g_ref, d_ref, nm_ref, nv_ref, sum_ref = refs[n_l + 3:]
        for layer in range(n_l):
            @pl.when(pl.program_id(0) == layer)
            def _(p_ref=p_refs[layer]):
                acc = p_ref[0].astype(F32)
                for dev in range(1, N_DEV):
                    acc = acc + p_ref[dev].astype(F32)
                sum_ref[...] = acc

        g = sum_ref[...]
        nm = ADAM_B1 * m_ref[...] + (1.0 - ADAM_B1) * g
        nv = ADAM_B2 * v_ref[...] + (1.0 - ADAM_B2) * (g * g)
        m_hat = nm / (1.0 - ADAM_B1 ** ADAM_STEP)
        v_hat = nv / (1.0 - ADAM_B2 ** ADAM_STEP)
        g_ref[...] = g
        nm_ref[...] = nm
        nv_ref[...] = nv
        d_ref[...] = -ADAM_LR * (m_hat / (jnp.sqrt(v_hat) + ADAM_EPS) + ADAM_WD * w_ref[...])

    tile = pl.BlockSpec((tr, c), lambda l, i: (l * steps + i, 0))

    def part(layer):
        return pl.BlockSpec((N_DEV, tr, c), lambda l, i: (0, jnp.where(l == layer, i, 0), 0))

    return _call(
        body, name=name, grid=(n_l, steps),
        in_specs=[part(layer) for layer in range(n_l)] + [tile, tile, tile], out_specs=[tile] * 4,
        out_shape=[_sds((n_l * r, c), F32)] * 4, scratch_shapes=[pltpu.VMEM((tr, c), F32)],
        compiler_params=_cparams("arbitrary", "arbitrary"),
    )(*parts, w, m, v)


def _shard_2d(w):
    return w.astype(BF16).reshape(w.shape[0] * w.shape[1], w.shape[2])


def _full_cols(g, w):
    l, k, n = w.shape
    g = g.reshape(N_DEV, l, k, n)
    return [jnp.transpose(g[:, i], (1, 0, 2)).reshape(k, N_DEV * n) for i in range(l)]


def _full_rows(g, w):
    l, k, n = w.shape
    g = g.reshape(N_DEV, l, k, n)
    return [g[:, i].reshape(N_DEV * k, n) for i in range(l)]


def _parts_cols(grads):
    k, n8 = grads[0].shape
    g = jnp.stack(grads).reshape(len(grads), k, N_DEV, n8 // N_DEV)
    return jnp.transpose(g, (2, 0, 1, 3)).reshape(N_DEV, len(grads) * k, n8 // N_DEV)


def _parts_rows(grads):
    k8, n = grads[0].shape
    g = jnp.stack(grads).reshape(len(grads), N_DEV, k8 // N_DEV, n)
    return jnp.transpose(g, (1, 0, 2, 3)).reshape(N_DEV, len(grads) * (k8 // N_DEV), n)


def _pad_rows(a, rows):
    return jnp.concatenate([a, jnp.zeros((rows - a.shape[0], a.shape[1]), a.dtype)], axis=0)


def kernel(x, norm_gains, sb_w_qkv, sb_q_gain, sb_k_gain, sb_w_o, hg_w_in, hg_lb_logits, hg_norm_gain, hg_w_o, mlp_w1, mlp_w2, loss_target, m_norm_gains, m_sb_w_qkv, m_sb_q_gain, m_sb_k_gain, m_sb_w_o, m_hg_w_in, m_hg_lb_logits, m_hg_norm_gain, m_hg_w_o, m_mlp_w1, m_mlp_w2, v_norm_gains, v_sb_w_qkv, v_sb_q_gain, v_sb_k_gain, v_sb_w_o, v_hg_w_in, v_hg_lb_logits, v_hg_norm_gain, v_hg_w_o, v_mlp_w1, v_mlp_w2):
    depth, _, d_loc = norm_gains.shape
    n_sb, n_hg = sb_w_qkv.shape[0], hg_w_in.shape[0]
    xs = x[0]
    target = loss_target[0]
    s, d = xs.shape
    me = 4 * lax.axis_index("x") + 2 * lax.axis_index("y") + lax.axis_index("c")

    assert n_sb >= 1
    w_qkv = _full_cols(_all_gather(_shard_2d(sb_w_qkv), "gather_w_qkv"), sb_w_qkv)
    riding = [(sb_w_o, _full_rows), (hg_w_in, _full_cols), (hg_w_o, _full_rows), (mlp_w1, _full_cols), (mlp_w2, _full_rows)]
    w_o = w_in = w_ho = w_1 = w_2 = None
    n_gain_rows = 2 * depth
    small_rows = -(-(n_gain_rows + n_hg) // 8) * 8
    small = _pad_rows(jnp.concatenate([norm_gains.reshape(n_gain_rows, d_loc), hg_lb_logits], axis=0), small_rows)
    small = _all_gather(small, "gather_small")
    gains_full = jnp.transpose(small[:, :n_gain_rows], (1, 0, 2)).reshape(depth, 2, 1, d)
    logits_full = jnp.transpose(small[:, n_gain_rows:n_gain_rows + n_hg], (1, 0, 2)).reshape(n_hg, d)
    lower = _lower_bounds(logits_full, "lower_bounds")

    saved = []
    cur = xs
    for layer in range(depth):
        j = layer // 2
        h = _rmsnorm(cur, gains_full[layer, 0], f"norm_mix_{layer}")
        if layer % 2 == 0:
            qkv = _matmul("nn", h, w_qkv[j], f"qkv_{layer}", [F32])
            qk_gains = _qk_gain_table(sb_q_gain[j], sb_k_gain[j], d)
            qkvn = _qknorm(qkv, qk_gains, f"qknorm_{layer}")
            if layer == 0:
                o, runs, got = _sb_attention_fwd(qkvn, f"sb_fwd_{layer}", [_shard_2d(w) for w, _ in riding])
                w_o, w_in, w_ho, w_1, w_2 = [full(g, w) for g, (w, full) in zip(got, riding)]
            else:
                o, runs, _ = _sb_attention_fwd(qkvn, f"sb_fwd_{layer}")
            mix = (qkv, qk_gains, qkvn, o, runs)
            x1 = _matmul("nn", o, w_o[j], f"sb_out_{layer}", [F32], _ep_add, [cur])
        else:
            proj = _matmul("nn", h, w_in[j], f"hg_in_{layer}", [F32])
            og, o, states = _hgrn2_fwd(proj, lower[j:j + 1], hg_norm_gain[j:j + 1], f"hg_fwd_{layer}")
            mix = (proj, og, o, states)
            x1 = _matmul("nn", og, w_ho[j], f"hg_out_{layer}", [F32], _ep_add, [cur])
        h2 = _rmsnorm(x1, gains_full[layer, 1], f"norm_mlp_{layer}")
        a, u = _matmul("nn", h2, w_1[layer], f"mlp_up_{layer}", [F32, BF16], _ep_relu2)
        x2 = _matmul("nn", u, w_2[layer], f"mlp_down_{layer}", [F32], _ep_add, [x1])
        saved.append((cur, h, mix, x1, h2, a, u))
        cur = x2

    loss_tile, dx, dxb = _loss_head(cur, target, "loss_head")
    loss = lax.psum(loss_tile[0, 0], AXES)

    d_gains = [[None, None] for _ in range(depth)]
    d_qk, d_lb, d_hgain = [None] * n_sb, [None] * n_hg, [None] * n_hg
    received = {"sb_w_qkv": [None] * n_sb, "sb_w_o": [None] * n_sb, "hg_w_in": [None] * n_hg,
                "hg_w_o": [None] * n_hg, "mlp_w1": [None] * depth, "mlp_w2": [None] * depth}
    pending = []

    def settle(got):
        for (wname, idx, _), arrived in zip(pending, got):
            received[wname][idx] = arrived
        pending.clear()

    for layer in reversed(range(depth)):
        j = layer // 2
        x0, h, mix, x1, h2, a, u = saved[layer]
        pending.append(("mlp_w2", layer, _parts_rows([_matmul("tn", u, dxb, f"d_mlp_w2_{layer}", [BF16])])))
        da = _matmul("nt", dxb, w_2[layer], f"d_mlp_act_{layer}", [BF16], _ep_relu2_bwd, [a])
        pending.append(("mlp_w1", layer, _parts_cols([_matmul("tn", h2, da, f"d_mlp_w1_{layer}", [BF16])])))
        dh2 = _matmul("nt", da, w_1[layer], f"d_mlp_in_{layer}", [F32])
        dx, dxb, d_gains[layer][1] = _rmsnorm_bwd(x1, gains_full[layer, 1], dh2, dx, f"d_norm_mlp_{layer}")
        if layer % 2 == 0:
            qkv, qk_gains, qkvn, o, runs = mix
            pending.append(("sb_w_o", j, _parts_rows([_matmul("tn", o, dxb, f"d_sb_w_o_{layer}", [BF16])])))
            do = _matmul("nt", dxb, w_o[j], f"d_sb_o_{layer}", [F32])
            dq, dk, dv, got = _sb_attention_bwd(qkvn, do, runs, f"sb_bwd_{layer}", [p for _, _, p in pending])
            settle(got)
            dqkv, d_qk[j] = _qknorm_bwd(qkv, dq, dk, dv, qk_gains, f"d_qknorm_{layer}")
            pending.append(("sb_w_qkv", j, _parts_cols([_matmul("tn", h, dqkv, f"d_sb_w_qkv_{layer}", [BF16])])))
            dh = _matmul("nt", dqkv, w_qkv[j], f"d_sb_in_{layer}", [F32])
        else:
            proj, og, o, states = mix
            pending.append(("hg_w_o", j, _parts_rows([_matmul("tn", og, dxb, f"d_hg_w_o_{layer}", [BF16])])))
            dog = _matmul("nt", dxb, w_ho[j], f"d_hg_o_{layer}", [F32])
            dq, df, di, dg, d_lb[j], d_hgain[j] = _hgrn2_bwd(
                proj, lower[j:j + 1], hg_norm_gain[j:j + 1], o, states, dog, f"hg_bwd_{layer}")
            dproj = jnp.concatenate([dq, df, di, dg], axis=1)
            pending.append(("hg_w_in", j, _parts_cols([_matmul("tn", h, dproj, f"d_hg_w_in_{layer}", [BF16])])))
            dh = _matmul("nt", dproj, w_in[j], f"d_hg_in_{layer}", [F32])
        dx, dxb, d_gains[layer][0] = _rmsnorm_bwd(x0, gains_full[layer, 0], dh, dx, f"d_norm_mix_{layer}")
    grad_x = dx[None]
    if pending:
        settle(_transfer("exchange", [p for _, _, p in pending], "exchange_tail"))

    def update(wname, w, m, v):
        shape = w.shape
        flat = (shape[0] * shape[1], shape[2])
        res = _adamw(received[wname], w.reshape(flat), m.reshape(flat), v.reshape(flat), "adamw_" + wname)
        return [r.reshape(shape) for r in res]

    big = {
        "sb_w_qkv": update("sb_w_qkv", sb_w_qkv, m_sb_w_qkv, v_sb_w_qkv),
        "sb_w_o": update("sb_w_o", sb_w_o, m_sb_w_o, v_sb_w_o),
        "hg_w_in": update("hg_w_in", hg_w_in, m_hg_w_in, v_hg_w_in),
        "hg_w_o": update("hg_w_o", hg_w_o, m_hg_w_o, v_hg_w_o),
        "mlp_w1": update("mlp_w1", mlp_w1, m_mlp_w1, v_mlp_w1),
        "mlp_w2": update("mlp_w2", mlp_w2, m_mlp_w2, v_mlp_w2),
    }

    d_gain_rows = jnp.concatenate([d_gains[l][t] for l in range(depth) for t in range(2)], axis=0)
    d_lb_rows = jnp.concatenate(d_lb, axis=0)
    def fold(t):
        return jnp.sum(t.reshape(d // SB_HEAD_DIM, SB_HEAD_DIM), axis=0, keepdims=True)
    d_qg = jnp.concatenate([fold(d_qk[i][0]) for i in range(n_sb)], axis=0) * SB_SCALE
    d_kg = jnp.concatenate([fold(d_qk[i][1]) for i in range(n_sb)], axis=0)
    d_hg = jnp.concatenate([jnp.sum(d_hgain[i], axis=0) for i in range(n_hg)], axis=0)
    per_row = d // LANES
    packed = jnp.concatenate([
        d_gain_rows.reshape(n_gain_rows * per_row, LANES), d_lb_rows.reshape(n_hg * per_row, LANES),
        jnp.concatenate([d_qg, d_kg], axis=1), d_hg], axis=0)
    n_packed = packed.shape[0]
    packed = _pad_rows(packed, -(-n_packed // 8) * 8)
    everyone = _all_gather(packed, "gather_small_grads")
    o_lb = n_gain_rows * per_row
    o_qk = o_lb + n_hg * per_row
    o_hg = o_qk + n_sb

    def mine_of(rows, count):
        return lax.dynamic_slice_in_dim(rows.reshape(N_DEV, count, per_row, LANES), me, 1, axis=2)[:, :, 0]

    d_logits_full = _lower_bounds_bwd(logits_full, everyone[:, o_lb:o_qk].reshape(N_DEV, n_hg, d), "lower_bounds_bwd")
    d_logits_mine = lax.dynamic_slice_in_dim(d_logits_full.reshape(n_hg, per_row, LANES), me, 1, axis=1)[:, 0]
    zeros7 = jnp.zeros((N_DEV - 1, n_hg, LANES), F32)
    small_parts = jnp.concatenate([
        mine_of(everyone[:, :o_lb], n_gain_rows),
        jnp.concatenate([d_logits_mine[None], zeros7], axis=0),
        everyone[:, o_qk:o_hg], everyone[:, o_hg:o_hg + n_hg]], axis=1)
    rows_small = small_parts.shape[1]
    pad_to = -(-rows_small // 8) * 8
    small_parts = jnp.concatenate([small_parts, jnp.zeros((N_DEV, pad_to - rows_small, LANES), F32)], axis=1)

    def pack_small(ng, qg, kg, lbl, hgn):
        return _pad_rows(jnp.concatenate([
            ng.reshape(n_gain_rows, d_loc), lbl, jnp.concatenate([qg, kg], axis=1), hgn], axis=0), pad_to)

    res = _adamw([small_parts],
                 pack_small(norm_gains, sb_q_gain, sb_k_gain, hg_lb_logits, hg_norm_gain),
                 pack_small(m_norm_gains, m_sb_q_gain, m_sb_k_gain, m_hg_lb_logits, m_hg_norm_gain),
                 pack_small(v_norm_gains, v_sb_q_gain, v_sb_k_gain, v_hg_lb_logits, v_hg_norm_gain), "adamw_small")

    def unpack_small(t):
        o1 = n_gain_rows
        o2 = o1 + n_hg
        o3 = o2 + n_sb
        return {"norm_gains": t[:o1].reshape(depth, 2, d_loc), "hg_lb_logits": t[o1:o2],
                "sb_q_gain": t[o2:o3, :SB_HEAD_DIM], "sb_k_gain": t[o2:o3, SB_HEAD_DIM:],
                "hg_norm_gain": t[o3:o3 + n_hg]}

    small_out = [unpack_small(t) for t in res]
    order = ["norm_gains", "sb_w_qkv", "sb_q_gain", "sb_k_gain", "sb_w_o", "hg_w_in", "hg_lb_logits",
             "hg_norm_gain", "hg_w_o", "mlp_w1", "mlp_w2"]
    outs = [loss, grad_x]
    for kind in range(4):
        outs += [big[n][kind] if n in big else small_out[kind][n] for n in order]
    return tuple(outs)
```

```python
import functools
import math

import numpy as np
import jax
import jax.numpy as jnp
from jax import lax
from jax.experimental import pallas as pl
from jax.experimental.pallas import tpu as pltpu

F32 = jnp.float32
BF16 = jnp.bfloat16
EPS = 1e-6
SB_HEAD_DIM = 64
HG_DIM = 128
LANES = 128
N_DEV = 8
AXES = ("x", "y", "c")
VMEM_LIMIT_BYTES = 48 * 1024 * 1024
MATMUL_VMEM_BUDGET = 40 * 1024 * 1024
SB_SCALE = 1.0 / math.sqrt(SB_HEAD_DIM)
ATT_BLOCK = 256
SB_DEAD = 104.0
SB_NEVER = -1e30
HG_CHUNK = 64
HG_STEP_CHUNKS = 4
HG_STEP_HEADS = 2
ADAM_LR, ADAM_B1, ADAM_B2, ADAM_EPS, ADAM_WD, ADAM_STEP = 0.001, 0.9, 0.999, 1e-08, 0.01, 10


def _call(body, **kw):
    return pl.pallas_call(body, **kw)


def _sds(shape, dtype):
    return jax.ShapeDtypeStruct(tuple(shape), dtype)


def _cparams(*sem):
    return pltpu.CompilerParams(dimension_semantics=sem or None, vmem_limit_bytes=VMEM_LIMIT_BYTES)


def _split_bf16(x):
    hi = x.astype(BF16)
    lo = (x - hi.astype(F32)).astype(BF16)
    return hi, lo


def _dot(a, b, dims):
    return lax.dot_general(a, b, (dims, ((), ())), preferred_element_type=F32)


NN = ((1,), (0,))
NT = ((1,), (1,))
TN = ((0,), (0,))


def _dot2(x, m, dims):
    hi, lo = _split_bf16(x)
    return _dot(hi, m, dims) + _dot(lo, m, dims)


def _mdot2(m, x, dims):
    hi, lo = _split_bf16(x)
    return _dot(m, hi, dims) + _dot(m, lo, dims)


def _matmul_tiles(m, n, k, a_dtype, b_dtype, io_dtypes):
    tm, tn = min(m, 1024), min(n, 1024)

    def need(tm, tn):
        blocks = tm * k * jnp.dtype(a_dtype).itemsize + tn * k * jnp.dtype(b_dtype).itemsize
        blocks += sum(tm * tn * jnp.dtype(dt).itemsize for dt in io_dtypes)
        return 2 * blocks + 2 * tm * tn * 4

    while need(tm, tn) > MATMUL_VMEM_BUDGET:
        if tm >= tn and tm > 256:
            tm //= 2
        else:
            tn //= 2
    return tm, tn


def _matmul(kind, a, b, name, out_dtypes, epilogue=None, extras=()):
    if kind == "nn":
        (m, k), n = a.shape, b.shape[1]
    elif kind == "nt":
        (m, k), n = a.shape, b.shape[0]
    else:
        (k, m), n = a.shape, b.shape[1]
    tm, tn = _matmul_tiles(m, n, k, a.dtype, b.dtype, list(out_dtypes) + [e.dtype for e in extras])
    assert m % tm == 0 and n % tn == 0, (name, a.shape, b.shape)
    a_spec = pl.BlockSpec((k, tm), lambda i, j: (0, i)) if kind == "tn" else pl.BlockSpec((tm, k), lambda i, j: (i, 0))
    b_spec = pl.BlockSpec((tn, k), lambda i, j: (j, 0)) if kind == "nt" else pl.BlockSpec((k, tn), lambda i, j: (0, j))
    o_spec = pl.BlockSpec((tm, tn), lambda i, j: (i, j))
    dims = {"nn": NN, "nt": NT, "tn": TN}[kind]
    n_ex = len(extras)

    def body(*refs):
        a_ref, b_ref = refs[:2]
        ex, outs = refs[2:2 + n_ex], refs[2 + n_ex:]
        acc = _dot(a_ref[...].astype(BF16), b_ref[...].astype(BF16), dims)
        res = epilogue(acc, *[e[...] for e in ex]) if epilogue is not None else (acc,)
        for o_ref, r in zip(outs, res):
            o_ref[...] = r.astype(o_ref.dtype)

    out = _call(
        body, name=name, grid=(m // tm, n // tn),
        in_specs=[a_spec, b_spec] + [o_spec] * n_ex,
        out_specs=[o_spec] * len(out_dtypes),
        out_shape=[_sds((m, n), dt) for dt in out_dtypes],
        compiler_params=_cparams("parallel", "parallel"),
    )(a, b, *extras)
    return out if len(out_dtypes) > 1 else out[0]


def _ep_add(acc, res):
    return (acc + res,)


def _ep_relu2(acc):
    r = jnp.maximum(acc, 0.0)
    return acc, r * r


def _ep_relu2_bwd(acc, a):
    return (acc * (2.0 * jnp.maximum(a, 0.0)),)


def _rmsnorm(x, g, name):
    s, d = x.shape
    tm = min(s, 512)

    def body(x_ref, g_ref, h_ref):
        xv = x_ref[...]
        r = lax.rsqrt(jnp.mean(xv * xv, axis=-1, keepdims=True) + EPS)
        h_ref[...] = (xv * r * g_ref[...]).astype(BF16)

    return _call(
        body, name=name, grid=(s // tm,),
        in_specs=[pl.BlockSpec((tm, d), lambda i: (i, 0)), pl.BlockSpec((1, d), lambda i: (0, 0))],
        out_specs=pl.BlockSpec((tm, d), lambda i: (i, 0)),
        out_shape=_sds((s, d), BF16), compiler_params=_cparams("parallel"),
    )(x, g)


def _rmsnorm_bwd(x, g, dh, dres, name):
    s, d = x.shape
    tm = min(s, 512)

    def body(x_ref, g_ref, dh_ref, dres_ref, dx_ref, dxb_ref, dg_ref):
        xv, dhv = x_ref[...], dh_ref[...]
        r = lax.rsqrt(jnp.mean(xv * xv, axis=-1, keepdims=True) + EPS)
        xr = xv * r
        t = dhv * g_ref[...]
        dx = dres_ref[...] + r * (t - xr * jnp.mean(t * xr, axis=-1, keepdims=True))
        dx_ref[...] = dx
        dxb_ref[...] = dx.astype(BF16)

        @pl.when(pl.program_id(0) == 0)
        def _():
            dg_ref[...] = jnp.zeros_like(dg_ref)

        dg_ref[...] += jnp.sum(dhv * xr, axis=0, keepdims=True)

    row = pl.BlockSpec((tm, d), lambda i: (i, 0))
    vec = pl.BlockSpec((1, d), lambda i: (0, 0))
    return _call(
        body, name=name, grid=(s // tm,), in_specs=[row, vec, row, row], out_specs=[row, row, vec],
        out_shape=[_sds((s, d), F32), _sds((s, d), BF16), _sds((1, d), F32)],
        compiler_params=_cparams("arbitrary"),
    )(x, g, dh, dres)


def _loss_head(y, target, name):
    s, d = y.shape
    tm = min(s, 512)

    def body(y_ref, t_ref, loss_ref, dy_ref, dyb_ref):
        err = y_ref[...] - t_ref[...]
        dy = err * (1.0 / d)
        dy_ref[...] = dy
        dyb_ref[...] = dy.astype(BF16)

        @pl.when(pl.program_id(0) == 0)
        def _():
            loss_ref[...] = jnp.zeros_like(loss_ref)

        part = 0.5 * jnp.sum(jnp.mean(err * err, axis=-1, keepdims=True), axis=0, keepdims=True)
        loss_ref[...] += part

    row = pl.BlockSpec((tm, d), lambda i: (i, 0))
    return _call(
        body, name=name, grid=(s // tm,), in_specs=[row, row],
        out_specs=[pl.BlockSpec((8, LANES), lambda i: (0, 0)), row, row],
        out_shape=[_sds((8, LANES), F32), _sds((s, d), F32), _sds((s, d), BF16)],
        compiler_params=_cparams("arbitrary"),
    )(y, target)


def _head_lane_mask():
    lane = lax.broadcasted_iota(jnp.int32, (1, LANES), 1)
    return lane < SB_HEAD_DIM


def _pair_rms(xv, first):
    x2 = xv * xv
    s0 = jnp.sum(jnp.where(first, x2, 0.0), axis=-1, keepdims=True)
    s1 = jnp.sum(jnp.where(first, 0.0, x2), axis=-1, keepdims=True)
    inv = 1.0 / SB_HEAD_DIM
    return jnp.where(first, lax.rsqrt(s0 * inv + EPS), lax.rsqrt(s1 * inv + EPS))


def _pair_mean(t, first):
    s0 = jnp.sum(jnp.where(first, t, 0.0), axis=-1, keepdims=True)
    s1 = jnp.sum(jnp.where(first, 0.0, t), axis=-1, keepdims=True)
    return jnp.where(first, s0, s1) * (1.0 / SB_HEAD_DIM)


def _qk_gain_table(q_gain, k_gain, d):
    reps = d // SB_HEAD_DIM
    return jnp.stack([jnp.tile(q_gain, reps) * SB_SCALE, jnp.tile(k_gain, reps), jnp.ones((d,), F32)])[:, None, :]


QKNORM_ROWS = 256


def _qknorm(qkv, gains, name):
    s, d3 = qkv.shape
    d = d3 // 3
    tm = min(s, QKNORM_ROWS)

    def body(x_ref, g_ref, o_ref):
        first = _head_lane_mask()
        is_v = pl.program_id(0) == 2
        for c in range(d // LANES):
            cols = slice(c * LANES, (c + 1) * LANES)
            xv = x_ref[:, cols]
            normed = xv * _pair_rms(xv, first) * g_ref[0, :, cols]
            o_ref[:, cols] = jnp.where(is_v, xv, normed).astype(BF16)

    tile = pl.BlockSpec((tm, d), lambda c, i: (i, c))
    return _call(
        body, name=name, grid=(3, s // tm),
        in_specs=[tile, pl.BlockSpec((1, 1, d), lambda c, i: (c, 0, 0))], out_specs=tile,
        out_shape=_sds((s, d3), BF16), compiler_params=_cparams("parallel", "parallel"),
    )(qkv, gains)


def _qknorm_bwd(qkv, dq, dk, dv, gains, name):
    s, d3 = qkv.shape
    d = d3 // 3
    tm = min(s, QKNORM_ROWS)

    def body(x_ref, dq_ref, dk_ref, dv_ref, g_ref, dx_ref, dg_ref):
        c, i = pl.program_id(0), pl.program_id(1)

        @pl.when(i == 0)
        def _():
            dg_ref[...] = jnp.zeros_like(dg_ref)

        first = _head_lane_mask()
        for col in range(d // LANES):
            cols = slice(col * LANES, (col + 1) * LANES)
            xv = x_ref[:, cols]
            dy = jnp.where(c == 0, dq_ref[:, cols], jnp.where(c == 1, dk_ref[:, cols], dv_ref[:, cols]))
            r = _pair_rms(xv, first)
            xr = xv * r
            t = dy * g_ref[0, :, cols]
            dx = r * (t - xr * _pair_mean(t * xr, first))
            dx_ref[:, cols] = jnp.where(c == 2, dy, dx).astype(BF16)
            dg_ref[0, :, cols] += jnp.sum(dy * xr, axis=0, keepdims=True)

    tile = pl.BlockSpec((tm, d), lambda c, i: (i, c))
    vec = pl.BlockSpec((1, 1, d), lambda c, i: (c, 0, 0))

    def part(kind):
        return pl.BlockSpec((tm, d), lambda c, i: (jnp.where(c == kind, i, 0), 0))

    return _call(
        body, name=name, grid=(3, s // tm), in_specs=[tile, part(0), part(1), part(2), vec], out_specs=[tile, vec],
        out_shape=[_sds((s, d3), BF16), _sds((3, 1, d), F32)],
        compiler_params=_cparams("arbitrary", "arbitrary"),
    )(qkv, dq, dk, dv, gains)


def _softplus_parts(z):
    sp = jnp.maximum(z, 0.0) + jnp.log(1.0 + jnp.exp(-jnp.abs(z)))
    return sp, z - sp


def _diag_causal(tb):
    return lax.broadcasted_iota(jnp.int32, (tb, tb), 1) < lax.broadcasted_iota(jnp.int32, (tb, tb), 0)


def _later_keys(tb):
    r = lax.broadcasted_iota(jnp.int32, (tb, tb), 0)
    c = lax.broadcasted_iota(jnp.int32, (tb, tb), 1)
    return jnp.where(r > c, 1.0, 0.0).astype(BF16)


def _sb_scores(qa, kj, causal):
    sp, logsig = _softplus_parts(_dot(qa, kj, NT))
    return (-sp if causal is None else jnp.where(causal, -sp, 0.0)), logsig


def _sb_weights(stay, logsig, run, later, causal):
    w = jnp.exp(logsig + _dot2(stay, later, NN) + run)
    return w if causal is None else jnp.where(causal, w, 0.0)


def _sb_attention_fwd(qkvn, name, riders=()):
    s, d3 = qkvn.shape
    d = d3 // 3
    pairs, tb = d // LANES, min(ATT_BLOCK, s)
    nb = s // tb
    assert nb <= LANES
    n_r = len(riders)
    ride_shapes, ride_scratch = _rider_shapes("gather", riders)

    def body(*refs):
        q_ref, k_ref, v_ref = refs[:3]
        o_ref, runs_ref = refs[3 + n_r:5 + n_r]
        ride_end = _ride("gather", refs[3:3 + n_r], refs[5 + n_r:5 + 2 * n_r], refs[5 + 2 * n_r:], (pairs, nb))
        i = pl.program_id(1)
        first = _head_lane_mask()
        lane = lax.broadcasted_iota(jnp.int32, (1, LANES), 1)
        later = _later_keys(tb)
        q2 = q_ref[...]
        qs = (jnp.where(first, q2, jnp.zeros_like(q2)), jnp.where(first, jnp.zeros_like(q2), q2))

        def tiles(js, carry, causals):
            kvs = []
            for j in js:
                rows = pl.ds(pl.multiple_of(j * tb, tb), tb)
                kvs.append((k_ref[rows, :], v_ref[rows, :]))
            scores = [[_sb_scores(qs[h], kj, causals[t]) for h in range(2)] for t, (kj, _) in enumerate(kvs)]
            run = [carry[0][0], carry[1][0]]
            acc = [carry[0][1], carry[1][1]]
            runs = [carry[0][2], carry[1][2]]
            weights = []
            for t, j in enumerate(js):
                weights.append([_sb_weights(*scores[t][h], run[h], later, causals[t]) for h in range(2)])
                for h in range(2):
                    runs[h] = jnp.where(lane == j, run[h], runs[h])
                    run[h] = run[h] + jnp.sum(scores[t][h][0], axis=-1, keepdims=True)
            for t, (_, vj) in enumerate(kvs):
                for h in range(2):
                    acc[h] = acc[h] + _dot(weights[t][h].astype(BF16), vj, NN)
            return tuple((run[h], acc[h], runs[h]) for h in range(2))

        def alive(carry):
            return jnp.maximum(jnp.max(carry[0][0]), jnp.max(carry[1][0])) >= -SB_DEAD

        never = jnp.full((tb, LANES), SB_NEVER, F32)
        zero = (jnp.zeros((tb, 1), F32), jnp.zeros((tb, LANES), F32), never)
        diag = _diag_causal(tb)
        carry = lax.cond(i == 0, lambda: tiles([i], (zero, zero), [diag]),
                         lambda: tiles([i, i - 1], (zero, zero), [diag, None]))
        left = jnp.maximum(i - 1, 0)
        _, carry = lax.while_loop(
            lambda st: (st[0] < left // 2) & alive(st[1]),
            lambda st: (st[0] + 1, tiles([i - 2 - 2 * st[0], i - 3 - 2 * st[0]], st[1], [None, None])),
            (jnp.int32(0), carry))
        carry = lax.cond((left % 2 == 1) & alive(carry), lambda c: tiles([0], c, [None]), lambda c: c, carry)
        o_ref[...] = jnp.where(first, carry[0][1], carry[1][1])
        runs_ref[0] = carry[0][2]
        runs_ref[1] = carry[1][2]
        ride_end()

    out = _call(
        body, name=name, grid=(pairs, nb),
        in_specs=[pl.BlockSpec((tb, LANES), lambda p, i: (i, p)),
                  pl.BlockSpec((s, LANES), lambda p, i: (0, pairs + p)),
                  pl.BlockSpec((s, LANES), lambda p, i: (0, 2 * pairs + p))] + [_ANY] * n_r,
        out_specs=[pl.BlockSpec((tb, LANES), lambda p, i: (i, p)),
                   pl.BlockSpec((2, tb, LANES), lambda p, i: (p * nb + i, 0, 0))] + [_ANY] * n_r,
        out_shape=[_sds((s, d), F32), _sds((pairs * nb * 2, tb, LANES), F32)] + ride_shapes,
        scratch_shapes=ride_scratch,
        compiler_params=_cparams("arbitrary", "arbitrary"),
    )(qkvn, qkvn, qkvn, *riders)
    return out[0], out[1], list(out[2:])


def _sb_attention_bwd(qkvn, do, runs, name, riders=()):
    s, d3 = qkvn.shape
    d = d3 // 3
    pairs, tb = d // LANES, min(ATT_BLOCK, s)
    nb = s // tb
    n_r = len(riders)
    ride_shapes, ride_scratch = _rider_shapes("exchange", riders)

    def body(*refs):
        q_ref, k_ref, v_ref, do_ref, runs_ref = refs[:5]
        dq_ref, dk_ref, dv_ref = refs[5 + n_r:8 + n_r]
        ride_end = _ride("exchange", refs[5:5 + n_r], refs[8 + n_r:8 + 2 * n_r], refs[8 + 2 * n_r:], (pairs, nb))
        i = pl.program_id(1)

        @pl.when(i == 0)
        def _():
            dk_ref[...] = jnp.zeros_like(dk_ref)
            dv_ref[...] = jnp.zeros_like(dv_ref)

        first = _head_lane_mask()
        lane = lax.broadcasted_iota(jnp.int32, (1, LANES), 1)
        later = _later_keys(tb)
        q2, do2 = q_ref[...], do_ref[...].astype(BF16)
        zq = jnp.zeros_like(q2)
        qs = (jnp.where(first, q2, zq), jnp.where(first, zq, q2))
        dos = (jnp.where(first, do2, zq), jnp.where(first, zq, do2))

        def tiles(js, carry, causals):
            rows, kv = [], []
            for j in js:
                r = pl.ds(pl.multiple_of(j * tb, tb), tb)
                kj, vj = k_ref[r, :], v_ref[r, :]
                zk = jnp.zeros_like(kj)
                rows.append(r)
                kv.append([(jnp.where(first, kj, zk), jnp.where(first, vj, zk)),
                           (jnp.where(first, zk, kj), jnp.where(first, zk, vj))])
            pairs_th = [(t, h) for t in range(len(js)) for h in range(2)]
            scores = {(t, h): _sb_scores(qs[h], kv[t][h][0], causals[t]) for t, h in pairs_th}
            w, g = {}, {}
            for t, h in pairs_th:
                run = jnp.sum(jnp.where(lane == js[t], runs_ref[h], 0.0), axis=-1, keepdims=True)
                w[t, h] = _sb_weights(*scores[t, h], run, later, causals[t])
                g[t, h] = w[t, h] * _dot(dos[h], kv[t][h][1], NT)
            gsum = [carry[0], carry[1]]
            dz = {}
            for t, h in pairs_th:
                before = _dot(g[t, h].astype(BF16), later, NT) + gsum[h]
                gsum[h] = gsum[h] + jnp.sum(g[t, h], axis=-1, keepdims=True)
                sig = jnp.exp(scores[t, h][1])
                d = g[t, h] * (1.0 - sig) - before * sig
                dz[t, h] = (d if causals[t] is None else jnp.where(causals[t], d, 0.0)).astype(BF16)
            dq = carry[2]
            for t, h in pairs_th:
                dq = dq + _dot(dz[t, h], kv[t][h][0], NN)
            for t in range(len(js)):
                dk_ref[rows[t], :] += _dot(dz[t, 0], qs[0], TN) + _dot(dz[t, 1], qs[1], TN)
                dv_ref[rows[t], :] += _dot(w[t, 0].astype(BF16), dos[0], TN) + _dot(w[t, 1].astype(BF16), dos[1], TN)
            return gsum[0], gsum[1], dq

        left = jnp.maximum(i - 1, 0)
        reach = jnp.max(jnp.maximum(runs_ref[0], runs_ref[1]), axis=0, keepdims=True)
        start = jnp.sum(jnp.where((reach < -SB_DEAD) & (lane < left), 1.0, 0.0)).astype(jnp.int32)
        carry = (jnp.zeros((tb, 1), F32), jnp.zeros((tb, 1), F32), jnp.zeros((tb, LANES), F32))
        live = left - start
        carry = lax.fori_loop(
            0, live // 2, lambda jj, c: tiles([start + 2 * jj, start + 2 * jj + 1], c, [None, None]), carry)
        carry = lax.cond(live % 2 == 1, lambda c: tiles([left - 1], c, [None]), lambda c: c, carry)
        diag = _diag_causal(tb)
        carry = lax.cond(i == 0, lambda c: tiles([i], c, [diag]), lambda c: tiles([i - 1, i], c, [None, diag]), carry)
        dq_ref[...] = carry[2]
        ride_end()

    q_spec = pl.BlockSpec((tb, LANES), lambda p, i: (i, p))
    out = _call(
        body, name=name, grid=(pairs, nb),
        in_specs=[q_spec,
                  pl.BlockSpec((s, LANES), lambda p, i: (0, pairs + p)),
                  pl.BlockSpec((s, LANES), lambda p, i: (0, 2 * pairs + p)),
                  q_spec,
                  pl.BlockSpec((2, tb, LANES), lambda p, i: (p * nb + i, 0, 0))] + [_ANY] * n_r,
        out_specs=[q_spec, pl.BlockSpec((s, LANES), lambda p, i: (0, p)),
                   pl.BlockSpec((s, LANES), lambda p, i: (0, p))] + [_ANY] * n_r,
        out_shape=[_sds((s, d), F32)] * 3 + ride_shapes,
        scratch_shapes=ride_scratch,
        compiler_params=_cparams("arbitrary", "arbitrary"),
    )(qkvn, qkvn, qkvn, do, runs, *riders)
    return out[0], out[1], out[2], list(out[3:])


def _hg_tables(c):
    t = np.arange(c)[:, None]
    j = np.arange(c)[None, :]
    sums = [j <= t, j > t]
    masks = []
    m = c // 2
    while m >= 1:
        pos, base = t % (2 * m), t - t % (2 * m)
        sums.append((pos >= m) & (j >= base + m) & (j <= t))
        sums.append((pos < m) & (j > t) & (j <= base + m - 1))
        masks.append((t // (2 * m) == j // (2 * m)) & (t % (2 * m) >= m) & (j % (2 * m) < m))
        m //= 2
    return (jnp.asarray(np.concatenate(sums, 0), BF16), jnp.asarray(np.stack(masks), F32), len(masks))


def _hg_gates(qr, fr, lb):
    sq = jax.nn.sigmoid(qr)
    sg = jax.nn.sigmoid(fr)
    forget = lb + (1.0 - lb) * sg
    return sq, qr * sq, sg, forget, jnp.log(forget), (1.0 - lb) * (1.0 - sg)


def _hg_scores(q, k, x, masks_ref, c, levels):
    eye = (lax.broadcasted_iota(jnp.int32, (c, c), 0) == lax.broadcasted_iota(jnp.int32, (c, c), 1)).astype(F32)
    scores = eye * jnp.sum(q * k, axis=-1, keepdims=True)
    ops = []
    for l in range(levels):
        qm = (q * x[(2 + 2 * l) * c:(3 + 2 * l) * c]).astype(BF16)
        km = (k * x[(3 + 2 * l) * c:(4 + 2 * l) * c]).astype(BF16)
        scores = scores + masks_ref[l] * _dot(qm, km, NT)
        ops.append((qm, km))
    return scores, eye, ops


def _hg_layout(s, d):
    heads, c = d // HG_DIM, min(HG_CHUNK, s)
    nsub = min(HG_STEP_CHUNKS, s // c)
    hp = HG_STEP_HEADS if heads % HG_STEP_HEADS == 0 else 1
    return heads, c, nsub, hp


def _hg_specs(s, d, reverse):
    heads, c, nsub, hp = _hg_layout(s, d)
    rows, width, groups, n_steps = c * nsub, hp * HG_DIM, heads // hp, s // (c * nsub)

    def step(si):
        return n_steps - 1 - si if reverse else si
    proj = [pl.BlockSpec((rows, width), functools.partial(lambda g, si, part: (step(si), part * groups + g), part=p))
            for p in range(4)]
    tile = pl.BlockSpec((rows, width), lambda g, si: (step(si), g))
    lb = pl.BlockSpec((1, width), lambda g, si: (0, g))
    gain = pl.BlockSpec((1, HG_DIM), lambda g, si: (0, 0))
    state = pl.BlockSpec((hp, nsub, HG_DIM, HG_DIM), lambda g, si: (g, step(si), 0, 0))
    return proj, tile, lb, gain, state, (groups, n_steps)


def _hgrn2_fwd(proj, lb, gain, name, riders=()):
    s, d4 = proj.shape
    d = d4 // 4
    heads, c, nsub, hp = _hg_layout(s, d)
    sums, masks, levels = _hg_tables(c)
    n_r = len(riders)
    ride_shapes, ride_scratch = _rider_shapes("gather", riders)
    pspecs, tile, lbs, gs, state, grid = _hg_specs(s, d, False)

    def body(*refs):
        qr_ref, fr_ref, ir_ref, gr_ref, lb_ref, gain_ref, sums_ref, masks_ref = refs[:8]
        og_ref, o_ref, states_ref = refs[8 + n_r:11 + n_r]
        st_ref = refs[11 + 2 * n_r]
        ride_end = _ride("gather", refs[8:8 + n_r], refs[11 + n_r:11 + 2 * n_r], refs[12 + 2 * n_r:], grid)

        @pl.when(pl.program_id(1) == 0)
        def _():
            st_ref[...] = jnp.zeros_like(st_ref)

        lbv, gainv = lb_ref[...], gain_ref[...]
        units = [(ci, hh) for ci in range(nsub) for hh in range(hp)]

        def lanes(hh):
            return slice(hh * HG_DIM, (hh + 1) * HG_DIM)

        pre = []
        for ci in range(nsub):
            rows = slice(ci * c, (ci + 1) * c)
            _, q, _, _, lf, k = _hg_gates(qr_ref[rows, :], fr_ref[rows, :], lbv)
            pre.append((q, k, jnp.exp(_mdot2(sums_ref[...], lf, NN))))
        scores, qh, vb, update = {}, {}, {}, {}
        for ci, hh in units:
            q, k, x = (a[:, lanes(hh)] for a in pre[ci])
            scores[ci, hh] = _hg_scores(q, k, x, masks_ref, c, levels)[0].astype(BF16)
            qh[ci, hh] = (q * x[0:c]).astype(BF16)
            vb[ci, hh] = ir_ref[ci * c:(ci + 1) * c, lanes(hh)].astype(BF16)
            update[ci, hh] = _dot(vb[ci, hh], (k * x[c:2 * c]).astype(BF16), TN)
        intra = {u: _dot(scores[u], vb[u], NN) for u in units}
        for hh in range(hp):
            st = st_ref[hh]
            for ci in range(nsub):
                rows = slice(ci * c, (ci + 1) * c)
                states_ref[hh, ci] = st
                o = _dot(qh[ci, hh], st.astype(BF16), NT) + intra[ci, hh]
                st = st * pre[ci][2][c - 1:c, lanes(hh)] + update[ci, hh]
                o_ref[rows, lanes(hh)] = o
                r = lax.rsqrt(jnp.mean(o * o, axis=-1, keepdims=True) + EPS)
                og_ref[rows, lanes(hh)] = (o * r * gainv * jax.nn.sigmoid(gr_ref[rows, lanes(hh)])).astype(BF16)
            st_ref[hh] = st
        ride_end()

    const = [pl.BlockSpec(sums.shape, lambda g, si: (0, 0)), pl.BlockSpec(masks.shape, lambda g, si: (0, 0, 0))]
    out = _call(
        body, name=name, grid=grid, in_specs=pspecs + [lbs, gs] + const + [_ANY] * n_r,
        out_specs=[tile, tile, state] + [_ANY] * n_r,
        out_shape=[_sds((s, d), BF16), _sds((s, d), F32), _sds((heads, s // c, HG_DIM, HG_DIM), F32)] + ride_shapes,
        scratch_shapes=[pltpu.VMEM((hp, HG_DIM, HG_DIM), F32)] + ride_scratch,
        compiler_params=_cparams("arbitrary", "arbitrary"),
    )(proj, proj, proj, proj, lb, gain, sums, masks, *riders)
    return out[0], out[1], out[2], list(out[3:])


def _hgrn2_bwd(proj, lb, gain, o, states, dog, name, riders=()):
    s, d4 = proj.shape
    d = d4 // 4
    heads, c, nsub, hp = _hg_layout(s, d)
    sums, masks, levels = _hg_tables(c)
    n_r = len(riders)
    ride_shapes, ride_scratch = _rider_shapes("exchange", riders)
    pspecs, tile, lbs, gs, state, grid = _hg_specs(s, d, True)

    def body(*refs):
        (qr_ref, fr_ref, ir_ref, gr_ref, lb_ref, gain_ref, sums_ref, masks_ref, o_ref, states_ref, dog_ref) = refs[:11]
        dq_ref, df_ref, di_ref, dg_ref, dlb_ref, dgain_ref = refs[11 + n_r:17 + n_r]
        dst_ref = refs[17 + 2 * n_r]
        ride_end = _ride("exchange", refs[11:11 + n_r], refs[17 + n_r:17 + 2 * n_r], refs[18 + 2 * n_r:], grid)

        @pl.when(pl.program_id(1) == 0)
        def _():
            dst_ref[...] = jnp.zeros_like(dst_ref)
            dlb_ref[...] = jnp.zeros_like(dlb_ref)
            dgain_ref[...] = jnp.zeros_like(dgain_ref)

        lbv, gainv = lb_ref[...], gain_ref[...]
        units = [(ci, hh) for ci in range(nsub) for hh in range(hp)]

        def lanes(hh):
            return slice(hh * HG_DIM, (hh + 1) * HG_DIM)

        def rows_of(ci):
            return slice(ci * c, (ci + 1) * c)

        pre = []
        for ci in range(nsub):
            qr = qr_ref[rows_of(ci), :]
            sq, q, sg, forget, lf, k = _hg_gates(qr, fr_ref[rows_of(ci), :], lbv)
            pre.append(dict(qr=qr, sq=sq, q=q, sg=sg, forget=forget, k=k, x=jnp.exp(_mdot2(sums_ref[...], lf, NN))))

        dob, vb, sc, qh_f, kh_f, feed = {}, {}, {}, {}, {}, {}
        dgain = [jnp.zeros((1, HG_DIM), F32) for _ in range(hp)]
        for ci, hh in units:
            rows, ln = rows_of(ci), lanes(hh)
            ov, gate = o_ref[rows, ln], jax.nn.sigmoid(gr_ref[rows, ln])
            r = lax.rsqrt(jnp.mean(ov * ov, axis=-1, keepdims=True) + EPS)
            orr = ov * r
            dogv = dog_ref[rows, ln]
            dg_ref[rows, ln] = (dogv * orr * gainv * gate * (1.0 - gate)).astype(BF16)
            don = dogv * gate
            dgain[hh] = dgain[hh] + jnp.sum(don * orr, axis=0, keepdims=True)
            t = don * gainv
            dob[ci, hh] = (r * (t - orr * jnp.mean(t * orr, axis=-1, keepdims=True))).astype(BF16)
            q, k, x = (pre[ci][n][:, ln] for n in ("q", "k", "x"))
            vb[ci, hh] = ir_ref[rows, ln].astype(BF16)
            sc[ci, hh] = _hg_scores(q, k, x, masks_ref, c, levels)
            qh_f[ci, hh], kh_f[ci, hh] = q * x[0:c], k * x[c:2 * c]
            feed[ci, hh] = _dot(dob[ci, hh], qh_f[ci, hh].astype(BF16), TN)

        dsts = {}
        for hh in range(hp):
            dst = dst_ref[hh]
            for ci in reversed(range(nsub)):
                dsts[ci, hh] = dst
                dst = dst * pre[ci]["x"][c - 1:c, lanes(hh)] + feed[ci, hh]
            dst_ref[hh] = dst

        dlb = [jnp.zeros((1, HG_DIM), F32) for _ in range(hp)]
        for ci, hh in units:
            rows, ln = rows_of(ci), lanes(hh)
            p = {n: v[:, ln] for n, v in pre[ci].items()}
            q, k, x = p["q"], p["k"], p["x"]
            scores, eye, ops = sc[ci, hh]
            st, dst = states_ref[hh, ci], dsts[ci, hh]
            dstb = dst.astype(BF16)
            dscores = _dot(dob[ci, hh], vb[ci, hh], NT)
            di_ref[rows, ln] = (_dot(scores.astype(BF16), dob[ci, hh], TN)
                                + _dot(kh_f[ci, hh].astype(BF16), dstb, NT)).astype(BF16)
            dqh = _dot(dob[ci, hh], st.astype(BF16), NN)
            dkh = _dot(vb[ci, hh], dstb, NN)
            decay_grad = x[c - 1:c] * jnp.sum(dst * st, axis=0, keepdims=True)
            ddiag = jnp.sum(eye * dscores, axis=-1, keepdims=True)
            dq = dqh * x[0:c] + ddiag * k
            dk = dkh * x[c:2 * c] + ddiag * q
            dexp = [dqh * qh_f[ci, hh], dkh * kh_f[ci, hh]]
            for l, (qm, km) in enumerate(ops):
                dsm = (masks_ref[l] * dscores).astype(BF16)
                dqm, dkm = _dot(dsm, km, NN), _dot(dsm, qm, TN)
                xq, xk = x[(2 + 2 * l) * c:(3 + 2 * l) * c], x[(3 + 2 * l) * c:(4 + 2 * l) * c]
                dq = dq + dqm * xq
                dk = dk + dkm * xk
                dexp += [dqm * (q * xq), dkm * (k * xk)]
            dlf = _mdot2(sums_ref[...], jnp.concatenate(dexp, axis=0), TN) + decay_grad
            dforget = dlf / p["forget"] - dk
            dlb[hh] = dlb[hh] + jnp.sum(dforget * (1.0 - p["sg"]), axis=0, keepdims=True)
            df_ref[rows, ln] = (dforget * (1.0 - lbv[:, ln]) * p["sg"] * (1.0 - p["sg"])).astype(BF16)
            dq_ref[rows, ln] = (dq * p["sq"] * (1.0 + p["qr"] * (1.0 - p["sq"]))).astype(BF16)
        for hh in range(hp):
            dlb_ref[:, lanes(hh)] += dlb[hh]
            dgain_ref[hh] += dgain[hh]
        ride_end()

    const = [pl.BlockSpec(sums.shape, lambda g, si: (0, 0)), pl.BlockSpec(masks.shape, lambda g, si: (0, 0, 0))]
    out = _call(
        body, name=name, grid=grid, in_specs=pspecs + [lbs, gs] + const + [tile, state, tile] + [_ANY] * n_r,
        out_specs=[tile, tile, tile, tile, lbs, pl.BlockSpec((hp, 1, HG_DIM), lambda g, si: (g, 0, 0))] + [_ANY] * n_r,
        out_shape=[_sds((s, d), BF16)] * 4 + [_sds((1, d), F32), _sds((heads, 1, HG_DIM), F32)] + ride_shapes,
        scratch_shapes=[pltpu.VMEM((hp, HG_DIM, HG_DIM), F32)] + ride_scratch,
        compiler_params=_cparams("arbitrary", "arbitrary"),
    )(proj, proj, proj, proj, lb, gain, sums, masks, o, states, dog, *riders)
    return tuple(out[:6]) + (list(out[6:]),)


def _lower_bounds(logits, name):
    n, d = logits.shape

    def body(l_ref, lb_ref):
        lv = l_ref[...]
        e = jnp.exp(lv - jnp.max(lv, axis=0, keepdims=True))
        p = e / jnp.sum(e, axis=0, keepdims=True)
        run = jnp.zeros((1, d), F32)
        for j in range(n):
            if j > 0:
                run = run + p[j:j + 1]
            lb_ref[j:j + 1, :] = run

    return _call(body, name=name, out_shape=_sds((n, d), F32))(logits)


def _lower_bounds_bwd(logits, dlb_parts, name):
    n, d = logits.shape

    def body(l_ref, dlb_ref, dl_ref):
        lv, dv = l_ref[...], dlb_ref[0]
        for dev in range(1, N_DEV):
            dv = dv + dlb_ref[dev]
        e = jnp.exp(lv - jnp.max(lv, axis=0, keepdims=True))
        p = e / jnp.sum(e, axis=0, keepdims=True)
        run = jnp.zeros((1, d), F32)
        dps = [None] * n
        for j in range(n - 1, 0, -1):
            run = run + dv[j:j + 1]
            dps[j] = run
        dps[0] = jnp.zeros((1, d), F32)
        inner = jnp.zeros((1, d), F32)
        for j in range(n):
            inner = inner + p[j:j + 1] * dps[j]
        for j in range(n):
            dl_ref[j:j + 1, :] = p[j:j + 1] * (dps[j] - inner)

    return _call(body, name=name, out_shape=_sds((n, d), F32))(logits, dlb_parts)


_ANY = pl.BlockSpec(memory_space=pl.ANY)
_MESH = pl.DeviceIdType.MESH


def _gather_stages(x_ref, out_ref, send_sems, recv_sems, local_sem):
    mx, my, mc = lax.axis_index("x"), lax.axis_index("y"), lax.axis_index("c")
    me, sibling = (mx, my, mc), (mx, my, 1 - mc)
    chips = [(1 - mx, my), (mx, 1 - my), (1 - mx, 1 - my)]

    def slot(px, py, pc):
        return out_ref.at[4 * px + 2 * py + pc]

    def copy(k, block, to, src=None):
        return pltpu.make_async_remote_copy(
            src_ref=slot(*block) if src is None else src, dst_ref=slot(*block),
            send_sem=send_sems.at[k], recv_sem=recv_sems.at[k], device_id=to, device_id_type=_MESH)

    mine = pltpu.make_async_copy(x_ref, slot(*me), local_sem)
    first = [copy(0, me, sibling, src=x_ref)] + [copy(1 + j, me, (*chip, mc), src=x_ref) for j, chip in enumerate(chips)]
    passed = [copy(4 + j, (*chip, mc), sibling) for j, chip in enumerate(chips)]

    def start():
        mine.start()
        for cp in first:
            cp.start()

    def middle():
        for j, chip in enumerate(chips):
            copy(1 + j, (*chip, mc), me).wait_recv()
            passed[j].start()

    def finish():
        copy(0, sibling, me).wait_recv()
        for j, chip in enumerate(chips):
            copy(4 + j, (*chip, 1 - mc), me).wait_recv()
        for cp in first + passed:
            cp.wait_send()
        mine.wait()

    return start, middle, finish


def _exchange_stages(g_ref, out_ref, send_sems, recv_sems, local_sem):
    mx, my, mc = lax.axis_index("x"), lax.axis_index("y"), lax.axis_index("c")
    me = 4 * mx + 2 * my + mc
    mine = pltpu.make_async_copy(g_ref.at[me], out_ref.at[me], local_sem)
    copies = []
    for k in range(1, N_DEV):
        px, py, pc = mx ^ (k >> 2), my ^ ((k >> 1) & 1), mc ^ (k & 1)
        peer = 4 * px + 2 * py + pc
        send = pltpu.make_async_remote_copy(
            src_ref=g_ref.at[peer], dst_ref=out_ref.at[me], send_sem=send_sems.at[k - 1],
            recv_sem=recv_sems.at[k - 1], device_id=(px, py, pc), device_id_type=_MESH)
        arrival = pltpu.make_async_remote_copy(
            src_ref=g_ref.at[peer], dst_ref=out_ref.at[peer], send_sem=send_sems.at[k - 1],
            recv_sem=recv_sems.at[k - 1], device_id=(px, py, pc), device_id_type=_MESH)
        copies.append((send, arrival))

    def start():
        mine.start()
        for send, _ in copies:
            send.start()

    def finish():
        for _, arrival in copies:
            arrival.wait_recv()
        for send, _ in copies:
            send.wait_send()
        mine.wait()

    return start, lambda: None, finish


_STAGES = {"gather": _gather_stages, "exchange": _exchange_stages}
SEMS_PER_TRANSFER = 3


def _rider_shapes(kind, arrays):
    outs = [_sds((N_DEV,) + a.shape if kind == "gather" else a.shape, a.dtype) for a in arrays]
    scratch = []
    for _ in arrays:
        scratch += [pltpu.SemaphoreType.DMA((7,)), pltpu.SemaphoreType.DMA((7,)), pltpu.SemaphoreType.DMA]
    return outs, scratch


def _ride(kind, in_refs, out_refs, sems, grid):
    stages = [_STAGES[kind](a, o, *sems[SEMS_PER_TRANSFER * n:SEMS_PER_TRANSFER * (n + 1)])
              for n, (a, o) in enumerate(zip(in_refs, out_refs))]
    if not stages:
        return lambda: None
    p, i = pl.program_id(0), pl.program_id(1)

    def run(stage):
        for st in stages:
            st[stage]()

    pl.when((p == 0) & (i == 0))(lambda: run(0))
    pl.when((p == grid[0] // 2) & (i == 0))(lambda: run(1))
    return lambda: pl.when((p == grid[0] - 1) & (i == grid[1] - 1))(lambda: run(2))


def _transfer(kind, arrays, name):
    outs, scratch = _rider_shapes(kind, arrays)
    n = len(arrays)

    def body(*refs):
        stages = [_STAGES[kind](refs[t], refs[n + t], *refs[2 * n + SEMS_PER_TRANSFER * t:2 * n + SEMS_PER_TRANSFER * (t + 1)])
                  for t in range(n)]
        for stage in range(3):
            for st in stages:
                st[stage]()

    return _call(body, name=name, out_shape=outs, in_specs=[_ANY] * n, out_specs=[_ANY] * n, scratch_shapes=scratch)(*arrays)


def _all_gather(x, name):
    return _transfer("gather", [x], name)[0]


def _adamw(parts, w, m, v, name):
    n_l = len(parts)
    _, r, c = parts[0].shape
    tr = r if r <= 256 else 256
    assert r % tr == 0 and w.shape == (n_l * r, c), (name, parts[0].shape, w.shape)
    steps = r // tr

    def body(*refs):
        p_refs, (w_ref, m_ref, v_ref) = refs[:n_l], refs[n_l:n_l + 3]
        g_ref, d_ref, nm_ref, nv_ref, sum_ref = refs[n_l + 3:]
        for layer in range(n_l):
            @pl.when(pl.program_id(0) == layer)
            def _(p_ref=p_refs[layer]):
                acc = p_ref[0].astype(F32)
                for dev in range(1, N_DEV):
                    acc = acc + p_ref[dev].astype(F32)
                sum_ref[...] = acc

        g = sum_ref[...]
        nm = ADAM_B1 * m_ref[...] + (1.0 - ADAM_B1) * g
        nv = ADAM_B2 * v_ref[...] + (1.0 - ADAM_B2) * (g * g)
        m_hat = nm / (1.0 - ADAM_B1 ** ADAM_STEP)
        v_hat = nv / (1.0 - ADAM_B2 ** ADAM_STEP)
        g_ref[...] = g
        nm_ref[...] = nm
        nv_ref[...] = nv
        d_ref[...] = -ADAM_LR * (m_hat / (jnp.sqrt(v_hat) + ADAM_EPS) + ADAM_WD * w_ref[...])

    tile = pl.BlockSpec((tr, c), lambda l, i: (l * steps + i, 0))

    def part(layer):
        return pl.BlockSpec((N_DEV, tr, c), lambda l, i: (0, jnp.where(l == layer, i, 0), 0))

    return _call(
        body, name=name, grid=(n_l, steps),
        in_specs=[part(layer) for layer in range(n_l)] + [tile, tile, tile], out_specs=[tile] * 4,
        out_shape=[_sds((n_l * r, c), F32)] * 4, scratch_shapes=[pltpu.VMEM((tr, c), F32)],
        compiler_params=_cparams("arbitrary", "arbitrary"),
    )(*parts, w, m, v)


def _shard_2d(w):
    return w.astype(BF16).reshape(w.shape[0] * w.shape[1], w.shape[2])


def _full_cols(g, w):
    l, k, n = w.shape
    g = g.reshape(N_DEV, l, k, n)
    return [jnp.transpose(g[:, i], (1, 0, 2)).reshape(k, N_DEV * n) for i in range(l)]


def _full_rows(g, w):
    l, k, n = w.shape
    g = g.reshape(N_DEV, l, k, n)
    return [g[:, i].reshape(N_DEV * k, n) for i in range(l)]


def _parts_cols(grads):
    k, n8 = grads[0].shape
    g = jnp.stack(grads).reshape(len(grads), k, N_DEV, n8 // N_DEV)
    return jnp.transpose(g, (2, 0, 1, 3)).reshape(N_DEV, len(grads) * k, n8 // N_DEV)


def _parts_rows(grads):
    k8, n = grads[0].shape
    g = jnp.stack(grads).reshape(len(grads), N_DEV, k8 // N_DEV, n)
    return jnp.transpose(g, (1, 0, 2, 3)).reshape(N_DEV, len(grads) * (k8 // N_DEV), n)


def _pad_rows(a, rows):
    return jnp.concatenate([a, jnp.zeros((rows - a.shape[0], a.shape[1]), a.dtype)], axis=0)


def kernel(x, norm_gains, sb_w_qkv, sb_q_gain, sb_k_gain, sb_w_o, hg_w_in, hg_lb_logits, hg_norm_gain, hg_w_o, mlp_w1, mlp_w2, loss_target, m_norm_gains, m_sb_w_qkv, m_sb_q_gain, m_sb_k_gain, m_sb_w_o, m_hg_w_in, m_hg_lb_logits, m_hg_norm_gain, m_hg_w_o, m_mlp_w1, m_mlp_w2, v_norm_gains, v_sb_w_qkv, v_sb_q_gain, v_sb_k_gain, v_sb_w_o, v_hg_w_in, v_hg_lb_logits, v_hg_norm_gain, v_hg_w_o, v_mlp_w1, v_mlp_w2):
    depth, _, d_loc = norm_gains.shape
    n_sb, n_hg = sb_w_qkv.shape[0], hg_w_in.shape[0]
    xs = x[0]
    target = loss_target[0]
    s, d = xs.shape
    me = 4 * lax.axis_index("x") + 2 * lax.axis_index("y") + lax.axis_index("c")

    assert n_sb >= 1
    w_qkv = _full_cols(_all_gather(_shard_2d(sb_w_qkv), "gather_w_qkv"), sb_w_qkv)
    riding = [(sb_w_o, _full_rows), (hg_w_in, _full_cols), (hg_w_o, _full_rows)]
    w_o = w_in = w_ho = None
    w_1s, w_2s = [None] * depth, [None] * depth
    n_gain_rows = 2 * depth
    small_rows = -(-(n_gain_rows + n_hg) // 8) * 8
    small = _pad_rows(jnp.concatenate([norm_gains.reshape(n_gain_rows, d_loc), hg_lb_logits], axis=0), small_rows)
    small = _all_gather(small, "gather_small")
    gains_full = jnp.transpose(small[:, :n_gain_rows], (1, 0, 2)).reshape(depth, 2, 1, d)
    logits_full = jnp.transpose(small[:, n_gain_rows:n_gain_rows + n_hg], (1, 0, 2)).reshape(n_hg, d)
    lower = _lower_bounds(logits_full, "lower_bounds")

    saved = []
    cur = xs
    for layer in range(depth):
        j = layer // 2
        h = _rmsnorm(cur, gains_full[layer, 0], f"norm_mix_{layer}")
        riders = [_shard_2d(mlp_w1[layer:layer + 1]), _shard_2d(mlp_w2[layer:layer + 1])]
        if layer == 0:
            riders += [_shard_2d(w) for w, _ in riding]
        if layer % 2 == 0:
            qkv = _matmul("nn", h, w_qkv[j], f"qkv_{layer}", [F32])
            qk_gains = _qk_gain_table(sb_q_gain[j], sb_k_gain[j], d)
            qkvn = _qknorm(qkv, qk_gains, f"qknorm_{layer}")
            o, runs, got = _sb_attention_fwd(qkvn, f"sb_fwd_{layer}", riders)
            mix = (qkv, qk_gains, qkvn, o, runs)
        else:
            proj = _matmul("nn", h, w_in[j], f"hg_in_{layer}", [F32])
            og, o, states, got = _hgrn2_fwd(proj, lower[j:j + 1], hg_norm_gain[j:j + 1], f"hg_fwd_{layer}", riders)
            mix = (proj, og, o, states)
        w_1 = _full_cols(got[0], mlp_w1[layer:layer + 1])[0]
        w_2s[layer] = w_2 = _full_rows(got[1], mlp_w2[layer:layer + 1])[0]
        w_1s[layer] = w_1
        if layer == 0:
            w_o, w_in, w_ho = [full(g, w) for g, (w, full) in zip(got[2:], riding)]
        if layer % 2 == 0:
            x1 = _matmul("nn", o, w_o[j], f"sb_out_{layer}", [F32], _ep_add, [cur])
        else:
            x1 = _matmul("nn", og, w_ho[j], f"hg_out_{layer}", [F32], _ep_add, [cur])
        h2 = _rmsnorm(x1, gains_full[layer, 1], f"norm_mlp_{layer}")
        a, u = _matmul("nn", h2, w_1, f"mlp_up_{layer}", [F32, BF16], _ep_relu2)
        x2 = _matmul("nn", u, w_2, f"mlp_down_{layer}", [F32], _ep_add, [x1])
        saved.append((cur, h, mix, x1, h2, a, u))
        cur = x2

    loss_tile, dx, dxb = _loss_head(cur, target, "loss_head")
    loss = lax.psum(loss_tile[0, 0], AXES)

    d_gains = [[None, None] for _ in range(depth)]
    d_qk, d_lb, d_hgain = [None] * n_sb, [None] * n_hg, [None] * n_hg
    received = {"sb_w_qkv": [None] * n_sb, "sb_w_o": [None] * n_sb, "hg_w_in": [None] * n_hg,
                "hg_w_o": [None] * n_hg, "mlp_w1": [None] * depth, "mlp_w2": [None] * depth}
    pending = []

    def settle(got):
        for (wname, idx, _), arrived in zip(pending, got):
            received[wname][idx] = arrived
        pending.clear()

    for layer in reversed(range(depth)):
        j = layer // 2
        x0, h, mix, x1, h2, a, u = saved[layer]
        pending.append(("mlp_w2", layer, _parts_rows([_matmul("tn", u, dxb, f"d_mlp_w2_{layer}", [BF16])])))
        da = _matmul("nt", dxb, w_2s[layer], f"d_mlp_act_{layer}", [BF16], _ep_relu2_bwd, [a])
        pending.append(("mlp_w1", layer, _parts_cols([_matmul("tn", h2, da, f"d_mlp_w1_{layer}", [BF16])])))
        dh2 = _matmul("nt", da, w_1s[layer], f"d_mlp_in_{layer}", [F32])
        dx, dxb, d_gains[layer][1] = _rmsnorm_bwd(x1, gains_full[layer, 1], dh2, dx, f"d_norm_mlp_{layer}")
        if layer % 2 == 0:
            qkv, qk_gains, qkvn, o, runs = mix
            pending.append(("sb_w_o", j, _parts_rows([_matmul("tn", o, dxb, f"d_sb_w_o_{layer}", [BF16])])))
            do = _matmul("nt", dxb, w_o[j], f"d_sb_o_{layer}", [F32])
            dq, dk, dv, got = _sb_attention_bwd(qkvn, do, runs, f"sb_bwd_{layer}", [p for _, _, p in pending])
            settle(got)
            dqkv, d_qk[j] = _qknorm_bwd(qkv, dq, dk, dv, qk_gains, f"d_qknorm_{layer}")
            pending.append(("sb_w_qkv", j, _parts_cols([_matmul("tn", h, dqkv, f"d_sb_w_qkv_{layer}", [BF16])])))
            dh = _matmul("nt", dqkv, w_qkv[j], f"d_sb_in_{layer}", [F32])
        else:
            proj, og, o, states = mix
            pending.append(("hg_w_o", j, _parts_rows([_matmul("tn", og, dxb, f"d_hg_w_o_{layer}", [BF16])])))
            dog = _matmul("nt", dxb, w_ho[j], f"d_hg_o_{layer}", [F32])
            dq, df, di, dg, d_lb[j], d_hgain[j], got = _hgrn2_bwd(
                proj, lower[j:j + 1], hg_norm_gain[j:j + 1], o, states, dog, f"hg_bwd_{layer}",
                [p for _, _, p in pending])
            settle(got)
            dproj = jnp.concatenate([dq, df, di, dg], axis=1)
            pending.append(("hg_w_in", j, _parts_cols([_matmul("tn", h, dproj, f"d_hg_w_in_{layer}", [BF16])])))
            dh = _matmul("nt", dproj, w_in[j], f"d_hg_in_{layer}", [F32])
        dx, dxb, d_gains[layer][0] = _rmsnorm_bwd(x0, gains_full[layer, 0], dh, dx, f"d_norm_mix_{layer}")
    grad_x = dx[None]
    if pending:
        settle(_transfer("exchange", [p for _, _, p in pending], "exchange_tail"))

    def update(wname, w, m, v):
        shape = w.shape
        flat = (shape[0] * shape[1], shape[2])
        res = _adamw(received[wname], w.reshape(flat), m.reshape(flat), v.reshape(flat), "adamw_" + wname)
        return [r.reshape(shape) for r in res]

    big = {
        "sb_w_qkv": update("sb_w_qkv", sb_w_qkv, m_sb_w_qkv, v_sb_w_qkv),
        "sb_w_o": update("sb_w_o", sb_w_o, m_sb_w_o, v_sb_w_o),
        "hg_w_in": update("hg_w_in", hg_w_in, m_hg_w_in, v_hg_w_in),
        "hg_w_o": update("hg_w_o", hg_w_o, m_hg_w_o, v_hg_w_o),
        "mlp_w1": update("mlp_w1", mlp_w1, m_mlp_w1, v_mlp_w1),
        "mlp_w2": update("mlp_w2", mlp_w2, m_mlp_w2, v_mlp_w2),
    }

    d_gain_rows = jnp.concatenate([d_gains[l][t] for l in range(depth) for t in range(2)], axis=0)
    d_lb_rows = jnp.concatenate(d_lb, axis=0)
    def fold(t):
        return jnp.sum(t.reshape(d // SB_HEAD_DIM, SB_HEAD_DIM), axis=0, keepdims=True)
    d_qg = jnp.concatenate([fold(d_qk[i][0]) for i in range(n_sb)], axis=0) * SB_SCALE
    d_kg = jnp.concatenate([fold(d_qk[i][1]) for i in range(n_sb)], axis=0)
    d_hg = jnp.concatenate([jnp.sum(d_hgain[i], axis=0) for i in range(n_hg)], axis=0)
    per_row = d // LANES
    packed = jnp.concatenate([
        d_gain_rows.reshape(n_gain_rows * per_row, LANES), d_lb_rows.reshape(n_hg * per_row, LANES),
        jnp.concatenate([d_qg, d_kg], axis=1), d_hg], axis=0)
    n_packed = packed.shape[0]
    packed = _pad_rows(packed, -(-n_packed // 8) * 8)
    everyone = _all_gather(packed, "gather_small_grads")
    o_lb = n_gain_rows * per_row
    o_qk = o_lb + n_hg * per_row
    o_hg = o_qk + n_sb

    def mine_of(rows, count):
        return lax.dynamic_slice_in_dim(rows.reshape(N_DEV, count, per_row, LANES), me, 1, axis=2)[:, :, 0]

    d_logits_full = _lower_bounds_bwd(logits_full, everyone[:, o_lb:o_qk].reshape(N_DEV, n_hg, d), "lower_bounds_bwd")
    d_logits_mine = lax.dynamic_slice_in_dim(d_logits_full.reshape(n_hg, per_row, LANES), me, 1, axis=1)[:, 0]
    zeros7 = jnp.zeros((N_DEV - 1, n_hg, LANES), F32)
    small_parts = jnp.concatenate([
        mine_of(everyone[:, :o_lb], n_gain_rows),
        jnp.concatenate([d_logits_mine[None], zeros7], axis=0),
        everyone[:, o_qk:o_hg], everyone[:, o_hg:o_hg + n_hg]], axis=1)
    rows_small = small_parts.shape[1]
    pad_to = -(-rows_small // 8) * 8
    small_parts = jnp.concatenate([small_parts, jnp.zeros((N_DEV, pad_to - rows_small, LANES), F32)], axis=1)

    def pack_small(ng, qg, kg, lbl, hgn):
        return _pad_rows(jnp.concatenate([
            ng.reshape(n_gain_rows, d_loc), lbl, jnp.concatenate([qg, kg], axis=1), hgn], axis=0), pad_to)

    res = _adamw([small_parts],
                 pack_small(norm_gains, sb_q_gain, sb_k_gain, hg_lb_logits, hg_norm_gain),
                 pack_small(m_norm_gains, m_sb_q_gain, m_sb_k_gain, m_hg_lb_logits, m_hg_norm_gain),
                 pack_small(v_norm_gains, v_sb_q_gain, v_sb_k_gain, v_hg_lb_logits, v_hg_norm_gain), "adamw_small")

    def unpack_small(t):
        o1 = n_gain_rows
        o2 = o1 + n_hg
        o3 = o2 + n_sb
        return {"norm_gains": t[:o1].reshape(depth, 2, d_loc), "hg_lb_logits": t[o1:o2],
                "sb_q_gain": t[o2:o3, :SB_HEAD_DIM], "sb_k_gain": t[o2:o3, SB_HEAD_DIM:],
                "hg_norm_gain": t[o3:o3 + n_hg]}

    small_out = [unpack_small(t) for t in res]
    order = ["norm_gains", "sb_w_qkv", "sb_q_gain", "sb_k_gain", "sb_w_o", "hg_w_in", "hg_lb_logits",
             "hg_norm_gain", "hg_w_o", "mlp_w1", "mlp_w2"]
    outs = [loss, grad_x]
    for kind in range(4):
        outs += [big[n][kind] if n in big else small_out[kind][n] for n in order]
    return tuple(outs)
```

```python
import functools
import math

import numpy as np
import jax
import jax.numpy as jnp
from jax import lax
from jax.experimental import pallas as pl
from jax.experimental.pallas import tpu as pltpu

F32 = jnp.float32
BF16 = jnp.bfloat16
EPS = 1e-6
SB_HEAD_DIM = 64
HG_DIM = 128
LANES = 128
N_DEV = 8
AXES = ("x", "y", "c")
VMEM_LIMIT_BYTES = 48 * 1024 * 1024
MATMUL_VMEM_BUDGET = 40 * 1024 * 1024
SB_SCALE = 1.0 / math.sqrt(SB_HEAD_DIM)
ATT_BLOCK = 256
SB_DEAD = 104.0
SB_NEVER = -1e30
HG_CHUNK = 64
HG_STEP_CHUNKS = 4
HG_STEP_HEADS = 4
ADAM_LR, ADAM_B1, ADAM_B2, ADAM_EPS, ADAM_WD, ADAM_STEP = 0.001, 0.9, 0.999, 1e-08, 0.01, 10


def _call(body, **kw):
    return pl.pallas_call(body, **kw)


def _sds(shape, dtype):
    return jax.ShapeDtypeStruct(tuple(shape), dtype)


def _cparams(*sem):
    return pltpu.CompilerParams(dimension_semantics=sem or None, vmem_limit_bytes=VMEM_LIMIT_BYTES)


def _split_bf16(x):
    hi = x.astype(BF16)
    lo = (x - hi.astype(F32)).astype(BF16)
    return hi, lo


def _dot(a, b, dims):
    return lax.dot_general(a, b, (dims, ((), ())), preferred_element_type=F32)


NN = ((1,), (0,))
NT = ((1,), (1,))
TN = ((0,), (0,))


def _dot2(x, m, dims):
    hi, lo = _split_bf16(x)
    return _dot(hi, m, dims) + _dot(lo, m, dims)


def _mdot2(m, x, dims):
    hi, lo = _split_bf16(x)
    return _dot(m, hi, dims) + _dot(m, lo, dims)


def _matmul_tiles(m, n, k, a_dtype, b_dtype, io_dtypes):
    tm, tn = min(m, 1024), min(n, 1024)

    def need(tm, tn):
        blocks = tm * k * jnp.dtype(a_dtype).itemsize + tn * k * jnp.dtype(b_dtype).itemsize
        blocks += sum(tm * tn * jnp.dtype(dt).itemsize for dt in io_dtypes)
        return 2 * blocks + 2 * tm * tn * 4

    while need(tm, tn) > MATMUL_VMEM_BUDGET:
        if tm >= tn and tm > 256:
            tm //= 2
        else:
            tn //= 2
    return tm, tn


def _matmul(kind, a, b, name, out_dtypes, epilogue=None, extras=()):
    if kind == "nn":
        (m, k), n = a.shape, b.shape[1]
    elif kind == "nt":
        (m, k), n = a.shape, b.shape[0]
    else:
        (k, m), n = a.shape, b.shape[1]
    tm, tn = _matmul_tiles(m, n, k, a.dtype, b.dtype, list(out_dtypes) + [e.dtype for e in extras])
    assert m % tm == 0 and n % tn == 0, (name, a.shape, b.shape)
    a_spec = pl.BlockSpec((k, tm), lambda i, j: (0, i)) if kind == "tn" else pl.BlockSpec((tm, k), lambda i, j: (i, 0))
    b_spec = pl.BlockSpec((tn, k), lambda i, j: (j, 0)) if kind == "nt" else pl.BlockSpec((k, tn), lambda i, j: (0, j))
    o_spec = pl.BlockSpec((tm, tn), lambda i, j: (i, j))
    dims = {"nn": NN, "nt": NT, "tn": TN}[kind]
    n_ex = len(extras)

    def body(*refs):
        a_ref, b_ref = refs[:2]
        ex, outs = refs[2:2 + n_ex], refs[2 + n_ex:]
        acc = _dot(a_ref[...].astype(BF16), b_ref[...].astype(BF16), dims)
        res = epilogue(acc, *[e[...] for e in ex]) if epilogue is not None else (acc,)
        for o_ref, r in zip(outs, res):
            o_ref[...] = r.astype(o_ref.dtype)

    out = _call(
        body, name=name, grid=(m // tm, n // tn),
        in_specs=[a_spec, b_spec] + [o_spec] * n_ex,
        out_specs=[o_spec] * len(out_dtypes),
        out_shape=[_sds((m, n), dt) for dt in out_dtypes],
        compiler_params=_cparams("parallel", "parallel"),
    )(a, b, *extras)
    return out if len(out_dtypes) > 1 else out[0]


def _ep_add(acc, res):
    return (acc + res,)


def _ep_relu2(acc):
    r = jnp.maximum(acc, 0.0)
    return acc, r * r


def _ep_relu2_bwd(acc, a):
    return (acc * (2.0 * jnp.maximum(a.astype(F32), 0.0)),)


def _rmsnorm(x, g, name):
    s, d = x.shape
    tm = min(s, 512)

    def body(x_ref, g_ref, h_ref):
        xv = x_ref[...]
        r = lax.rsqrt(jnp.mean(xv * xv, axis=-1, keepdims=True) + EPS)
        h_ref[...] = (xv * r * g_ref[...]).astype(BF16)

    return _call(
        body, name=name, grid=(s // tm,),
        in_specs=[pl.BlockSpec((tm, d), lambda i: (i, 0)), pl.BlockSpec((1, d), lambda i: (0, 0))],
        out_specs=pl.BlockSpec((tm, d), lambda i: (i, 0)),
        out_shape=_sds((s, d), BF16), compiler_params=_cparams("parallel"),
    )(x, g)


def _rmsnorm_bwd(x, g, dh, dres, name):
    s, d = x.shape
    tm = min(s, 512)

    def body(x_ref, g_ref, dh_ref, dres_ref, dx_ref, dxb_ref, dg_ref):
        xv, dhv = x_ref[...], dh_ref[...]
        r = lax.rsqrt(jnp.mean(xv * xv, axis=-1, keepdims=True) + EPS)
        xr = xv * r
        t = dhv * g_ref[...]
        dx = dres_ref[...] + r * (t - xr * jnp.mean(t * xr, axis=-1, keepdims=True))
        dx_ref[...] = dx
        dxb_ref[...] = dx.astype(BF16)

        @pl.when(pl.program_id(0) == 0)
        def _():
            dg_ref[...] = jnp.zeros_like(dg_ref)

        dg_ref[...] += jnp.sum(dhv * xr, axis=0, keepdims=True)

    row = pl.BlockSpec((tm, d), lambda i: (i, 0))
    vec = pl.BlockSpec((1, d), lambda i: (0, 0))
    return _call(
        body, name=name, grid=(s // tm,), in_specs=[row, vec, row, row], out_specs=[row, row, vec],
        out_shape=[_sds((s, d), F32), _sds((s, d), BF16), _sds((1, d), F32)],
        compiler_params=_cparams("arbitrary"),
    )(x, g, dh, dres)


def _loss_head(y, target, name):
    s, d = y.shape
    tm = min(s, 512)

    def body(y_ref, t_ref, loss_ref, dy_ref, dyb_ref):
        err = y_ref[...] - t_ref[...]
        dy = err * (1.0 / d)
        dy_ref[...] = dy
        dyb_ref[...] = dy.astype(BF16)

        @pl.when(pl.program_id(0) == 0)
        def _():
            loss_ref[...] = jnp.zeros_like(loss_ref)

        part = 0.5 * jnp.sum(jnp.mean(err * err, axis=-1, keepdims=True), axis=0, keepdims=True)
        loss_ref[...] += part

    row = pl.BlockSpec((tm, d), lambda i: (i, 0))
    return _call(
        body, name=name, grid=(s // tm,), in_specs=[row, row],
        out_specs=[pl.BlockSpec((8, LANES), lambda i: (0, 0)), row, row],
        out_shape=[_sds((8, LANES), F32), _sds((s, d), F32), _sds((s, d), BF16)],
        compiler_params=_cparams("arbitrary"),
    )(y, target)


def _head_lane_mask():
    lane = lax.broadcasted_iota(jnp.int32, (1, LANES), 1)
    return lane < SB_HEAD_DIM


def _pair_rms(xv, first):
    x2 = xv * xv
    s0 = jnp.sum(jnp.where(first, x2, 0.0), axis=-1, keepdims=True)
    s1 = jnp.sum(jnp.where(first, 0.0, x2), axis=-1, keepdims=True)
    inv = 1.0 / SB_HEAD_DIM
    return jnp.where(first, lax.rsqrt(s0 * inv + EPS), lax.rsqrt(s1 * inv + EPS))


def _pair_mean(t, first):
    s0 = jnp.sum(jnp.where(first, t, 0.0), axis=-1, keepdims=True)
    s1 = jnp.sum(jnp.where(first, 0.0, t), axis=-1, keepdims=True)
    return jnp.where(first, s0, s1) * (1.0 / SB_HEAD_DIM)


def _qk_gain_table(q_gain, k_gain, d):
    reps = d // SB_HEAD_DIM
    return jnp.stack([jnp.tile(q_gain, reps) * SB_SCALE, jnp.tile(k_gain, reps), jnp.ones((d,), F32)])[:, None, :]


QKNORM_ROWS = 256


def _qknorm(qkv, gains, name):
    s, d3 = qkv.shape
    d = d3 // 3
    tm = min(s, QKNORM_ROWS)

    def body(x_ref, g_ref, o_ref):
        first = _head_lane_mask()
        is_v = pl.program_id(0) == 2
        for c in range(d // LANES):
            cols = slice(c * LANES, (c + 1) * LANES)
            xv = x_ref[:, cols]
            normed = xv * _pair_rms(xv, first) * g_ref[0, :, cols]
            o_ref[:, cols] = jnp.where(is_v, xv, normed).astype(BF16)

    tile = pl.BlockSpec((tm, d), lambda c, i: (i, c))
    return _call(
        body, name=name, grid=(3, s // tm),
        in_specs=[tile, pl.BlockSpec((1, 1, d), lambda c, i: (c, 0, 0))], out_specs=tile,
        out_shape=_sds((s, d3), BF16), compiler_params=_cparams("parallel", "parallel"),
    )(qkv, gains)


def _qknorm_bwd(qkv, dq, dk, dv, gains, name):
    s, d3 = qkv.shape
    d = d3 // 3
    tm = min(s, QKNORM_ROWS)

    def body(x_ref, dq_ref, dk_ref, dv_ref, g_ref, dx_ref, dg_ref):
        c, i = pl.program_id(0), pl.program_id(1)

        @pl.when(i == 0)
        def _():
            dg_ref[...] = jnp.zeros_like(dg_ref)

        first = _head_lane_mask()
        for col in range(d // LANES):
            cols = slice(col * LANES, (col + 1) * LANES)
            xv = x_ref[:, cols]
            dy = jnp.where(c == 0, dq_ref[:, cols], jnp.where(c == 1, dk_ref[:, cols], dv_ref[:, cols]))
            r = _pair_rms(xv, first)
            xr = xv * r
            t = dy * g_ref[0, :, cols]
            dx = r * (t - xr * _pair_mean(t * xr, first))
            dx_ref[:, cols] = jnp.where(c == 2, dy, dx).astype(BF16)
            dg_ref[0, :, cols] += jnp.sum(dy * xr, axis=0, keepdims=True)

    tile = pl.BlockSpec((tm, d), lambda c, i: (i, c))
    vec = pl.BlockSpec((1, 1, d), lambda c, i: (c, 0, 0))

    def part(kind):
        return pl.BlockSpec((tm, d), lambda c, i: (jnp.where(c == kind, i, 0), 0))

    return _call(
        body, name=name, grid=(3, s // tm), in_specs=[tile, part(0), part(1), part(2), vec], out_specs=[tile, vec],
        out_shape=[_sds((s, d3), BF16), _sds((3, 1, d), F32)],
        compiler_params=_cparams("arbitrary", "arbitrary"),
    )(qkv, dq, dk, dv, gains)


def _softplus_parts(z):
    sp = jnp.maximum(z, 0.0) + jnp.log(1.0 + jnp.exp(-jnp.abs(z)))
    return sp, z - sp


def _diag_causal(tb):
    return lax.broadcasted_iota(jnp.int32, (tb, tb), 1) < lax.broadcasted_iota(jnp.int32, (tb, tb), 0)


def _later_keys(tb):
    r = lax.broadcasted_iota(jnp.int32, (tb, tb), 0)
    c = lax.broadcasted_iota(jnp.int32, (tb, tb), 1)
    return jnp.where(r > c, 1.0, 0.0).astype(BF16)


def _sb_scores(qa, kj, causal):
    sp, logsig = _softplus_parts(_dot(qa, kj, NT))
    return (-sp if causal is None else jnp.where(causal, -sp, 0.0)), logsig


def _sb_weights(stay, logsig, run, later, causal):
    w = jnp.exp(logsig + _dot2(stay, later, NN) + run)
    return w if causal is None else jnp.where(causal, w, 0.0)


def _sb_attention_fwd(qkvn, name, riders=()):
    s, d3 = qkvn.shape
    d = d3 // 3
    pairs, tb = d // LANES, min(ATT_BLOCK, s)
    nb = s // tb
    assert nb <= LANES
    n_r = len(riders)
    ride_shapes, ride_scratch = _rider_shapes("gather", riders)

    def body(*refs):
        q_ref, k_ref, v_ref = refs[:3]
        o_ref, runs_ref = refs[3 + n_r:5 + n_r]
        ride_end = _ride("gather", refs[3:3 + n_r], refs[5 + n_r:5 + 2 * n_r], refs[5 + 2 * n_r:], (pairs, nb))
        i = pl.program_id(1)
        first = _head_lane_mask()
        lane = lax.broadcasted_iota(jnp.int32, (1, LANES), 1)
        later = _later_keys(tb)
        q2 = q_ref[...]
        qs = (jnp.where(first, q2, jnp.zeros_like(q2)), jnp.where(first, jnp.zeros_like(q2), q2))

        def tiles(js, carry, causals):
            kvs = []
            for j in js:
                rows = pl.ds(pl.multiple_of(j * tb, tb), tb)
                kvs.append((k_ref[rows, :], v_ref[rows, :]))
            scores = [[_sb_scores(qs[h], kj, causals[t]) for h in range(2)] for t, (kj, _) in enumerate(kvs)]
            run = [carry[0][0], carry[1][0]]
            acc = [carry[0][1], carry[1][1]]
            runs = [carry[0][2], carry[1][2]]
            weights = []
            for t, j in enumerate(js):
                weights.append([_sb_weights(*scores[t][h], run[h], later, causals[t]) for h in range(2)])
                for h in range(2):
                    runs[h] = jnp.where(lane == j, run[h], runs[h])
                    run[h] = run[h] + jnp.sum(scores[t][h][0], axis=-1, keepdims=True)
            for t, (_, vj) in enumerate(kvs):
                for h in range(2):
                    acc[h] = acc[h] + _dot(weights[t][h].astype(BF16), vj, NN)
            return tuple((run[h], acc[h], runs[h]) for h in range(2))

        def alive(carry):
            return jnp.maximum(jnp.max(carry[0][0]), jnp.max(carry[1][0])) >= -SB_DEAD

        never = jnp.full((tb, LANES), SB_NEVER, F32)
        zero = (jnp.zeros((tb, 1), F32), jnp.zeros((tb, LANES), F32), never)
        diag = _diag_causal(tb)
        carry = lax.cond(i == 0, lambda: tiles([i], (zero, zero), [diag]),
                         lambda: tiles([i, i - 1], (zero, zero), [diag, None]))
        left = jnp.maximum(i - 1, 0)
        _, carry = lax.while_loop(
            lambda st: (st[0] < left // 2) & alive(st[1]),
            lambda st: (st[0] + 1, tiles([i - 2 - 2 * st[0], i - 3 - 2 * st[0]], st[1], [None, None])),
            (jnp.int32(0), carry))
        carry = lax.cond((left % 2 == 1) & alive(carry), lambda c: tiles([0], c, [None]), lambda c: c, carry)
        o_ref[...] = jnp.where(first, carry[0][1], carry[1][1])
        runs_ref[0] = carry[0][2]
        runs_ref[1] = carry[1][2]
        ride_end()

    out = _call(
        body, name=name, grid=(pairs, nb),
        in_specs=[pl.BlockSpec((tb, LANES), lambda p, i: (i, p)),
                  pl.BlockSpec((s, LANES), lambda p, i: (0, pairs + p)),
                  pl.BlockSpec((s, LANES), lambda p, i: (0, 2 * pairs + p))] + [_ANY] * n_r,
        out_specs=[pl.BlockSpec((tb, LANES), lambda p, i: (i, p)),
                   pl.BlockSpec((2, tb, LANES), lambda p, i: (p * nb + i, 0, 0))] + [_ANY] * n_r,
        out_shape=[_sds((s, d), F32), _sds((pairs * nb * 2, tb, LANES), F32)] + ride_shapes,
        scratch_shapes=ride_scratch,
        compiler_params=_cparams("arbitrary", "arbitrary"),
    )(qkvn, qkvn, qkvn, *riders)
    return out[0], out[1], list(out[2:])


def _sb_attention_bwd(qkvn, do, runs, name, riders=()):
    s, d3 = qkvn.shape
    d = d3 // 3
    pairs, tb = d // LANES, min(ATT_BLOCK, s)
    nb = s // tb
    n_r = len(riders)
    ride_shapes, ride_scratch = _rider_shapes("exchange", riders)

    def body(*refs):
        q_ref, k_ref, v_ref, do_ref, runs_ref = refs[:5]
        dq_ref, dk_ref, dv_ref = refs[5 + n_r:8 + n_r]
        ride_end = _ride("exchange", refs[5:5 + n_r], refs[8 + n_r:8 + 2 * n_r], refs[8 + 2 * n_r:], (pairs, nb))
        i = pl.program_id(1)

        @pl.when(i == 0)
        def _():
            dk_ref[...] = jnp.zeros_like(dk_ref)
            dv_ref[...] = jnp.zeros_like(dv_ref)

        first = _head_lane_mask()
        lane = lax.broadcasted_iota(jnp.int32, (1, LANES), 1)
        later = _later_keys(tb)
        q2, do2 = q_ref[...], do_ref[...].astype(BF16)
        zq = jnp.zeros_like(q2)
        qs = (jnp.where(first, q2, zq), jnp.where(first, zq, q2))
        dos = (jnp.where(first, do2, zq), jnp.where(first, zq, do2))

        def tiles(js, carry, causals):
            rows, kv = [], []
            for j in js:
                r = pl.ds(pl.multiple_of(j * tb, tb), tb)
                kj, vj = k_ref[r, :], v_ref[r, :]
                zk = jnp.zeros_like(kj)
                rows.append(r)
                kv.append([(jnp.where(first, kj, zk), jnp.where(first, vj, zk)),
                           (jnp.where(first, zk, kj), jnp.where(first, zk, vj))])
            pairs_th = [(t, h) for t in range(len(js)) for h in range(2)]
            scores = {(t, h): _sb_scores(qs[h], kv[t][h][0], causals[t]) for t, h in pairs_th}
            w, g = {}, {}
            for t, h in pairs_th:
                run = jnp.sum(jnp.where(lane == js[t], runs_ref[h], 0.0), axis=-1, keepdims=True)
                w[t, h] = _sb_weights(*scores[t, h], run, later, causals[t])
                g[t, h] = w[t, h] * _dot(dos[h], kv[t][h][1], NT)
            gsum = [carry[0], carry[1]]
            dz = {}
            for t, h in pairs_th:
                before = _dot(g[t, h].astype(BF16), later, NT) + gsum[h]
                gsum[h] = gsum[h] + jnp.sum(g[t, h], axis=-1, keepdims=True)
                sig = jnp.exp(scores[t, h][1])
                d = g[t, h] * (1.0 - sig) - before * sig
                dz[t, h] = (d if causals[t] is None else jnp.where(causals[t], d, 0.0)).astype(BF16)
            dq = carry[2]
            for t, h in pairs_th:
                dq = dq + _dot(dz[t, h], kv[t][h][0], NN)
            for t in range(len(js)):
                dk_ref[rows[t], :] += _dot(dz[t, 0], qs[0], TN) + _dot(dz[t, 1], qs[1], TN)
                dv_ref[rows[t], :] += _dot(w[t, 0].astype(BF16), dos[0], TN) + _dot(w[t, 1].astype(BF16), dos[1], TN)
            return gsum[0], gsum[1], dq

        left = jnp.maximum(i - 1, 0)
        reach = jnp.max(jnp.maximum(runs_ref[0], runs_ref[1]), axis=0, keepdims=True)
        start = jnp.sum(jnp.where((reach < -SB_DEAD) & (lane < left), 1.0, 0.0)).astype(jnp.int32)
        carry = (jnp.zeros((tb, 1), F32), jnp.zeros((tb, 1), F32), jnp.zeros((tb, LANES), F32))
        live = left - start
        carry = lax.fori_loop(
            0, live // 2, lambda jj, c: tiles([start + 2 * jj, start + 2 * jj + 1], c, [None, None]), carry)
        carry = lax.cond(live % 2 == 1, lambda c: tiles([left - 1], c, [None]), lambda c: c, carry)
        diag = _diag_causal(tb)
        carry = lax.cond(i == 0, lambda c: tiles([i], c, [diag]), lambda c: tiles([i - 1, i], c, [None, diag]), carry)
        dq_ref[...] = carry[2]
        ride_end()

    q_spec = pl.BlockSpec((tb, LANES), lambda p, i: (i, p))
    out = _call(
        body, name=name, grid=(pairs, nb),
        in_specs=[q_spec,
                  pl.BlockSpec((s, LANES), lambda p, i: (0, pairs + p)),
                  pl.BlockSpec((s, LANES), lambda p, i: (0, 2 * pairs + p)),
                  q_spec,
                  pl.BlockSpec((2, tb, LANES), lambda p, i: (p * nb + i, 0, 0))] + [_ANY] * n_r,
        out_specs=[q_spec, pl.BlockSpec((s, LANES), lambda p, i: (0, p)),
                   pl.BlockSpec((s, LANES), lambda p, i: (0, p))] + [_ANY] * n_r,
        out_shape=[_sds((s, d), F32)] * 3 + ride_shapes,
        scratch_shapes=ride_scratch,
        compiler_params=_cparams("arbitrary", "arbitrary"),
    )(qkvn, qkvn, qkvn, do, runs, *riders)
    return out[0], out[1], out[2], list(out[3:])


def _hg_tables(c):
    t = np.arange(c)[:, None]
    j = np.arange(c)[None, :]
    sums = [j <= t, j > t]
    masks = []
    m = c // 2
    while m >= 1:
        pos, base = t % (2 * m), t - t % (2 * m)
        sums.append((pos >= m) & (j >= base + m) & (j <= t))
        sums.append((pos < m) & (j > t) & (j <= base + m - 1))
        masks.append((t // (2 * m) == j // (2 * m)) & (t % (2 * m) >= m) & (j % (2 * m) < m))
        m //= 2
    return (jnp.asarray(np.concatenate(sums, 0), BF16), jnp.asarray(np.stack(masks), F32), len(masks))


def _hg_gates(qr, fr, lb):
    sq = jax.nn.sigmoid(qr)
    sg = jax.nn.sigmoid(fr)
    forget = lb + (1.0 - lb) * sg
    return sq, qr * sq, sg, forget, jnp.log(forget), (1.0 - lb) * (1.0 - sg)


def _hg_scores(q, k, x, masks_ref, c, levels):
    eye = (lax.broadcasted_iota(jnp.int32, (c, c), 0) == lax.broadcasted_iota(jnp.int32, (c, c), 1)).astype(F32)
    scores = eye * jnp.sum(q * k, axis=-1, keepdims=True)
    ops = []
    for l in range(levels):
        qm = (q * x[(2 + 2 * l) * c:(3 + 2 * l) * c]).astype(BF16)
        km = (k * x[(3 + 2 * l) * c:(4 + 2 * l) * c]).astype(BF16)
        scores = scores + masks_ref[l] * _dot(qm, km, NT)
        ops.append((qm, km))
    return scores, eye, ops


def _hg_layout(s, d):
    heads, c = d // HG_DIM, min(HG_CHUNK, s)
    nsub = min(HG_STEP_CHUNKS, s // c)
    hp = HG_STEP_HEADS if heads % HG_STEP_HEADS == 0 else 1
    return heads, c, nsub, hp


def _hg_specs(s, d, reverse):
    heads, c, nsub, hp = _hg_layout(s, d)
    rows, width, groups, n_steps = c * nsub, hp * HG_DIM, heads // hp, s // (c * nsub)

    def step(si):
        return n_steps - 1 - si if reverse else si
    proj = [pl.BlockSpec((rows, width), functools.partial(lambda g, si, part: (step(si), part * groups + g), part=p))
            for p in range(4)]
    tile = pl.BlockSpec((rows, width), lambda g, si: (step(si), g))
    lb = pl.BlockSpec((1, width), lambda g, si: (0, g))
    gain = pl.BlockSpec((1, HG_DIM), lambda g, si: (0, 0))
    state = pl.BlockSpec((hp, nsub, HG_DIM, HG_DIM), lambda g, si: (g, step(si), 0, 0))
    return proj, tile, lb, gain, state, (groups, n_steps)


def _hgrn2_fwd(proj, lb, gain, name, riders=()):
    s, d4 = proj.shape
    d = d4 // 4
    heads, c, nsub, hp = _hg_layout(s, d)
    sums, masks, levels = _hg_tables(c)
    n_r = len(riders)
    ride_shapes, ride_scratch = _rider_shapes("gather", riders)
    pspecs, tile, lbs, gs, state, grid = _hg_specs(s, d, False)

    def body(*refs):
        qr_ref, fr_ref, ir_ref, gr_ref, lb_ref, gain_ref, sums_ref, masks_ref = refs[:8]
        og_ref, o_ref, states_ref = refs[8 + n_r:11 + n_r]
        st_ref = refs[11 + 2 * n_r]
        ride_end = _ride("gather", refs[8:8 + n_r], refs[11 + n_r:11 + 2 * n_r], refs[12 + 2 * n_r:], grid)

        @pl.when(pl.program_id(1) == 0)
        def _():
            st_ref[...] = jnp.zeros_like(st_ref)

        lbv, gainv = lb_ref[...], gain_ref[...]
        units = [(ci, hh) for ci in range(nsub) for hh in range(hp)]

        def lanes(hh):
            return slice(hh * HG_DIM, (hh + 1) * HG_DIM)

        pre = []
        for ci in range(nsub):
            rows = slice(ci * c, (ci + 1) * c)
            _, q, _, _, lf, k = _hg_gates(qr_ref[rows, :], fr_ref[rows, :], lbv)
            pre.append((q, k, jnp.exp(_mdot2(sums_ref[...], lf, NN))))
        scores, qh, vb, update = {}, {}, {}, {}
        for ci, hh in units:
            q, k, x = (a[:, lanes(hh)] for a in pre[ci])
            scores[ci, hh] = _hg_scores(q, k, x, masks_ref, c, levels)[0].astype(BF16)
            qh[ci, hh] = (q * x[0:c]).astype(BF16)
            vb[ci, hh] = ir_ref[ci * c:(ci + 1) * c, lanes(hh)].astype(BF16)
            update[ci, hh] = _dot(vb[ci, hh], (k * x[c:2 * c]).astype(BF16), TN)
        intra = {u: _dot(scores[u], vb[u], NN) for u in units}
        for hh in range(hp):
            st = st_ref[hh]
            for ci in range(nsub):
                rows = slice(ci * c, (ci + 1) * c)
                states_ref[hh, ci] = st
                o = _dot(qh[ci, hh], st.astype(BF16), NT) + intra[ci, hh]
                st = st * pre[ci][2][c - 1:c, lanes(hh)] + update[ci, hh]
                o_ref[rows, lanes(hh)] = o
                r = lax.rsqrt(jnp.mean(o * o, axis=-1, keepdims=True) + EPS)
                og_ref[rows, lanes(hh)] = (o * r * gainv * jax.nn.sigmoid(gr_ref[rows, lanes(hh)])).astype(BF16)
            st_ref[hh] = st
        ride_end()

    const = [pl.BlockSpec(sums.shape, lambda g, si: (0, 0)), pl.BlockSpec(masks.shape, lambda g, si: (0, 0, 0))]
    out = _call(
        body, name=name, grid=grid, in_specs=pspecs + [lbs, gs] + const + [_ANY] * n_r,
        out_specs=[tile, tile, state] + [_ANY] * n_r,
        out_shape=[_sds((s, d), BF16), _sds((s, d), F32), _sds((heads, s // c, HG_DIM, HG_DIM), F32)] + ride_shapes,
        scratch_shapes=[pltpu.VMEM((hp, HG_DIM, HG_DIM), F32)] + ride_scratch,
        compiler_params=_cparams("arbitrary", "arbitrary"),
    )(proj, proj, proj, proj, lb, gain, sums, masks, *riders)
    return out[0], out[1], out[2], list(out[3:])


def _hgrn2_bwd(proj, lb, gain, o, states, dog, name, riders=()):
    s, d4 = proj.shape
    d = d4 // 4
    heads, c, nsub, hp = _hg_layout(s, d)
    sums, masks, levels = _hg_tables(c)
    n_r = len(riders)
    ride_shapes, ride_scratch = _rider_shapes("exchange", riders)
    pspecs, tile, lbs, gs, state, grid = _hg_specs(s, d, True)

    def body(*refs):
        (qr_ref, fr_ref, ir_ref, gr_ref, lb_ref, gain_ref, sums_ref, masks_ref, o_ref, states_ref, dog_ref) = refs[:11]
        dq_ref, df_ref, di_ref, dg_ref, dlb_ref, dgain_ref = refs[11 + n_r:17 + n_r]
        dst_ref = refs[17 + 2 * n_r]
        ride_end = _ride("exchange", refs[11:11 + n_r], refs[17 + n_r:17 + 2 * n_r], refs[18 + 2 * n_r:], grid)

        @pl.when(pl.program_id(1) == 0)
        def _():
            dst_ref[...] = jnp.zeros_like(dst_ref)
            dlb_ref[...] = jnp.zeros_like(dlb_ref)
            dgain_ref[...] = jnp.zeros_like(dgain_ref)

        lbv, gainv = lb_ref[...], gain_ref[...]
        units = [(ci, hh) for ci in range(nsub) for hh in range(hp)]

        def lanes(hh):
            return slice(hh * HG_DIM, (hh + 1) * HG_DIM)

        def rows_of(ci):
            return slice(ci * c, (ci + 1) * c)

        pre = []
        for ci in range(nsub):
            qr = qr_ref[rows_of(ci), :]
            sq, q, sg, forget, lf, k = _hg_gates(qr, fr_ref[rows_of(ci), :], lbv)
            pre.append(dict(qr=qr, sq=sq, q=q, sg=sg, forget=forget, k=k, x=jnp.exp(_mdot2(sums_ref[...], lf, NN))))

        dob, vb, sc, qh_f, kh_f, feed = {}, {}, {}, {}, {}, {}
        dgain = [jnp.zeros((1, HG_DIM), F32) for _ in range(hp)]
        for ci, hh in units:
            rows, ln = rows_of(ci), lanes(hh)
            ov, gate = o_ref[rows, ln], jax.nn.sigmoid(gr_ref[rows, ln])
            r = lax.rsqrt(jnp.mean(ov * ov, axis=-1, keepdims=True) + EPS)
            orr = ov * r
            dogv = dog_ref[rows, ln]
            dg_ref[rows, ln] = (dogv * orr * gainv * gate * (1.0 - gate)).astype(BF16)
            don = dogv * gate
            dgain[hh] = dgain[hh] + jnp.sum(don * orr, axis=0, keepdims=True)
            t = don * gainv
            dob[ci, hh] = (r * (t - orr * jnp.mean(t * orr, axis=-1, keepdims=True))).astype(BF16)
            q, k, x = (pre[ci][n][:, ln] for n in ("q", "k", "x"))
            vb[ci, hh] = ir_ref[rows, ln].astype(BF16)
            sc[ci, hh] = _hg_scores(q, k, x, masks_ref, c, levels)
            qh_f[ci, hh], kh_f[ci, hh] = q * x[0:c], k * x[c:2 * c]
            feed[ci, hh] = _dot(dob[ci, hh], qh_f[ci, hh].astype(BF16), TN)

        dsts = {}
        for hh in range(hp):
            dst = dst_ref[hh]
            for ci in reversed(range(nsub)):
                dsts[ci, hh] = dst
                dst = dst * pre[ci]["x"][c - 1:c, lanes(hh)] + feed[ci, hh]
            dst_ref[hh] = dst

        dlb = [jnp.zeros((1, HG_DIM), F32) for _ in range(hp)]
        part = {(ci, hh): {n: v[:, lanes(hh)] for n, v in pre[ci].items()} for ci, hh in units}
        dscores, decay_grad, dq, dk, dexp = {}, {}, {}, {}, {}
        for u in units:
            ci, hh = u
            x, (scores, eye, _) = part[u]["x"], sc[u]
            st, dst = states_ref[hh, ci], dsts[u]
            dstb = dst.astype(BF16)
            dscores[u] = _dot(dob[u], vb[u], NT)
            di_ref[rows_of(ci), lanes(hh)] = (_dot(scores.astype(BF16), dob[u], TN)
                                              + _dot(kh_f[u].astype(BF16), dstb, NT)).astype(BF16)
            dqh = _dot(dob[u], st.astype(BF16), NN)
            dkh = _dot(vb[u], dstb, NN)
            decay_grad[u] = x[c - 1:c] * jnp.sum(dst * st, axis=0, keepdims=True)
            ddiag = jnp.sum(eye * dscores[u], axis=-1, keepdims=True)
            dq[u] = dqh * x[0:c] + ddiag * part[u]["k"]
            dk[u] = dkh * x[c:2 * c] + ddiag * part[u]["q"]
            dexp[u] = [dqh * qh_f[u], dkh * kh_f[u]]
        for l in range(levels):
            for u in units:
                q, k, x = part[u]["q"], part[u]["k"], part[u]["x"]
                qm, km = sc[u][2][l]
                dsm = (masks_ref[l] * dscores[u]).astype(BF16)
                dqm, dkm = _dot(dsm, km, NN), _dot(dsm, qm, TN)
                xq, xk = x[(2 + 2 * l) * c:(3 + 2 * l) * c], x[(3 + 2 * l) * c:(4 + 2 * l) * c]
                dq[u] = dq[u] + dqm * xq
                dk[u] = dk[u] + dkm * xk
                dexp[u] += [dqm * (q * xq), dkm * (k * xk)]
        for u in units:
            ci, hh = u
            rows, ln, p = rows_of(ci), lanes(hh), part[u]
            dlf = _mdot2(sums_ref[...], jnp.concatenate(dexp[u], axis=0), TN) + decay_grad[u]
            dforget = dlf / p["forget"] - dk[u]
            dlb[hh] = dlb[hh] + jnp.sum(dforget * (1.0 - p["sg"]), axis=0, keepdims=True)
            df_ref[rows, ln] = (dforget * (1.0 - lbv[:, ln]) * p["sg"] * (1.0 - p["sg"])).astype(BF16)
            dq_ref[rows, ln] = (dq[u] * p["sq"] * (1.0 + p["qr"] * (1.0 - p["sq"]))).astype(BF16)
        for hh in range(hp):
            dlb_ref[:, lanes(hh)] += dlb[hh]
            dgain_ref[hh] += dgain[hh]
        ride_end()

    const = [pl.BlockSpec(sums.shape, lambda g, si: (0, 0)), pl.BlockSpec(masks.shape, lambda g, si: (0, 0, 0))]
    out = _call(
        body, name=name, grid=grid, in_specs=pspecs + [lbs, gs] + const + [tile, state, tile] + [_ANY] * n_r,
        out_specs=[tile, tile, tile, tile, lbs, pl.BlockSpec((hp, 1, HG_DIM), lambda g, si: (g, 0, 0))] + [_ANY] * n_r,
        out_shape=[_sds((s, d), BF16)] * 4 + [_sds((1, d), F32), _sds((heads, 1, HG_DIM), F32)] + ride_shapes,
        scratch_shapes=[pltpu.VMEM((hp, HG_DIM, HG_DIM), F32)] + ride_scratch,
        compiler_params=_cparams("arbitrary", "arbitrary"),
    )(proj, proj, proj, proj, lb, gain, sums, masks, o, states, dog, *riders)
    return tuple(out[:6]) + (list(out[6:]),)


def _lower_bounds(logits, name):
    n, d = logits.shape

    def body(l_ref, lb_ref):
        lv = l_ref[...]
        e = jnp.exp(lv - jnp.max(lv, axis=0, keepdims=True))
        p = e / jnp.sum(e, axis=0, keepdims=True)
        run = jnp.zeros((1, d), F32)
        for j in range(n):
            if j > 0:
                run = run + p[j:j + 1]
            lb_ref[j:j + 1, :] = run

    return _call(body, name=name, out_shape=_sds((n, d), F32))(logits)


def _lower_bounds_bwd(logits, dlb_parts, name):
    n, d = logits.shape

    def body(l_ref, dlb_ref, dl_ref):
        lv, dv = l_ref[...], dlb_ref[0]
        for dev in range(1, N_DEV):
            dv = dv + dlb_ref[dev]
        e = jnp.exp(lv - jnp.max(lv, axis=0, keepdims=True))
        p = e / jnp.sum(e, axis=0, keepdims=True)
        run = jnp.zeros((1, d), F32)
        dps = [None] * n
        for j in range(n - 1, 0, -1):
            run = run + dv[j:j + 1]
            dps[j] = run
        dps[0] = jnp.zeros((1, d), F32)
        inner = jnp.zeros((1, d), F32)
        for j in range(n):
            inner = inner + p[j:j + 1] * dps[j]
        for j in range(n):
            dl_ref[j:j + 1, :] = p[j:j + 1] * (dps[j] - inner)

    return _call(body, name=name, out_shape=_sds((n, d), F32))(logits, dlb_parts)


_ANY = pl.BlockSpec(memory_space=pl.ANY)
_MESH = pl.DeviceIdType.MESH


def _gather_stages(x_ref, out_ref, send_sems, recv_sems, local_sem):
    mx, my, mc = lax.axis_index("x"), lax.axis_index("y"), lax.axis_index("c")
    me, sibling = (mx, my, mc), (mx, my, 1 - mc)
    chips = [(1 - mx, my), (mx, 1 - my), (1 - mx, 1 - my)]

    def slot(px, py, pc):
        return out_ref.at[4 * px + 2 * py + pc]

    def copy(k, block, to, src=None):
        return pltpu.make_async_remote_copy(
            src_ref=slot(*block) if src is None else src, dst_ref=slot(*block),
            send_sem=send_sems.at[k], recv_sem=recv_sems.at[k], device_id=to, device_id_type=_MESH)

    mine = pltpu.make_async_copy(x_ref, slot(*me), local_sem)
    first = [copy(0, me, sibling, src=x_ref)] + [copy(1 + j, me, (*chip, mc), src=x_ref) for j, chip in enumerate(chips)]
    passed = [copy(4 + j, (*chip, mc), sibling) for j, chip in enumerate(chips)]

    def start():
        mine.start()
        for cp in first:
            cp.start()

    def middle():
        for j, chip in enumerate(chips):
            copy(1 + j, (*chip, mc), me).wait_recv()
            passed[j].start()

    def finish():
        copy(0, sibling, me).wait_recv()
        for j, chip in enumerate(chips):
            copy(4 + j, (*chip, 1 - mc), me).wait_recv()
        for cp in first + passed:
            cp.wait_send()
        mine.wait()

    return start, middle, finish


def _exchange_stages(g_ref, out_ref, send_sems, recv_sems, local_sem):
    mx, my, mc = lax.axis_index("x"), lax.axis_index("y"), lax.axis_index("c")
    me = 4 * mx + 2 * my + mc
    mine = pltpu.make_async_copy(g_ref.at[me], out_ref.at[me], local_sem)
    copies = []
    for k in range(1, N_DEV):
        px, py, pc = mx ^ (k >> 2), my ^ ((k >> 1) & 1), mc ^ (k & 1)
        peer = 4 * px + 2 * py + pc
        send = pltpu.make_async_remote_copy(
            src_ref=g_ref.at[peer], dst_ref=out_ref.at[me], send_sem=send_sems.at[k - 1],
            recv_sem=recv_sems.at[k - 1], device_id=(px, py, pc), device_id_type=_MESH)
        arrival = pltpu.make_async_remote_copy(
            src_ref=g_ref.at[peer], dst_ref=out_ref.at[peer], send_sem=send_sems.at[k - 1],
            recv_sem=recv_sems.at[k - 1], device_id=(px, py, pc), device_id_type=_MESH)
        copies.append((send, arrival))

    def start():
        mine.start()
        for send, _ in copies:
            send.start()

    def finish():
        for _, arrival in copies:
            arrival.wait_recv()
        for send, _ in copies:
            send.wait_send()
        mine.wait()

    return start, lambda: None, finish


_STAGES = {"gather": _gather_stages, "exchange": _exchange_stages}
SEMS_PER_TRANSFER = 3


def _rider_shapes(kind, arrays):
    outs = [_sds((N_DEV,) + a.shape if kind == "gather" else a.shape, a.dtype) for a in arrays]
    scratch = []
    for _ in arrays:
        scratch += [pltpu.SemaphoreType.DMA((7,)), pltpu.SemaphoreType.DMA((7,)), pltpu.SemaphoreType.DMA]
    return outs, scratch


def _ride(kind, in_refs, out_refs, sems, grid):
    stages = [_STAGES[kind](a, o, *sems[SEMS_PER_TRANSFER * n:SEMS_PER_TRANSFER * (n + 1)])
              for n, (a, o) in enumerate(zip(in_refs, out_refs))]
    if not stages:
        return lambda: None
    p, i = pl.program_id(0), pl.program_id(1)

    def run(stage):
        for st in stages:
            st[stage]()

    pl.when((p == 0) & (i == 0))(lambda: run(0))
    pl.when((p == grid[0] // 2) & (i == 0))(lambda: run(1))
    return lambda: pl.when((p == grid[0] - 1) & (i == grid[1] - 1))(lambda: run(2))


def _transfer(kind, arrays, name):
    outs, scratch = _rider_shapes(kind, arrays)
    n = len(arrays)

    def body(*refs):
        stages = [_STAGES[kind](refs[t], refs[n + t], *refs[2 * n + SEMS_PER_TRANSFER * t:2 * n + SEMS_PER_TRANSFER * (t + 1)])
                  for t in range(n)]
        for stage in range(3):
            for st in stages:
                st[stage]()

    return _call(body, name=name, out_shape=outs, in_specs=[_ANY] * n, out_specs=[_ANY] * n, scratch_shapes=scratch)(*arrays)


def _all_gather(x, name):
    return _transfer("gather", [x], name)[0]


def _adamw(parts, w, m, v, name):
    n_l = len(parts)
    _, r, c = parts[0].shape
    tr = r if r <= 256 else 256
    assert r % tr == 0 and w.shape == (n_l * r, c), (name, parts[0].shape, w.shape)
    steps = r // tr

    def body(*refs):
        p_refs, (w_ref, m_ref, v_ref) = refs[:n_l], refs[n_l:n_l + 3]
        g_ref, d_ref, nm_ref, nv_ref, sum_ref = refs[n_l + 3:]
        for layer in range(n_l):
            @pl.when(pl.program_id(0) == layer)
            def _(p_ref=p_refs[layer]):
                acc = p_ref[0].astype(F32)
                for dev in range(1, N_DEV):
                    acc = acc + p_ref[dev].astype(F32)
                sum_ref[...] = acc

        g = sum_ref[...]
        nm = ADAM_B1 * m_ref[...] + (1.0 - ADAM_B1) * g
        nv = ADAM_B2 * v_ref[...] + (1.0 - ADAM_B2) * (g * g)
        m_hat = nm / (1.0 - ADAM_B1 ** ADAM_STEP)
        v_hat = nv / (1.0 - ADAM_B2 ** ADAM_STEP)
        g_ref[...] = g
        nm_ref[...] = nm
        nv_ref[...] = nv
        d_ref[...] = -ADAM_LR * (m_hat / (jnp.sqrt(v_hat) + ADAM_EPS) + ADAM_WD * w_ref[...])

    tile = pl.BlockSpec((tr, c), lambda l, i: (l * steps + i, 0))

    def part(layer):
        return pl.BlockSpec((N_DEV, tr, c), lambda l, i: (0, jnp.where(l == layer, i, 0), 0))

    return _call(
        body, name=name, grid=(n_l, steps),
        in_specs=[part(layer) for layer in range(n_l)] + [tile, tile, tile], out_specs=[tile] * 4,
        out_shape=[_sds((n_l * r, c), F32)] * 4, scratch_shapes=[pltpu.VMEM((tr, c), F32)],
        compiler_params=_cparams("arbitrary", "arbitrary"),
    )(*parts, w, m, v)


def _shard_2d(w):
    return w.astype(BF16).reshape(w.shape[0] * w.shape[1], w.shape[2])


def _full_cols(g, w):
    l, k, n = w.shape
    g = g.reshape(N_DEV, l, k, n)
    return [jnp.transpose(g[:, i], (1, 0, 2)).reshape(k, N_DEV * n) for i in range(l)]


def _full_rows(g, w):
    l, k, n = w.shape
    g = g.reshape(N_DEV, l, k, n)
    return [g[:, i].reshape(N_DEV * k, n) for i in range(l)]


def _parts_cols(grads):
    k, n8 = grads[0].shape
    g = jnp.stack(grads).reshape(len(grads), k, N_DEV, n8 // N_DEV)
    return jnp.transpose(g, (2, 0, 1, 3)).reshape(N_DEV, len(grads) * k, n8 // N_DEV)


def _parts_rows(grads):
    k8, n = grads[0].shape
    g = jnp.stack(grads).reshape(len(grads), N_DEV, k8 // N_DEV, n)
    return jnp.transpose(g, (1, 0, 2, 3)).reshape(N_DEV, len(grads) * (k8 // N_DEV), n)


def _pad_rows(a, rows):
    return jnp.concatenate([a, jnp.zeros((rows - a.shape[0], a.shape[1]), a.dtype)], axis=0)


def kernel(x, norm_gains, sb_w_qkv, sb_q_gain, sb_k_gain, sb_w_o, hg_w_in, hg_lb_logits, hg_norm_gain, hg_w_o, mlp_w1, mlp_w2, loss_target, m_norm_gains, m_sb_w_qkv, m_sb_q_gain, m_sb_k_gain, m_sb_w_o, m_hg_w_in, m_hg_lb_logits, m_hg_norm_gain, m_hg_w_o, m_mlp_w1, m_mlp_w2, v_norm_gains, v_sb_w_qkv, v_sb_q_gain, v_sb_k_gain, v_sb_w_o, v_hg_w_in, v_hg_lb_logits, v_hg_norm_gain, v_hg_w_o, v_mlp_w1, v_mlp_w2):
    depth, _, d_loc = norm_gains.shape
    n_sb, n_hg = sb_w_qkv.shape[0], hg_w_in.shape[0]
    xs = x[0]
    target = loss_target[0]
    s, d = xs.shape
    me = 4 * lax.axis_index("x") + 2 * lax.axis_index("y") + lax.axis_index("c")

    assert n_sb >= 1
    w_qkv = _full_cols(_all_gather(_shard_2d(sb_w_qkv[:1]), "gather_w_qkv"), sb_w_qkv[:1])
    riding = [(sb_w_o, _full_rows), (hg_w_in, _full_cols), (hg_w_o, _full_rows)]
    if n_sb > 1:
        riding.append((sb_w_qkv[1:], _full_cols))
    w_o = w_in = w_ho = None
    w_1s, w_2s = [None] * depth, [None] * depth
    n_gain_rows = 2 * depth
    small_rows = -(-(n_gain_rows + n_hg) // 8) * 8
    small = _pad_rows(jnp.concatenate([norm_gains.reshape(n_gain_rows, d_loc), hg_lb_logits], axis=0), small_rows)
    small = _all_gather(small, "gather_small")
    gains_full = jnp.transpose(small[:, :n_gain_rows], (1, 0, 2)).reshape(depth, 2, 1, d)
    logits_full = jnp.transpose(small[:, n_gain_rows:n_gain_rows + n_hg], (1, 0, 2)).reshape(n_hg, d)
    lower = _lower_bounds(logits_full, "lower_bounds")

    saved = []
    cur = xs
    for layer in range(depth):
        j = layer // 2
        h = _rmsnorm(cur, gains_full[layer, 0], f"norm_mix_{layer}")
        riders = [_shard_2d(mlp_w1[layer:layer + 1]), _shard_2d(mlp_w2[layer:layer + 1])]
        if layer == 0:
            riders += [_shard_2d(w) for w, _ in riding]
        if layer % 2 == 0:
            qkv = _matmul("nn", h, w_qkv[j], f"qkv_{layer}", [F32])
            qk_gains = _qk_gain_table(sb_q_gain[j], sb_k_gain[j], d)
            qkvn = _qknorm(qkv, qk_gains, f"qknorm_{layer}")
            o, runs, got = _sb_attention_fwd(qkvn, f"sb_fwd_{layer}", riders)
            mix = (qkv, qk_gains, qkvn, o, runs)
        else:
            proj = _matmul("nn", h, w_in[j], f"hg_in_{layer}", [F32])
            og, o, states, got = _hgrn2_fwd(proj, lower[j:j + 1], hg_norm_gain[j:j + 1], f"hg_fwd_{layer}", riders)
            mix = (proj, og, o, states)
        w_1 = _full_cols(got[0], mlp_w1[layer:layer + 1])[0]
        w_2s[layer] = w_2 = _full_rows(got[1], mlp_w2[layer:layer + 1])[0]
        w_1s[layer] = w_1
        if layer == 0:
            fulls = [full(g, w) for g, (w, full) in zip(got[2:], riding)]
            w_o, w_in, w_ho = fulls[:3]
            w_qkv = w_qkv + (fulls[3] if n_sb > 1 else [])
        if layer % 2 == 0:
            x1 = _matmul("nn", o, w_o[j], f"sb_out_{layer}", [F32], _ep_add, [cur])
        else:
            x1 = _matmul("nn", og, w_ho[j], f"hg_out_{layer}", [F32], _ep_add, [cur])
        h2 = _rmsnorm(x1, gains_full[layer, 1], f"norm_mlp_{layer}")
        a, u = _matmul("nn", h2, w_1, f"mlp_up_{layer}", [BF16, BF16], _ep_relu2)
        x2 = _matmul("nn", u, w_2, f"mlp_down_{layer}", [F32], _ep_add, [x1])
        saved.append((cur, h, mix, x1, h2, a, u))
        cur = x2

    loss_tile, dx, dxb = _loss_head(cur, target, "loss_head")
    loss = lax.psum(loss_tile[0, 0], AXES)

    d_gains = [[None, None] for _ in range(depth)]
    d_qk, d_lb, d_hgain = [None] * n_sb, [None] * n_hg, [None] * n_hg
    received = {"sb_w_qkv": [None] * n_sb, "sb_w_o": [None] * n_sb, "hg_w_in": [None] * n_hg,
                "hg_w_o": [None] * n_hg, "mlp_w1": [None] * depth, "mlp_w2": [None] * depth}
    pending = []

    def settle(got):
        for (wname, idx, _), arrived in zip(pending, got):
            received[wname][idx] = arrived
        pending.clear()

    for layer in reversed(range(depth)):
        j = layer // 2
        x0, h, mix, x1, h2, a, u = saved[layer]
        pending.append(("mlp_w2", layer, _parts_rows([_matmul("tn", u, dxb, f"d_mlp_w2_{layer}", [BF16])])))
        da = _matmul("nt", dxb, w_2s[layer], f"d_mlp_act_{layer}", [BF16], _ep_relu2_bwd, [a])
        pending.append(("mlp_w1", layer, _parts_cols([_matmul("tn", h2, da, f"d_mlp_w1_{layer}", [BF16])])))
        dh2 = _matmul("nt", da, w_1s[layer], f"d_mlp_in_{layer}", [F32])
        dx, dxb, d_gains[layer][1] = _rmsnorm_bwd(x1, gains_full[layer, 1], dh2, dx, f"d_norm_mlp_{layer}")
        if layer % 2 == 0:
            qkv, qk_gains, qkvn, o, runs = mix
            pending.append(("sb_w_o", j, _parts_rows([_matmul("tn", o, dxb, f"d_sb_w_o_{layer}", [BF16])])))
            do = _matmul("nt", dxb, w_o[j], f"d_sb_o_{layer}", [F32])
            dq, dk, dv, got = _sb_attention_bwd(qkvn, do, runs, f"sb_bwd_{layer}", [p for _, _, p in pending])
            settle(got)
            dqkv, d_qk[j] = _qknorm_bwd(qkv, dq, dk, dv, qk_gains, f"d_qknorm_{layer}")
            pending.append(("sb_w_qkv", j, _parts_cols([_matmul("tn", h, dqkv, f"d_sb_w_qkv_{layer}", [BF16])])))
            dh = _matmul("nt", dqkv, w_qkv[j], f"d_sb_in_{layer}", [F32])
        else:
            proj, og, o, states = mix
            pending.append(("hg_w_o", j, _parts_rows([_matmul("tn", og, dxb, f"d_hg_w_o_{layer}", [BF16])])))
            dog = _matmul("nt", dxb, w_ho[j], f"d_hg_o_{layer}", [F32])
            dq, df, di, dg, d_lb[j], d_hgain[j], got = _hgrn2_bwd(
                proj, lower[j:j + 1], hg_norm_gain[j:j + 1], o, states, dog, f"hg_bwd_{layer}",
                [p for _, _, p in pending])
            settle(got)
            dproj = jnp.concatenate([dq, df, di, dg], axis=1)
            pending.append(("hg_w_in", j, _parts_cols([_matmul("tn", h, dproj, f"d_hg_w_in_{layer}", [BF16])])))
            dh = _matmul("nt", dproj, w_in[j], f"d_hg_in_{layer}", [F32])
        dx, dxb, d_gains[layer][0] = _rmsnorm_bwd(x0, gains_full[layer, 0], dh, dx, f"d_norm_mix_{layer}")
    grad_x = dx[None]
    if pending:
        settle(_transfer("exchange", [p for _, _, p in pending], "exchange_tail"))

    def update(wname, w, m, v):
        shape = w.shape
        flat = (shape[0] * shape[1], shape[2])
        res = _adamw(received[wname], w.reshape(flat), m.reshape(flat), v.reshape(flat), "adamw_" + wname)
        return [r.reshape(shape) for r in res]

    big = {
        "sb_w_qkv": update("sb_w_qkv", sb_w_qkv, m_sb_w_qkv, v_sb_w_qkv),
        "sb_w_o": update("sb_w_o", sb_w_o, m_sb_w_o, v_sb_w_o),
        "hg_w_in": update("hg_w_in", hg_w_in, m_hg_w_in, v_hg_w_in),
        "hg_w_o": update("hg_w_o", hg_w_o, m_hg_w_o, v_hg_w_o),
        "mlp_w1": update("mlp_w1", mlp_w1, m_mlp_w1, v_mlp_w1),
        "mlp_w2": update("mlp_w2", mlp_w2, m_mlp_w2, v_mlp_w2),
    }

    d_gain_rows = jnp.concatenate([d_gains[l][t] for l in range(depth) for t in range(2)], axis=0)
    d_lb_rows = jnp.concatenate(d_lb, axis=0)
    def fold(t):
        return jnp.sum(t.reshape(d // SB_HEAD_DIM, SB_HEAD_DIM), axis=0, keepdims=True)
    d_qg = jnp.concatenate([fold(d_qk[i][0]) for i in range(n_sb)], axis=0) * SB_SCALE
    d_kg = jnp.concatenate([fold(d_qk[i][1]) for i in range(n_sb)], axis=0)
    d_hg = jnp.concatenate([jnp.sum(d_hgain[i], axis=0) for i in range(n_hg)], axis=0)
    per_row = d // LANES
    packed = jnp.concatenate([
        d_gain_rows.reshape(n_gain_rows * per_row, LANES), d_lb_rows.reshape(n_hg * per_row, LANES),
        jnp.concatenate([d_qg, d_kg], axis=1), d_hg], axis=0)
    n_packed = packed.shape[0]
    packed = _pad_rows(packed, -(-n_packed // 8) * 8)
    everyone = _all_gather(packed, "gather_small_grads")
    o_lb = n_gain_rows * per_row
    o_qk = o_lb + n_hg * per_row
    o_hg = o_qk + n_sb

    def mine_of(rows, count):
        return lax.dynamic_slice_in_dim(rows.reshape(N_DEV, count, per_row, LANES), me, 1, axis=2)[:, :, 0]

    d_logits_full = _lower_bounds_bwd(logits_full, everyone[:, o_lb:o_qk].reshape(N_DEV, n_hg, d), "lower_bounds_bwd")
    d_logits_mine = lax.dynamic_slice_in_dim(d_logits_full.reshape(n_hg, per_row, LANES), me, 1, axis=1)[:, 0]
    zeros7 = jnp.zeros((N_DEV - 1, n_hg, LANES), F32)
    small_parts = jnp.concatenate([
        mine_of(everyone[:, :o_lb], n_gain_rows),
        jnp.concatenate([d_logits_mine[None], zeros7], axis=0),
        everyone[:, o_qk:o_hg], everyone[:, o_hg:o_hg + n_hg]], axis=1)
    rows_small = small_parts.shape[1]
    pad_to = -(-rows_small // 8) * 8
    small_parts = jnp.concatenate([small_parts, jnp.zeros((N_DEV, pad_to - rows_small, LANES), F32)], axis=1)

    def pack_small(ng, qg, kg, lbl, hgn):
        return _pad_rows(jnp.concatenate([
            ng.reshape(n_gain_rows, d_loc), lbl, jnp.concatenate([qg, kg], axis=1), hgn], axis=0), pad_to)

    res = _adamw([small_parts],
                 pack_small(norm_gains, sb_q_gain, sb_k_gain, hg_lb_logits, hg_norm_gain),
                 pack_small(m_norm_gains, m_sb_q_gain, m_sb_k_gain, m_hg_lb_logits, m_hg_norm_gain),
                 pack_small(v_norm_gains, v_sb_q_gain, v_sb_k_gain, v_hg_lb_logits, v_hg_norm_gain), "adamw_small")

    def unpack_small(t):
        o1 = n_gain_rows
        o2 = o1 + n_hg
        o3 = o2 + n_sb
        return {"norm_gains": t[:o1].reshape(depth, 2, d_loc), "hg_lb_logits": t[o1:o2],
                "sb_q_gain": t[o2:o3, :SB_HEAD_DIM], "sb_k_gain": t[o2:o3, SB_HEAD_DIM:],
                "hg_norm_gain": t[o3:o3 + n_hg]}

    small_out = [unpack_small(t) for t in res]
    order = ["norm_gains", "sb_w_qkv", "sb_q_gain", "sb_k_gain", "sb_w_o", "hg_w_in", "hg_lb_logits",
             "hg_norm_gain", "hg_w_o", "mlp_w1", "mlp_w2"]
    outs = [loss, grad_x]
    for kind in range(4):
        outs += [big[n][kind] if n in big else small_out[kind][n] for n in order]
    return tuple(outs)
```

```python
import functools
import math

import numpy as np
import jax
import jax.numpy as jnp
from jax import lax
from jax.experimental import pallas as pl
from jax.experimental.pallas import tpu as pltpu

F32 = jnp.float32
BF16 = jnp.bfloat16
EPS = 1e-6
SB_HEAD_DIM = 64
HG_DIM = 128
LANES = 128
N_DEV = 8
AXES = ("x", "y", "c")
VMEM_LIMIT_BYTES = 48 * 1024 * 1024
MATMUL_VMEM_BUDGET = 40 * 1024 * 1024
SB_SCALE = 1.0 / math.sqrt(SB_HEAD_DIM)
ATT_BLOCK = 256
SB_DEAD = 104.0
SB_NEVER = -1e30
HG_CHUNK = 64
HG_STEP_CHUNKS = 4
HG_STEP_HEADS = 4
ADAM_LR, ADAM_B1, ADAM_B2, ADAM_EPS, ADAM_WD, ADAM_STEP = 0.001, 0.9, 0.999, 1e-08, 0.01, 10


def _call(body, **kw):
    return pl.pallas_call(body, **kw)


def _sds(shape, dtype):
    return jax.ShapeDtypeStruct(tuple(shape), dtype)


def _cparams(*sem):
    return pltpu.CompilerParams(dimension_semantics=sem or None, vmem_limit_bytes=VMEM_LIMIT_BYTES)


def _split_bf16(x):
    hi = x.astype(BF16)
    lo = (x - hi.astype(F32)).astype(BF16)
    return hi, lo


def _dot(a, b, dims):
    return lax.dot_general(a, b, (dims, ((), ())), preferred_element_type=F32)


NN = ((1,), (0,))
NT = ((1,), (1,))
TN = ((0,), (0,))


def _dot2(x, m, dims):
    hi, lo = _split_bf16(x)
    return _dot(hi, m, dims) + _dot(lo, m, dims)


def _mdot2(m, x, dims):
    hi, lo = _split_bf16(x)
    return _dot(m, hi, dims) + _dot(m, lo, dims)


def _matmul_tiles(m, n, k, a_dtype, b_dtype, io_dtypes):
    tm, tn = min(m, 1024), min(n, 1024)

    def need(tm, tn):
        blocks = tm * k * jnp.dtype(a_dtype).itemsize + tn * k * jnp.dtype(b_dtype).itemsize
        blocks += sum(tm * tn * jnp.dtype(dt).itemsize for dt in io_dtypes)
        return 2 * blocks + 2 * tm * tn * 4

    while need(tm, tn) > MATMUL_VMEM_BUDGET:
        if tm > 256:
            tm //= 2
        else:
            tn //= 2
    return tm, tn


def _matmul(kind, a, b, name, out_dtypes, epilogue=None, extras=(), rows=(), n_sums=0, riders=None):
    if kind == "nn":
        (m, k), n = a.shape, b.shape[1]
    elif kind == "nt":
        (m, k), n = a.shape, b.shape[0]
    else:
        (k, m), n = a.shape, b.shape[1]
    tm, tn = _matmul_tiles(m, n, k, a.dtype, b.dtype, list(out_dtypes) + [e.dtype for e in extras])
    assert m % tm == 0 and n % tn == 0 and (n == tn or not (rows or n_sums)), (name, a.shape, b.shape)
    a_spec = pl.BlockSpec((k, tm), lambda i, j: (0, i)) if kind == "tn" else pl.BlockSpec((tm, k), lambda i, j: (i, 0))
    b_spec = pl.BlockSpec((tn, k), lambda i, j: (j, 0)) if kind == "nt" else pl.BlockSpec((k, tn), lambda i, j: (0, j))
    o_spec = pl.BlockSpec((tm, tn), lambda i, j: (i, j))
    r_spec = pl.BlockSpec((1, tn), lambda i, j: (0, j))
    dims = {"nn": NN, "nt": NT, "tn": TN}[kind]
    n_ex, n_rows, n_out = len(extras), len(rows), len(out_dtypes)
    ride_kind, ride_arrays = riders if riders else ("gather", [])
    n_r = len(ride_arrays)
    ride_shapes, ride_scratch = _rider_shapes(ride_kind, ride_arrays)
    grid = (m // tm, n // tn)

    def body(*refs):
        a_ref, b_ref = refs[:2]
        n_in = 2 + n_ex + n_rows
        ins = refs[2:n_in]
        outs = refs[n_in + n_r:n_in + n_r + n_out + n_sums]
        ride_end = _ride(ride_kind, refs[n_in:n_in + n_r], refs[n_in + n_r + n_out + n_sums:n_in + 2 * n_r + n_out + n_sums],
                         refs[n_in + 2 * n_r + n_out + n_sums:], grid)
        acc = _dot(a_ref[...].astype(BF16), b_ref[...].astype(BF16), dims)
        res = epilogue(acc, *[e[...] for e in ins]) if epilogue is not None else (acc,)
        for o_ref, r in zip(outs[:n_out], res):
            o_ref[...] = r.astype(o_ref.dtype)
        if n_sums:
            @pl.when(pl.program_id(0) == 0)
            def _():
                for o_ref in outs[n_out:]:
                    o_ref[...] = jnp.zeros_like(o_ref)

            for o_ref, r in zip(outs[n_out:], res[n_out:]):
                o_ref[...] += r
        ride_end()

    out = _call(
        body, name=name, grid=grid,
        in_specs=[a_spec, b_spec] + [o_spec] * n_ex + [r_spec] * n_rows + [_ANY] * n_r,
        out_specs=[o_spec] * n_out + [r_spec] * n_sums + [_ANY] * n_r,
        out_shape=[_sds((m, n), dt) for dt in out_dtypes] + [_sds((1, n), F32)] * n_sums + ride_shapes,
        scratch_shapes=ride_scratch,
        compiler_params=_cparams("arbitrary", "arbitrary") if (n_sums or n_r) else _cparams("parallel", "parallel"),
    )(a, b, *extras, *rows, *ride_arrays)
    if n_r:
        return tuple(out[:n_out + n_sums]) + (list(out[n_out + n_sums:]),)
    return out if len(out) > 1 else out[0]


def _ep_add(acc, res):
    return (acc + res,)


def _ep_add_norm(acc, res, gain):
    x = acc + res
    r = lax.rsqrt(jnp.mean(x * x, axis=-1, keepdims=True) + EPS)
    return x, x * r * gain


def _ep_norm_bwd(dh, x, dres, gain):
    r = lax.rsqrt(jnp.mean(x * x, axis=-1, keepdims=True) + EPS)
    xr = x * r
    t = dh * gain
    dx = dres + r * (t - xr * jnp.mean(t * xr, axis=-1, keepdims=True))
    return dx, dx, jnp.sum(dh * xr, axis=0, keepdims=True)


def _ep_relu2(acc):
    r = jnp.maximum(acc, 0.0)
    return acc, r * r


def _ep_relu2_bwd(acc, a):
    return (acc * (2.0 * jnp.maximum(a.astype(F32), 0.0)),)


def _rmsnorm(x, g, name):
    s, d = x.shape
    tm = min(s, 512)

    def body(x_ref, g_ref, h_ref):
        xv = x_ref[...]
        r = lax.rsqrt(jnp.mean(xv * xv, axis=-1, keepdims=True) + EPS)
        h_ref[...] = (xv * r * g_ref[...]).astype(BF16)

    return _call(
        body, name=name, grid=(s // tm,),
        in_specs=[pl.BlockSpec((tm, d), lambda i: (i, 0)), pl.BlockSpec((1, d), lambda i: (0, 0))],
        out_specs=pl.BlockSpec((tm, d), lambda i: (i, 0)),
        out_shape=_sds((s, d), BF16), compiler_params=_cparams("parallel"),
    )(x, g)


def _loss_head(y, target, name):
    s, d = y.shape
    tm = min(s, 512)

    def body(y_ref, t_ref, loss_ref, dy_ref, dyb_ref):
        err = y_ref[...] - t_ref[...]
        dy = err * (1.0 / d)
        dy_ref[...] = dy
        dyb_ref[...] = dy.astype(BF16)

        @pl.when(pl.program_id(0) == 0)
        def _():
            loss_ref[...] = jnp.zeros_like(loss_ref)

        part = 0.5 * jnp.sum(jnp.mean(err * err, axis=-1, keepdims=True), axis=0, keepdims=True)
        loss_ref[...] += part

    row = pl.BlockSpec((tm, d), lambda i: (i, 0))
    return _call(
        body, name=name, grid=(s // tm,), in_specs=[row, row],
        out_specs=[pl.BlockSpec((8, LANES), lambda i: (0, 0)), row, row],
        out_shape=[_sds((8, LANES), F32), _sds((s, d), F32), _sds((s, d), BF16)],
        compiler_params=_cparams("arbitrary"),
    )(y, target)


def _head_lane_mask():
    lane = lax.broadcasted_iota(jnp.int32, (1, LANES), 1)
    return lane < SB_HEAD_DIM


def _pair_rms(xv, first):
    x2 = xv * xv
    s0 = jnp.sum(jnp.where(first, x2, 0.0), axis=-1, keepdims=True)
    s1 = jnp.sum(jnp.where(first, 0.0, x2), axis=-1, keepdims=True)
    inv = 1.0 / SB_HEAD_DIM
    return jnp.where(first, lax.rsqrt(s0 * inv + EPS), lax.rsqrt(s1 * inv + EPS))


def _pair_mean(t, first):
    s0 = jnp.sum(jnp.where(first, t, 0.0), axis=-1, keepdims=True)
    s1 = jnp.sum(jnp.where(first, 0.0, t), axis=-1, keepdims=True)
    return jnp.where(first, s0, s1) * (1.0 / SB_HEAD_DIM)


def _qk_gain_table(q_gain, k_gain, d):
    reps = d // SB_HEAD_DIM
    return jnp.stack([jnp.tile(q_gain, reps) * SB_SCALE, jnp.tile(k_gain, reps), jnp.ones((d,), F32)])[:, None, :]


QKNORM_ROWS = 256


def _qknorm(qkv, gains, name):
    s, d3 = qkv.shape
    d = d3 // 3
    tm = min(s, QKNORM_ROWS)

    def body(x_ref, g_ref, o_ref):
        first = _head_lane_mask()
        is_v = pl.program_id(0) == 2
        for c in range(d // LANES):
            cols = slice(c * LANES, (c + 1) * LANES)
            xv = x_ref[:, cols]
            normed = xv * _pair_rms(xv, first) * g_ref[0, :, cols]
            o_ref[:, cols] = jnp.where(is_v, xv, normed).astype(BF16)

    tile = pl.BlockSpec((tm, d), lambda c, i: (i, c))
    return _call(
        body, name=name, grid=(3, s // tm),
        in_specs=[tile, pl.BlockSpec((1, 1, d), lambda c, i: (c, 0, 0))], out_specs=tile,
        out_shape=_sds((s, d3), BF16), compiler_params=_cparams("parallel", "parallel"),
    )(qkv, gains)


def _qknorm_bwd(qkv, dq, dk, dv, gains, name):
    s, d3 = qkv.shape
    d = d3 // 3
    tm = min(s, QKNORM_ROWS)

    def body(x_ref, dq_ref, dk_ref, dv_ref, g_ref, dx_ref, dg_ref):
        c, i = pl.program_id(0), pl.program_id(1)

        @pl.when(i == 0)
        def _():
            dg_ref[...] = jnp.zeros_like(dg_ref)

        first = _head_lane_mask()
        for col in range(d // LANES):
            cols = slice(col * LANES, (col + 1) * LANES)
            xv = x_ref[:, cols]
            dy = jnp.where(c == 0, dq_ref[:, cols], jnp.where(c == 1, dk_ref[:, cols], dv_ref[:, cols]))
            r = _pair_rms(xv, first)
            xr = xv * r
            t = dy * g_ref[0, :, cols]
            dx = r * (t - xr * _pair_mean(t * xr, first))
            dx_ref[:, cols] = jnp.where(c == 2, dy, dx).astype(BF16)
            dg_ref[0, :, cols] += jnp.sum(dy * xr, axis=0, keepdims=True)

    tile = pl.BlockSpec((tm, d), lambda c, i: (i, c))
    vec = pl.BlockSpec((1, 1, d), lambda c, i: (c, 0, 0))

    def part(kind):
        return pl.BlockSpec((tm, d), lambda c, i: (jnp.where(c == kind, i, 0), 0))

    return _call(
        body, name=name, grid=(3, s // tm), in_specs=[tile, part(0), part(1), part(2), vec], out_specs=[tile, vec],
        out_shape=[_sds((s, d3), BF16), _sds((3, 1, d), F32)],
        compiler_params=_cparams("arbitrary", "arbitrary"),
    )(qkv, dq, dk, dv, gains)


def _softplus_parts(z):
    sp = jnp.maximum(z, 0.0) + jnp.log(1.0 + jnp.exp(-jnp.abs(z)))
    return sp, z - sp


def _diag_causal(tb):
    return lax.broadcasted_iota(jnp.int32, (tb, tb), 1) < lax.broadcasted_iota(jnp.int32, (tb, tb), 0)


def _later_keys(tb):
    r = lax.broadcasted_iota(jnp.int32, (tb, tb), 0)
    c = lax.broadcasted_iota(jnp.int32, (tb, tb), 1)
    return jnp.where(r > c, 1.0, 0.0).astype(BF16)


def _sb_scores(qa, kj, causal):
    sp, logsig = _softplus_parts(_dot(qa, kj, NT))
    return (-sp if causal is None else jnp.where(causal, -sp, 0.0)), logsig


def _sb_weights(stay, logsig, run, later, causal):
    w = jnp.exp(logsig + _dot2(stay, later, NN) + run)
    return w if causal is None else jnp.where(causal, w, 0.0)


def _sb_attention_fwd(qkvn, name, riders=()):
    s, d3 = qkvn.shape
    d = d3 // 3
    pairs, tb = d // LANES, min(ATT_BLOCK, s)
    nb = s // tb
    assert nb <= LANES
    n_r = len(riders)
    ride_shapes, ride_scratch = _rider_shapes("gather", riders)

    def body(*refs):
        q_ref, k_ref, v_ref = refs[:3]
        o_ref, runs_ref = refs[3 + n_r:5 + n_r]
        ride_end = _ride("gather", refs[3:3 + n_r], refs[5 + n_r:5 + 2 * n_r], refs[5 + 2 * n_r:], (pairs, nb))
        i = pl.program_id(1)
        first = _head_lane_mask()
        lane = lax.broadcasted_iota(jnp.int32, (1, LANES), 1)
        later = _later_keys(tb)
        q2 = q_ref[...]
        qs = (jnp.where(first, q2, jnp.zeros_like(q2)), jnp.where(first, jnp.zeros_like(q2), q2))

        def tiles(js, carry, causals):
            kvs = []
            for j in js:
                rows = pl.ds(pl.multiple_of(j * tb, tb), tb)
                kvs.append((k_ref[rows, :], v_ref[rows, :]))
            scores = [[_sb_scores(qs[h], kj, causals[t]) for h in range(2)] for t, (kj, _) in enumerate(kvs)]
            run = [carry[0][0], carry[1][0]]
            acc = [carry[0][1], carry[1][1]]
            runs = [carry[0][2], carry[1][2]]
            weights = []
            for t, j in enumerate(js):
                weights.append([_sb_weights(*scores[t][h], run[h], later, causals[t]) for h in range(2)])
                for h in range(2):
                    runs[h] = jnp.where(lane == j, run[h], runs[h])
                    run[h] = run[h] + jnp.sum(scores[t][h][0], axis=-1, keepdims=True)
            for t, (_, vj) in enumerate(kvs):
                for h in range(2):
                    acc[h] = acc[h] + _dot(weights[t][h].astype(BF16), vj, NN)
            return tuple((run[h], acc[h], runs[h]) for h in range(2))

        def alive(carry):
            return jnp.maximum(jnp.max(carry[0][0]), jnp.max(carry[1][0])) >= -SB_DEAD

        never = jnp.full((tb, LANES), SB_NEVER, F32)
        zero = (jnp.zeros((tb, 1), F32), jnp.zeros((tb, LANES), F32), never)
        diag = _diag_causal(tb)
        carry = lax.cond(i == 0, lambda: tiles([i], (zero, zero), [diag]),
                         lambda: tiles([i, i - 1], (zero, zero), [diag, None]))
        left = jnp.maximum(i - 1, 0)
        _, carry = lax.while_loop(
            lambda st: (st[0] < left // 2) & alive(st[1]),
            lambda st: (st[0] + 1, tiles([i - 2 - 2 * st[0], i - 3 - 2 * st[0]], st[1], [None, None])),
            (jnp.int32(0), carry))
        carry = lax.cond((left % 2 == 1) & alive(carry), lambda c: tiles([0], c, [None]), lambda c: c, carry)
        o_ref[...] = jnp.where(first, carry[0][1], carry[1][1])
        runs_ref[0] = carry[0][2]
        runs_ref[1] = carry[1][2]
        ride_end()

    out = _call(
        body, name=name, grid=(pairs, nb),
        in_specs=[pl.BlockSpec((tb, LANES), lambda p, i: (i, p)),
                  pl.BlockSpec((s, LANES), lambda p, i: (0, pairs + p)),
                  pl.BlockSpec((s, LANES), lambda p, i: (0, 2 * pairs + p))] + [_ANY] * n_r,
        out_specs=[pl.BlockSpec((tb, LANES), lambda p, i: (i, p)),
                   pl.BlockSpec((2, tb, LANES), lambda p, i: (p * nb + i, 0, 0))] + [_ANY] * n_r,
        out_shape=[_sds((s, d), F32), _sds((pairs * nb * 2, tb, LANES), F32)] + ride_shapes,
        scratch_shapes=ride_scratch,
        compiler_params=_cparams("arbitrary", "arbitrary"),
    )(qkvn, qkvn, qkvn, *riders)
    return out[0], out[1], list(out[2:])


def _sb_attention_bwd(qkvn, do, runs, name, riders=()):
    s, d3 = qkvn.shape
    d = d3 // 3
    pairs, tb = d // LANES, min(ATT_BLOCK, s)
    nb = s // tb
    n_r = len(riders)
    ride_shapes, ride_scratch = _rider_shapes("exchange", riders)

    def body(*refs):
        q_ref, k_ref, v_ref, do_ref, runs_ref = refs[:5]
        dq_ref, dk_ref, dv_ref = refs[5 + n_r:8 + n_r]
        ride_end = _ride("exchange", refs[5:5 + n_r], refs[8 + n_r:8 + 2 * n_r], refs[8 + 2 * n_r:], (pairs, nb))
        i = pl.program_id(1)

        @pl.when(i == 0)
        def _():
            dk_ref[...] = jnp.zeros_like(dk_ref)
            dv_ref[...] = jnp.zeros_like(dv_ref)

        first = _head_lane_mask()
        lane = lax.broadcasted_iota(jnp.int32, (1, LANES), 1)
        later = _later_keys(tb)
        q2, do2 = q_ref[...], do_ref[...].astype(BF16)
        zq = jnp.zeros_like(q2)
        qs = (jnp.where(first, q2, zq), jnp.where(first, zq, q2))
        dos = (jnp.where(first, do2, zq), jnp.where(first, zq, do2))

        def tiles(js, carry, causals):
            rows, kv = [], []
            for j in js:
                r = pl.ds(pl.multiple_of(j * tb, tb), tb)
                kj, vj = k_ref[r, :], v_ref[r, :]
                zk = jnp.zeros_like(kj)
                rows.append(r)
                kv.append([(jnp.where(first, kj, zk), jnp.where(first, vj, zk)),
                           (jnp.where(first, zk, kj), jnp.where(first, zk, vj))])
            pairs_th = [(t, h) for t in range(len(js)) for h in range(2)]
            scores = {(t, h): _sb_scores(qs[h], kv[t][h][0], causals[t]) for t, h in pairs_th}
            w, g = {}, {}
            for t, h in pairs_th:
                run = jnp.sum(jnp.where(lane == js[t], runs_ref[h], 0.0), axis=-1, keepdims=True)
                w[t, h] = _sb_weights(*scores[t, h], run, later, causals[t])
                g[t, h] = w[t, h] * _dot(dos[h], kv[t][h][1], NT)
            gsum = [carry[0], carry[1]]
            dz = {}
            for t, h in pairs_th:
                before = _dot(g[t, h].astype(BF16), later, NT) + gsum[h]
                gsum[h] = gsum[h] + jnp.sum(g[t, h], axis=-1, keepdims=True)
                sig = jnp.exp(scores[t, h][1])
                d = g[t, h] * (1.0 - sig) - before * sig
                dz[t, h] = (d if causals[t] is None else jnp.where(causals[t], d, 0.0)).astype(BF16)
            dq = carry[2]
            for t, h in pairs_th:
                dq = dq + _dot(dz[t, h], kv[t][h][0], NN)
            for t in range(len(js)):
                dk_ref[rows[t], :] += _dot(dz[t, 0], qs[0], TN) + _dot(dz[t, 1], qs[1], TN)
                dv_ref[rows[t], :] += _dot(w[t, 0].astype(BF16), dos[0], TN) + _dot(w[t, 1].astype(BF16), dos[1], TN)
            return gsum[0], gsum[1], dq

        left = jnp.maximum(i - 1, 0)
        reach = jnp.max(jnp.maximum(runs_ref[0], runs_ref[1]), axis=0, keepdims=True)
        start = jnp.sum(jnp.where((reach < -SB_DEAD) & (lane < left), 1.0, 0.0)).astype(jnp.int32)
        carry = (jnp.zeros((tb, 1), F32), jnp.zeros((tb, 1), F32), jnp.zeros((tb, LANES), F32))
        live = left - start
        carry = lax.fori_loop(
            0, live // 2, lambda jj, c: tiles([start + 2 * jj, start + 2 * jj + 1], c, [None, None]), carry)
        carry = lax.cond(live % 2 == 1, lambda c: tiles([left - 1], c, [None]), lambda c: c, carry)
        diag = _diag_causal(tb)
        carry = lax.cond(i == 0, lambda c: tiles([i], c, [diag]), lambda c: tiles([i - 1, i], c, [None, diag]), carry)
        dq_ref[...] = carry[2]
        ride_end()

    q_spec = pl.BlockSpec((tb, LANES), lambda p, i: (i, p))
    out = _call(
        body, name=name, grid=(pairs, nb),
        in_specs=[q_spec,
                  pl.BlockSpec((s, LANES), lambda p, i: (0, pairs + p)),
                  pl.BlockSpec((s, LANES), lambda p, i: (0, 2 * pairs + p)),
                  q_spec,
                  pl.BlockSpec((2, tb, LANES), lambda p, i: (p * nb + i, 0, 0))] + [_ANY] * n_r,
        out_specs=[q_spec, pl.BlockSpec((s, LANES), lambda p, i: (0, p)),
                   pl.BlockSpec((s, LANES), lambda p, i: (0, p))] + [_ANY] * n_r,
        out_shape=[_sds((s, d), F32)] * 3 + ride_shapes,
        scratch_shapes=ride_scratch,
        compiler_params=_cparams("arbitrary", "arbitrary"),
    )(qkvn, qkvn, qkvn, do, runs, *riders)
    return out[0], out[1], out[2], list(out[3:])


def _hg_tables(c):
    t = np.arange(c)[:, None]
    j = np.arange(c)[None, :]
    sums = [j <= t, j > t]
    masks = []
    m = c // 2
    while m >= 1:
        pos, base = t % (2 * m), t - t % (2 * m)
        sums.append((pos >= m) & (j >= base + m) & (j <= t))
        sums.append((pos < m) & (j > t) & (j <= base + m - 1))
        masks.append((t // (2 * m) == j // (2 * m)) & (t % (2 * m) >= m) & (j % (2 * m) < m))
        m //= 2
    return (jnp.asarray(np.concatenate(sums, 0), BF16), jnp.asarray(np.stack(masks), F32), len(masks))


def _hg_gates(qr, fr, lb):
    sq = jax.nn.sigmoid(qr)
    sg = jax.nn.sigmoid(fr)
    forget = lb + (1.0 - lb) * sg
    return sq, qr * sq, sg, forget, jnp.log(forget), (1.0 - lb) * (1.0 - sg)


def _hg_scores(q, k, x, masks_ref, c, levels):
    eye = (lax.broadcasted_iota(jnp.int32, (c, c), 0) == lax.broadcasted_iota(jnp.int32, (c, c), 1)).astype(F32)
    scores = eye * jnp.sum(q * k, axis=-1, keepdims=True)
    ops = []
    for l in range(levels):
        qm = (q * x[(2 + 2 * l) * c:(3 + 2 * l) * c]).astype(BF16)
        km = (k * x[(3 + 2 * l) * c:(4 + 2 * l) * c]).astype(BF16)
        scores = scores + masks_ref[l] * _dot(qm, km, NT)
        ops.append((qm, km))
    return scores, eye, ops


def _hg_layout(s, d):
    heads, c = d // HG_DIM, min(HG_CHUNK, s)
    nsub = min(HG_STEP_CHUNKS, s // c)
    hp = HG_STEP_HEADS if heads % HG_STEP_HEADS == 0 else 1
    return heads, c, nsub, hp


def _hg_specs(s, d, reverse):
    heads, c, nsub, hp = _hg_layout(s, d)
    rows, width, groups, n_steps = c * nsub, hp * HG_DIM, heads // hp, s // (c * nsub)

    def step(si):
        return n_steps - 1 - si if reverse else si
    proj = [pl.BlockSpec((rows, width), functools.partial(lambda g, si, part: (step(si), part * groups + g), part=p))
            for p in range(4)]
    tile = pl.BlockSpec((rows, width), lambda g, si: (step(si), g))
    lb = pl.BlockSpec((1, width), lambda g, si: (0, g))
    gain = pl.BlockSpec((1, HG_DIM), lambda g, si: (0, 0))
    state = pl.BlockSpec((hp, nsub, HG_DIM, HG_DIM), lambda g, si: (g, step(si), 0, 0))
    return proj, tile, lb, gain, state, (groups, n_steps)


def _hgrn2_fwd(proj, lb, gain, name, riders=()):
    s, d4 = proj.shape
    d = d4 // 4
    heads, c, nsub, hp = _hg_layout(s, d)
    sums, masks, levels = _hg_tables(c)
    n_r = len(riders)
    ride_shapes, ride_scratch = _rider_shapes("gather", riders)
    pspecs, tile, lbs, gs, state, grid = _hg_specs(s, d, False)

    def body(*refs):
        qr_ref, fr_ref, ir_ref, gr_ref, lb_ref, gain_ref, sums_ref, masks_ref = refs[:8]
        og_ref, o_ref, states_ref = refs[8 + n_r:11 + n_r]
        st_ref = refs[11 + 2 * n_r]
        ride_end = _ride("gather", refs[8:8 + n_r], refs[11 + n_r:11 + 2 * n_r], refs[12 + 2 * n_r:], grid)

        @pl.when(pl.program_id(1) == 0)
        def _():
            st_ref[...] = jnp.zeros_like(st_ref)

        lbv, gainv = lb_ref[...], gain_ref[...]
        units = [(ci, hh) for ci in range(nsub) for hh in range(hp)]

        def lanes(hh):
            return slice(hh * HG_DIM, (hh + 1) * HG_DIM)

        pre = []
        for ci in range(nsub):
            rows = slice(ci * c, (ci + 1) * c)
            _, q, _, _, lf, k = _hg_gates(qr_ref[rows, :], fr_ref[rows, :], lbv)
            pre.append((q, k, jnp.exp(_mdot2(sums_ref[...], lf, NN))))
        scores, qh, vb, update = {}, {}, {}, {}
        for ci, hh in units:
            q, k, x = (a[:, lanes(hh)] for a in pre[ci])
            scores[ci, hh] = _hg_scores(q, k, x, masks_ref, c, levels)[0].astype(BF16)
            qh[ci, hh] = (q * x[0:c]).astype(BF16)
            vb[ci, hh] = ir_ref[ci * c:(ci + 1) * c, lanes(hh)].astype(BF16)
            update[ci, hh] = _dot(vb[ci, hh], (k * x[c:2 * c]).astype(BF16), TN)
        intra = {u: _dot(scores[u], vb[u], NN) for u in units}
        for hh in range(hp):
            st = st_ref[hh]
            for ci in range(nsub):
                rows = slice(ci * c, (ci + 1) * c)
                states_ref[hh, ci] = st
                o = _dot(qh[ci, hh], st.astype(BF16), NT) + intra[ci, hh]
                st = st * pre[ci][2][c - 1:c, lanes(hh)] + update[ci, hh]
                o_ref[rows, lanes(hh)] = o
                r = lax.rsqrt(jnp.mean(o * o, axis=-1, keepdims=True) + EPS)
                og_ref[rows, lanes(hh)] = (o * r * gainv * jax.nn.sigmoid(gr_ref[rows, lanes(hh)])).astype(BF16)
            st_ref[hh] = st
        ride_end()

    const = [pl.BlockSpec(sums.shape, lambda g, si: (0, 0)), pl.BlockSpec(masks.shape, lambda g, si: (0, 0, 0))]
    out = _call(
        body, name=name, grid=grid, in_specs=pspecs + [lbs, gs] + const + [_ANY] * n_r,
        out_specs=[tile, tile, state] + [_ANY] * n_r,
        out_shape=[_sds((s, d), BF16), _sds((s, d), F32), _sds((heads, s // c, HG_DIM, HG_DIM), F32)] + ride_shapes,
        scratch_shapes=[pltpu.VMEM((hp, HG_DIM, HG_DIM), F32)] + ride_scratch,
        compiler_params=_cparams("arbitrary", "arbitrary"),
    )(proj, proj, proj, proj, lb, gain, sums, masks, *riders)
    return out[0], out[1], out[2], list(out[3:])


def _hgrn2_bwd(proj, lb, gain, o, states, dog, name, riders=()):
    s, d4 = proj.shape
    d = d4 // 4
    heads, c, nsub, hp = _hg_layout(s, d)
    sums, masks, levels = _hg_tables(c)
    n_r = len(riders)
    ride_shapes, ride_scratch = _rider_shapes("exchange", riders)
    pspecs, tile, lbs, gs, state, grid = _hg_specs(s, d, True)

    def body(*refs):
        (qr_ref, fr_ref, ir_ref, gr_ref, lb_ref, gain_ref, sums_ref, masks_ref, o_ref, states_ref, dog_ref) = refs[:11]
        dq_ref, df_ref, di_ref, dg_ref, dlb_ref, dgain_ref = refs[11 + n_r:17 + n_r]
        dst_ref = refs[17 + 2 * n_r]
        ride_end = _ride("exchange", refs[11:11 + n_r], refs[17 + n_r:17 + 2 * n_r], refs[18 + 2 * n_r:], grid)

        @pl.when(pl.program_id(1) == 0)
        def _():
            dst_ref[...] = jnp.zeros_like(dst_ref)
            dlb_ref[...] = jnp.zeros_like(dlb_ref)
            dgain_ref[...] = jnp.zeros_like(dgain_ref)

        lbv, gainv = lb_ref[...], gain_ref[...]
        units = [(ci, hh) for ci in range(nsub) for hh in range(hp)]

        def lanes(hh):
            return slice(hh * HG_DIM, (hh + 1) * HG_DIM)

        def rows_of(ci):
            return slice(ci * c, (ci + 1) * c)

        pre = []
        for ci in range(nsub):
            qr = qr_ref[rows_of(ci), :]
            sq, q, sg, forget, lf, k = _hg_gates(qr, fr_ref[rows_of(ci), :], lbv)
            pre.append(dict(qr=qr, sq=sq, q=q, sg=sg, forget=forget, k=k, x=jnp.exp(_mdot2(sums_ref[...], lf, NN))))

        dob, vb, sc, qh_f, kh_f, feed = {}, {}, {}, {}, {}, {}
        dgain = [jnp.zeros((1, HG_DIM), F32) for _ in range(hp)]
        for ci, hh in units:
            rows, ln = rows_of(ci), lanes(hh)
            ov, gate = o_ref[rows, ln], jax.nn.sigmoid(gr_ref[rows, ln])
            r = lax.rsqrt(jnp.mean(ov * ov, axis=-1, keepdims=True) + EPS)
            orr = ov * r
            dogv = dog_ref[rows, ln]
            dg_ref[rows, ln] = (dogv * orr * gainv * gate * (1.0 - gate)).astype(BF16)
            don = dogv * gate
            dgain[hh] = dgain[hh] + jnp.sum(don * orr, axis=0, keepdims=True)
            t = don * gainv
            dob[ci, hh] = (r * (t - orr * jnp.mean(t * orr, axis=-1, keepdims=True))).astype(BF16)
            q, k, x = (pre[ci][n][:, ln] for n in ("q", "k", "x"))
            vb[ci, hh] = ir_ref[rows, ln].astype(BF16)
            sc[ci, hh] = _hg_scores(q, k, x, masks_ref, c, levels)
            qh_f[ci, hh], kh_f[ci, hh] = q * x[0:c], k * x[c:2 * c]
            feed[ci, hh] = _dot(dob[ci, hh], qh_f[ci, hh].astype(BF16), TN)

        dsts = {}
        for hh in range(hp):
            dst = dst_ref[hh]
            for ci in reversed(range(nsub)):
                dsts[ci, hh] = dst
                dst = dst * pre[ci]["x"][c - 1:c, lanes(hh)] + feed[ci, hh]
            dst_ref[hh] = dst

        dlb = [jnp.zeros((1, HG_DIM), F32) for _ in range(hp)]
        part = {(ci, hh): {n: v[:, lanes(hh)] for n, v in pre[ci].items()} for ci, hh in units}
        dscores, decay_grad, dq, dk, dexp = {}, {}, {}, {}, {}
        for u in units:
            ci, hh = u
            x, (scores, eye, _) = part[u]["x"], sc[u]
            st, dst = states_ref[hh, ci], dsts[u]
            dstb = dst.astype(BF16)
            dscores[u] = _dot(dob[u], vb[u], NT)
            di_ref[rows_of(ci), lanes(hh)] = (_dot(scores.astype(BF16), dob[u], TN)
                                              + _dot(kh_f[u].astype(BF16), dstb, NT)).astype(BF16)
            dqh = _dot(dob[u], st.astype(BF16), NN)
            dkh = _dot(vb[u], dstb, NN)
            decay_grad[u] = x[c - 1:c] * jnp.sum(dst * st, axis=0, keepdims=True)
            ddiag = jnp.sum(eye * dscores[u], axis=-1, keepdims=True)
            dq[u] = dqh * x[0:c] + ddiag * part[u]["k"]
            dk[u] = dkh * x[c:2 * c] + ddiag * part[u]["q"]
            dexp[u] = [dqh * qh_f[u], dkh * kh_f[u]]
        for l in range(levels):
            for u in units:
                q, k, x = part[u]["q"], part[u]["k"], part[u]["x"]
                qm, km = sc[u][2][l]
                dsm = (masks_ref[l] * dscores[u]).astype(BF16)
                dqm, dkm = _dot(dsm, km, NN), _dot(dsm, qm, TN)
                xq, xk = x[(2 + 2 * l) * c:(3 + 2 * l) * c], x[(3 + 2 * l) * c:(4 + 2 * l) * c]
                dq[u] = dq[u] + dqm * xq
                dk[u] = dk[u] + dkm * xk
                dexp[u] += [dqm * (q * xq), dkm * (k * xk)]
        for u in units:
            ci, hh = u
            rows, ln, p = rows_of(ci), lanes(hh), part[u]
            dlf = _mdot2(sums_ref[...], jnp.concatenate(dexp[u], axis=0), TN) + decay_grad[u]
            dforget = dlf / p["forget"] - dk[u]
            dlb[hh] = dlb[hh] + jnp.sum(dforget * (1.0 - p["sg"]), axis=0, keepdims=True)
            df_ref[rows, ln] = (dforget * (1.0 - lbv[:, ln]) * p["sg"] * (1.0 - p["sg"])).astype(BF16)
            dq_ref[rows, ln] = (dq[u] * p["sq"] * (1.0 + p["qr"] * (1.0 - p["sq"]))).astype(BF16)
        for hh in range(hp):
            dlb_ref[:, lanes(hh)] += dlb[hh]
            dgain_ref[hh] += dgain[hh]
        ride_end()

    const = [pl.BlockSpec(sums.shape, lambda g, si: (0, 0)), pl.BlockSpec(masks.shape, lambda g, si: (0, 0, 0))]
    out = _call(
        body, name=name, grid=grid, in_specs=pspecs + [lbs, gs] + const + [tile, state, tile] + [_ANY] * n_r,
        out_specs=[tile, tile, tile, tile, lbs, pl.BlockSpec((hp, 1, HG_DIM), lambda g, si: (g, 0, 0))] + [_ANY] * n_r,
        out_shape=[_sds((s, d), BF16)] * 4 + [_sds((1, d), F32), _sds((heads, 1, HG_DIM), F32)] + ride_shapes,
        scratch_shapes=[pltpu.VMEM((hp, HG_DIM, HG_DIM), F32)] + ride_scratch,
        compiler_params=_cparams("arbitrary", "arbitrary"),
    )(proj, proj, proj, proj, lb, gain, sums, masks, o, states, dog, *riders)
    return tuple(out[:6]) + (list(out[6:]),)


def _lower_bounds(logits, name):
    n, d = logits.shape

    def body(l_ref, lb_ref):
        lv = l_ref[...]
        e = jnp.exp(lv - jnp.max(lv, axis=0, keepdims=True))
        p = e / jnp.sum(e, axis=0, keepdims=True)
        run = jnp.zeros((1, d), F32)
        for j in range(n):
            if j > 0:
                run = run + p[j:j + 1]
            lb_ref[j:j + 1, :] = run

    return _call(body, name=name, out_shape=_sds((n, d), F32))(logits)


def _lower_bounds_bwd(logits, dlb_parts, name):
    n, d = logits.shape

    def body(l_ref, dlb_ref, dl_ref):
        lv, dv = l_ref[...], dlb_ref[0]
        for dev in range(1, N_DEV):
            dv = dv + dlb_ref[dev]
        e = jnp.exp(lv - jnp.max(lv, axis=0, keepdims=True))
        p = e / jnp.sum(e, axis=0, keepdims=True)
        run = jnp.zeros((1, d), F32)
        dps = [None] * n
        for j in range(n - 1, 0, -1):
            run = run + dv[j:j + 1]
            dps[j] = run
        dps[0] = jnp.zeros((1, d), F32)
        inner = jnp.zeros((1, d), F32)
        for j in range(n):
            inner = inner + p[j:j + 1] * dps[j]
        for j in range(n):
            dl_ref[j:j + 1, :] = p[j:j + 1] * (dps[j] - inner)

    return _call(body, name=name, out_shape=_sds((n, d), F32))(logits, dlb_parts)


_ANY = pl.BlockSpec(memory_space=pl.ANY)
_MESH = pl.DeviceIdType.MESH


def _gather_stages(x_ref, out_ref, send_sems, recv_sems, local_sem):
    mx, my, mc = lax.axis_index("x"), lax.axis_index("y"), lax.axis_index("c")
    me, sibling = (mx, my, mc), (mx, my, 1 - mc)
    chips = [(1 - mx, my), (mx, 1 - my), (1 - mx, 1 - my)]

    def slot(px, py, pc):
        return out_ref.at[4 * px + 2 * py + pc]

    def copy(k, block, to, src=None):
        return pltpu.make_async_remote_copy(
            src_ref=slot(*block) if src is None else src, dst_ref=slot(*block),
            send_sem=send_sems.at[k], recv_sem=recv_sems.at[k], device_id=to, device_id_type=_MESH)

    mine = pltpu.make_async_copy(x_ref, slot(*me), local_sem)
    first = [copy(0, me, sibling, src=x_ref)] + [copy(1 + j, me, (*chip, mc), src=x_ref) for j, chip in enumerate(chips)]
    passed = [copy(4 + j, (*chip, mc), sibling) for j, chip in enumerate(chips)]

    def start():
        mine.start()
        for cp in first:
            cp.start()

    def middle():
        for j, chip in enumerate(chips):
            copy(1 + j, (*chip, mc), me).wait_recv()
            passed[j].start()

    def finish():
        copy(0, sibling, me).wait_recv()
        for j, chip in enumerate(chips):
            copy(4 + j, (*chip, 1 - mc), me).wait_recv()
        for cp in first + passed:
            cp.wait_send()
        mine.wait()

    return start, middle, finish


def _exchange_stages(g_ref, out_ref, send_sems, recv_sems, local_sem):
    mx, my, mc = lax.axis_index("x"), lax.axis_index("y"), lax.axis_index("c")
    me = 4 * mx + 2 * my + mc
    mine = pltpu.make_async_copy(g_ref.at[me], out_ref.at[me], local_sem)
    copies = []
    for k in range(1, N_DEV):
        px, py, pc = mx ^ (k >> 2), my ^ ((k >> 1) & 1), mc ^ (k & 1)
        peer = 4 * px + 2 * py + pc
        send = pltpu.make_async_remote_copy(
            src_ref=g_ref.at[peer], dst_ref=out_ref.at[me], send_sem=send_sems.at[k - 1],
            recv_sem=recv_sems.at[k - 1], device_id=(px, py, pc), device_id_type=_MESH)
        arrival = pltpu.make_async_remote_copy(
            src_ref=g_ref.at[peer], dst_ref=out_ref.at[peer], send_sem=send_sems.at[k - 1],
            recv_sem=recv_sems.at[k - 1], device_id=(px, py, pc), device_id_type=_MESH)
        copies.append((send, arrival))

    def start():
        mine.start()
        for send, _ in copies:
            send.start()

    def finish():
        for _, arrival in copies:
            arrival.wait_recv()
        for send, _ in copies:
            send.wait_send()
        mine.wait()

    return start, lambda: None, finish


_STAGES = {"gather": _gather_stages, "exchange": _exchange_stages}
SEMS_PER_TRANSFER = 3


def _rider_shapes(kind, arrays):
    outs = [_sds((N_DEV,) + a.shape if kind == "gather" else a.shape, a.dtype) for a in arrays]
    scratch = []
    for _ in arrays:
        scratch += [pltpu.SemaphoreType.DMA((7,)), pltpu.SemaphoreType.DMA((7,)), pltpu.SemaphoreType.DMA]
    return outs, scratch


def _ride(kind, in_refs, out_refs, sems, grid):
    stages = [_STAGES[kind](a, o, *sems[SEMS_PER_TRANSFER * n:SEMS_PER_TRANSFER * (n + 1)])
              for n, (a, o) in enumerate(zip(in_refs, out_refs))]
    if not stages:
        return lambda: None
    p, i = pl.program_id(0), pl.program_id(1)

    def run(stage):
        for st in stages:
            st[stage]()

    pl.when((p == 0) & (i == 0))(lambda: run(0))
    pl.when((p == grid[0] // 2) & (i == 0))(lambda: run(1))
    return lambda: pl.when((p == grid[0] - 1) & (i == grid[1] - 1))(lambda: run(2))


def _transfer(kind, arrays, name):
    outs, scratch = _rider_shapes(kind, arrays)
    n = len(arrays)

    def body(*refs):
        stages = [_STAGES[kind](refs[t], refs[n + t], *refs[2 * n + SEMS_PER_TRANSFER * t:2 * n + SEMS_PER_TRANSFER * (t + 1)])
                  for t in range(n)]
        for stage in range(3):
            for st in stages:
                st[stage]()

    return _call(body, name=name, out_shape=outs, in_specs=[_ANY] * n, out_specs=[_ANY] * n, scratch_shapes=scratch)(*arrays)


def _all_gather(x, name):
    return _transfer("gather", [x], name)[0]


def _adamw(parts, w, m, v, name):
    n_l = len(parts)
    _, r, c = parts[0].shape
    tr = r if r <= 256 else 256
    assert r % tr == 0 and w.shape == (n_l * r, c), (name, parts[0].shape, w.shape)
    steps = r // tr

    def body(*refs):
        p_refs, (w_ref, m_ref, v_ref) = refs[:n_l], refs[n_l:n_l + 3]
        g_ref, d_ref, nm_ref, nv_ref, sum_ref = refs[n_l + 3:]
        for layer in range(n_l):
            @pl.when(pl.program_id(0) == layer)
            def _(p_ref=p_refs[layer]):
                acc = p_ref[0].astype(F32)
                for dev in range(1, N_DEV):
                    acc = acc + p_ref[dev].astype(F32)
                sum_ref[...] = acc

        g = sum_ref[...]
        nm = ADAM_B1 * m_ref[...] + (1.0 - ADAM_B1) * g
        nv = ADAM_B2 * v_ref[...] + (1.0 - ADAM_B2) * (g * g)
        m_hat = nm / (1.0 - ADAM_B1 ** ADAM_STEP)
        v_hat = nv / (1.0 - ADAM_B2 ** ADAM_STEP)
        g_ref[...] = g
        nm_ref[...] = nm
        nv_ref[...] = nv
        d_ref[...] = -ADAM_LR * (m_hat / (jnp.sqrt(v_hat) + ADAM_EPS) + ADAM_WD * w_ref[...])

    tile = pl.BlockSpec((tr, c), lambda l, i: (l * steps + i, 0))

    def part(layer):
        return pl.BlockSpec((N_DEV, tr, c), lambda l, i: (0, jnp.where(l == layer, i, 0), 0))

    return _call(
        body, name=name, grid=(n_l, steps),
        in_specs=[part(layer) for layer in range(n_l)] + [tile, tile, tile], out_specs=[tile] * 4,
        out_shape=[_sds((n_l * r, c), F32)] * 4, scratch_shapes=[pltpu.VMEM((tr, c), F32)],
        compiler_params=_cparams("arbitrary", "arbitrary"),
    )(*parts, w, m, v)


def _shard_2d(w):
    return w.astype(BF16).reshape(w.shape[0] * w.shape[1], w.shape[2])


def _full_cols(g, w):
    l, k, n = w.shape
    g = g.reshape(N_DEV, l, k, n)
    return [jnp.transpose(g[:, i], (1, 0, 2)).reshape(k, N_DEV * n) for i in range(l)]


def _full_rows(g, w):
    l, k, n = w.shape
    g = g.reshape(N_DEV, l, k, n)
    return [g[:, i].reshape(N_DEV * k, n) for i in range(l)]


def _parts_cols(grads):
    k, n8 = grads[0].shape
    g = jnp.stack(grads).reshape(len(grads), k, N_DEV, n8 // N_DEV)
    return jnp.transpose(g, (2, 0, 1, 3)).reshape(N_DEV, len(grads) * k, n8 // N_DEV)


def _parts_rows(grads):
    k8, n = grads[0].shape
    g = jnp.stack(grads).reshape(len(grads), N_DEV, k8 // N_DEV, n)
    return jnp.transpose(g, (1, 0, 2, 3)).reshape(N_DEV, len(grads) * (k8 // N_DEV), n)


def _pad_rows(a, rows):
    return jnp.concatenate([a, jnp.zeros((rows - a.shape[0], a.shape[1]), a.dtype)], axis=0)


def kernel(x, norm_gains, sb_w_qkv, sb_q_gain, sb_k_gain, sb_w_o, hg_w_in, hg_lb_logits, hg_norm_gain, hg_w_o, mlp_w1, mlp_w2, loss_target, m_norm_gains, m_sb_w_qkv, m_sb_q_gain, m_sb_k_gain, m_sb_w_o, m_hg_w_in, m_hg_lb_logits, m_hg_norm_gain, m_hg_w_o, m_mlp_w1, m_mlp_w2, v_norm_gains, v_sb_w_qkv, v_sb_q_gain, v_sb_k_gain, v_sb_w_o, v_hg_w_in, v_hg_lb_logits, v_hg_norm_gain, v_hg_w_o, v_mlp_w1, v_mlp_w2):
    depth, _, d_loc = norm_gains.shape
    n_sb, n_hg = sb_w_qkv.shape[0], hg_w_in.shape[0]
    xs = x[0]
    target = loss_target[0]
    s, d = xs.shape
    me = 4 * lax.axis_index("x") + 2 * lax.axis_index("y") + lax.axis_index("c")

    assert n_sb >= 1
    w_qkv = _full_cols(_all_gather(_shard_2d(sb_w_qkv[:1]), "gather_w_qkv"), sb_w_qkv[:1])
    w_o = w_in = w_ho = None
    w_1s, w_2s = [None] * depth, [None] * depth
    n_gain_rows = 2 * depth
    small_rows = -(-(n_gain_rows + n_hg) // 8) * 8
    small = _pad_rows(jnp.concatenate([norm_gains.reshape(n_gain_rows, d_loc), hg_lb_logits], axis=0), small_rows)
    small = _all_gather(small, "gather_small")
    gains_full = jnp.transpose(small[:, :n_gain_rows], (1, 0, 2)).reshape(depth, 2, 1, d)
    logits_full = jnp.transpose(small[:, n_gain_rows:n_gain_rows + n_hg], (1, 0, 2)).reshape(n_hg, d)
    lower = _lower_bounds(logits_full, "lower_bounds")

    saved = []
    cur = xs
    h = _rmsnorm(cur, gains_full[0, 0], "norm_mix_0")
    for layer in range(depth):
        j = layer // 2
        first = layer == 0
        riders = [_shard_2d(mlp_w1[layer:layer + 1]), _shard_2d(mlp_w2[layer:layer + 1])]
        if layer % 2 == 0:
            if first:
                qkv, (got_o,) = _matmul("nn", h, w_qkv[j], f"qkv_{layer}", [F32], riders=("gather", [_shard_2d(sb_w_o)]))
                w_o = _full_rows(got_o, sb_w_o)
                riders.append(_shard_2d(hg_w_in))
            else:
                qkv = _matmul("nn", h, w_qkv[j], f"qkv_{layer}", [F32])
            qk_gains = _qk_gain_table(sb_q_gain[j], sb_k_gain[j], d)
            qkvn = _qknorm(qkv, qk_gains, f"qknorm_{layer}")
            o, runs, got = _sb_attention_fwd(qkvn, f"sb_fwd_{layer}", riders)
            if first:
                w_in = _full_cols(got[2], hg_w_in)
            mix, mixed, w_out = (qkv, qk_gains, qkvn, o, runs), o, w_o[j]
        else:
            proj = _matmul("nn", h, w_in[j], f"hg_in_{layer}", [F32])
            og, o, states, got = _hgrn2_fwd(proj, lower[j:j + 1], hg_norm_gain[j:j + 1], f"hg_fwd_{layer}", riders)
            mix, mixed, w_out = (proj, og, o, states), og, w_ho[j]
        w_1s[layer] = w_1 = _full_cols(got[0], mlp_w1[layer:layer + 1])[0]
        w_2s[layer] = w_2 = _full_rows(got[1], mlp_w2[layer:layer + 1])[0]
        x1, h2 = _matmul("nn", mixed, w_out, f"mix_out_{layer}", [F32, BF16], _ep_add_norm, [cur],
                         rows=[gains_full[layer, 1]])
        if first:
            a, u, (got_ho,) = _matmul("nn", h2, w_1, f"mlp_up_{layer}", [BF16, BF16], _ep_relu2,
                                      riders=("gather", [_shard_2d(hg_w_o)]))
            w_ho = _full_rows(got_ho, hg_w_o)
        else:
            a, u = _matmul("nn", h2, w_1, f"mlp_up_{layer}", [BF16, BF16], _ep_relu2)
        saved.append((cur, h, mix, x1, h2, a, u))
        if layer == depth - 1:
            cur = _matmul("nn", u, w_2, f"mlp_down_{layer}", [F32], _ep_add, [x1])
        elif first and n_sb > 1:
            cur, h, (got_qkv,) = _matmul("nn", u, w_2, f"mlp_down_{layer}", [F32, BF16], _ep_add_norm, [x1],
                                         rows=[gains_full[layer + 1, 0]], riders=("gather", [_shard_2d(sb_w_qkv[1:])]))
            w_qkv = w_qkv + _full_cols(got_qkv, sb_w_qkv[1:])
        else:
            cur, h = _matmul("nn", u, w_2, f"mlp_down_{layer}", [F32, BF16], _ep_add_norm, [x1],
                             rows=[gains_full[layer + 1, 0]])

    loss_tile, dx, dxb = _loss_head(cur, target, "loss_head")
    loss = lax.psum(loss_tile[0, 0], AXES)

    d_gains = [[None, None] for _ in range(depth)]
    d_qk, d_lb, d_hgain = [None] * n_sb, [None] * n_hg, [None] * n_hg
    received = {"sb_w_qkv": [None] * n_sb, "sb_w_o": [None] * n_sb, "hg_w_in": [None] * n_hg,
                "hg_w_o": [None] * n_hg, "mlp_w1": [None] * depth, "mlp_w2": [None] * depth}
    pending = []

    def settle(got):
        for (wname, idx, _), arrived in zip(pending, got):
            received[wname][idx] = arrived
        pending.clear()

    for layer in reversed(range(depth)):
        j = layer // 2
        x0, h, mix, x1, h2, a, u = saved[layer]
        pending.append(("mlp_w2", layer, _parts_rows([_matmul("tn", u, dxb, f"d_mlp_w2_{layer}", [BF16])])))
        da = _matmul("nt", dxb, w_2s[layer], f"d_mlp_act_{layer}", [BF16], _ep_relu2_bwd, [a])
        pending.append(("mlp_w1", layer, _parts_cols([_matmul("tn", h2, da, f"d_mlp_w1_{layer}", [BF16])])))
        dx, dxb, d_gains[layer][1] = _matmul("nt", da, w_1s[layer], f"d_mlp_in_{layer}", [F32, BF16], _ep_norm_bwd,
                                             [x1, dx], rows=[gains_full[layer, 1]], n_sums=1)
        if layer % 2 == 0:
            qkv, qk_gains, qkvn, o, runs = mix
            pending.append(("sb_w_o", j, _parts_rows([_matmul("tn", o, dxb, f"d_sb_w_o_{layer}", [BF16])])))
            do = _matmul("nt", dxb, w_o[j], f"d_sb_o_{layer}", [F32])
            dq, dk, dv, got = _sb_attention_bwd(qkvn, do, runs, f"sb_bwd_{layer}", [p for _, _, p in pending])
            settle(got)
            dqkv, d_qk[j] = _qknorm_bwd(qkv, dq, dk, dv, qk_gains, f"d_qknorm_{layer}")
            pending.append(("sb_w_qkv", j, _parts_cols([_matmul("tn", h, dqkv, f"d_sb_w_qkv_{layer}", [BF16])])))
            dx, dxb, d_gains[layer][0], got = _matmul(
                "nt", dqkv, w_qkv[j], f"d_sb_in_{layer}", [F32, BF16], _ep_norm_bwd, [x0, dx],
                rows=[gains_full[layer, 0]], n_sums=1, riders=("exchange", [p for _, _, p in pending]))
            settle(got)
        else:
            proj, og, o, states = mix
            pending.append(("hg_w_o", j, _parts_rows([_matmul("tn", og, dxb, f"d_hg_w_o_{layer}", [BF16])])))
            dog = _matmul("nt", dxb, w_ho[j], f"d_hg_o_{layer}", [F32])
            dq, df, di, dg, d_lb[j], d_hgain[j], got = _hgrn2_bwd(
                proj, lower[j:j + 1], hg_norm_gain[j:j + 1], o, states, dog, f"hg_bwd_{layer}",
                [p for _, _, p in pending])
            settle(got)
            dproj = jnp.concatenate([dq, df, di, dg], axis=1)
            pending.append(("hg_w_in", j, _parts_cols([_matmul("tn", h, dproj, f"d_hg_w_in_{layer}", [BF16])])))
            dx, dxb, d_gains[layer][0] = _matmul("nt", dproj, w_in[j], f"d_hg_in_{layer}", [F32, BF16], _ep_norm_bwd,
                                                 [x0, dx], rows=[gains_full[layer, 0]], n_sums=1)
    grad_x = dx[None]
    if pending:
        settle(_transfer("exchange", [p for _, _, p in pending], "exchange_tail"))

    def update(wname, w, m, v):
        shape = w.shape
        flat = (shape[0] * shape[1], shape[2])
        res = _adamw(received[wname], w.reshape(flat), m.reshape(flat), v.reshape(flat), "adamw_" + wname)
        return [r.reshape(shape) for r in res]

    big = {
        "sb_w_qkv": update("sb_w_qkv", sb_w_qkv, m_sb_w_qkv, v_sb_w_qkv),
        "sb_w_o": update("sb_w_o", sb_w_o, m_sb_w_o, v_sb_w_o),
        "hg_w_in": update("hg_w_in", hg_w_in, m_hg_w_in, v_hg_w_in),
        "hg_w_o": update("hg_w_o", hg_w_o, m_hg_w_o, v_hg_w_o),
        "mlp_w1": update("mlp_w1", mlp_w1, m_mlp_w1, v_mlp_w1),
        "mlp_w2": update("mlp_w2", mlp_w2, m_mlp_w2, v_mlp_w2),
    }

    d_gain_rows = jnp.concatenate([d_gains[l][t] for l in range(depth) for t in range(2)], axis=0)
    d_lb_rows = jnp.concatenate(d_lb, axis=0)
    def fold(t):
        return jnp.sum(t.reshape(d // SB_HEAD_DIM, SB_HEAD_DIM), axis=0, keepdims=True)
    d_qg = jnp.concatenate([fold(d_qk[i][0]) for i in range(n_sb)], axis=0) * SB_SCALE
    d_kg = jnp.concatenate([fold(d_qk[i][1]) for i in range(n_sb)], axis=0)
    d_hg = jnp.concatenate([jnp.sum(d_hgain[i], axis=0) for i in range(n_hg)], axis=0)
    per_row = d // LANES
    packed = jnp.concatenate([
        d_gain_rows.reshape(n_gain_rows * per_row, LANES), d_lb_rows.reshape(n_hg * per_row, LANES),
        jnp.concatenate([d_qg, d_kg], axis=1), d_hg], axis=0)
    n_packed = packed.shape[0]
    packed = _pad_rows(packed, -(-n_packed // 8) * 8)
    everyone = _all_gather(packed, "gather_small_grads")
    o_lb = n_gain_rows * per_row
    o_qk = o_lb + n_hg * per_row
    o_hg = o_qk + n_sb

    def mine_of(rows, count):
        return lax.dynamic_slice_in_dim(rows.reshape(N_DEV, count, per_row, LANES), me, 1, axis=2)[:, :, 0]

    d_logits_full = _lower_bounds_bwd(logits_full, everyone[:, o_lb:o_qk].reshape(N_DEV, n_hg, d), "lower_bounds_bwd")
    d_logits_mine = lax.dynamic_slice_in_dim(d_logits_full.reshape(n_hg, per_row, LANES), me, 1, axis=1)[:, 0]
    zeros7 = jnp.zeros((N_DEV - 1, n_hg, LANES), F32)
    small_parts = jnp.concatenate([
        mine_of(everyone[:, :o_lb], n_gain_rows),
        jnp.concatenate([d_logits_mine[None], zeros7], axis=0),
        everyone[:, o_qk:o_hg], everyone[:, o_hg:o_hg + n_hg]], axis=1)
    rows_small = small_parts.shape[1]
    pad_to = -(-rows_small // 8) * 8
    small_parts = jnp.concatenate([small_parts, jnp.zeros((N_DEV, pad_to - rows_small, LANES), F32)], axis=1)

    def pack_small(ng, qg, kg, lbl, hgn):
        return _pad_rows(jnp.concatenate([
            ng.reshape(n_gain_rows, d_loc), lbl, jnp.concatenate([qg, kg], axis=1), hgn], axis=0), pad_to)

    res = _adamw([small_parts],
                 pack_small(norm_gains, sb_q_gain, sb_k_gain, hg_lb_logits, hg_norm_gain),
                 pack_small(m_norm_gains, m_sb_q_gain, m_sb_k_gain, m_hg_lb_logits, m_hg_norm_gain),
                 pack_small(v_norm_gains, v_sb_q_gain, v_sb_k_gain, v_hg_lb_logits, v_hg_norm_gain), "adamw_small")

    def unpack_small(t):
        o1 = n_gain_rows
        o2 = o1 + n_hg
        o3 = o2 + n_sb
        return {"norm_gains": t[:o1].reshape(depth, 2, d_loc), "hg_lb_logits": t[o1:o2],
                "sb_q_gain": t[o2:o3, :SB_HEAD_DIM], "sb_k_gain": t[o2:o3, SB_HEAD_DIM:],
                "hg_norm_gain": t[o3:o3 + n_hg]}

    small_out = [unpack_small(t) for t in res]
    order = ["norm_gains", "sb_w_qkv", "sb_q_gain", "sb_k_gain", "sb_w_o", "hg_w_in", "hg_lb_logits",
             "hg_norm_gain", "hg_w_o", "mlp_w1", "mlp_w2"]
    outs = [loss, grad_x]
    for kind in range(4):
        outs += [big[n][kind] if n in big else small_out[kind][n] for n in order]
    return tuple(outs)
```

```python
import functools
import math

import numpy as np
import jax
import jax.numpy as jnp
from jax import lax
from jax.experimental import pallas as pl
from jax.experimental.pallas import tpu as pltpu

F32 = jnp.float32
BF16 = jnp.bfloat16
EPS = 1e-6
SB_HEAD_DIM = 64
HG_DIM = 128
LANES = 128
N_DEV = 8
AXES = ("x", "y", "c")
VMEM_LIMIT_BYTES = 48 * 1024 * 1024
MATMUL_VMEM_BUDGET = 40 * 1024 * 1024
SB_SCALE = 1.0 / math.sqrt(SB_HEAD_DIM)
ATT_BLOCK = 256
SB_DEAD = 104.0
SB_NEVER = -1e30
HG_CHUNK = 64
HG_STEP_CHUNKS = 4
HG_STEP_HEADS = 4
ADAM_LR, ADAM_B1, ADAM_B2, ADAM_EPS, ADAM_WD, ADAM_STEP = 0.001, 0.9, 0.999, 1e-08, 0.01, 10


def _call(body, **kw):
    return pl.pallas_call(body, **kw)


def _sds(shape, dtype):
    return jax.ShapeDtypeStruct(tuple(shape), dtype)


def _cparams(*sem):
    return pltpu.CompilerParams(dimension_semantics=sem or None, vmem_limit_bytes=VMEM_LIMIT_BYTES)


def _split_bf16(x):
    hi = x.astype(BF16)
    lo = (x - hi.astype(F32)).astype(BF16)
    return hi, lo


def _dot(a, b, dims):
    return lax.dot_general(a, b, (dims, ((), ())), preferred_element_type=F32)


NN = ((1,), (0,))
NT = ((1,), (1,))
TN = ((0,), (0,))


def _dot2(x, m, dims):
    hi, lo = _split_bf16(x)
    return _dot(hi, m, dims) + _dot(lo, m, dims)


def _mdot2(m, x, dims):
    hi, lo = _split_bf16(x)
    return _dot(m, hi, dims) + _dot(m, lo, dims)


def _matmul_tiles(m, n, k, a_dtype, b_dtype, io_dtypes):
    tm, tn = min(m, 1024), min(n, 1024)

    def need(tm, tn):
        blocks = tm * k * jnp.dtype(a_dtype).itemsize + tn * k * jnp.dtype(b_dtype).itemsize
        blocks += sum(tm * tn * jnp.dtype(dt).itemsize for dt in io_dtypes)
        return 2 * blocks + 2 * tm * tn * 4

    while need(tm, tn) > MATMUL_VMEM_BUDGET:
        if tm > 256:
            tm //= 2
        else:
            tn //= 2
    return tm, tn


def _matmul(kind, a, b, name, out_dtypes, epilogue=None, extras=(), rows=(), n_sums=0, riders=None):
    if kind == "nn":
        (m, k), n = a.shape, b.shape[1]
    elif kind == "nt":
        (m, k), n = a.shape, b.shape[0]
    else:
        (k, m), n = a.shape, b.shape[1]
    tm, tn = _matmul_tiles(m, n, k, a.dtype, b.dtype, list(out_dtypes) + [e.dtype for e in extras])
    assert m % tm == 0 and n % tn == 0 and (n == tn or not (rows or n_sums)), (name, a.shape, b.shape)
    a_spec = pl.BlockSpec((k, tm), lambda i, j: (0, i)) if kind == "tn" else pl.BlockSpec((tm, k), lambda i, j: (i, 0))
    b_spec = pl.BlockSpec((tn, k), lambda i, j: (j, 0)) if kind == "nt" else pl.BlockSpec((k, tn), lambda i, j: (0, j))
    o_spec = pl.BlockSpec((tm, tn), lambda i, j: (i, j))
    r_spec = pl.BlockSpec((1, tn), lambda i, j: (0, j))
    dims = {"nn": NN, "nt": NT, "tn": TN}[kind]
    n_ex, n_rows, n_out = len(extras), len(rows), len(out_dtypes)
    ride_kind, ride_arrays = riders if riders else ("gather", [])
    n_r = len(ride_arrays)
    ride_shapes, ride_scratch = _rider_shapes(ride_kind, ride_arrays)
    grid = (m // tm, n // tn)

    def body(*refs):
        a_ref, b_ref = refs[:2]
        n_in = 2 + n_ex + n_rows
        ins = refs[2:n_in]
        outs = refs[n_in + n_r:n_in + n_r + n_out + n_sums]
        ride_end = _ride(ride_kind, refs[n_in:n_in + n_r], refs[n_in + n_r + n_out + n_sums:n_in + 2 * n_r + n_out + n_sums],
                         refs[n_in + 2 * n_r + n_out + n_sums:], grid)
        acc = _dot(a_ref[...].astype(BF16), b_ref[...].astype(BF16), dims)
        res = epilogue(acc, *[e[...] for e in ins]) if epilogue is not None else (acc,)
        for o_ref, r in zip(outs[:n_out], res):
            o_ref[...] = r.astype(o_ref.dtype)
        if n_sums:
            @pl.when(pl.program_id(0) == 0)
            def _():
                for o_ref in outs[n_out:]:
                    o_ref[...] = jnp.zeros_like(o_ref)

            for o_ref, r in zip(outs[n_out:], res[n_out:]):
                o_ref[...] += r
        ride_end()

    out = _call(
        body, name=name, grid=grid,
        in_specs=[a_spec, b_spec] + [o_spec] * n_ex + [r_spec] * n_rows + [_ANY] * n_r,
        out_specs=[o_spec] * n_out + [r_spec] * n_sums + [_ANY] * n_r,
        out_shape=[_sds((m, n), dt) for dt in out_dtypes] + [_sds((1, n), F32)] * n_sums + ride_shapes,
        scratch_shapes=ride_scratch,
        compiler_params=_cparams("arbitrary", "arbitrary") if (n_sums or n_r) else _cparams("parallel", "parallel"),
    )(a, b, *extras, *rows, *ride_arrays)
    if n_r:
        return tuple(out[:n_out + n_sums]) + (list(out[n_out + n_sums:]),)
    return out if len(out) > 1 else out[0]


def _ep_add(acc, res):
    return (acc + res,)


def _ep_add_norm(acc, res, gain):
    x = acc + res
    r = lax.rsqrt(jnp.mean(x * x, axis=-1, keepdims=True) + EPS)
    return x, x * r * gain


def _ep_norm_bwd(dh, x, dres, gain):
    r = lax.rsqrt(jnp.mean(x * x, axis=-1, keepdims=True) + EPS)
    xr = x * r
    t = dh * gain
    dx = dres + r * (t - xr * jnp.mean(t * xr, axis=-1, keepdims=True))
    return dx, dx, jnp.sum(dh * xr, axis=0, keepdims=True)


def _ep_relu2(acc):
    r = jnp.maximum(acc, 0.0)
    return acc, r * r


def _ep_relu2_bwd(acc, a):
    return (acc * (2.0 * jnp.maximum(a.astype(F32), 0.0)),)


def _rmsnorm(x, g, name):
    s, d = x.shape
    tm = min(s, 512)

    def body(x_ref, g_ref, h_ref):
        xv = x_ref[...]
        r = lax.rsqrt(jnp.mean(xv * xv, axis=-1, keepdims=True) + EPS)
        h_ref[...] = (xv * r * g_ref[...]).astype(BF16)

    return _call(
        body, name=name, grid=(s // tm,),
        in_specs=[pl.BlockSpec((tm, d), lambda i: (i, 0)), pl.BlockSpec((1, d), lambda i: (0, 0))],
        out_specs=pl.BlockSpec((tm, d), lambda i: (i, 0)),
        out_shape=_sds((s, d), BF16), compiler_params=_cparams("parallel"),
    )(x, g)


def _loss_head(y, target, name):
    s, d = y.shape
    tm = min(s, 512)

    def body(y_ref, t_ref, loss_ref, dy_ref, dyb_ref):
        err = y_ref[...] - t_ref[...]
        dy = err * (1.0 / d)
        dy_ref[...] = dy
        dyb_ref[...] = dy.astype(BF16)

        @pl.when(pl.program_id(0) == 0)
        def _():
            loss_ref[...] = jnp.zeros_like(loss_ref)

        part = 0.5 * jnp.sum(jnp.mean(err * err, axis=-1, keepdims=True), axis=0, keepdims=True)
        loss_ref[...] += part

    row = pl.BlockSpec((tm, d), lambda i: (i, 0))
    return _call(
        body, name=name, grid=(s // tm,), in_specs=[row, row],
        out_specs=[pl.BlockSpec((8, LANES), lambda i: (0, 0)), row, row],
        out_shape=[_sds((8, LANES), F32), _sds((s, d), F32), _sds((s, d), BF16)],
        compiler_params=_cparams("arbitrary"),
    )(y, target)


def _head_lane_mask():
    lane = lax.broadcasted_iota(jnp.int32, (1, LANES), 1)
    return lane < SB_HEAD_DIM


def _pair_mean(t, first):
    del first
    r = lax.broadcasted_iota(jnp.int32, (LANES, LANES), 0) // SB_HEAD_DIM
    c = lax.broadcasted_iota(jnp.int32, (LANES, LANES), 1) // SB_HEAD_DIM
    same_half = jnp.where(r == c, 1.0, 0.0).astype(BF16)
    return _dot2(t, same_half, NN) * (1.0 / SB_HEAD_DIM)


def _pair_rms(xv, first):
    return lax.rsqrt(_pair_mean(xv * xv, first) + EPS)


def _qk_gain_table(q_gain, k_gain, d):
    reps = d // SB_HEAD_DIM
    return jnp.stack([jnp.tile(q_gain, reps) * SB_SCALE, jnp.tile(k_gain, reps), jnp.ones((d,), F32)])[:, None, :]


QKNORM_ROWS = 256


def _qknorm(qkv, gains, name):
    s, d3 = qkv.shape
    d = d3 // 3
    tm = min(s, QKNORM_ROWS)

    def body(x_ref, g_ref, o_ref):
        first = _head_lane_mask()
        is_v = pl.program_id(0) == 2
        for c in range(d // LANES):
            cols = slice(c * LANES, (c + 1) * LANES)
            xv = x_ref[:, cols]
            normed = xv * _pair_rms(xv, first) * g_ref[0, :, cols]
            o_ref[:, cols] = jnp.where(is_v, xv, normed).astype(BF16)

    tile = pl.BlockSpec((tm, d), lambda c, i: (i, c))
    return _call(
        body, name=name, grid=(3, s // tm),
        in_specs=[tile, pl.BlockSpec((1, 1, d), lambda c, i: (c, 0, 0))], out_specs=tile,
        out_shape=_sds((s, d3), BF16), compiler_params=_cparams("parallel", "parallel"),
    )(qkv, gains)


def _qknorm_bwd(qkv, dq, dk, dv, gains, name):
    s, d3 = qkv.shape
    d = d3 // 3
    tm = min(s, QKNORM_ROWS)

    def body(x_ref, dq_ref, dk_ref, dv_ref, g_ref, dx_ref, dg_ref):
        c, i = pl.program_id(0), pl.program_id(1)

        @pl.when(i == 0)
        def _():
            dg_ref[...] = jnp.zeros_like(dg_ref)

        first = _head_lane_mask()
        for col in range(d // LANES):
            cols = slice(col * LANES, (col + 1) * LANES)
            xv = x_ref[:, cols]
            dy = jnp.where(c == 0, dq_ref[:, cols], jnp.where(c == 1, dk_ref[:, cols], dv_ref[:, cols]))
            r = _pair_rms(xv, first)
            xr = xv * r
            t = dy * g_ref[0, :, cols]
            dx = r * (t - xr * _pair_mean(t * xr, first))
            dx_ref[:, cols] = jnp.where(c == 2, dy, dx).astype(BF16)
            dg_ref[0, :, cols] += jnp.sum(dy * xr, axis=0, keepdims=True)

    tile = pl.BlockSpec((tm, d), lambda c, i: (i, c))
    vec = pl.BlockSpec((1, 1, d), lambda c, i: (c, 0, 0))

    def part(kind):
        return pl.BlockSpec((tm, d), lambda c, i: (jnp.where(c == kind, i, 0), 0))

    return _call(
        body, name=name, grid=(3, s // tm), in_specs=[tile, part(0), part(1), part(2), vec], out_specs=[tile, vec],
        out_shape=[_sds((s, d3), BF16), _sds((3, 1, d), F32)],
        compiler_params=_cparams("arbitrary", "arbitrary"),
    )(qkv, dq, dk, dv, gains)


def _softplus_parts(z):
    sp = jnp.maximum(z, 0.0) + jnp.log(1.0 + jnp.exp(-jnp.abs(z)))
    return sp, z - sp


def _diag_causal(tb):
    return lax.broadcasted_iota(jnp.int32, (tb, tb), 1) < lax.broadcasted_iota(jnp.int32, (tb, tb), 0)


def _later_keys(tb):
    r = lax.broadcasted_iota(jnp.int32, (tb, tb), 0)
    c = lax.broadcasted_iota(jnp.int32, (tb, tb), 1)
    return jnp.where(r > c, 1.0, 0.0).astype(BF16)


def _sb_scores(qa, kj, causal):
    sp, logsig = _softplus_parts(_dot(qa, kj, NT))
    return (-sp if causal is None else jnp.where(causal, -sp, 0.0)), logsig


def _sb_weights(stay, logsig, run, later, causal):
    w = jnp.exp(logsig + _dot2(stay, later, NN) + run)
    return w if causal is None else jnp.where(causal, w, 0.0)


def _sb_attention_fwd(qkvn, name, riders=()):
    s, d3 = qkvn.shape
    d = d3 // 3
    pairs, tb = d // LANES, min(ATT_BLOCK, s)
    nb = s // tb
    assert nb <= LANES
    n_r = len(riders)
    ride_shapes, ride_scratch = _rider_shapes("gather", riders)

    def body(*refs):
        q_ref, k_ref, v_ref = refs[:3]
        o_ref, runs_ref = refs[3 + n_r:5 + n_r]
        ride_end = _ride("gather", refs[3:3 + n_r], refs[5 + n_r:5 + 2 * n_r], refs[5 + 2 * n_r:], (pairs, nb))
        i = pl.program_id(1)
        first = _head_lane_mask()
        lane = lax.broadcasted_iota(jnp.int32, (1, LANES), 1)
        later = _later_keys(tb)
        q2 = q_ref[...]
        qs = (jnp.where(first, q2, jnp.zeros_like(q2)), jnp.where(first, jnp.zeros_like(q2), q2))

        def tiles(js, carry, causals):
            kvs = []
            for j in js:
                rows = pl.ds(pl.multiple_of(j * tb, tb), tb)
                kvs.append((k_ref[rows, :], v_ref[rows, :]))
            scores = [[_sb_scores(qs[h], kj, causals[t]) for h in range(2)] for t, (kj, _) in enumerate(kvs)]
            run = [carry[0][0], carry[1][0]]
            acc = [carry[0][1], carry[1][1]]
            runs = [carry[0][2], carry[1][2]]
            weights = []
            for t, j in enumerate(js):
                weights.append([_sb_weights(*scores[t][h], run[h], later, causals[t]) for h in range(2)])
                for h in range(2):
                    runs[h] = jnp.where(lane == j, run[h], runs[h])
                    run[h] = run[h] + jnp.sum(scores[t][h][0], axis=-1, keepdims=True)
            for t, (_, vj) in enumerate(kvs):
                for h in range(2):
                    acc[h] = acc[h] + _dot(weights[t][h].astype(BF16), vj, NN)
            return tuple((run[h], acc[h], runs[h]) for h in range(2))

        def alive(carry):
            return jnp.maximum(jnp.max(carry[0][0]), jnp.max(carry[1][0])) >= -SB_DEAD

        never = jnp.full((tb, LANES), SB_NEVER, F32)
        zero = (jnp.zeros((tb, 1), F32), jnp.zeros((tb, LANES), F32), never)
        diag = _diag_causal(tb)
        carry = lax.cond(i == 0, lambda: tiles([i], (zero, zero), [diag]),
                         lambda: tiles([i, i - 1], (zero, zero), [diag, None]))
        left = jnp.maximum(i - 1, 0)
        _, carry = lax.while_loop(
            lambda st: (st[0] < left // 2) & alive(st[1]),
            lambda st: (st[0] + 1, tiles([i - 2 - 2 * st[0], i - 3 - 2 * st[0]], st[1], [None, None])),
            (jnp.int32(0), carry))
        carry = lax.cond((left % 2 == 1) & alive(carry), lambda c: tiles([0], c, [None]), lambda c: c, carry)
        o_ref[...] = jnp.where(first, carry[0][1], carry[1][1])
        runs_ref[0] = carry[0][2]
        runs_ref[1] = carry[1][2]
        ride_end()

    out = _call(
        body, name=name, grid=(pairs, nb),
        in_specs=[pl.BlockSpec((tb, LANES), lambda p, i: (i, p)),
                  pl.BlockSpec((s, LANES), lambda p, i: (0, pairs + p)),
                  pl.BlockSpec((s, LANES), lambda p, i: (0, 2 * pairs + p))] + [_ANY] * n_r,
        out_specs=[pl.BlockSpec((tb, LANES), lambda p, i: (i, p)),
                   pl.BlockSpec((2, tb, LANES), lambda p, i: (p * nb + i, 0, 0))] + [_ANY] * n_r,
        out_shape=[_sds((s, d), F32), _sds((pairs * nb * 2, tb, LANES), F32)] + ride_shapes,
        scratch_shapes=ride_scratch,
        compiler_params=_cparams("arbitrary", "arbitrary"),
    )(qkvn, qkvn, qkvn, *riders)
    return out[0], out[1], list(out[2:])


def _sb_attention_bwd(qkvn, do, runs, name, riders=()):
    s, d3 = qkvn.shape
    d = d3 // 3
    pairs, tb = d // LANES, min(ATT_BLOCK, s)
    nb = s // tb
    n_r = len(riders)
    ride_shapes, ride_scratch = _rider_shapes("exchange", riders)

    def body(*refs):
        q_ref, k_ref, v_ref, do_ref, runs_ref = refs[:5]
        dq_ref, dk_ref, dv_ref = refs[5 + n_r:8 + n_r]
        ride_end = _ride("exchange", refs[5:5 + n_r], refs[8 + n_r:8 + 2 * n_r], refs[8 + 2 * n_r:], (pairs, nb))
        i = pl.program_id(1)

        @pl.when(i == 0)
        def _():
            dk_ref[...] = jnp.zeros_like(dk_ref)
            dv_ref[...] = jnp.zeros_like(dv_ref)

        first = _head_lane_mask()
        lane = lax.broadcasted_iota(jnp.int32, (1, LANES), 1)
        later = _later_keys(tb)
        q2, do2 = q_ref[...], do_ref[...].astype(BF16)
        zq = jnp.zeros_like(q2)
        qs = (jnp.where(first, q2, zq), jnp.where(first, zq, q2))
        dos = (jnp.where(first, do2, zq), jnp.where(first, zq, do2))

        def tiles(js, carry, causals):
            rows, kv = [], []
            for j in js:
                r = pl.ds(pl.multiple_of(j * tb, tb), tb)
                kj, vj = k_ref[r, :], v_ref[r, :]
                zk = jnp.zeros_like(kj)
                rows.append(r)
                kv.append([(jnp.where(first, kj, zk), jnp.where(first, vj, zk)),
                           (jnp.where(first, zk, kj), jnp.where(first, zk, vj))])
            pairs_th = [(t, h) for t in range(len(js)) for h in range(2)]
            scores = {(t, h): _sb_scores(qs[h], kv[t][h][0], causals[t]) for t, h in pairs_th}
            w, g = {}, {}
            for t, h in pairs_th:
                run = jnp.sum(jnp.where(lane == js[t], runs_ref[h], 0.0), axis=-1, keepdims=True)
                w[t, h] = _sb_weights(*scores[t, h], run, later, causals[t])
                g[t, h] = w[t, h] * _dot(dos[h], kv[t][h][1], NT)
            gsum = [carry[0], carry[1]]
            dz = {}
            for t, h in pairs_th:
                before = _dot(g[t, h].astype(BF16), later, NT) + gsum[h]
                gsum[h] = gsum[h] + jnp.sum(g[t, h], axis=-1, keepdims=True)
                sig = jnp.exp(scores[t, h][1])
                d = g[t, h] * (1.0 - sig) - before * sig
                dz[t, h] = (d if causals[t] is None else jnp.where(causals[t], d, 0.0)).astype(BF16)
            dq = carry[2]
            for t, h in pairs_th:
                dq = dq + _dot(dz[t, h], kv[t][h][0], NN)
            for t in range(len(js)):
                dk_ref[rows[t], :] += _dot(dz[t, 0], qs[0], TN) + _dot(dz[t, 1], qs[1], TN)
                dv_ref[rows[t], :] += _dot(w[t, 0].astype(BF16), dos[0], TN) + _dot(w[t, 1].astype(BF16), dos[1], TN)
            return gsum[0], gsum[1], dq

        left = jnp.maximum(i - 1, 0)
        reach = jnp.max(jnp.maximum(runs_ref[0], runs_ref[1]), axis=0, keepdims=True)
        start = jnp.sum(jnp.where((reach < -SB_DEAD) & (lane < left), 1.0, 0.0)).astype(jnp.int32)
        carry = (jnp.zeros((tb, 1), F32), jnp.zeros((tb, 1), F32), jnp.zeros((tb, LANES), F32))
        live = left - start
        carry = lax.fori_loop(
            0, live // 2, lambda jj, c: tiles([start + 2 * jj, start + 2 * jj + 1], c, [None, None]), carry)
        carry = lax.cond(live % 2 == 1, lambda c: tiles([left - 1], c, [None]), lambda c: c, carry)
        diag = _diag_causal(tb)
        carry = lax.cond(i == 0, lambda c: tiles([i], c, [diag]), lambda c: tiles([i - 1, i], c, [None, diag]), carry)
        dq_ref[...] = carry[2]
        ride_end()

    q_spec = pl.BlockSpec((tb, LANES), lambda p, i: (i, p))
    out = _call(
        body, name=name, grid=(pairs, nb),
        in_specs=[q_spec,
                  pl.BlockSpec((s, LANES), lambda p, i: (0, pairs + p)),
                  pl.BlockSpec((s, LANES), lambda p, i: (0, 2 * pairs + p)),
                  q_spec,
                  pl.BlockSpec((2, tb, LANES), lambda p, i: (p * nb + i, 0, 0))] + [_ANY] * n_r,
        out_specs=[q_spec, pl.BlockSpec((s, LANES), lambda p, i: (0, p)),
                   pl.BlockSpec((s, LANES), lambda p, i: (0, p))] + [_ANY] * n_r,
        out_shape=[_sds((s, d), F32)] * 3 + ride_shapes,
        scratch_shapes=ride_scratch,
        compiler_params=_cparams("arbitrary", "arbitrary"),
    )(qkvn, qkvn, qkvn, do, runs, *riders)
    return out[0], out[1], out[2], list(out[3:])


def _hg_tables(c):
    t = np.arange(c)[:, None]
    j = np.arange(c)[None, :]
    sums = [j <= t, j > t]
    masks = []
    m = c // 2
    while m >= 1:
        pos, base = t % (2 * m), t - t % (2 * m)
        sums.append((pos >= m) & (j >= base + m) & (j <= t))
        sums.append((pos < m) & (j > t) & (j <= base + m - 1))
        masks.append((t // (2 * m) == j // (2 * m)) & (t % (2 * m) >= m) & (j % (2 * m) < m))
        m //= 2
    return (jnp.asarray(np.concatenate(sums, 0), BF16), jnp.asarray(np.stack(masks), F32), len(masks))


def _hg_gates(qr, fr, lb):
    sq = jax.nn.sigmoid(qr)
    sg = jax.nn.sigmoid(fr)
    forget = lb + (1.0 - lb) * sg
    return sq, qr * sq, sg, forget, jnp.log(forget), (1.0 - lb) * (1.0 - sg)


def _hg_scores(q, k, x, masks_ref, c, levels):
    eye = (lax.broadcasted_iota(jnp.int32, (c, c), 0) == lax.broadcasted_iota(jnp.int32, (c, c), 1)).astype(F32)
    scores = eye * jnp.sum(q * k, axis=-1, keepdims=True)
    ops = []
    for l in range(levels):
        qm = (q * x[(2 + 2 * l) * c:(3 + 2 * l) * c]).astype(BF16)
        km = (k * x[(3 + 2 * l) * c:(4 + 2 * l) * c]).astype(BF16)
        scores = scores + masks_ref[l] * _dot(qm, km, NT)
        ops.append((qm, km))
    return scores, eye, ops


def _hg_layout(s, d):
    heads, c = d // HG_DIM, min(HG_CHUNK, s)
    nsub = min(HG_STEP_CHUNKS, s // c)
    hp = HG_STEP_HEADS if heads % HG_STEP_HEADS == 0 else 1
    return heads, c, nsub, hp


def _hg_specs(s, d, reverse):
    heads, c, nsub, hp = _hg_layout(s, d)
    rows, width, groups, n_steps = c * nsub, hp * HG_DIM, heads // hp, s // (c * nsub)

    def step(si):
        return n_steps - 1 - si if reverse else si
    proj = [pl.BlockSpec((rows, width), functools.partial(lambda g, si, part: (step(si), part * groups + g), part=p))
            for p in range(4)]
    tile = pl.BlockSpec((rows, width), lambda g, si: (step(si), g))
    lb = pl.BlockSpec((1, width), lambda g, si: (0, g))
    gain = pl.BlockSpec((1, HG_DIM), lambda g, si: (0, 0))
    state = pl.BlockSpec((hp, nsub, HG_DIM, HG_DIM), lambda g, si: (g, step(si), 0, 0))
    return proj, tile, lb, gain, state, (groups, n_steps)


def _hgrn2_fwd(proj, lb, gain, name, riders=()):
    s, d4 = proj.shape
    d = d4 // 4
    heads, c, nsub, hp = _hg_layout(s, d)
    sums, masks, levels = _hg_tables(c)
    n_r = len(riders)
    ride_shapes, ride_scratch = _rider_shapes("gather", riders)
    pspecs, tile, lbs, gs, state, grid = _hg_specs(s, d, False)

    def body(*refs):
        qr_ref, fr_ref, ir_ref, gr_ref, lb_ref, gain_ref, sums_ref, masks_ref = refs[:8]
        og_ref, o_ref, states_ref = refs[8 + n_r:11 + n_r]
        st_ref = refs[11 + 2 * n_r]
        ride_end = _ride("gather", refs[8:8 + n_r], refs[11 + n_r:11 + 2 * n_r], refs[12 + 2 * n_r:], grid)

        @pl.when(pl.program_id(1) == 0)
        def _():
            st_ref[...] = jnp.zeros_like(st_ref)

        lbv, gainv = lb_ref[...], gain_ref[...]
        units = [(ci, hh) for ci in range(nsub) for hh in range(hp)]

        def lanes(hh):
            return slice(hh * HG_DIM, (hh + 1) * HG_DIM)

        pre = []
        for ci in range(nsub):
            rows = slice(ci * c, (ci + 1) * c)
            _, q, _, _, lf, k = _hg_gates(qr_ref[rows, :], fr_ref[rows, :], lbv)
            pre.append((q, k, jnp.exp(_mdot2(sums_ref[...], lf, NN))))
        scores, qh, vb, update = {}, {}, {}, {}
        for ci, hh in units:
            q, k, x = (a[:, lanes(hh)] for a in pre[ci])
            scores[ci, hh] = _hg_scores(q, k, x, masks_ref, c, levels)[0].astype(BF16)
            qh[ci, hh] = (q * x[0:c]).astype(BF16)
            vb[ci, hh] = ir_ref[ci * c:(ci + 1) * c, lanes(hh)].astype(BF16)
            update[ci, hh] = _dot(vb[ci, hh], (k * x[c:2 * c]).astype(BF16), TN)
        intra = {u: _dot(scores[u], vb[u], NN) for u in units}
        for hh in range(hp):
            st = st_ref[hh]
            for ci in range(nsub):
                rows = slice(ci * c, (ci + 1) * c)
                states_ref[hh, ci] = st
                o = _dot(qh[ci, hh], st.astype(BF16), NT) + intra[ci, hh]
                st = st * pre[ci][2][c - 1:c, lanes(hh)] + update[ci, hh]
                o_ref[rows, lanes(hh)] = o
                r = lax.rsqrt(jnp.mean(o * o, axis=-1, keepdims=True) + EPS)
                og_ref[rows, lanes(hh)] = (o * r * gainv * jax.nn.sigmoid(gr_ref[rows, lanes(hh)])).astype(BF16)
            st_ref[hh] = st
        ride_end()

    const = [pl.BlockSpec(sums.shape, lambda g, si: (0, 0)), pl.BlockSpec(masks.shape, lambda g, si: (0, 0, 0))]
    out = _call(
        body, name=name, grid=grid, in_specs=pspecs + [lbs, gs] + const + [_ANY] * n_r,
        out_specs=[tile, tile, state] + [_ANY] * n_r,
        out_shape=[_sds((s, d), BF16), _sds((s, d), F32), _sds((heads, s // c, HG_DIM, HG_DIM), F32)] + ride_shapes,
        scratch_shapes=[pltpu.VMEM((hp, HG_DIM, HG_DIM), F32)] + ride_scratch,
        compiler_params=_cparams("arbitrary", "arbitrary"),
    )(proj, proj, proj, proj, lb, gain, sums, masks, *riders)
    return out[0], out[1], out[2], list(out[3:])


def _hgrn2_bwd(proj, lb, gain, o, states, dog, name, riders=()):
    s, d4 = proj.shape
    d = d4 // 4
    heads, c, nsub, hp = _hg_layout(s, d)
    sums, masks, levels = _hg_tables(c)
    n_r = len(riders)
    ride_shapes, ride_scratch = _rider_shapes("exchange", riders)
    pspecs, tile, lbs, gs, state, grid = _hg_specs(s, d, True)

    def body(*refs):
        (qr_ref, fr_ref, ir_ref, gr_ref, lb_ref, gain_ref, sums_ref, masks_ref, o_ref, states_ref, dog_ref) = refs[:11]
        dq_ref, df_ref, di_ref, dg_ref, dlb_ref, dgain_ref = refs[11 + n_r:17 + n_r]
        dst_ref = refs[17 + 2 * n_r]
        ride_end = _ride("exchange", refs[11:11 + n_r], refs[17 + n_r:17 + 2 * n_r], refs[18 + 2 * n_r:], grid)

        @pl.when(pl.program_id(1) == 0)
        def _():
            dst_ref[...] = jnp.zeros_like(dst_ref)
            dlb_ref[...] = jnp.zeros_like(dlb_ref)
            dgain_ref[...] = jnp.zeros_like(dgain_ref)

        lbv, gainv = lb_ref[...], gain_ref[...]
        units = [(ci, hh) for ci in range(nsub) for hh in range(hp)]

        def lanes(hh):
            return slice(hh * HG_DIM, (hh + 1) * HG_DIM)

        def rows_of(ci):
            return slice(ci * c, (ci + 1) * c)

        pre = []
        for ci in range(nsub):
            qr = qr_ref[rows_of(ci), :]
            sq, q, sg, forget, lf, k = _hg_gates(qr, fr_ref[rows_of(ci), :], lbv)
            pre.append(dict(qr=qr, sq=sq, q=q, sg=sg, forget=forget, k=k, x=jnp.exp(_mdot2(sums_ref[...], lf, NN))))

        dob, vb, sc, qh_f, kh_f, feed = {}, {}, {}, {}, {}, {}
        dgain = [jnp.zeros((1, HG_DIM), F32) for _ in range(hp)]
        for ci, hh in units:
            rows, ln = rows_of(ci), lanes(hh)
            ov, gate = o_ref[rows, ln], jax.nn.sigmoid(gr_ref[rows, ln])
            r = lax.rsqrt(jnp.mean(ov * ov, axis=-1, keepdims=True) + EPS)
            orr = ov * r
            dogv = dog_ref[rows, ln]
            dg_ref[rows, ln] = (dogv * orr * gainv * gate * (1.0 - gate)).astype(BF16)
            don = dogv * gate
            dgain[hh] = dgain[hh] + jnp.sum(don * orr, axis=0, keepdims=True)
            t = don * gainv
            dob[ci, hh] = (r * (t - orr * jnp.mean(t * orr, axis=-1, keepdims=True))).astype(BF16)
            q, k, x = (pre[ci][n][:, ln] for n in ("q", "k", "x"))
            vb[ci, hh] = ir_ref[rows, ln].astype(BF16)
            sc[ci, hh] = _hg_scores(q, k, x, masks_ref, c, levels)
            qh_f[ci, hh], kh_f[ci, hh] = q * x[0:c], k * x[c:2 * c]
            feed[ci, hh] = _dot(dob[ci, hh], qh_f[ci, hh].astype(BF16), TN)

        dsts = {}
        for hh in range(hp):
            dst = dst_ref[hh]
            for ci in reversed(range(nsub)):
                dsts[ci, hh] = dst
                dst = dst * pre[ci]["x"][c - 1:c, lanes(hh)] + feed[ci, hh]
            dst_ref[hh] = dst

        dlb = [jnp.zeros((1, HG_DIM), F32) for _ in range(hp)]
        part = {(ci, hh): {n: v[:, lanes(hh)] for n, v in pre[ci].items()} for ci, hh in units}
        dscores, decay_grad, dq, dk, dexp = {}, {}, {}, {}, {}
        for u in units:
            ci, hh = u
            x, (scores, eye, _) = part[u]["x"], sc[u]
            st, dst = states_ref[hh, ci], dsts[u]
            dstb = dst.astype(BF16)
            dscores[u] = _dot(dob[u], vb[u], NT)
            di_ref[rows_of(ci), lanes(hh)] = (_dot(scores.astype(BF16), dob[u], TN)
                                              + _dot(kh_f[u].astype(BF16), dstb, NT)).astype(BF16)
            dqh = _dot(dob[u], st.astype(BF16), NN)
            dkh = _dot(vb[u], dstb, NN)
            decay_grad[u] = x[c - 1:c] * jnp.sum(dst * st, axis=0, keepdims=True)
            ddiag = jnp.sum(eye * dscores[u], axis=-1, keepdims=True)
            dq[u] = dqh * x[0:c] + ddiag * part[u]["k"]
            dk[u] = dkh * x[c:2 * c] + ddiag * part[u]["q"]
            dexp[u] = [dqh * qh_f[u], dkh * kh_f[u]]
        for l in range(levels):
            for u in units:
                q, k, x = part[u]["q"], part[u]["k"], part[u]["x"]
                qm, km = sc[u][2][l]
                dsm = (masks_ref[l] * dscores[u]).astype(BF16)
                dqm, dkm = _dot(dsm, km, NN), _dot(dsm, qm, TN)
                xq, xk = x[(2 + 2 * l) * c:(3 + 2 * l) * c], x[(3 + 2 * l) * c:(4 + 2 * l) * c]
                dq[u] = dq[u] + dqm * xq
                dk[u] = dk[u] + dkm * xk
                dexp[u] += [dqm * (q * xq), dkm * (k * xk)]
        for u in units:
            ci, hh = u
            rows, ln, p = rows_of(ci), lanes(hh), part[u]
            dlf = _mdot2(sums_ref[...], jnp.concatenate(dexp[u], axis=0), TN) + decay_grad[u]
            dforget = dlf / p["forget"] - dk[u]
            dlb[hh] = dlb[hh] + jnp.sum(dforget * (1.0 - p["sg"]), axis=0, keepdims=True)
            df_ref[rows, ln] = (dforget * (1.0 - lbv[:, ln]) * p["sg"] * (1.0 - p["sg"])).astype(BF16)
            dq_ref[rows, ln] = (dq[u] * p["sq"] * (1.0 + p["qr"] * (1.0 - p["sq"]))).astype(BF16)
        for hh in range(hp):
            dlb_ref[:, lanes(hh)] += dlb[hh]
            dgain_ref[hh] += dgain[hh]
        ride_end()

    const = [pl.BlockSpec(sums.shape, lambda g, si: (0, 0)), pl.BlockSpec(masks.shape, lambda g, si: (0, 0, 0))]
    out = _call(
        body, name=name, grid=grid, in_specs=pspecs + [lbs, gs] + const + [tile, state, tile] + [_ANY] * n_r,
        out_specs=[tile, tile, tile, tile, lbs, pl.BlockSpec((hp, 1, HG_DIM), lambda g, si: (g, 0, 0))] + [_ANY] * n_r,
        out_shape=[_sds((s, d), BF16)] * 4 + [_sds((1, d), F32), _sds((heads, 1, HG_DIM), F32)] + ride_shapes,
        scratch_shapes=[pltpu.VMEM((hp, HG_DIM, HG_DIM), F32)] + ride_scratch,
        compiler_params=_cparams("arbitrary", "arbitrary"),
    )(proj, proj, proj, proj, lb, gain, sums, masks, o, states, dog, *riders)
    return tuple(out[:6]) + (list(out[6:]),)


def _lower_bounds(logits, name):
    n, d = logits.shape

    def body(l_ref, lb_ref):
        lv = l_ref[...]
        e = jnp.exp(lv - jnp.max(lv, axis=0, keepdims=True))
        p = e / jnp.sum(e, axis=0, keepdims=True)
        run = jnp.zeros((1, d), F32)
        for j in range(n):
            if j > 0:
                run = run + p[j:j + 1]
            lb_ref[j:j + 1, :] = run

    return _call(body, name=name, out_shape=_sds((n, d), F32))(logits)


def _lower_bounds_bwd(logits, dlb_parts, name):
    n, d = logits.shape

    def body(l_ref, dlb_ref, dl_ref):
        lv, dv = l_ref[...], dlb_ref[0]
        for dev in range(1, N_DEV):
            dv = dv + dlb_ref[dev]
        e = jnp.exp(lv - jnp.max(lv, axis=0, keepdims=True))
        p = e / jnp.sum(e, axis=0, keepdims=True)
        run = jnp.zeros((1, d), F32)
        dps = [None] * n
        for j in range(n - 1, 0, -1):
            run = run + dv[j:j + 1]
            dps[j] = run
        dps[0] = jnp.zeros((1, d), F32)
        inner = jnp.zeros((1, d), F32)
        for j in range(n):
            inner = inner + p[j:j + 1] * dps[j]
        for j in range(n):
            dl_ref[j:j + 1, :] = p[j:j + 1] * (dps[j] - inner)

    return _call(body, name=name, out_shape=_sds((n, d), F32))(logits, dlb_parts)


_ANY = pl.BlockSpec(memory_space=pl.ANY)
_MESH = pl.DeviceIdType.MESH


def _gather_stages(x_ref, out_ref, send_sems, recv_sems, local_sem):
    mx, my, mc = lax.axis_index("x"), lax.axis_index("y"), lax.axis_index("c")
    me, sibling = (mx, my, mc), (mx, my, 1 - mc)
    chips = [(1 - mx, my), (mx, 1 - my), (1 - mx, 1 - my)]

    def slot(px, py, pc):
        return out_ref.at[4 * px + 2 * py + pc]

    def copy(k, block, to, src=None):
        return pltpu.make_async_remote_copy(
            src_ref=slot(*block) if src is None else src, dst_ref=slot(*block),
            send_sem=send_sems.at[k], recv_sem=recv_sems.at[k], device_id=to, device_id_type=_MESH)

    mine = pltpu.make_async_copy(x_ref, slot(*me), local_sem)
    first = [copy(0, me, sibling, src=x_ref)] + [copy(1 + j, me, (*chip, mc), src=x_ref) for j, chip in enumerate(chips)]
    passed = [copy(4 + j, (*chip, mc), sibling) for j, chip in enumerate(chips)]

    def start():
        mine.start()
        for cp in first:
            cp.start()

    def middle():
        for j, chip in enumerate(chips):
            copy(1 + j, (*chip, mc), me).wait_recv()
            passed[j].start()

    def finish():
        copy(0, sibling, me).wait_recv()
        for j, chip in enumerate(chips):
            copy(4 + j, (*chip, 1 - mc), me).wait_recv()
        for cp in first + passed:
            cp.wait_send()
        mine.wait()

    return start, middle, finish


def _exchange_stages(g_ref, out_ref, send_sems, recv_sems, local_sem):
    mx, my, mc = lax.axis_index("x"), lax.axis_index("y"), lax.axis_index("c")
    me = 4 * mx + 2 * my + mc
    mine = pltpu.make_async_copy(g_ref.at[me], out_ref.at[me], local_sem)
    copies = []
    for k in range(1, N_DEV):
        px, py, pc = mx ^ (k >> 2), my ^ ((k >> 1) & 1), mc ^ (k & 1)
        peer = 4 * px + 2 * py + pc
        send = pltpu.make_async_remote_copy(
            src_ref=g_ref.at[peer], dst_ref=out_ref.at[me], send_sem=send_sems.at[k - 1],
            recv_sem=recv_sems.at[k - 1], device_id=(px, py, pc), device_id_type=_MESH)
        arrival = pltpu.make_async_remote_copy(
            src_ref=g_ref.at[peer], dst_ref=out_ref.at[peer], send_sem=send_sems.at[k - 1],
            recv_sem=recv_sems.at[k - 1], device_id=(px, py, pc), device_id_type=_MESH)
        copies.append((send, arrival))

    def start():
        mine.start()
        for send, _ in copies:
            send.start()

    def finish():
        for _, arrival in copies:
            arrival.wait_recv()
        for send, _ in copies:
            send.wait_send()
        mine.wait()

    return start, lambda: None, finish


_STAGES = {"gather": _gather_stages, "exchange": _exchange_stages}
SEMS_PER_TRANSFER = 3


def _rider_shapes(kind, arrays):
    outs = [_sds((N_DEV,) + a.shape if kind == "gather" else a.shape, a.dtype) for a in arrays]
    scratch = []
    for _ in arrays:
        scratch += [pltpu.SemaphoreType.DMA((7,)), pltpu.SemaphoreType.DMA((7,)), pltpu.SemaphoreType.DMA]
    return outs, scratch


def _ride(kind, in_refs, out_refs, sems, grid):
    stages = [_STAGES[kind](a, o, *sems[SEMS_PER_TRANSFER * n:SEMS_PER_TRANSFER * (n + 1)])
              for n, (a, o) in enumerate(zip(in_refs, out_refs))]
    if not stages:
        return lambda: None
    p, i = pl.program_id(0), pl.program_id(1)

    def run(stage):
        for st in stages:
            st[stage]()

    pl.when((p == 0) & (i == 0))(lambda: run(0))
    pl.when((p == grid[0] // 2) & (i == 0))(lambda: run(1))
    return lambda: pl.when((p == grid[0] - 1) & (i == grid[1] - 1))(lambda: run(2))


def _transfer(kind, arrays, name):
    outs, scratch = _rider_shapes(kind, arrays)
    n = len(arrays)

    def body(*refs):
        stages = [_STAGES[kind](refs[t], refs[n + t], *refs[2 * n + SEMS_PER_TRANSFER * t:2 * n + SEMS_PER_TRANSFER * (t + 1)])
                  for t in range(n)]
        for stage in range(3):
            for st in stages:
                st[stage]()

    return _call(body, name=name, out_shape=outs, in_specs=[_ANY] * n, out_specs=[_ANY] * n, scratch_shapes=scratch)(*arrays)


def _all_gather(x, name):
    return _transfer("gather", [x], name)[0]


def _adamw(parts, w, m, v, name):
    n_l = len(parts)
    _, r, c = parts[0].shape
    tr = r if r <= 256 else 256
    assert r % tr == 0 and w.shape == (n_l * r, c), (name, parts[0].shape, w.shape)
    steps = r // tr

    def body(*refs):
        p_refs, (w_ref, m_ref, v_ref) = refs[:n_l], refs[n_l:n_l + 3]
        g_ref, d_ref, nm_ref, nv_ref, sum_ref = refs[n_l + 3:]
        for layer in range(n_l):
            @pl.when(pl.program_id(0) == layer)
            def _(p_ref=p_refs[layer]):
                acc = p_ref[0].astype(F32)
                for dev in range(1, N_DEV):
                    acc = acc + p_ref[dev].astype(F32)
                sum_ref[...] = acc

        g = sum_ref[...]
        nm = ADAM_B1 * m_ref[...] + (1.0 - ADAM_B1) * g
        nv = ADAM_B2 * v_ref[...] + (1.0 - ADAM_B2) * (g * g)
        m_hat = nm / (1.0 - ADAM_B1 ** ADAM_STEP)
        v_hat = nv / (1.0 - ADAM_B2 ** ADAM_STEP)
        g_ref[...] = g
        nm_ref[...] = nm
        nv_ref[...] = nv
        d_ref[...] = -ADAM_LR * (m_hat / (jnp.sqrt(v_hat) + ADAM_EPS) + ADAM_WD * w_ref[...])

    tile = pl.BlockSpec((tr, c), lambda l, i: (l * steps + i, 0))

    def part(layer):
        return pl.BlockSpec((N_DEV, tr, c), lambda l, i: (0, jnp.where(l == layer, i, 0), 0))

    return _call(
        body, name=name, grid=(n_l, steps),
        in_specs=[part(layer) for layer in range(n_l)] + [tile, tile, tile], out_specs=[tile] * 4,
        out_shape=[_sds((n_l * r, c), F32)] * 4, scratch_shapes=[pltpu.VMEM((tr, c), F32)],
        compiler_params=_cparams("arbitrary", "arbitrary"),
    )(*parts, w, m, v)


def _shard_2d(w):
    return w.astype(BF16).reshape(w.shape[0] * w.shape[1], w.shape[2])


def _full_cols(g, w):
    l, k, n = w.shape
    g = g.reshape(N_DEV, l, k, n)
    return [jnp.transpose(g[:, i], (1, 0, 2)).reshape(k, N_DEV * n) for i in range(l)]


def _full_rows(g, w):
    l, k, n = w.shape
    g = g.reshape(N_DEV, l, k, n)
    return [g[:, i].reshape(N_DEV * k, n) for i in range(l)]


def _parts_cols(grads):
    k, n8 = grads[0].shape
    g = jnp.stack(grads).reshape(len(grads), k, N_DEV, n8 // N_DEV)
    return jnp.transpose(g, (2, 0, 1, 3)).reshape(N_DEV, len(grads) * k, n8 // N_DEV)


def _parts_rows(grads):
    k8, n = grads[0].shape
    g = jnp.stack(grads).reshape(len(grads), N_DEV, k8 // N_DEV, n)
    return jnp.transpose(g, (1, 0, 2, 3)).reshape(N_DEV, len(grads) * (k8 // N_DEV), n)


def _pad_rows(a, rows):
    return jnp.concatenate([a, jnp.zeros((rows - a.shape[0], a.shape[1]), a.dtype)], axis=0)


def kernel(x, norm_gains, sb_w_qkv, sb_q_gain, sb_k_gain, sb_w_o, hg_w_in, hg_lb_logits, hg_norm_gain, hg_w_o, mlp_w1, mlp_w2, loss_target, m_norm_gains, m_sb_w_qkv, m_sb_q_gain, m_sb_k_gain, m_sb_w_o, m_hg_w_in, m_hg_lb_logits, m_hg_norm_gain, m_hg_w_o, m_mlp_w1, m_mlp_w2, v_norm_gains, v_sb_w_qkv, v_sb_q_gain, v_sb_k_gain, v_sb_w_o, v_hg_w_in, v_hg_lb_logits, v_hg_norm_gain, v_hg_w_o, v_mlp_w1, v_mlp_w2):
    depth, _, d_loc = norm_gains.shape
    n_sb, n_hg = sb_w_qkv.shape[0], hg_w_in.shape[0]
    xs = x[0]
    target = loss_target[0]
    s, d = xs.shape
    me = 4 * lax.axis_index("x") + 2 * lax.axis_index("y") + lax.axis_index("c")

    assert n_sb >= 1
    w_qkv, w_o = [None] * n_sb, [None] * n_sb
    w_in, w_ho = [None] * n_hg, [None] * n_hg
    w_qkv[0] = _full_cols(_all_gather(_shard_2d(sb_w_qkv[:1]), "gather_w_qkv"), sb_w_qkv[:1])[0]
    w_1s, w_2s = [None] * depth, [None] * depth

    def mixer_weights(layer):
        j = layer // 2
        if layer % 2 == 0:
            return [(w_qkv, j, sb_w_qkv[j:j + 1], _full_cols), (w_o, j, sb_w_o[j:j + 1], _full_rows)]
        return [(w_in, j, hg_w_in[j:j + 1], _full_cols), (w_ho, j, hg_w_o[j:j + 1], _full_rows)]
    n_gain_rows = 2 * depth
    small_rows = -(-(n_gain_rows + n_hg) // 8) * 8
    small = _pad_rows(jnp.concatenate([norm_gains.reshape(n_gain_rows, d_loc), hg_lb_logits], axis=0), small_rows)
    small = _all_gather(small, "gather_small")
    gains_full = jnp.transpose(small[:, :n_gain_rows], (1, 0, 2)).reshape(depth, 2, 1, d)
    logits_full = jnp.transpose(small[:, n_gain_rows:n_gain_rows + n_hg], (1, 0, 2)).reshape(n_hg, d)
    lower = _lower_bounds(logits_full, "lower_bounds")

    saved = []
    cur = xs
    h = _rmsnorm(cur, gains_full[0, 0], "norm_mix_0")
    for layer in range(depth):
        j = layer // 2
        ahead = mixer_weights(layer + 1) if layer + 1 < depth else []
        riders = [_shard_2d(mlp_w1[layer:layer + 1]), _shard_2d(mlp_w2[layer:layer + 1])]
        riders += [_shard_2d(shard) for _, _, shard, _ in ahead]
        if layer % 2 == 0:
            if layer == 0:
                qkv, (got_o,) = _matmul("nn", h, w_qkv[j], f"qkv_{layer}", [F32],
                                        riders=("gather", [_shard_2d(sb_w_o[:1])]))
                w_o[0] = _full_rows(got_o, sb_w_o[:1])[0]
            else:
                qkv = _matmul("nn", h, w_qkv[j], f"qkv_{layer}", [F32])
            qk_gains = _qk_gain_table(sb_q_gain[j], sb_k_gain[j], d)
            qkvn = _qknorm(qkv, qk_gains, f"qknorm_{layer}")
            o, runs, got = _sb_attention_fwd(qkvn, f"sb_fwd_{layer}", riders)
            mix, mixed, w_out = (qkv, qk_gains, qkvn, o, runs), o, w_o[j]
        else:
            proj = _matmul("nn", h, w_in[j], f"hg_in_{layer}", [F32])
            og, o, states, got = _hgrn2_fwd(proj, lower[j:j + 1], hg_norm_gain[j:j + 1], f"hg_fwd_{layer}", riders)
            mix, mixed, w_out = (proj, og, o, states), og, w_ho[j]
        w_1s[layer] = w_1 = _full_cols(got[0], mlp_w1[layer:layer + 1])[0]
        w_2s[layer] = w_2 = _full_rows(got[1], mlp_w2[layer:layer + 1])[0]
        for (dest, idx, shard, full), g in zip(ahead, got[2:]):
            dest[idx] = full(g, shard)[0]
        x1, h2 = _matmul("nn", mixed, w_out, f"mix_out_{layer}", [F32, BF16], _ep_add_norm, [cur],
                         rows=[gains_full[layer, 1]])
        a, u = _matmul("nn", h2, w_1, f"mlp_up_{layer}", [BF16, BF16], _ep_relu2)
        saved.append((cur, h, mix, x1, h2, a, u))
        if layer == depth - 1:
            cur = _matmul("nn", u, w_2, f"mlp_down_{layer}", [F32], _ep_add, [x1])
        else:
            cur, h = _matmul("nn", u, w_2, f"mlp_down_{layer}", [F32, BF16], _ep_add_norm, [x1],
                             rows=[gains_full[layer + 1, 0]])

    loss_tile, dx, dxb = _loss_head(cur, target, "loss_head")
    loss = lax.psum(loss_tile[0, 0], AXES)

    d_gains = [[None, None] for _ in range(depth)]
    d_qk, d_lb, d_hgain = [None] * n_sb, [None] * n_hg, [None] * n_hg
    received = {"sb_w_qkv": [None] * n_sb, "sb_w_o": [None] * n_sb, "hg_w_in": [None] * n_hg,
                "hg_w_o": [None] * n_hg, "mlp_w1": [None] * depth, "mlp_w2": [None] * depth}
    pending = []

    def settle(got):
        for (wname, idx, _), arrived in zip(pending, got):
            received[wname][idx] = arrived
        pending.clear()

    for layer in reversed(range(depth)):
        j = layer // 2
        x0, h, mix, x1, h2, a, u = saved[layer]
        pending.append(("mlp_w2", layer, _parts_rows([_matmul("tn", u, dxb, f"d_mlp_w2_{layer}", [BF16])])))
        da = _matmul("nt", dxb, w_2s[layer], f"d_mlp_act_{layer}", [BF16], _ep_relu2_bwd, [a])
        pending.append(("mlp_w1", layer, _parts_cols([_matmul("tn", h2, da, f"d_mlp_w1_{layer}", [BF16])])))
        dx, dxb, d_gains[layer][1] = _matmul("nt", da, w_1s[layer], f"d_mlp_in_{layer}", [F32, BF16], _ep_norm_bwd,
                                             [x1, dx], rows=[gains_full[layer, 1]], n_sums=1)
        if layer % 2 == 0:
            qkv, qk_gains, qkvn, o, runs = mix
            pending.append(("sb_w_o", j, _parts_rows([_matmul("tn", o, dxb, f"d_sb_w_o_{layer}", [BF16])])))
            do = _matmul("nt", dxb, w_o[j], f"d_sb_o_{layer}", [F32])
            dq, dk, dv, got = _sb_attention_bwd(qkvn, do, runs, f"sb_bwd_{layer}", [p for _, _, p in pending])
            settle(got)
            dqkv, d_qk[j] = _qknorm_bwd(qkv, dq, dk, dv, qk_gains, f"d_qknorm_{layer}")
            pending.append(("sb_w_qkv", j, _parts_cols([_matmul("tn", h, dqkv, f"d_sb_w_qkv_{layer}", [BF16])])))
            tail = [p for _, _, p in pending] if layer == 0 else []
            res = _matmul("nt", dqkv, w_qkv[j], f"d_sb_in_{layer}", [F32, BF16], _ep_norm_bwd, [x0, dx],
                          rows=[gains_full[layer, 0]], n_sums=1, riders=("exchange", tail) if tail else None)
            dx, dxb, d_gains[layer][0] = res[:3]
            if tail:
                settle(res[3])
        else:
            proj, og, o, states = mix
            pending.append(("hg_w_o", j, _parts_rows([_matmul("tn", og, dxb, f"d_hg_w_o_{layer}", [BF16])])))
            dog = _matmul("nt", dxb, w_ho[j], f"d_hg_o_{layer}", [F32])
            dq, df, di, dg, d_lb[j], d_hgain[j], got = _hgrn2_bwd(
                proj, lower[j:j + 1], hg_norm_gain[j:j + 1], o, states, dog, f"hg_bwd_{layer}",
                [p for _, _, p in pending])
            settle(got)
            dproj = jnp.concatenate([dq, df, di, dg], axis=1)
            pending.append(("hg_w_in", j, _parts_cols([_matmul("tn", h, dproj, f"d_hg_w_in_{layer}", [BF16])])))
            dx, dxb, d_gains[layer][0] = _matmul("nt", dproj, w_in[j], f"d_hg_in_{layer}", [F32, BF16], _ep_norm_bwd,
                                                 [x0, dx], rows=[gains_full[layer, 0]], n_sums=1)
    grad_x = dx[None]
    if pending:
        settle(_transfer("exchange", [p for _, _, p in pending], "exchange_tail"))

    def update(wname, w, m, v):
        shape = w.shape
        flat = (shape[0] * shape[1], shape[2])
        res = _adamw(received[wname], w.reshape(flat), m.reshape(flat), v.reshape(flat), "adamw_" + wname)
        return [r.reshape(shape) for r in res]

    big = {
        "sb_w_qkv": update("sb_w_qkv", sb_w_qkv, m_sb_w_qkv, v_sb_w_qkv),
        "sb_w_o": update("sb_w_o", sb_w_o, m_sb_w_o, v_sb_w_o),
        "hg_w_in": update("hg_w_in", hg_w_in, m_hg_w_in, v_hg_w_in),
        "hg_w_o": update("hg_w_o", hg_w_o, m_hg_w_o, v_hg_w_o),
        "mlp_w1": update("mlp_w1", mlp_w1, m_mlp_w1, v_mlp_w1),
        "mlp_w2": update("mlp_w2", mlp_w2, m_mlp_w2, v_mlp_w2),
    }

    d_gain_rows = jnp.concatenate([d_gains[l][t] for l in range(depth) for t in range(2)], axis=0)
    d_lb_rows = jnp.concatenate(d_lb, axis=0)
    def fold(t):
        return jnp.sum(t.reshape(d // SB_HEAD_DIM, SB_HEAD_DIM), axis=0, keepdims=True)
    d_qg = jnp.concatenate([fold(d_qk[i][0]) for i in range(n_sb)], axis=0) * SB_SCALE
    d_kg = jnp.concatenate([fold(d_qk[i][1]) for i in range(n_sb)], axis=0)
    d_hg = jnp.concatenate([jnp.sum(d_hgain[i], axis=0) for i in range(n_hg)], axis=0)
    per_row = d // LANES
    packed = jnp.concatenate([
        d_gain_rows.reshape(n_gain_rows * per_row, LANES), d_lb_rows.reshape(n_hg * per_row, LANES),
        jnp.concatenate([d_qg, d_kg], axis=1), d_hg], axis=0)
    n_packed = packed.shape[0]
    packed = _pad_rows(packed, -(-n_packed // 8) * 8)
    everyone = _all_gather(packed, "gather_small_grads")
    o_lb = n_gain_rows * per_row
    o_qk = o_lb + n_hg * per_row
    o_hg = o_qk + n_sb

    def mine_of(rows, count):
        return lax.dynamic_slice_in_dim(rows.reshape(N_DEV, count, per_row, LANES), me, 1, axis=2)[:, :, 0]

    d_logits_full = _lower_bounds_bwd(logits_full, everyone[:, o_lb:o_qk].reshape(N_DEV, n_hg, d), "lower_bounds_bwd")
    d_logits_mine = lax.dynamic_slice_in_dim(d_logits_full.reshape(n_hg, per_row, LANES), me, 1, axis=1)[:, 0]
    zeros7 = jnp.zeros((N_DEV - 1, n_hg, LANES), F32)
    small_parts = jnp.concatenate([
        mine_of(everyone[:, :o_lb], n_gain_rows),
        jnp.concatenate([d_logits_mine[None], zeros7], axis=0),
        everyone[:, o_qk:o_hg], everyone[:, o_hg:o_hg + n_hg]], axis=1)
    rows_small = small_parts.shape[1]
    pad_to = -(-rows_small // 8) * 8
    small_parts = jnp.concatenate([small_parts, jnp.zeros((N_DEV, pad_to - rows_small, LANES), F32)], axis=1)

    def pack_small(ng, qg, kg, lbl, hgn):
        return _pad_rows(jnp.concatenate([
            ng.reshape(n_gain_rows, d_loc), lbl, jnp.concatenate([qg, kg], axis=1), hgn], axis=0), pad_to)

    res = _adamw([small_parts],
                 pack_small(norm_gains, sb_q_gain, sb_k_gain, hg_lb_logits, hg_norm_gain),
                 pack_small(m_norm_gains, m_sb_q_gain, m_sb_k_gain, m_hg_lb_logits, m_hg_norm_gain),
                 pack_small(v_norm_gains, v_sb_q_gain, v_sb_k_gain, v_hg_lb_logits, v_hg_norm_gain), "adamw_small")

    def unpack_small(t):
        o1 = n_gain_rows
        o2 = o1 + n_hg
        o3 = o2 + n_sb
        return {"norm_gains": t[:o1].reshape(depth, 2, d_loc), "hg_lb_logits": t[o1:o2],
                "sb_q_gain": t[o2:o3, :SB_HEAD_DIM], "sb_k_gain": t[o2:o3, SB_HEAD_DIM:],
                "hg_norm_gain": t[o3:o3 + n_hg]}

    small_out = [unpack_small(t) for t in res]
    order = ["norm_gains", "sb_w_qkv", "sb_q_gain", "sb_k_gain", "sb_w_o", "hg_w_in", "hg_lb_logits",
             "hg_norm_gain", "hg_w_o", "mlp_w1", "mlp_w2"]
    outs = [loss, grad_x]
    for kind in range(4):
        outs += [big[n][kind] if n in big else small_out[kind][n] for n in order]
    return tuple(outs)
```

```python
import functools
import math

import numpy as np
import jax
import jax.numpy as jnp
from jax import lax
from jax.experimental import pallas as pl
from jax.experimental.pallas import tpu as pltpu

F32 = jnp.float32
BF16 = jnp.bfloat16
EPS = 1e-6
SB_HEAD_DIM = 64
HG_DIM = 128
LANES = 128
N_DEV = 8
AXES = ("x", "y", "c")
VMEM_LIMIT_BYTES = 48 * 1024 * 1024
MATMUL_VMEM_BUDGET = 40 * 1024 * 1024
SB_SCALE = 1.0 / math.sqrt(SB_HEAD_DIM)
ATT_BLOCK = 256
SB_DEAD = 104.0
SB_NEVER = -1e30
HG_CHUNK = 64
HG_STEP_CHUNKS = 4
HG_STEP_HEADS = 4
ADAM_LR, ADAM_B1, ADAM_B2, ADAM_EPS, ADAM_WD, ADAM_STEP = 0.001, 0.9, 0.999, 1e-08, 0.01, 10


def _call(body, **kw):
    return pl.pallas_call(body, **kw)


def _sds(shape, dtype):
    return jax.ShapeDtypeStruct(tuple(shape), dtype)


def _cparams(*sem):
    return pltpu.CompilerParams(dimension_semantics=sem or None, vmem_limit_bytes=VMEM_LIMIT_BYTES)


def _split_bf16(x):
    hi = x.astype(BF16)
    lo = (x - hi.astype(F32)).astype(BF16)
    return hi, lo


def _dot(a, b, dims):
    return lax.dot_general(a, b, (dims, ((), ())), preferred_element_type=F32)


NN = ((1,), (0,))
NT = ((1,), (1,))
TN = ((0,), (0,))


def _dot2(x, m, dims):
    hi, lo = _split_bf16(x)
    return _dot(hi, m, dims) + _dot(lo, m, dims)


def _mdot2(m, x, dims):
    hi, lo = _split_bf16(x)
    return _dot(m, hi, dims) + _dot(m, lo, dims)


def _matmul_tiles(m, n, k, a_dtype, b_dtype, io_dtypes):
    tm, tn = min(m, 1024), min(n, 1024)

    def need(tm, tn):
        blocks = tm * k * jnp.dtype(a_dtype).itemsize + tn * k * jnp.dtype(b_dtype).itemsize
        blocks += sum(tm * tn * jnp.dtype(dt).itemsize for dt in io_dtypes)
        return 2 * blocks + 2 * tm * tn * 4

    while need(tm, tn) > MATMUL_VMEM_BUDGET:
        if tm > 256:
            tm //= 2
        else:
            tn //= 2
    return tm, tn


def _matmul(kind, a, b, name, out_dtypes, epilogue=None, extras=(), rows=(), n_sums=0, riders=None):
    if kind == "nn":
        (m, k), n = a.shape, b.shape[1]
    elif kind == "nt":
        (m, k), n = a.shape, b.shape[0]
    else:
        (k, m), n = a.shape, b.shape[1]
    tm, tn = _matmul_tiles(m, n, k, a.dtype, b.dtype, list(out_dtypes) + [e.dtype for e in extras])
    assert m % tm == 0 and n % tn == 0 and (n == tn or not (rows or n_sums)), (name, a.shape, b.shape)
    a_spec = pl.BlockSpec((k, tm), lambda i, j: (0, i)) if kind == "tn" else pl.BlockSpec((tm, k), lambda i, j: (i, 0))
    b_spec = pl.BlockSpec((tn, k), lambda i, j: (j, 0)) if kind == "nt" else pl.BlockSpec((k, tn), lambda i, j: (0, j))
    o_spec = pl.BlockSpec((tm, tn), lambda i, j: (i, j))
    r_spec = pl.BlockSpec((1, tn), lambda i, j: (0, j))
    dims = {"nn": NN, "nt": NT, "tn": TN}[kind]
    n_ex, n_rows, n_out = len(extras), len(rows), len(out_dtypes)
    ride_kind, ride_arrays = riders if riders else ("gather", [])
    n_r = len(ride_arrays)
    ride_shapes, ride_scratch = _rider_shapes(ride_kind, ride_arrays)
    grid = (m // tm, n // tn)

    def body(*refs):
        a_ref, b_ref = refs[:2]
        n_in = 2 + n_ex + n_rows
        ins = refs[2:n_in]
        outs = refs[n_in + n_r:n_in + n_r + n_out + n_sums]
        ride_end = _ride(ride_kind, refs[n_in:n_in + n_r], refs[n_in + n_r + n_out + n_sums:n_in + 2 * n_r + n_out + n_sums],
                         refs[n_in + 2 * n_r + n_out + n_sums:], grid)
        acc = _dot(a_ref[...].astype(BF16), b_ref[...].astype(BF16), dims)
        res = epilogue(acc, *[e[...] for e in ins]) if epilogue is not None else (acc,)
        for o_ref, r in zip(outs[:n_out], res):
            o_ref[...] = r.astype(o_ref.dtype)
        if n_sums:
            @pl.when(pl.program_id(0) == 0)
            def _():
                for o_ref in outs[n_out:]:
                    o_ref[...] = jnp.zeros_like(o_ref)

            for o_ref, r in zip(outs[n_out:], res[n_out:]):
                o_ref[...] += r
        ride_end()

    out = _call(
        body, name=name, grid=grid,
        in_specs=[a_spec, b_spec] + [o_spec] * n_ex + [r_spec] * n_rows + [_ANY] * n_r,
        out_specs=[o_spec] * n_out + [r_spec] * n_sums + [_ANY] * n_r,
        out_shape=[_sds((m, n), dt) for dt in out_dtypes] + [_sds((1, n), F32)] * n_sums + ride_shapes,
        scratch_shapes=ride_scratch,
        compiler_params=_cparams("arbitrary", "arbitrary") if (n_sums or n_r) else _cparams("parallel", "parallel"),
    )(a, b, *extras, *rows, *ride_arrays)
    if n_r:
        return tuple(out[:n_out + n_sums]) + (list(out[n_out + n_sums:]),)
    return out if len(out) > 1 else out[0]


def _ep_add(acc, res):
    return (acc + res,)


def _ep_add_norm(acc, res, gain):
    x = acc + res
    r = lax.rsqrt(jnp.mean(x * x, axis=-1, keepdims=True) + EPS)
    return x, x * r * gain


def _ep_norm_bwd(dh, x, dres, gain):
    r = lax.rsqrt(jnp.mean(x * x, axis=-1, keepdims=True) + EPS)
    xr = x * r
    t = dh * gain
    dx = dres + r * (t - xr * jnp.mean(t * xr, axis=-1, keepdims=True))
    return dx, dx, jnp.sum(dh * xr, axis=0, keepdims=True)


def _ep_relu2(acc):
    r = jnp.maximum(acc, 0.0)
    return acc, r * r


def _ep_relu2_bwd(acc, a):
    return (acc * (2.0 * jnp.maximum(a.astype(F32), 0.0)),)


def _rmsnorm(x, g, name):
    s, d = x.shape
    tm = min(s, 512)

    def body(x_ref, g_ref, h_ref):
        xv = x_ref[...]
        r = lax.rsqrt(jnp.mean(xv * xv, axis=-1, keepdims=True) + EPS)
        h_ref[...] = (xv * r * g_ref[...]).astype(BF16)

    return _call(
        body, name=name, grid=(s // tm,),
        in_specs=[pl.BlockSpec((tm, d), lambda i: (i, 0)), pl.BlockSpec((1, d), lambda i: (0, 0))],
        out_specs=pl.BlockSpec((tm, d), lambda i: (i, 0)),
        out_shape=_sds((s, d), BF16), compiler_params=_cparams("parallel"),
    )(x, g)


def _loss_head(y, target, name):
    s, d = y.shape
    tm = min(s, 512)

    def body(y_ref, t_ref, loss_ref, dy_ref, dyb_ref):
        err = y_ref[...] - t_ref[...]
        dy = err * (1.0 / d)
        dy_ref[...] = dy
        dyb_ref[...] = dy.astype(BF16)

        @pl.when(pl.program_id(0) == 0)
        def _():
            loss_ref[...] = jnp.zeros_like(loss_ref)

        part = 0.5 * jnp.sum(jnp.mean(err * err, axis=-1, keepdims=True), axis=0, keepdims=True)
        loss_ref[...] += part

    row = pl.BlockSpec((tm, d), lambda i: (i, 0))
    return _call(
        body, name=name, grid=(s // tm,), in_specs=[row, row],
        out_specs=[pl.BlockSpec((8, LANES), lambda i: (0, 0)), row, row],
        out_shape=[_sds((8, LANES), F32), _sds((s, d), F32), _sds((s, d), BF16)],
        compiler_params=_cparams("arbitrary"),
    )(y, target)


def _head_lane_mask():
    lane = lax.broadcasted_iota(jnp.int32, (1, LANES), 1)
    return lane < SB_HEAD_DIM


def _pair_mean(t, first):
    del first
    r = lax.broadcasted_iota(jnp.int32, (LANES, LANES), 0) // SB_HEAD_DIM
    c = lax.broadcasted_iota(jnp.int32, (LANES, LANES), 1) // SB_HEAD_DIM
    same_half = jnp.where(r == c, 1.0, 0.0).astype(BF16)
    return _dot2(t, same_half, NN) * (1.0 / SB_HEAD_DIM)


def _pair_rms(xv, first):
    return lax.rsqrt(_pair_mean(xv * xv, first) + EPS)


def _qk_gain_table(q_gain, k_gain, d):
    reps = d // SB_HEAD_DIM
    return jnp.stack([jnp.tile(q_gain, reps) * SB_SCALE, jnp.tile(k_gain, reps), jnp.ones((d,), F32)])[:, None, :]


QKNORM_ROWS = 256


def _qknorm(qkv, gains, name):
    s, d3 = qkv.shape
    d = d3 // 3
    tm = min(s, QKNORM_ROWS)

    def body(x_ref, g_ref, o_ref):
        first = _head_lane_mask()
        is_v = pl.program_id(0) == 2
        for c in range(d // LANES):
            cols = slice(c * LANES, (c + 1) * LANES)
            xv = x_ref[:, cols]
            normed = xv * _pair_rms(xv, first) * g_ref[0, :, cols]
            o_ref[:, cols] = jnp.where(is_v, xv, normed).astype(BF16)

    tile = pl.BlockSpec((tm, d), lambda c, i: (i, c))
    return _call(
        body, name=name, grid=(3, s // tm),
        in_specs=[tile, pl.BlockSpec((1, 1, d), lambda c, i: (c, 0, 0))], out_specs=tile,
        out_shape=_sds((s, d3), BF16), compiler_params=_cparams("parallel", "parallel"),
    )(qkv, gains)


def _qknorm_bwd(qkv, dq, dk, dv, gains, name):
    s, d3 = qkv.shape
    d = d3 // 3
    tm = min(s, QKNORM_ROWS)

    def body(x_ref, dq_ref, dk_ref, dv_ref, g_ref, dx_ref, dg_ref):
        c, i = pl.program_id(0), pl.program_id(1)

        @pl.when(i == 0)
        def _():
            dg_ref[...] = jnp.zeros_like(dg_ref)

        first = _head_lane_mask()
        for col in range(d // LANES):
            cols = slice(col * LANES, (col + 1) * LANES)
            xv = x_ref[:, cols]
            dy = jnp.where(c == 0, dq_ref[:, cols], jnp.where(c == 1, dk_ref[:, cols], dv_ref[:, cols]))
            r = _pair_rms(xv, first)
            xr = xv * r
            t = dy * g_ref[0, :, cols]
            dx = r * (t - xr * _pair_mean(t * xr, first))
            dx_ref[:, cols] = jnp.where(c == 2, dy, dx).astype(BF16)
            dg_ref[0, :, cols] += jnp.sum(dy * xr, axis=0, keepdims=True)

    tile = pl.BlockSpec((tm, d), lambda c, i: (i, c))
    vec = pl.BlockSpec((1, 1, d), lambda c, i: (c, 0, 0))

    def part(kind):
        return pl.BlockSpec((tm, d), lambda c, i: (jnp.where(c == kind, i, 0), 0))

    return _call(
        body, name=name, grid=(3, s // tm), in_specs=[tile, part(0), part(1), part(2), vec], out_specs=[tile, vec],
        out_shape=[_sds((s, d3), BF16), _sds((3, 1, d), F32)],
        compiler_params=_cparams("arbitrary", "arbitrary"),
    )(qkv, dq, dk, dv, gains)


def _softplus_parts(z):
    sp = jnp.maximum(z, 0.0) + jnp.log(1.0 + jnp.exp(-jnp.abs(z)))
    return sp, z - sp


def _diag_causal(tb):
    return lax.broadcasted_iota(jnp.int32, (tb, tb), 1) < lax.broadcasted_iota(jnp.int32, (tb, tb), 0)


def _later_keys(tb):
    r = lax.broadcasted_iota(jnp.int32, (tb, tb), 0)
    c = lax.broadcasted_iota(jnp.int32, (tb, tb), 1)
    return jnp.where(r > c, 1.0, 0.0).astype(BF16)


def _sb_scores(qa, kj, causal):
    sp, logsig = _softplus_parts(_dot(qa, kj, NT))
    return (-sp if causal is None else jnp.where(causal, -sp, 0.0)), logsig


def _sb_weights(stay, logsig, run, later, causal):
    w = jnp.exp(logsig + _dot2(stay, later, NN) + run)
    return w if causal is None else jnp.where(causal, w, 0.0)


def _sb_attention_fwd(qkvn, name, riders=()):
    s, d3 = qkvn.shape
    d = d3 // 3
    pairs, tb = d // LANES, min(ATT_BLOCK, s)
    nb = s // tb
    assert nb <= LANES
    n_r = len(riders)
    ride_shapes, ride_scratch = _rider_shapes("gather", riders)

    def body(*refs):
        q_ref, k_ref, v_ref = refs[:3]
        o_ref, runs_ref = refs[3 + n_r:5 + n_r]
        ride_end = _ride("gather", refs[3:3 + n_r], refs[5 + n_r:5 + 2 * n_r], refs[5 + 2 * n_r:], (pairs, nb))
        i = pl.program_id(1)
        first = _head_lane_mask()
        lane = lax.broadcasted_iota(jnp.int32, (1, LANES), 1)
        later = _later_keys(tb)
        q2 = q_ref[...]
        qs = (jnp.where(first, q2, jnp.zeros_like(q2)), jnp.where(first, jnp.zeros_like(q2), q2))

        def tiles(js, carry, causals):
            kvs = []
            for j in js:
                rows = pl.ds(pl.multiple_of(j * tb, tb), tb)
                kvs.append((k_ref[rows, :], v_ref[rows, :]))
            scores = [[_sb_scores(qs[h], kj, causals[t]) for h in range(2)] for t, (kj, _) in enumerate(kvs)]
            run = [carry[0][0], carry[1][0]]
            acc = [carry[0][1], carry[1][1]]
            runs = [carry[0][2], carry[1][2]]
            weights = []
            for t, j in enumerate(js):
                weights.append([_sb_weights(*scores[t][h], run[h], later, causals[t]) for h in range(2)])
                for h in range(2):
                    runs[h] = jnp.where(lane == j, run[h], runs[h])
                    run[h] = run[h] + jnp.sum(scores[t][h][0], axis=-1, keepdims=True)
            for t, (_, vj) in enumerate(kvs):
                for h in range(2):
                    acc[h] = acc[h] + _dot(weights[t][h].astype(BF16), vj, NN)
            return tuple((run[h], acc[h], runs[h]) for h in range(2))

        def alive(carry):
            return jnp.maximum(jnp.max(carry[0][0]), jnp.max(carry[1][0])) >= -SB_DEAD

        never = jnp.full((tb, LANES), SB_NEVER, F32)
        zero = (jnp.zeros((tb, 1), F32), jnp.zeros((tb, LANES), F32), never)
        diag = _diag_causal(tb)
        carry = lax.cond(i == 0, lambda: tiles([i], (zero, zero), [diag]),
                         lambda: tiles([i, i - 1], (zero, zero), [diag, None]))
        left = jnp.maximum(i - 1, 0)
        _, carry = lax.while_loop(
            lambda st: (st[0] < left // 2) & alive(st[1]),
            lambda st: (st[0] + 1, tiles([i - 2 - 2 * st[0], i - 3 - 2 * st[0]], st[1], [None, None])),
            (jnp.int32(0), carry))
        carry = lax.cond((left % 2 == 1) & alive(carry), lambda c: tiles([0], c, [None]), lambda c: c, carry)
        o_ref[...] = jnp.where(first, carry[0][1], carry[1][1])
        runs_ref[0] = carry[0][2]
        runs_ref[1] = carry[1][2]
        ride_end()

    out = _call(
        body, name=name, grid=(pairs, nb),
        in_specs=[pl.BlockSpec((tb, LANES), lambda p, i: (i, p)),
                  pl.BlockSpec((s, LANES), lambda p, i: (0, pairs + p)),
                  pl.BlockSpec((s, LANES), lambda p, i: (0, 2 * pairs + p))] + [_ANY] * n_r,
        out_specs=[pl.BlockSpec((tb, LANES), lambda p, i: (i, p)),
                   pl.BlockSpec((2, tb, LANES), lambda p, i: (p * nb + i, 0, 0))] + [_ANY] * n_r,
        out_shape=[_sds((s, d), F32), _sds((pairs * nb * 2, tb, LANES), F32)] + ride_shapes,
        scratch_shapes=ride_scratch,
        compiler_params=_cparams("arbitrary", "arbitrary"),
    )(qkvn, qkvn, qkvn, *riders)
    return out[0], out[1], list(out[2:])


def _sb_attention_bwd(qkvn, do, runs, name, riders=()):
    s, d3 = qkvn.shape
    d = d3 // 3
    pairs, tb = d // LANES, min(ATT_BLOCK, s)
    nb = s // tb
    n_r = len(riders)
    ride_shapes, ride_scratch = _rider_shapes("exchange", riders)

    def body(*refs):
        q_ref, k_ref, v_ref, do_ref, runs_ref = refs[:5]
        dq_ref, dk_ref, dv_ref = refs[5 + n_r:8 + n_r]
        ride_end = _ride("exchange", refs[5:5 + n_r], refs[8 + n_r:8 + 2 * n_r], refs[8 + 2 * n_r:], (pairs, nb))
        i = pl.program_id(1)

        @pl.when(i == 0)
        def _():
            dk_ref[...] = jnp.zeros_like(dk_ref)
            dv_ref[...] = jnp.zeros_like(dv_ref)

        first = _head_lane_mask()
        lane = lax.broadcasted_iota(jnp.int32, (1, LANES), 1)
        later = _later_keys(tb)
        q2, do2 = q_ref[...], do_ref[...].astype(BF16)
        zq = jnp.zeros_like(q2)
        qs = (jnp.where(first, q2, zq), jnp.where(first, zq, q2))
        dos = (jnp.where(first, do2, zq), jnp.where(first, zq, do2))

        def tiles(js, carry, causals):
            rows, kv = [], []
            for j in js:
                r = pl.ds(pl.multiple_of(j * tb, tb), tb)
                kj, vj = k_ref[r, :], v_ref[r, :]
                zk = jnp.zeros_like(kj)
                rows.append(r)
                kv.append([(jnp.where(first, kj, zk), jnp.where(first, vj, zk)),
                           (jnp.where(first, zk, kj), jnp.where(first, zk, vj))])
            pairs_th = [(t, h) for t in range(len(js)) for h in range(2)]
            scores = {(t, h): _sb_scores(qs[h], kv[t][h][0], causals[t]) for t, h in pairs_th}
            w, g = {}, {}
            for t, h in pairs_th:
                run = jnp.sum(jnp.where(lane == js[t], runs_ref[h], 0.0), axis=-1, keepdims=True)
                w[t, h] = _sb_weights(*scores[t, h], run, later, causals[t])
                g[t, h] = w[t, h] * _dot(dos[h], kv[t][h][1], NT)
            gsum = [carry[0], carry[1]]
            dz = {}
            for t, h in pairs_th:
                before = _dot(g[t, h].astype(BF16), later, NT) + gsum[h]
                gsum[h] = gsum[h] + jnp.sum(g[t, h], axis=-1, keepdims=True)
                sig = jnp.exp(scores[t, h][1])
                d = g[t, h] * (1.0 - sig) - before * sig
                dz[t, h] = (d if causals[t] is None else jnp.where(causals[t], d, 0.0)).astype(BF16)
            dq = carry[2]
            for t, h in pairs_th:
                dq = dq + _dot(dz[t, h], kv[t][h][0], NN)
            for t in range(len(js)):
                dk_ref[rows[t], :] += _dot(dz[t, 0], qs[0], TN) + _dot(dz[t, 1], qs[1], TN)
                dv_ref[rows[t], :] += _dot(w[t, 0].astype(BF16), dos[0], TN) + _dot(w[t, 1].astype(BF16), dos[1], TN)
            return gsum[0], gsum[1], dq

        left = jnp.maximum(i - 1, 0)
        reach = jnp.max(jnp.maximum(runs_ref[0], runs_ref[1]), axis=0, keepdims=True)
        start = jnp.sum(jnp.where((reach < -SB_DEAD) & (lane < left), 1.0, 0.0)).astype(jnp.int32)
        carry = (jnp.zeros((tb, 1), F32), jnp.zeros((tb, 1), F32), jnp.zeros((tb, LANES), F32))
        live = left - start
        carry = lax.fori_loop(
            0, live // 2, lambda jj, c: tiles([start + 2 * jj, start + 2 * jj + 1], c, [None, None]), carry)
        carry = lax.cond(live % 2 == 1, lambda c: tiles([left - 1], c, [None]), lambda c: c, carry)
        diag = _diag_causal(tb)
        carry = lax.cond(i == 0, lambda c: tiles([i], c, [diag]), lambda c: tiles([i - 1, i], c, [None, diag]), carry)
        dq_ref[...] = carry[2]
        ride_end()

    q_spec = pl.BlockSpec((tb, LANES), lambda p, i: (i, p))
    out = _call(
        body, name=name, grid=(pairs, nb),
        in_specs=[q_spec,
                  pl.BlockSpec((s, LANES), lambda p, i: (0, pairs + p)),
                  pl.BlockSpec((s, LANES), lambda p, i: (0, 2 * pairs + p)),
                  q_spec,
                  pl.BlockSpec((2, tb, LANES), lambda p, i: (p * nb + i, 0, 0))] + [_ANY] * n_r,
        out_specs=[q_spec, pl.BlockSpec((s, LANES), lambda p, i: (0, p)),
                   pl.BlockSpec((s, LANES), lambda p, i: (0, p))] + [_ANY] * n_r,
        out_shape=[_sds((s, d), F32)] * 3 + ride_shapes,
        scratch_shapes=ride_scratch,
        compiler_params=_cparams("arbitrary", "arbitrary"),
    )(qkvn, qkvn, qkvn, do, runs, *riders)
    return out[0], out[1], out[2], list(out[3:])


def _hg_tables(c):
    t = np.arange(c)[:, None]
    j = np.arange(c)[None, :]
    sums = [j <= t, j > t]
    masks = []
    m = c // 2
    while m >= 1:
        pos, base = t % (2 * m), t - t % (2 * m)
        sums.append((pos >= m) & (j >= base + m) & (j <= t))
        sums.append((pos < m) & (j > t) & (j <= base + m - 1))
        masks.append((t // (2 * m) == j // (2 * m)) & (t % (2 * m) >= m) & (j % (2 * m) < m))
        m //= 2
    return (jnp.asarray(np.concatenate(sums, 0), BF16), jnp.asarray(np.stack(masks), F32), len(masks))


def _hg_gates(qr, fr, lb):
    sq = jax.nn.sigmoid(qr)
    sg = jax.nn.sigmoid(fr)
    forget = lb + (1.0 - lb) * sg
    return sq, qr * sq, sg, forget, jnp.log(forget), (1.0 - lb) * (1.0 - sg)


def _hg_scores(q, k, x, masks_ref, c, levels):
    eye = (lax.broadcasted_iota(jnp.int32, (c, c), 0) == lax.broadcasted_iota(jnp.int32, (c, c), 1)).astype(F32)
    scores = eye * jnp.sum(q * k, axis=-1, keepdims=True)
    ops = []
    for l in range(levels):
        qm = (q * x[(2 + 2 * l) * c:(3 + 2 * l) * c]).astype(BF16)
        km = (k * x[(3 + 2 * l) * c:(4 + 2 * l) * c]).astype(BF16)
        scores = scores + masks_ref[l] * _dot(qm, km, NT)
        ops.append((qm, km))
    return scores, eye, ops


def _hg_layout(s, d):
    heads, c = d // HG_DIM, min(HG_CHUNK, s)
    nsub = min(HG_STEP_CHUNKS, s // c)
    hp = HG_STEP_HEADS if heads % HG_STEP_HEADS == 0 else 1
    return heads, c, nsub, hp


def _hg_specs(s, d, reverse):
    heads, c, nsub, hp = _hg_layout(s, d)
    rows, width, groups, n_steps = c * nsub, hp * HG_DIM, heads // hp, s // (c * nsub)

    def step(si):
        return n_steps - 1 - si if reverse else si
    proj = [pl.BlockSpec((rows, width), functools.partial(lambda g, si, part: (step(si), part * groups + g), part=p))
            for p in range(4)]
    tile = pl.BlockSpec((rows, width), lambda g, si: (step(si), g))
    lb = pl.BlockSpec((1, width), lambda g, si: (0, g))
    gain = pl.BlockSpec((1, HG_DIM), lambda g, si: (0, 0))
    state = pl.BlockSpec((hp, nsub, HG_DIM, HG_DIM), lambda g, si: (g, step(si), 0, 0))
    return proj, tile, lb, gain, state, (groups, n_steps)


def _hgrn2_fwd(proj, lb, gain, name, riders=()):
    s, d4 = proj.shape
    d = d4 // 4
    heads, c, nsub, hp = _hg_layout(s, d)
    sums, masks, levels = _hg_tables(c)
    n_r = len(riders)
    ride_shapes, ride_scratch = _rider_shapes("gather", riders)
    pspecs, tile, lbs, gs, state, grid = _hg_specs(s, d, False)

    def body(*refs):
        qr_ref, fr_ref, ir_ref, gr_ref, lb_ref, gain_ref, sums_ref, masks_ref = refs[:8]
        og_ref, o_ref, states_ref = refs[8 + n_r:11 + n_r]
        st_ref = refs[11 + 2 * n_r]
        ride_end = _ride("gather", refs[8:8 + n_r], refs[11 + n_r:11 + 2 * n_r], refs[12 + 2 * n_r:], grid)

        @pl.when(pl.program_id(1) == 0)
        def _():
            st_ref[...] = jnp.zeros_like(st_ref)

        lbv, gainv = lb_ref[...], gain_ref[...]
        units = [(ci, hh) for ci in range(nsub) for hh in range(hp)]

        def lanes(hh):
            return slice(hh * HG_DIM, (hh + 1) * HG_DIM)

        pre = []
        for ci in range(nsub):
            rows = slice(ci * c, (ci + 1) * c)
            _, q, _, _, lf, k = _hg_gates(qr_ref[rows, :], fr_ref[rows, :], lbv)
            pre.append((q, k, jnp.exp(_mdot2(sums_ref[...], lf, NN))))
        scores, qh, vb, update = {}, {}, {}, {}
        for ci, hh in units:
            q, k, x = (a[:, lanes(hh)] for a in pre[ci])
            scores[ci, hh] = _hg_scores(q, k, x, masks_ref, c, levels)[0].astype(BF16)
            qh[ci, hh] = (q * x[0:c]).astype(BF16)
            vb[ci, hh] = ir_ref[ci * c:(ci + 1) * c, lanes(hh)].astype(BF16)
            update[ci, hh] = _dot(vb[ci, hh], (k * x[c:2 * c]).astype(BF16), TN)
        intra = {u: _dot(scores[u], vb[u], NN) for u in units}
        for hh in range(hp):
            st = st_ref[hh]
            for ci in range(nsub):
                rows = slice(ci * c, (ci + 1) * c)
                states_ref[hh, ci] = st
                o = _dot(qh[ci, hh], st.astype(BF16), NT) + intra[ci, hh]
                st = st * pre[ci][2][c - 1:c, lanes(hh)] + update[ci, hh]
                o_ref[rows, lanes(hh)] = o
                r = lax.rsqrt(jnp.mean(o * o, axis=-1, keepdims=True) + EPS)
                og_ref[rows, lanes(hh)] = (o * r * gainv * jax.nn.sigmoid(gr_ref[rows, lanes(hh)])).astype(BF16)
            st_ref[hh] = st
        ride_end()

    const = [pl.BlockSpec(sums.shape, lambda g, si: (0, 0)), pl.BlockSpec(masks.shape, lambda g, si: (0, 0, 0))]
    out = _call(
        body, name=name, grid=grid, in_specs=pspecs + [lbs, gs] + const + [_ANY] * n_r,
        out_specs=[tile, tile, state] + [_ANY] * n_r,
        out_shape=[_sds((s, d), BF16), _sds((s, d), F32), _sds((heads, s // c, HG_DIM, HG_DIM), F32)] + ride_shapes,
        scratch_shapes=[pltpu.VMEM((hp, HG_DIM, HG_DIM), F32)] + ride_scratch,
        compiler_params=_cparams("arbitrary", "arbitrary"),
    )(proj, proj, proj, proj, lb, gain, sums, masks, *riders)
    return out[0], out[1], out[2], list(out[3:])


def _hgrn2_bwd(proj, lb, gain, o, states, dog, name, riders=()):
    s, d4 = proj.shape
    d = d4 // 4
    heads, c, nsub, hp = _hg_layout(s, d)
    sums, masks, levels = _hg_tables(c)
    n_r = len(riders)
    ride_shapes, ride_scratch = _rider_shapes("exchange", riders)
    pspecs, tile, lbs, gs, state, grid = _hg_specs(s, d, True)

    def body(*refs):
        (qr_ref, fr_ref, ir_ref, gr_ref, lb_ref, gain_ref, sums_ref, masks_ref, o_ref, states_ref, dog_ref) = refs[:11]
        dq_ref, df_ref, di_ref, dg_ref, dlb_ref, dgain_ref = refs[11 + n_r:17 + n_r]
        dst_ref = refs[17 + 2 * n_r]
        ride_end = _ride("exchange", refs[11:11 + n_r], refs[17 + n_r:17 + 2 * n_r], refs[18 + 2 * n_r:], grid)

        @pl.when(pl.program_id(1) == 0)
        def _():
            dst_ref[...] = jnp.zeros_like(dst_ref)
            dlb_ref[...] = jnp.zeros_like(dlb_ref)
            dgain_ref[...] = jnp.zeros_like(dgain_ref)

        lbv, gainv = lb_ref[...], gain_ref[...]
        units = [(ci, hh) for ci in range(nsub) for hh in range(hp)]

        def lanes(hh):
            return slice(hh * HG_DIM, (hh + 1) * HG_DIM)

        def rows_of(ci):
            return slice(ci * c, (ci + 1) * c)

        pre = []
        for ci in range(nsub):
            qr = qr_ref[rows_of(ci), :]
            sq, q, sg, forget, lf, k = _hg_gates(qr, fr_ref[rows_of(ci), :], lbv)
            pre.append(dict(qr=qr, sq=sq, q=q, sg=sg, forget=forget, k=k, x=jnp.exp(_mdot2(sums_ref[...], lf, NN))))

        dob, vb, sc, qh_f, kh_f, feed = {}, {}, {}, {}, {}, {}
        dgain = [jnp.zeros((1, HG_DIM), F32) for _ in range(hp)]
        for ci, hh in units:
            rows, ln = rows_of(ci), lanes(hh)
            ov, gate = o_ref[rows, ln], jax.nn.sigmoid(gr_ref[rows, ln])
            r = lax.rsqrt(jnp.mean(ov * ov, axis=-1, keepdims=True) + EPS)
            orr = ov * r
            dogv = dog_ref[rows, ln]
            dg_ref[rows, ln] = (dogv * orr * gainv * gate * (1.0 - gate)).astype(BF16)
            don = dogv * gate
            dgain[hh] = dgain[hh] + jnp.sum(don * orr, axis=0, keepdims=True)
            t = don * gainv
            dob[ci, hh] = (r * (t - orr * jnp.mean(t * orr, axis=-1, keepdims=True))).astype(BF16)
            q, k, x = (pre[ci][n][:, ln] for n in ("q", "k", "x"))
            vb[ci, hh] = ir_ref[rows, ln].astype(BF16)
            sc[ci, hh] = _hg_scores(q, k, x, masks_ref, c, levels)
            qh_f[ci, hh], kh_f[ci, hh] = q * x[0:c], k * x[c:2 * c]
            feed[ci, hh] = _dot(dob[ci, hh], qh_f[ci, hh].astype(BF16), TN)

        dsts = {}
        for hh in range(hp):
            dst = dst_ref[hh]
            for ci in reversed(range(nsub)):
                dsts[ci, hh] = dst
                dst = dst * pre[ci]["x"][c - 1:c, lanes(hh)] + feed[ci, hh]
            dst_ref[hh] = dst

        dlb = [jnp.zeros((1, HG_DIM), F32) for _ in range(hp)]
        part = {(ci, hh): {n: v[:, lanes(hh)] for n, v in pre[ci].items()} for ci, hh in units}
        dscores, decay_grad, dq, dk, dexp = {}, {}, {}, {}, {}
        for u in units:
            ci, hh = u
            x, (scores, eye, _) = part[u]["x"], sc[u]
            st, dst = states_ref[hh, ci], dsts[u]
            dstb = dst.astype(BF16)
            dscores[u] = _dot(dob[u], vb[u], NT)
            di_ref[rows_of(ci), lanes(hh)] = (_dot(scores.astype(BF16), dob[u], TN)
                                              + _dot(kh_f[u].astype(BF16), dstb, NT)).astype(BF16)
            dqh = _dot(dob[u], st.astype(BF16), NN)
            dkh = _dot(vb[u], dstb, NN)
            decay_grad[u] = x[c - 1:c] * jnp.sum(dst * st, axis=0, keepdims=True)
            ddiag = jnp.sum(eye * dscores[u], axis=-1, keepdims=True)
            dq[u] = dqh * x[0:c] + ddiag * part[u]["k"]
            dk[u] = dkh * x[c:2 * c] + ddiag * part[u]["q"]
            dexp[u] = [dqh * qh_f[u], dkh * kh_f[u]]
        for l in range(levels):
            for u in units:
                q, k, x = part[u]["q"], part[u]["k"], part[u]["x"]
                qm, km = sc[u][2][l]
                dsm = (masks_ref[l] * dscores[u]).astype(BF16)
                dqm, dkm = _dot(dsm, km, NN), _dot(dsm, qm, TN)
                xq, xk = x[(2 + 2 * l) * c:(3 + 2 * l) * c], x[(3 + 2 * l) * c:(4 + 2 * l) * c]
                dq[u] = dq[u] + dqm * xq
                dk[u] = dk[u] + dkm * xk
                dexp[u] += [dqm * (q * xq), dkm * (k * xk)]
        for u in units:
            ci, hh = u
            rows, ln, p = rows_of(ci), lanes(hh), part[u]
            dlf = _mdot2(sums_ref[...], jnp.concatenate(dexp[u], axis=0), TN) + decay_grad[u]
            dforget = dlf / p["forget"] - dk[u]
            dlb[hh] = dlb[hh] + jnp.sum(dforget * (1.0 - p["sg"]), axis=0, keepdims=True)
            df_ref[rows, ln] = (dforget * (1.0 - lbv[:, ln]) * p["sg"] * (1.0 - p["sg"])).astype(BF16)
            dq_ref[rows, ln] = (dq[u] * p["sq"] * (1.0 + p["qr"] * (1.0 - p["sq"]))).astype(BF16)
        for hh in range(hp):
            dlb_ref[:, lanes(hh)] += dlb[hh]
            dgain_ref[hh] += dgain[hh]
        ride_end()

    const = [pl.BlockSpec(sums.shape, lambda g, si: (0, 0)), pl.BlockSpec(masks.shape, lambda g, si: (0, 0, 0))]
    out = _call(
        body, name=name, grid=grid, in_specs=pspecs + [lbs, gs] + const + [tile, state, tile] + [_ANY] * n_r,
        out_specs=[tile, tile, tile, tile, lbs, pl.BlockSpec((hp, 1, HG_DIM), lambda g, si: (g, 0, 0))] + [_ANY] * n_r,
        out_shape=[_sds((s, d), BF16)] * 4 + [_sds((1, d), F32), _sds((heads, 1, HG_DIM), F32)] + ride_shapes,
        scratch_shapes=[pltpu.VMEM((hp, HG_DIM, HG_DIM), F32)] + ride_scratch,
        compiler_params=_cparams("arbitrary", "arbitrary"),
    )(proj, proj, proj, proj, lb, gain, sums, masks, o, states, dog, *riders)
    return tuple(out[:6]) + (list(out[6:]),)


def _lower_bounds(logits, name):
    n, d = logits.shape

    def body(l_ref, lb_ref):
        lv = l_ref[...]
        e = jnp.exp(lv - jnp.max(lv, axis=0, keepdims=True))
        p = e / jnp.sum(e, axis=0, keepdims=True)
        run = jnp.zeros((1, d), F32)
        for j in range(n):
            if j > 0:
                run = run + p[j:j + 1]
            lb_ref[j:j + 1, :] = run

    return _call(body, name=name, out_shape=_sds((n, d), F32))(logits)


def _lower_bounds_bwd(logits, dlb_parts, name):
    n, d = logits.shape

    def body(l_ref, dlb_ref, dl_ref):
        lv, dv = l_ref[...], dlb_ref[0]
        for dev in range(1, N_DEV):
            dv = dv + dlb_ref[dev]
        e = jnp.exp(lv - jnp.max(lv, axis=0, keepdims=True))
        p = e / jnp.sum(e, axis=0, keepdims=True)
        run = jnp.zeros((1, d), F32)
        dps = [None] * n
        for j in range(n - 1, 0, -1):
            run = run + dv[j:j + 1]
            dps[j] = run
        dps[0] = jnp.zeros((1, d), F32)
        inner = jnp.zeros((1, d), F32)
        for j in range(n):
            inner = inner + p[j:j + 1] * dps[j]
        for j in range(n):
            dl_ref[j:j + 1, :] = p[j:j + 1] * (dps[j] - inner)

    return _call(body, name=name, out_shape=_sds((n, d), F32))(logits, dlb_parts)


_ANY = pl.BlockSpec(memory_space=pl.ANY)
_MESH = pl.DeviceIdType.MESH


def _gather_stages(x_ref, out_ref, send_sems, recv_sems, local_sem):
    mx, my, mc = lax.axis_index("x"), lax.axis_index("y"), lax.axis_index("c")
    me, sibling = (mx, my, mc), (mx, my, 1 - mc)
    chips = [(1 - mx, my), (mx, 1 - my), (1 - mx, 1 - my)]

    def slot(px, py, pc):
        return out_ref.at[4 * px + 2 * py + pc]

    def copy(k, block, to, src=None):
        return pltpu.make_async_remote_copy(
            src_ref=slot(*block) if src is None else src, dst_ref=slot(*block),
            send_sem=send_sems.at[k], recv_sem=recv_sems.at[k], device_id=to, device_id_type=_MESH)

    mine = pltpu.make_async_copy(x_ref, slot(*me), local_sem)
    first = [copy(0, me, sibling, src=x_ref)] + [copy(1 + j, me, (*chip, mc), src=x_ref) for j, chip in enumerate(chips)]
    passed = [copy(4 + j, (*chip, mc), sibling) for j, chip in enumerate(chips)]

    def start():
        mine.start()
        for cp in first:
            cp.start()

    def middle():
        for j, chip in enumerate(chips):
            copy(1 + j, (*chip, mc), me).wait_recv()
            passed[j].start()

    def finish():
        copy(0, sibling, me).wait_recv()
        for j, chip in enumerate(chips):
            copy(4 + j, (*chip, 1 - mc), me).wait_recv()
        for cp in first + passed:
            cp.wait_send()
        mine.wait()

    return start, middle, finish


def _exchange_stages(g_ref, out_ref, send_sems, recv_sems, local_sem):
    mx, my, mc = lax.axis_index("x"), lax.axis_index("y"), lax.axis_index("c")
    me = 4 * mx + 2 * my + mc
    mine = pltpu.make_async_copy(g_ref.at[me], out_ref.at[me], local_sem)
    copies = []
    for k in range(1, N_DEV):
        px, py, pc = mx ^ (k >> 2), my ^ ((k >> 1) & 1), mc ^ (k & 1)
        peer = 4 * px + 2 * py + pc
        send = pltpu.make_async_remote_copy(
            src_ref=g_ref.at[peer], dst_ref=out_ref.at[me], send_sem=send_sems.at[k - 1],
            recv_sem=recv_sems.at[k - 1], device_id=(px, py, pc), device_id_type=_MESH)
        arrival = pltpu.make_async_remote_copy(
            src_ref=g_ref.at[peer], dst_ref=out_ref.at[peer], send_sem=send_sems.at[k - 1],
            recv_sem=recv_sems.at[k - 1], device_id=(px, py, pc), device_id_type=_MESH)
        copies.append((send, arrival))

    def start():
        mine.start()
        for send, _ in copies:
            send.start()

    def finish():
        for _, arrival in copies:
            arrival.wait_recv()
        for send, _ in copies:
            send.wait_send()
        mine.wait()

    return start, lambda: None, finish


_STAGES = {"gather": _gather_stages, "exchange": _exchange_stages}
SEMS_PER_TRANSFER = 3


def _rider_shapes(kind, arrays):
    outs = [_sds((N_DEV,) + a.shape if kind == "gather" else a.shape, a.dtype) for a in arrays]
    scratch = []
    for _ in arrays:
        scratch += [pltpu.SemaphoreType.DMA((7,)), pltpu.SemaphoreType.DMA((7,)), pltpu.SemaphoreType.DMA]
    return outs, scratch


def _ride(kind, in_refs, out_refs, sems, grid):
    stages = [_STAGES[kind](a, o, *sems[SEMS_PER_TRANSFER * n:SEMS_PER_TRANSFER * (n + 1)])
              for n, (a, o) in enumerate(zip(in_refs, out_refs))]
    if not stages:
        return lambda: None
    step, steps = pl.program_id(0) * grid[1] + pl.program_id(1), grid[0] * grid[1]

    def run(stage):
        for st in stages:
            st[stage]()

    pl.when(step == 0)(lambda: run(0))
    pl.when(step == (3 * steps) // 4)(lambda: run(1))
    return lambda: pl.when(step == steps - 1)(lambda: run(2))


def _transfer(kind, arrays, name):
    outs, scratch = _rider_shapes(kind, arrays)
    n = len(arrays)

    def body(*refs):
        stages = [_STAGES[kind](refs[t], refs[n + t], *refs[2 * n + SEMS_PER_TRANSFER * t:2 * n + SEMS_PER_TRANSFER * (t + 1)])
                  for t in range(n)]
        for stage in range(3):
            for st in stages:
                st[stage]()

    return _call(body, name=name, out_shape=outs, in_specs=[_ANY] * n, out_specs=[_ANY] * n, scratch_shapes=scratch)(*arrays)


def _all_gather(x, name):
    return _transfer("gather", [x], name)[0]


def _adamw(parts, w, m, v, name):
    n_l = len(parts)
    _, r, c = parts[0].shape
    tr = r if r <= 256 else 256
    assert r % tr == 0 and w.shape == (n_l * r, c), (name, parts[0].shape, w.shape)
    steps = r // tr

    def body(*refs):
        p_refs, (w_ref, m_ref, v_ref) = refs[:n_l], refs[n_l:n_l + 3]
        g_ref, d_ref, nm_ref, nv_ref, sum_ref = refs[n_l + 3:]
        for layer in range(n_l):
            @pl.when(pl.program_id(0) == layer)
            def _(p_ref=p_refs[layer]):
                acc = p_ref[0].astype(F32)
                for dev in range(1, N_DEV):
                    acc = acc + p_ref[dev].astype(F32)
                sum_ref[...] = acc

        g = sum_ref[...]
        nm = ADAM_B1 * m_ref[...] + (1.0 - ADAM_B1) * g
        nv = ADAM_B2 * v_ref[...] + (1.0 - ADAM_B2) * (g * g)
        m_hat = nm / (1.0 - ADAM_B1 ** ADAM_STEP)
        v_hat = nv / (1.0 - ADAM_B2 ** ADAM_STEP)
        g_ref[...] = g
        nm_ref[...] = nm
        nv_ref[...] = nv
        d_ref[...] = -ADAM_LR * (m_hat / (jnp.sqrt(v_hat) + ADAM_EPS) + ADAM_WD * w_ref[...])

    tile = pl.BlockSpec((tr, c), lambda l, i: (l * steps + i, 0))

    def part(layer):
        return pl.BlockSpec((N_DEV, tr, c), lambda l, i: (0, jnp.where(l == layer, i, 0), 0))

    return _call(
        body, name=name, grid=(n_l, steps),
        in_specs=[part(layer) for layer in range(n_l)] + [tile, tile, tile], out_specs=[tile] * 4,
        out_shape=[_sds((n_l * r, c), F32)] * 4, scratch_shapes=[pltpu.VMEM((tr, c), F32)],
        compiler_params=_cparams("arbitrary", "arbitrary"),
    )(*parts, w, m, v)


def _shard_2d(w):
    return w.astype(BF16).reshape(w.shape[0] * w.shape[1], w.shape[2])


def _full_cols(g, w):
    l, k, n = w.shape
    g = g.reshape(N_DEV, l, k, n)
    return [jnp.transpose(g[:, i], (1, 0, 2)).reshape(k, N_DEV * n) for i in range(l)]


def _full_rows(g, w):
    l, k, n = w.shape
    g = g.reshape(N_DEV, l, k, n)
    return [g[:, i].reshape(N_DEV * k, n) for i in range(l)]


def _parts_cols(grads):
    k, n8 = grads[0].shape
    g = jnp.stack(grads).reshape(len(grads), k, N_DEV, n8 // N_DEV)
    return jnp.transpose(g, (2, 0, 1, 3)).reshape(N_DEV, len(grads) * k, n8 // N_DEV)


def _parts_rows(grads):
    k8, n = grads[0].shape
    g = jnp.stack(grads).reshape(len(grads), N_DEV, k8 // N_DEV, n)
    return jnp.transpose(g, (1, 0, 2, 3)).reshape(N_DEV, len(grads) * (k8 // N_DEV), n)


def _pad_rows(a, rows):
    return jnp.concatenate([a, jnp.zeros((rows - a.shape[0], a.shape[1]), a.dtype)], axis=0)


def kernel(x, norm_gains, sb_w_qkv, sb_q_gain, sb_k_gain, sb_w_o, hg_w_in, hg_lb_logits, hg_norm_gain, hg_w_o, mlp_w1, mlp_w2, loss_target, m_norm_gains, m_sb_w_qkv, m_sb_q_gain, m_sb_k_gain, m_sb_w_o, m_hg_w_in, m_hg_lb_logits, m_hg_norm_gain, m_hg_w_o, m_mlp_w1, m_mlp_w2, v_norm_gains, v_sb_w_qkv, v_sb_q_gain, v_sb_k_gain, v_sb_w_o, v_hg_w_in, v_hg_lb_logits, v_hg_norm_gain, v_hg_w_o, v_mlp_w1, v_mlp_w2):
    depth, _, d_loc = norm_gains.shape
    n_sb, n_hg = sb_w_qkv.shape[0], hg_w_in.shape[0]
    xs = x[0]
    target = loss_target[0]
    s, d = xs.shape
    me = 4 * lax.axis_index("x") + 2 * lax.axis_index("y") + lax.axis_index("c")

    assert n_sb >= 1
    w_qkv, w_o = [None] * n_sb, [None] * n_sb
    w_in, w_ho = [None] * n_hg, [None] * n_hg
    w_qkv[0] = _full_cols(_all_gather(_shard_2d(sb_w_qkv[:1]), "gather_w_qkv"), sb_w_qkv[:1])[0]
    w_1s, w_2s = [None] * depth, [None] * depth

    def mixer_weights(layer):
        j = layer // 2
        if layer % 2 == 0:
            return [(w_qkv, j, sb_w_qkv[j:j + 1], _full_cols), (w_o, j, sb_w_o[j:j + 1], _full_rows)]
        return [(w_in, j, hg_w_in[j:j + 1], _full_cols), (w_ho, j, hg_w_o[j:j + 1], _full_rows)]
    n_gain_rows = 2 * depth
    small_rows = -(-(n_gain_rows + n_hg) // 8) * 8
    small = _pad_rows(jnp.concatenate([norm_gains.reshape(n_gain_rows, d_loc), hg_lb_logits], axis=0), small_rows)
    small = _all_gather(small, "gather_small")
    gains_full = jnp.transpose(small[:, :n_gain_rows], (1, 0, 2)).reshape(depth, 2, 1, d)
    logits_full = jnp.transpose(small[:, n_gain_rows:n_gain_rows + n_hg], (1, 0, 2)).reshape(n_hg, d)
    lower = _lower_bounds(logits_full, "lower_bounds")

    saved = []
    cur = xs
    h = _rmsnorm(cur, gains_full[0, 0], "norm_mix_0")
    for layer in range(depth):
        j = layer // 2
        ahead = mixer_weights(layer + 1) if layer + 1 < depth else []
        riders = [_shard_2d(mlp_w1[layer:layer + 1]), _shard_2d(mlp_w2[layer:layer + 1])]
        riders += [_shard_2d(shard) for _, _, shard, _ in ahead]
        if layer % 2 == 0:
            if layer == 0:
                qkv, (got_o,) = _matmul("nn", h, w_qkv[j], f"qkv_{layer}", [F32],
                                        riders=("gather", [_shard_2d(sb_w_o[:1])]))
                w_o[0] = _full_rows(got_o, sb_w_o[:1])[0]
            else:
                qkv = _matmul("nn", h, w_qkv[j], f"qkv_{layer}", [F32])
            qk_gains = _qk_gain_table(sb_q_gain[j], sb_k_gain[j], d)
            qkvn = _qknorm(qkv, qk_gains, f"qknorm_{layer}")
            o, runs, got = _sb_attention_fwd(qkvn, f"sb_fwd_{layer}", riders)
            mix, mixed, w_out = (qkv, qk_gains, qkvn, o, runs), o, w_o[j]
        else:
            proj = _matmul("nn", h, w_in[j], f"hg_in_{layer}", [F32])
            og, o, states, got = _hgrn2_fwd(proj, lower[j:j + 1], hg_norm_gain[j:j + 1], f"hg_fwd_{layer}", riders)
            mix, mixed, w_out = (proj, og, o, states), og, w_ho[j]
        w_1s[layer] = w_1 = _full_cols(got[0], mlp_w1[layer:layer + 1])[0]
        w_2s[layer] = w_2 = _full_rows(got[1], mlp_w2[layer:layer + 1])[0]
        for (dest, idx, shard, full), g in zip(ahead, got[2:]):
            dest[idx] = full(g, shard)[0]
        x1, h2 = _matmul("nn", mixed, w_out, f"mix_out_{layer}", [F32, BF16], _ep_add_norm, [cur],
                         rows=[gains_full[layer, 1]])
        a, u = _matmul("nn", h2, w_1, f"mlp_up_{layer}", [BF16, BF16], _ep_relu2)
        saved.append((cur, h, mix, x1, h2, a, u))
        if layer == depth - 1:
            cur = _matmul("nn", u, w_2, f"mlp_down_{layer}", [F32], _ep_add, [x1])
        else:
            cur, h = _matmul("nn", u, w_2, f"mlp_down_{layer}", [F32, BF16], _ep_add_norm, [x1],
                             rows=[gains_full[layer + 1, 0]])

    loss_tile, dx, dxb = _loss_head(cur, target, "loss_head")
    loss = lax.psum(loss_tile[0, 0], AXES)

    d_gains = [[None, None] for _ in range(depth)]
    d_qk, d_lb, d_hgain = [None] * n_sb, [None] * n_hg, [None] * n_hg
    received = {"sb_w_qkv": [None] * n_sb, "sb_w_o": [None] * n_sb, "hg_w_in": [None] * n_hg,
                "hg_w_o": [None] * n_hg, "mlp_w1": [None] * depth, "mlp_w2": [None] * depth}
    pending = []

    def settle(got):
        for (wname, idx, _), arrived in zip(pending, got):
            received[wname][idx] = arrived
        pending.clear()

    for layer in reversed(range(depth)):
        j = layer // 2
        x0, h, mix, x1, h2, a, u = saved[layer]
        pending.append(("mlp_w2", layer, _parts_rows([_matmul("tn", u, dxb, f"d_mlp_w2_{layer}", [BF16])])))
        da = _matmul("nt", dxb, w_2s[layer], f"d_mlp_act_{layer}", [BF16], _ep_relu2_bwd, [a])
        pending.append(("mlp_w1", layer, _parts_cols([_matmul("tn", h2, da, f"d_mlp_w1_{layer}", [BF16])])))
        dx, dxb, d_gains[layer][1] = _matmul("nt", da, w_1s[layer], f"d_mlp_in_{layer}", [F32, BF16], _ep_norm_bwd,
                                             [x1, dx], rows=[gains_full[layer, 1]], n_sums=1)
        if layer % 2 == 0:
            qkv, qk_gains, qkvn, o, runs = mix
            pending.append(("sb_w_o", j, _parts_rows([_matmul("tn", o, dxb, f"d_sb_w_o_{layer}", [BF16])])))
            do = _matmul("nt", dxb, w_o[j], f"d_sb_o_{layer}", [F32])
            dq, dk, dv, got = _sb_attention_bwd(qkvn, do, runs, f"sb_bwd_{layer}", [p for _, _, p in pending])
            settle(got)
            dqkv, d_qk[j] = _qknorm_bwd(qkv, dq, dk, dv, qk_gains, f"d_qknorm_{layer}")
            pending.append(("sb_w_qkv", j, _parts_cols([_matmul("tn", h, dqkv, f"d_sb_w_qkv_{layer}", [BF16])])))
            tail = [p for _, _, p in pending] if layer == 0 else []
            res = _matmul("nt", dqkv, w_qkv[j], f"d_sb_in_{layer}", [F32, BF16], _ep_norm_bwd, [x0, dx],
                          rows=[gains_full[layer, 0]], n_sums=1, riders=("exchange", tail) if tail else None)
            dx, dxb, d_gains[layer][0] = res[:3]
            if tail:
                settle(res[3])
        else:
            proj, og, o, states = mix
            pending.append(("hg_w_o", j, _parts_rows([_matmul("tn", og, dxb, f"d_hg_w_o_{layer}", [BF16])])))
            dog = _matmul("nt", dxb, w_ho[j], f"d_hg_o_{layer}", [F32])
            dq, df, di, dg, d_lb[j], d_hgain[j], got = _hgrn2_bwd(
                proj, lower[j:j + 1], hg_norm_gain[j:j + 1], o, states, dog, f"hg_bwd_{layer}",
                [p for _, _, p in pending])
            settle(got)
            dproj = jnp.concatenate([dq, df, di, dg], axis=1)
            pending.append(("hg_w_in", j, _parts_cols([_matmul("tn", h, dproj, f"d_hg_w_in_{layer}", [BF16])])))
            dx, dxb, d_gains[layer][0] = _matmul("nt", dproj, w_in[j], f"d_hg_in_{layer}", [F32, BF16], _ep_norm_bwd,
                                                 [x0, dx], rows=[gains_full[layer, 0]], n_sums=1)
    grad_x = dx[None]
    if pending:
        settle(_transfer("exchange", [p for _, _, p in pending], "exchange_tail"))

    def update(wname, w, m, v):
        shape = w.shape
        flat = (shape[0] * shape[1], shape[2])
        res = _adamw(received[wname], w.reshape(flat), m.reshape(flat), v.reshape(flat), "adamw_" + wname)
        return [r.reshape(shape) for r in res]

    big = {
        "sb_w_qkv": update("sb_w_qkv", sb_w_qkv, m_sb_w_qkv, v_sb_w_qkv),
        "sb_w_o": update("sb_w_o", sb_w_o, m_sb_w_o, v_sb_w_o),
        "hg_w_in": update("hg_w_in", hg_w_in, m_hg_w_in, v_hg_w_in),
        "hg_w_o": update("hg_w_o", hg_w_o, m_hg_w_o, v_hg_w_o),
        "mlp_w1": update("mlp_w1", mlp_w1, m_mlp_w1, v_mlp_w1),
        "mlp_w2": update("mlp_w2", mlp_w2, m_mlp_w2, v_mlp_w2),
    }

    d_gain_rows = jnp.concatenate([d_gains[l][t] for l in range(depth) for t in range(2)], axis=0)
    d_lb_rows = jnp.concatenate(d_lb, axis=0)
    def fold(t):
        return jnp.sum(t.reshape(d // SB_HEAD_DIM, SB_HEAD_DIM), axis=0, keepdims=True)
    d_qg = jnp.concatenate([fold(d_qk[i][0]) for i in range(n_sb)], axis=0) * SB_SCALE
    d_kg = jnp.concatenate([fold(d_qk[i][1]) for i in range(n_sb)], axis=0)
    d_hg = jnp.concatenate([jnp.sum(d_hgain[i], axis=0) for i in range(n_hg)], axis=0)
    per_row = d // LANES
    packed = jnp.concatenate([
        d_gain_rows.reshape(n_gain_rows * per_row, LANES), d_lb_rows.reshape(n_hg * per_row, LANES),
        jnp.concatenate([d_qg, d_kg], axis=1), d_hg], axis=0)
    n_packed = packed.shape[0]
    packed = _pad_rows(packed, -(-n_packed // 8) * 8)
    everyone = _all_gather(packed, "gather_small_grads")
    o_lb = n_gain_rows * per_row
    o_qk = o_lb + n_hg * per_row
    o_hg = o_qk + n_sb

    def mine_of(rows, count):
        return lax.dynamic_slice_in_dim(rows.reshape(N_DEV, count, per_row, LANES), me, 1, axis=2)[:, :, 0]

    d_logits_full = _lower_bounds_bwd(logits_full, everyone[:, o_lb:o_qk].reshape(N_DEV, n_hg, d), "lower_bounds_bwd")
    d_logits_mine = lax.dynamic_slice_in_dim(d_logits_full.reshape(n_hg, per_row, LANES), me, 1, axis=1)[:, 0]
    zeros7 = jnp.zeros((N_DEV - 1, n_hg, LANES), F32)
    small_parts = jnp.concatenate([
        mine_of(everyone[:, :o_lb], n_gain_rows),
        jnp.concatenate([d_logits_mine[None], zeros7], axis=0),
        everyone[:, o_qk:o_hg], everyone[:, o_hg:o_hg + n_hg]], axis=1)
    rows_small = small_parts.shape[1]
    pad_to = -(-rows_small // 8) * 8
    small_parts = jnp.concatenate([small_parts, jnp.zeros((N_DEV, pad_to - rows_small, LANES), F32)], axis=1)

    def pack_small(ng, qg, kg, lbl, hgn):
        return _pad_rows(jnp.concatenate([
            ng.reshape(n_gain_rows, d_loc), lbl, jnp.concatenate([qg, kg], axis=1), hgn], axis=0), pad_to)

    res = _adamw([small_parts],
                 pack_small(norm_gains, sb_q_gain, sb_k_gain, hg_lb_logits, hg_norm_gain),
                 pack_small(m_norm_gains, m_sb_q_gain, m_sb_k_gain, m_hg_lb_logits, m_hg_norm_gain),
                 pack_small(v_norm_gains, v_sb_q_gain, v_sb_k_gain, v_hg_lb_logits, v_hg_norm_gain), "adamw_small")

    def unpack_small(t):
        o1 = n_gain_rows
        o2 = o1 + n_hg
        o3 = o2 + n_sb
        return {"norm_gains": t[:o1].reshape(depth, 2, d_loc), "hg_lb_logits": t[o1:o2],
                "sb_q_gain": t[o2:o3, :SB_HEAD_DIM], "sb_k_gain": t[o2:o3, SB_HEAD_DIM:],
                "hg_norm_gain": t[o3:o3 + n_hg]}

    small_out = [unpack_small(t) for t in res]
    order = ["norm_gains", "sb_w_qkv", "sb_q_gain", "sb_k_gain", "sb_w_o", "hg_w_in", "hg_lb_logits",
             "hg_norm_gain", "hg_w_o", "mlp_w1", "mlp_w2"]
    outs = [loss, grad_x]
    for kind in range(4):
        outs += [big[n][kind] if n in big else small_out[kind][n] for n in order]
    return tuple(outs)
```

```python
import functools
import math

import numpy as np
import jax
import jax.numpy as jnp
from jax import lax
from jax.experimental import pallas as pl
from jax.experimental.pallas import tpu as pltpu

F32 = jnp.float32
BF16 = jnp.bfloat16
EPS = 1e-6
SB_HEAD_DIM = 64
HG_DIM = 128
LANES = 128
N_DEV = 8
AXES = ("x", "y", "c")
VMEM_LIMIT_BYTES = 48 * 1024 * 1024
MATMUL_VMEM_BUDGET = 40 * 1024 * 1024
SB_SCALE = 1.0 / math.sqrt(SB_HEAD_DIM)
ATT_BLOCK = 256
SB_DEAD = 104.0
SB_NEVER = -1e30
HG_CHUNK = 64
HG_STEP_CHUNKS = 4
HG_STEP_HEADS = 4
ADAM_LR, ADAM_B1, ADAM_B2, ADAM_EPS, ADAM_WD, ADAM_STEP = 0.001, 0.9, 0.999, 1e-08, 0.01, 10


def _call(body, **kw):
    return pl.pallas_call(body, **kw)


def _sds(shape, dtype):
    return jax.ShapeDtypeStruct(tuple(shape), dtype)


def _cparams(*sem):
    return pltpu.CompilerParams(dimension_semantics=sem or None, vmem_limit_bytes=VMEM_LIMIT_BYTES)


def _split_bf16(x):
    hi = x.astype(BF16)
    lo = (x - hi.astype(F32)).astype(BF16)
    return hi, lo


def _dot(a, b, dims):
    return lax.dot_general(a, b, (dims, ((), ())), preferred_element_type=F32)


NN = ((1,), (0,))
NT = ((1,), (1,))
TN = ((0,), (0,))


def _dot2(x, m, dims):
    hi, lo = _split_bf16(x)
    return _dot(hi, m, dims) + _dot(lo, m, dims)


def _mdot2(m, x, dims):
    hi, lo = _split_bf16(x)
    return _dot(m, hi, dims) + _dot(m, lo, dims)


def _matmul_tiles(m, n, k, a_dtype, b_dtype, io_dtypes):
    tm, tn = min(m, 1024), min(n, 1024)

    def need(tm, tn):
        blocks = tm * k * jnp.dtype(a_dtype).itemsize + tn * k * jnp.dtype(b_dtype).itemsize
        blocks += sum(tm * tn * jnp.dtype(dt).itemsize for dt in io_dtypes)
        return 2 * blocks + 2 * tm * tn * 4

    while need(tm, tn) > MATMUL_VMEM_BUDGET:
        if tm > 256:
            tm //= 2
        else:
            tn //= 2
    return tm, tn


def _matmul(kind, a, b, name, out_dtypes, epilogue=None, extras=(), rows=(), n_sums=0, riders=None):
    if kind == "nn":
        (m, k), n = a.shape, b.shape[1]
    elif kind == "nt":
        (m, k), n = a.shape, b.shape[0]
    else:
        (k, m), n = a.shape, b.shape[1]
    tm, tn = _matmul_tiles(m, n, k, a.dtype, b.dtype, list(out_dtypes) + [e.dtype for e in extras])
    assert m % tm == 0 and n % tn == 0 and (n == tn or not (rows or n_sums)), (name, a.shape, b.shape)
    a_spec = pl.BlockSpec((k, tm), lambda i, j: (0, i)) if kind == "tn" else pl.BlockSpec((tm, k), lambda i, j: (i, 0))
    b_spec = pl.BlockSpec((tn, k), lambda i, j: (j, 0)) if kind == "nt" else pl.BlockSpec((k, tn), lambda i, j: (0, j))
    o_spec = pl.BlockSpec((tm, tn), lambda i, j: (i, j))
    r_spec = pl.BlockSpec((1, tn), lambda i, j: (0, j))
    dims = {"nn": NN, "nt": NT, "tn": TN}[kind]
    n_ex, n_rows, n_out = len(extras), len(rows), len(out_dtypes)
    ride_kind, ride_arrays = riders if riders else ("gather", [])
    n_r = len(ride_arrays)
    ride_shapes, ride_scratch = _rider_shapes(ride_kind, ride_arrays)
    grid = (m // tm, n // tn)

    def body(*refs):
        a_ref, b_ref = refs[:2]
        n_in = 2 + n_ex + n_rows
        ins = refs[2:n_in]
        outs = refs[n_in + n_r:n_in + n_r + n_out + n_sums]
        ride_end = _ride(ride_kind, refs[n_in:n_in + n_r], refs[n_in + n_r + n_out + n_sums:n_in + 2 * n_r + n_out + n_sums],
                         refs[n_in + 2 * n_r + n_out + n_sums:], grid)
        acc = _dot(a_ref[...].astype(BF16), b_ref[...].astype(BF16), dims)
        res = epilogue(acc, *[e[...] for e in ins]) if epilogue is not None else (acc,)
        for o_ref, r in zip(outs[:n_out], res):
            o_ref[...] = r.astype(o_ref.dtype)
        if n_sums:
            @pl.when(pl.program_id(0) == 0)
            def _():
                for o_ref in outs[n_out:]:
                    o_ref[...] = jnp.zeros_like(o_ref)

            for o_ref, r in zip(outs[n_out:], res[n_out:]):
                o_ref[...] += r
        ride_end()

    out = _call(
        body, name=name, grid=grid,
        in_specs=[a_spec, b_spec] + [o_spec] * n_ex + [r_spec] * n_rows + [_ANY] * n_r,
        out_specs=[o_spec] * n_out + [r_spec] * n_sums + [_ANY] * n_r,
        out_shape=[_sds((m, n), dt) for dt in out_dtypes] + [_sds((1, n), F32)] * n_sums + ride_shapes,
        scratch_shapes=ride_scratch,
        compiler_params=_cparams("arbitrary", "arbitrary") if (n_sums or n_r) else _cparams("parallel", "parallel"),
    )(a, b, *extras, *rows, *ride_arrays)
    if n_r:
        return tuple(out[:n_out + n_sums]) + (list(out[n_out + n_sums:]),)
    return out if len(out) > 1 else out[0]


def _ep_add(acc, res):
    return (acc + res,)


def _ep_add_norm(acc, res, gain):
    x = acc + res
    r = lax.rsqrt(jnp.mean(x * x, axis=-1, keepdims=True) + EPS)
    return x, x * r * gain


def _ep_norm_bwd(dh, x, dres, gain):
    r = lax.rsqrt(jnp.mean(x * x, axis=-1, keepdims=True) + EPS)
    xr = x * r
    t = dh * gain
    dx = dres + r * (t - xr * jnp.mean(t * xr, axis=-1, keepdims=True))
    return dx, dx, jnp.sum(dh * xr, axis=0, keepdims=True)


def _ep_relu2(acc):
    r = jnp.maximum(acc, 0.0)
    return acc, r * r


def _ep_relu2_bwd(acc, a):
    return (acc * (2.0 * jnp.maximum(a.astype(F32), 0.0)),)


def _rmsnorm(x, g, name):
    s, d = x.shape
    tm = min(s, 512)

    def body(x_ref, g_ref, h_ref):
        xv = x_ref[...]
        r = lax.rsqrt(jnp.mean(xv * xv, axis=-1, keepdims=True) + EPS)
        h_ref[...] = (xv * r * g_ref[...]).astype(BF16)

    return _call(
        body, name=name, grid=(s // tm,),
        in_specs=[pl.BlockSpec((tm, d), lambda i: (i, 0)), pl.BlockSpec((1, d), lambda i: (0, 0))],
        out_specs=pl.BlockSpec((tm, d), lambda i: (i, 0)),
        out_shape=_sds((s, d), BF16), compiler_params=_cparams("parallel"),
    )(x, g)


def _loss_head(y, target, name):
    s, d = y.shape
    tm = min(s, 512)

    def body(y_ref, t_ref, loss_ref, dy_ref, dyb_ref):
        err = y_ref[...] - t_ref[...]
        dy = err * (1.0 / d)
        dy_ref[...] = dy
        dyb_ref[...] = dy.astype(BF16)

        @pl.when(pl.program_id(0) == 0)
        def _():
            loss_ref[...] = jnp.zeros_like(loss_ref)

        part = 0.5 * jnp.sum(jnp.mean(err * err, axis=-1, keepdims=True), axis=0, keepdims=True)
        loss_ref[...] += part

    row = pl.BlockSpec((tm, d), lambda i: (i, 0))
    return _call(
        body, name=name, grid=(s // tm,), in_specs=[row, row],
        out_specs=[pl.BlockSpec((8, LANES), lambda i: (0, 0)), row, row],
        out_shape=[_sds((8, LANES), F32), _sds((s, d), F32), _sds((s, d), BF16)],
        compiler_params=_cparams("arbitrary"),
    )(y, target)


def _head_lane_mask():
    lane = lax.broadcasted_iota(jnp.int32, (1, LANES), 1)
    return lane < SB_HEAD_DIM


def _pair_mean(t, first):
    del first
    r = lax.broadcasted_iota(jnp.int32, (LANES, LANES), 0) // SB_HEAD_DIM
    c = lax.broadcasted_iota(jnp.int32, (LANES, LANES), 1) // SB_HEAD_DIM
    same_half = jnp.where(r == c, 1.0, 0.0).astype(BF16)
    return _dot2(t, same_half, NN) * (1.0 / SB_HEAD_DIM)


def _pair_rms(xv, first):
    return lax.rsqrt(_pair_mean(xv * xv, first) + EPS)


def _qk_gain_table(q_gain, k_gain, d):
    reps = d // SB_HEAD_DIM
    return jnp.stack([jnp.tile(q_gain, reps) * SB_SCALE, jnp.tile(k_gain, reps), jnp.ones((d,), F32)])[:, None, :]


QKNORM_ROWS = 256


def _qknorm(qkv, gains, name):
    s, d3 = qkv.shape
    d = d3 // 3
    tm = min(s, QKNORM_ROWS)

    def body(x_ref, g_ref, o_ref):
        first = _head_lane_mask()
        is_v = pl.program_id(0) == 2
        for c in range(d // LANES):
            cols = slice(c * LANES, (c + 1) * LANES)
            xv = x_ref[:, cols]
            normed = xv * _pair_rms(xv, first) * g_ref[0, :, cols]
            o_ref[:, cols] = jnp.where(is_v, xv, normed).astype(BF16)

    tile = pl.BlockSpec((tm, d), lambda c, i: (i, c))
    return _call(
        body, name=name, grid=(3, s // tm),
        in_specs=[tile, pl.BlockSpec((1, 1, d), lambda c, i: (c, 0, 0))], out_specs=tile,
        out_shape=_sds((s, d3), BF16), compiler_params=_cparams("parallel", "parallel"),
    )(qkv, gains)


def _qknorm_bwd(qkv, dq, dk, dv, gains, name):
    s, d3 = qkv.shape
    d = d3 // 3
    tm = min(s, QKNORM_ROWS)

    def body(x_ref, dq_ref, dk_ref, dv_ref, g_ref, dx_ref, dg_ref):
        c, i = pl.program_id(0), pl.program_id(1)

        @pl.when(i == 0)
        def _():
            dg_ref[...] = jnp.zeros_like(dg_ref)

        first = _head_lane_mask()
        for col in range(d // LANES):
            cols = slice(col * LANES, (col + 1) * LANES)
            xv = x_ref[:, cols]
            dy = jnp.where(c == 0, dq_ref[:, cols], jnp.where(c == 1, dk_ref[:, cols], dv_ref[:, cols]))
            r = _pair_rms(xv, first)
            xr = xv * r
            t = dy * g_ref[0, :, cols]
            dx = r * (t - xr * _pair_mean(t * xr, first))
            dx_ref[:, cols] = jnp.where(c == 2, dy, dx).astype(BF16)
            dg_ref[0, :, cols] += jnp.sum(dy * xr, axis=0, keepdims=True)

    tile = pl.BlockSpec((tm, d), lambda c, i: (i, c))
    vec = pl.BlockSpec((1, 1, d), lambda c, i: (c, 0, 0))

    def part(kind):
        return pl.BlockSpec((tm, d), lambda c, i: (jnp.where(c == kind, i, 0), 0))

    return _call(
        body, name=name, grid=(3, s // tm), in_specs=[tile, part(0), part(1), part(2), vec], out_specs=[tile, vec],
        out_shape=[_sds((s, d3), BF16), _sds((3, 1, d), F32)],
        compiler_params=_cparams("arbitrary", "arbitrary"),
    )(qkv, dq, dk, dv, gains)


def _softplus_parts(z):
    sp = jnp.maximum(z, 0.0) + jnp.log(1.0 + jnp.exp(-jnp.abs(z)))
    return sp, z - sp


def _diag_causal(tb):
    return lax.broadcasted_iota(jnp.int32, (tb, tb), 1) < lax.broadcasted_iota(jnp.int32, (tb, tb), 0)


def _later_keys(tb):
    r = lax.broadcasted_iota(jnp.int32, (tb, tb), 0)
    c = lax.broadcasted_iota(jnp.int32, (tb, tb), 1)
    return jnp.where(r > c, 1.0, 0.0).astype(BF16)


def _sb_scores(qa, kj, causal):
    sp, logsig = _softplus_parts(_dot(qa, kj, NT))
    return (-sp if causal is None else jnp.where(causal, -sp, 0.0)), logsig


def _sb_weights(stay, logsig, run, later, causal):
    w = jnp.exp(logsig + _dot2(stay, later, NN) + run)
    return w if causal is None else jnp.where(causal, w, 0.0)


def _sb_attention_fwd(qkvn, name, riders=()):
    s, d3 = qkvn.shape
    d = d3 // 3
    pairs, tb = d // LANES, min(ATT_BLOCK, s)
    nb = s // tb
    assert nb <= LANES
    n_r = len(riders)
    ride_shapes, ride_scratch = _rider_shapes("gather", riders)

    def body(*refs):
        q_ref, k_ref, v_ref = refs[:3]
        o_ref, runs_ref = refs[3 + n_r:5 + n_r]
        ride_end = _ride("gather", refs[3:3 + n_r], refs[5 + n_r:5 + 2 * n_r], refs[5 + 2 * n_r:], (pairs, nb))
        i = pl.program_id(1)
        first = _head_lane_mask()
        lane = lax.broadcasted_iota(jnp.int32, (1, LANES), 1)
        later = _later_keys(tb)
        q2 = q_ref[...]
        qs = (jnp.where(first, q2, jnp.zeros_like(q2)), jnp.where(first, jnp.zeros_like(q2), q2))

        def tiles(js, carry, causals):
            kvs = []
            for j in js:
                rows = pl.ds(pl.multiple_of(j * tb, tb), tb)
                kvs.append((k_ref[rows, :], v_ref[rows, :]))
            scores = [[_sb_scores(qs[h], kj, causals[t]) for h in range(2)] for t, (kj, _) in enumerate(kvs)]
            run = [carry[0][0], carry[1][0]]
            acc = [carry[0][1], carry[1][1]]
            runs = [carry[0][2], carry[1][2]]
            weights = []
            for t, j in enumerate(js):
                weights.append([_sb_weights(*scores[t][h], run[h], later, causals[t]) for h in range(2)])
                for h in range(2):
                    runs[h] = jnp.where(lane == j, run[h], runs[h])
                    run[h] = run[h] + jnp.sum(scores[t][h][0], axis=-1, keepdims=True)
            for t, (_, vj) in enumerate(kvs):
                for h in range(2):
                    acc[h] = acc[h] + _dot(weights[t][h].astype(BF16), vj, NN)
            return tuple((run[h], acc[h], runs[h]) for h in range(2))

        def alive(carry):
            return jnp.maximum(jnp.max(carry[0][0]), jnp.max(carry[1][0])) >= -SB_DEAD

        never = jnp.full((tb, LANES), SB_NEVER, F32)
        zero = (jnp.zeros((tb, 1), F32), jnp.zeros((tb, LANES), F32), never)
        diag = _diag_causal(tb)
        carry = lax.cond(i == 0, lambda: tiles([i], (zero, zero), [diag]),
                         lambda: tiles([i, i - 1], (zero, zero), [diag, None]))
        left = jnp.maximum(i - 1, 0)
        _, carry = lax.while_loop(
            lambda st: (st[0] < left // 2) & alive(st[1]),
            lambda st: (st[0] + 1, tiles([i - 2 - 2 * st[0], i - 3 - 2 * st[0]], st[1], [None, None])),
            (jnp.int32(0), carry))
        carry = lax.cond((left % 2 == 1) & alive(carry), lambda c: tiles([0], c, [None]), lambda c: c, carry)
        o_ref[...] = jnp.where(first, carry[0][1], carry[1][1])
        runs_ref[0] = carry[0][2]
        runs_ref[1] = carry[1][2]
        ride_end()

    out = _call(
        body, name=name, grid=(pairs, nb),
        in_specs=[pl.BlockSpec((tb, LANES), lambda p, i: (i, p)),
                  pl.BlockSpec((s, LANES), lambda p, i: (0, pairs + p)),
                  pl.BlockSpec((s, LANES), lambda p, i: (0, 2 * pairs + p))] + [_ANY] * n_r,
        out_specs=[pl.BlockSpec((tb, LANES), lambda p, i: (i, p)),
                   pl.BlockSpec((2, tb, LANES), lambda p, i: (p * nb + i, 0, 0))] + [_ANY] * n_r,
        out_shape=[_sds((s, d), F32), _sds((pairs * nb * 2, tb, LANES), F32)] + ride_shapes,
        scratch_shapes=ride_scratch,
        compiler_params=_cparams("arbitrary", "arbitrary"),
    )(qkvn, qkvn, qkvn, *riders)
    return out[0], out[1], list(out[2:])


def _sb_attention_bwd(qkvn, do, runs, name, riders=()):
    s, d3 = qkvn.shape
    d = d3 // 3
    pairs, tb = d // LANES, min(ATT_BLOCK, s)
    nb = s // tb
    n_r = len(riders)
    ride_shapes, ride_scratch = _rider_shapes("exchange", riders)

    def body(*refs):
        q_ref, k_ref, v_ref, do_ref, runs_ref = refs[:5]
        dq_ref, dk_ref, dv_ref = refs[5 + n_r:8 + n_r]
        ride_end = _ride("exchange", refs[5:5 + n_r], refs[8 + n_r:8 + 2 * n_r], refs[8 + 2 * n_r:], (pairs, nb))
        i = pl.program_id(1)

        @pl.when(i == 0)
        def _():
            dk_ref[...] = jnp.zeros_like(dk_ref)
            dv_ref[...] = jnp.zeros_like(dv_ref)

        first = _head_lane_mask()
        lane = lax.broadcasted_iota(jnp.int32, (1, LANES), 1)
        later = _later_keys(tb)
        q2, do2 = q_ref[...], do_ref[...].astype(BF16)
        zq = jnp.zeros_like(q2)
        qs = (jnp.where(first, q2, zq), jnp.where(first, zq, q2))
        dos = (jnp.where(first, do2, zq), jnp.where(first, zq, do2))

        def tiles(js, carry, causals):
            rows, kv = [], []
            for j in js:
                r = pl.ds(pl.multiple_of(j * tb, tb), tb)
                kj, vj = k_ref[r, :], v_ref[r, :]
                zk = jnp.zeros_like(kj)
                rows.append(r)
                kv.append([(jnp.where(first, kj, zk), jnp.where(first, vj, zk)),
                           (jnp.where(first, zk, kj), jnp.where(first, zk, vj))])
            pairs_th = [(t, h) for t in range(len(js)) for h in range(2)]
            scores = {(t, h): _sb_scores(qs[h], kv[t][h][0], causals[t]) for t, h in pairs_th}
            w, g = {}, {}
            for t, h in pairs_th:
                run = jnp.sum(jnp.where(lane == js[t], runs_ref[h], 0.0), axis=-1, keepdims=True)
                w[t, h] = _sb_weights(*scores[t, h], run, later, causals[t])
                g[t, h] = w[t, h] * _dot(dos[h], kv[t][h][1], NT)
            gsum = [carry[0], carry[1]]
            dz = {}
            for t, h in pairs_th:
                before = _dot(g[t, h].astype(BF16), later, NT) + gsum[h]
                gsum[h] = gsum[h] + jnp.sum(g[t, h], axis=-1, keepdims=True)
                sig = jnp.exp(scores[t, h][1])
                d = g[t, h] * (1.0 - sig) - before * sig
                dz[t, h] = (d if causals[t] is None else jnp.where(causals[t], d, 0.0)).astype(BF16)
            dq = carry[2]
            for t, h in pairs_th:
                dq = dq + _dot(dz[t, h], kv[t][h][0], NN)
            for t in range(len(js)):
                dk_ref[rows[t], :] += _dot(dz[t, 0], qs[0], TN) + _dot(dz[t, 1], qs[1], TN)
                dv_ref[rows[t], :] += _dot(w[t, 0].astype(BF16), dos[0], TN) + _dot(w[t, 1].astype(BF16), dos[1], TN)
            return gsum[0], gsum[1], dq

        left = jnp.maximum(i - 1, 0)
        reach = jnp.max(jnp.maximum(runs_ref[0], runs_ref[1]), axis=0, keepdims=True)
        start = jnp.sum(jnp.where((reach < -SB_DEAD) & (lane < left), 1.0, 0.0)).astype(jnp.int32)
        carry = (jnp.zeros((tb, 1), F32), jnp.zeros((tb, 1), F32), jnp.zeros((tb, LANES), F32))
        live = left - start
        carry = lax.fori_loop(
            0, live // 2, lambda jj, c: tiles([start + 2 * jj, start + 2 * jj + 1], c, [None, None]), carry)
        carry = lax.cond(live % 2 == 1, lambda c: tiles([left - 1], c, [None]), lambda c: c, carry)
        diag = _diag_causal(tb)
        carry = lax.cond(i == 0, lambda c: tiles([i], c, [diag]), lambda c: tiles([i - 1, i], c, [None, diag]), carry)
        dq_ref[...] = carry[2]
        ride_end()

    q_spec = pl.BlockSpec((tb, LANES), lambda p, i: (i, p))
    out = _call(
        body, name=name, grid=(pairs, nb),
        in_specs=[q_spec,
                  pl.BlockSpec((s, LANES), lambda p, i: (0, pairs + p)),
                  pl.BlockSpec((s, LANES), lambda p, i: (0, 2 * pairs + p)),
                  q_spec,
                  pl.BlockSpec((2, tb, LANES), lambda p, i: (p * nb + i, 0, 0))] + [_ANY] * n_r,
        out_specs=[q_spec, pl.BlockSpec((s, LANES), lambda p, i: (0, p)),
                   pl.BlockSpec((s, LANES), lambda p, i: (0, p))] + [_ANY] * n_r,
        out_shape=[_sds((s, d), F32)] * 3 + ride_shapes,
        scratch_shapes=ride_scratch,
        compiler_params=_cparams("arbitrary", "arbitrary"),
    )(qkvn, qkvn, qkvn, do, runs, *riders)
    return out[0], out[1], out[2], list(out[3:])


def _hg_tables(c):
    t = np.arange(c)[:, None]
    j = np.arange(c)[None, :]
    sums = [j <= t, j > t]
    masks = []
    m = c // 2
    while m >= 1:
        pos, base = t % (2 * m), t - t % (2 * m)
        sums.append((pos >= m) & (j >= base + m) & (j <= t))
        sums.append((pos < m) & (j > t) & (j <= base + m - 1))
        masks.append((t // (2 * m) == j // (2 * m)) & (t % (2 * m) >= m) & (j % (2 * m) < m))
        m //= 2
    return (jnp.asarray(np.concatenate(sums, 0), BF16), jnp.asarray(np.stack(masks), F32), len(masks))


def _hg_gates(qr, fr, lb):
    sq = jax.nn.sigmoid(qr)
    sg = jax.nn.sigmoid(fr)
    forget = lb + (1.0 - lb) * sg
    return sq, qr * sq, sg, forget, jnp.log(forget), (1.0 - lb) * (1.0 - sg)


def _hg_scores(q, k, x, masks_ref, c, levels):
    eye = (lax.broadcasted_iota(jnp.int32, (c, c), 0) == lax.broadcasted_iota(jnp.int32, (c, c), 1)).astype(F32)
    scores = eye * jnp.sum(q * k, axis=-1, keepdims=True)
    ops = []
    for l in range(levels):
        qm = (q * x[(2 + 2 * l) * c:(3 + 2 * l) * c]).astype(BF16)
        km = (k * x[(3 + 2 * l) * c:(4 + 2 * l) * c]).astype(BF16)
        scores = scores + masks_ref[l] * _dot(qm, km, NT)
        ops.append((qm, km))
    return scores, eye, ops


def _hg_layout(s, d):
    heads, c = d // HG_DIM, min(HG_CHUNK, s)
    nsub = min(HG_STEP_CHUNKS, s // c)
    hp = HG_STEP_HEADS if heads % HG_STEP_HEADS == 0 else 1
    return heads, c, nsub, hp


def _hg_specs(s, d, reverse):
    heads, c, nsub, hp = _hg_layout(s, d)
    rows, width, groups, n_steps = c * nsub, hp * HG_DIM, heads // hp, s // (c * nsub)

    def step(si):
        return n_steps - 1 - si if reverse else si
    proj = [pl.BlockSpec((rows, width), functools.partial(lambda g, si, part: (step(si), part * groups + g), part=p))
            for p in range(4)]
    tile = pl.BlockSpec((rows, width), lambda g, si: (step(si), g))
    lb = pl.BlockSpec((1, width), lambda g, si: (0, g))
    gain = pl.BlockSpec((1, HG_DIM), lambda g, si: (0, 0))
    state = pl.BlockSpec((hp, nsub, HG_DIM, HG_DIM), lambda g, si: (g, step(si), 0, 0))
    return proj, tile, lb, gain, state, (groups, n_steps)


def _hgrn2_fwd(proj, lb, gain, name, riders=()):
    s, d4 = proj.shape
    d = d4 // 4
    heads, c, nsub, hp = _hg_layout(s, d)
    sums, masks, levels = _hg_tables(c)
    n_r = len(riders)
    ride_shapes, ride_scratch = _rider_shapes("gather", riders)
    pspecs, tile, lbs, gs, state, grid = _hg_specs(s, d, False)

    def body(*refs):
        qr_ref, fr_ref, ir_ref, gr_ref, lb_ref, gain_ref, sums_ref, masks_ref = refs[:8]
        og_ref, o_ref, states_ref = refs[8 + n_r:11 + n_r]
        st_ref = refs[11 + 2 * n_r]
        ride_end = _ride("gather", refs[8:8 + n_r], refs[11 + n_r:11 + 2 * n_r], refs[12 + 2 * n_r:], grid)

        @pl.when(pl.program_id(1) == 0)
        def _():
            st_ref[...] = jnp.zeros_like(st_ref)

        lbv, gainv = lb_ref[...], gain_ref[...]
        units = [(ci, hh) for ci in range(nsub) for hh in range(hp)]

        def lanes(hh):
            return slice(hh * HG_DIM, (hh + 1) * HG_DIM)

        pre = []
        for ci in range(nsub):
            rows = slice(ci * c, (ci + 1) * c)
            _, q, _, _, lf, k = _hg_gates(qr_ref[rows, :], fr_ref[rows, :], lbv)
            pre.append((q, k, jnp.exp(_mdot2(sums_ref[...], lf, NN))))
        scores, qh, vb, update = {}, {}, {}, {}
        for ci, hh in units:
            q, k, x = (a[:, lanes(hh)] for a in pre[ci])
            scores[ci, hh] = _hg_scores(q, k, x, masks_ref, c, levels)[0].astype(BF16)
            qh[ci, hh] = (q * x[0:c]).astype(BF16)
            vb[ci, hh] = ir_ref[ci * c:(ci + 1) * c, lanes(hh)].astype(BF16)
            update[ci, hh] = _dot(vb[ci, hh], (k * x[c:2 * c]).astype(BF16), TN)
        intra = {u: _dot(scores[u], vb[u], NN) for u in units}
        for hh in range(hp):
            st = st_ref[hh]
            for ci in range(nsub):
                rows = slice(ci * c, (ci + 1) * c)
                states_ref[hh, ci] = st
                o = _dot(qh[ci, hh], st.astype(BF16), NT) + intra[ci, hh]
                st = st * pre[ci][2][c - 1:c, lanes(hh)] + update[ci, hh]
                o_ref[rows, lanes(hh)] = o
                r = lax.rsqrt(jnp.mean(o * o, axis=-1, keepdims=True) + EPS)
                og_ref[rows, lanes(hh)] = (o * r * gainv * jax.nn.sigmoid(gr_ref[rows, lanes(hh)])).astype(BF16)
            st_ref[hh] = st
        ride_end()

    const = [pl.BlockSpec(sums.shape, lambda g, si: (0, 0)), pl.BlockSpec(masks.shape, lambda g, si: (0, 0, 0))]
    out = _call(
        body, name=name, grid=grid, in_specs=pspecs + [lbs, gs] + const + [_ANY] * n_r,
        out_specs=[tile, tile, state] + [_ANY] * n_r,
        out_shape=[_sds((s, d), BF16), _sds((s, d), F32), _sds((heads, s // c, HG_DIM, HG_DIM), F32)] + ride_shapes,
        scratch_shapes=[pltpu.VMEM((hp, HG_DIM, HG_DIM), F32)] + ride_scratch,
        compiler_params=_cparams("arbitrary", "arbitrary"),
    )(proj, proj, proj, proj, lb, gain, sums, masks, *riders)
    return out[0], out[1], out[2], list(out[3:])


def _hgrn2_bwd(proj, lb, gain, o, states, dog, name, riders=()):
    s, d4 = proj.shape
    d = d4 // 4
    heads, c, nsub, hp = _hg_layout(s, d)
    sums, masks, levels = _hg_tables(c)
    n_r = len(riders)
    ride_shapes, ride_scratch = _rider_shapes("exchange", riders)
    pspecs, tile, lbs, gs, state, grid = _hg_specs(s, d, True)

    def body(*refs):
        (qr_ref, fr_ref, ir_ref, gr_ref, lb_ref, gain_ref, sums_ref, masks_ref, o_ref, states_ref, dog_ref) = refs[:11]
        dq_ref, df_ref, di_ref, dg_ref, dlb_ref, dgain_ref = refs[11 + n_r:17 + n_r]
        dst_ref = refs[17 + 2 * n_r]
        ride_end = _ride("exchange", refs[11:11 + n_r], refs[17 + n_r:17 + 2 * n_r], refs[18 + 2 * n_r:], grid)

        @pl.when(pl.program_id(1) == 0)
        def _():
            dst_ref[...] = jnp.zeros_like(dst_ref)
            dlb_ref[...] = jnp.zeros_like(dlb_ref)
            dgain_ref[...] = jnp.zeros_like(dgain_ref)

        lbv, gainv = lb_ref[...], gain_ref[...]
        units = [(ci, hh) for ci in range(nsub) for hh in range(hp)]

        def lanes(hh):
            return slice(hh * HG_DIM, (hh + 1) * HG_DIM)

        def rows_of(ci):
            return slice(ci * c, (ci + 1) * c)

        pre = []
        for ci in range(nsub):
            qr = qr_ref[rows_of(ci), :]
            sq, q, sg, forget, lf, k = _hg_gates(qr, fr_ref[rows_of(ci), :], lbv)
            pre.append(dict(qr=qr, sq=sq, q=q, sg=sg, forget=forget, k=k, x=jnp.exp(_mdot2(sums_ref[...], lf, NN))))

        dob, vb, sc, qh_f, kh_f, feed = {}, {}, {}, {}, {}, {}
        dgain = [jnp.zeros((1, HG_DIM), F32) for _ in range(hp)]
        for ci, hh in units:
            rows, ln = rows_of(ci), lanes(hh)
            ov, gate = o_ref[rows, ln], jax.nn.sigmoid(gr_ref[rows, ln])
            r = lax.rsqrt(jnp.mean(ov * ov, axis=-1, keepdims=True) + EPS)
            orr = ov * r
            dogv = dog_ref[rows, ln]
            dg_ref[rows, ln] = (dogv * orr * gainv * gate * (1.0 - gate)).astype(BF16)
            don = dogv * gate
            dgain[hh] = dgain[hh] + jnp.sum(don * orr, axis=0, keepdims=True)
            t = don * gainv
            dob[ci, hh] = (r * (t - orr * jnp.mean(t * orr, axis=-1, keepdims=True))).astype(BF16)
            q, k, x = (pre[ci][n][:, ln] for n in ("q", "k", "x"))
            vb[ci, hh] = ir_ref[rows, ln].astype(BF16)
            sc[ci, hh] = _hg_scores(q, k, x, masks_ref, c, levels)
            qh_f[ci, hh], kh_f[ci, hh] = q * x[0:c], k * x[c:2 * c]
            feed[ci, hh] = _dot(dob[ci, hh], qh_f[ci, hh].astype(BF16), TN)

        dsts = {}
        for hh in range(hp):
            dst = dst_ref[hh]
            for ci in reversed(range(nsub)):
                dsts[ci, hh] = dst
                dst = dst * pre[ci]["x"][c - 1:c, lanes(hh)] + feed[ci, hh]
            dst_ref[hh] = dst

        dlb = [jnp.zeros((1, HG_DIM), F32) for _ in range(hp)]
        part = {(ci, hh): {n: v[:, lanes(hh)] for n, v in pre[ci].items()} for ci, hh in units}
        dscores, decay_grad, dq, dk, dexp = {}, {}, {}, {}, {}
        for u in units:
            ci, hh = u
            x, (scores, eye, _) = part[u]["x"], sc[u]
            st, dst = states_ref[hh, ci], dsts[u]
            dstb = dst.astype(BF16)
            dscores[u] = _dot(dob[u], vb[u], NT)
            di_ref[rows_of(ci), lanes(hh)] = (_dot(scores.astype(BF16), dob[u], TN)
                                              + _dot(kh_f[u].astype(BF16), dstb, NT)).astype(BF16)
            dqh = _dot(dob[u], st.astype(BF16), NN)
            dkh = _dot(vb[u], dstb, NN)
            decay_grad[u] = x[c - 1:c] * jnp.sum(dst * st, axis=0, keepdims=True)
            ddiag = jnp.sum(eye * dscores[u], axis=-1, keepdims=True)
            dq[u] = dqh * x[0:c] + ddiag * part[u]["k"]
            dk[u] = dkh * x[c:2 * c] + ddiag * part[u]["q"]
            dexp[u] = [dqh * qh_f[u], dkh * kh_f[u]]
        for l in range(levels):
            for u in units:
                q, k, x = part[u]["q"], part[u]["k"], part[u]["x"]
                qm, km = sc[u][2][l]
                dsm = (masks_ref[l] * dscores[u]).astype(BF16)
                dqm, dkm = _dot(dsm, km, NN), _dot(dsm, qm, TN)
                xq, xk = x[(2 + 2 * l) * c:(3 + 2 * l) * c], x[(3 + 2 * l) * c:(4 + 2 * l) * c]
                dq[u] = dq[u] + dqm * xq
                dk[u] = dk[u] + dkm * xk
                dexp[u] += [dqm * (q * xq), dkm * (k * xk)]
        for u in units:
            ci, hh = u
            rows, ln, p = rows_of(ci), lanes(hh), part[u]
            dlf = _mdot2(sums_ref[...], jnp.concatenate(dexp[u], axis=0), TN) + decay_grad[u]
            dforget = dlf / p["forget"] - dk[u]
            dlb[hh] = dlb[hh] + jnp.sum(dforget * (1.0 - p["sg"]), axis=0, keepdims=True)
            df_ref[rows, ln] = (dforget * (1.0 - lbv[:, ln]) * p["sg"] * (1.0 - p["sg"])).astype(BF16)
            dq_ref[rows, ln] = (dq[u] * p["sq"] * (1.0 + p["qr"] * (1.0 - p["sq"]))).astype(BF16)
        for hh in range(hp):
            dlb_ref[:, lanes(hh)] += dlb[hh]
            dgain_ref[hh] += dgain[hh]
        ride_end()

    const = [pl.BlockSpec(sums.shape, lambda g, si: (0, 0)), pl.BlockSpec(masks.shape, lambda g, si: (0, 0, 0))]
    out = _call(
        body, name=name, grid=grid, in_specs=pspecs + [lbs, gs] + const + [tile, state, tile] + [_ANY] * n_r,
        out_specs=[tile, tile, tile, tile, lbs, pl.BlockSpec((hp, 1, HG_DIM), lambda g, si: (g, 0, 0))] + [_ANY] * n_r,
        out_shape=[_sds((s, d), BF16)] * 4 + [_sds((1, d), F32), _sds((heads, 1, HG_DIM), F32)] + ride_shapes,
        scratch_shapes=[pltpu.VMEM((hp, HG_DIM, HG_DIM), F32)] + ride_scratch,
        compiler_params=_cparams("arbitrary", "arbitrary"),
    )(proj, proj, proj, proj, lb, gain, sums, masks, o, states, dog, *riders)
    return tuple(out[:6]) + (list(out[6:]),)


def _lower_bounds(logits, name):
    n, d = logits.shape

    def body(l_ref, lb_ref):
        lv = l_ref[...]
        e = jnp.exp(lv - jnp.max(lv, axis=0, keepdims=True))
        p = e / jnp.sum(e, axis=0, keepdims=True)
        run = jnp.zeros((1, d), F32)
        for j in range(n):
            if j > 0:
                run = run + p[j:j + 1]
            lb_ref[j:j + 1, :] = run

    return _call(body, name=name, out_shape=_sds((n, d), F32))(logits)


def _lower_bounds_bwd(logits, dlb_parts, name):
    n, d = logits.shape

    def body(l_ref, dlb_ref, dl_ref):
        lv, dv = l_ref[...], dlb_ref[0]
        for dev in range(1, N_DEV):
            dv = dv + dlb_ref[dev]
        e = jnp.exp(lv - jnp.max(lv, axis=0, keepdims=True))
        p = e / jnp.sum(e, axis=0, keepdims=True)
        run = jnp.zeros((1, d), F32)
        dps = [None] * n
        for j in range(n - 1, 0, -1):
            run = run + dv[j:j + 1]
            dps[j] = run
        dps[0] = jnp.zeros((1, d), F32)
        inner = jnp.zeros((1, d), F32)
        for j in range(n):
            inner = inner + p[j:j + 1] * dps[j]
        for j in range(n):
            dl_ref[j:j + 1, :] = p[j:j + 1] * (dps[j] - inner)

    return _call(body, name=name, out_shape=_sds((n, d), F32))(logits, dlb_parts)


_ANY = pl.BlockSpec(memory_space=pl.ANY)
_MESH = pl.DeviceIdType.MESH


def _gather_stages(x_ref, out_ref, send_sems, recv_sems, local_sem):
    mx, my, mc = lax.axis_index("x"), lax.axis_index("y"), lax.axis_index("c")
    me, sibling = (mx, my, mc), (mx, my, 1 - mc)
    chips = [(1 - mx, my), (mx, 1 - my), (1 - mx, 1 - my)]

    def slot(px, py, pc):
        return out_ref.at[4 * px + 2 * py + pc]

    def copy(k, block, to, src=None):
        return pltpu.make_async_remote_copy(
            src_ref=slot(*block) if src is None else src, dst_ref=slot(*block),
            send_sem=send_sems.at[k], recv_sem=recv_sems.at[k], device_id=to, device_id_type=_MESH)

    mine = pltpu.make_async_copy(x_ref, slot(*me), local_sem)
    first = [copy(0, me, sibling, src=x_ref)] + [copy(1 + j, me, (*chip, mc), src=x_ref) for j, chip in enumerate(chips)]
    passed = [copy(4 + j, (*chip, mc), sibling) for j, chip in enumerate(chips)]

    def start():
        mine.start()
        for cp in first:
            cp.start()

    def middle():
        for j, chip in enumerate(chips):
            copy(1 + j, (*chip, mc), me).wait_recv()
            passed[j].start()

    def finish():
        copy(0, sibling, me).wait_recv()
        for j, chip in enumerate(chips):
            copy(4 + j, (*chip, 1 - mc), me).wait_recv()
        for cp in first + passed:
            cp.wait_send()
        mine.wait()

    return start, middle, finish


def _exchange_stages(g_ref, out_ref, send_sems, recv_sems, local_sem):
    mx, my, mc = lax.axis_index("x"), lax.axis_index("y"), lax.axis_index("c")
    me = 4 * mx + 2 * my + mc
    mine = pltpu.make_async_copy(g_ref.at[me], out_ref.at[me], local_sem)
    copies = []
    for k in range(1, N_DEV):
        px, py, pc = mx ^ (k >> 2), my ^ ((k >> 1) & 1), mc ^ (k & 1)
        peer = 4 * px + 2 * py + pc
        send = pltpu.make_async_remote_copy(
            src_ref=g_ref.at[peer], dst_ref=out_ref.at[me], send_sem=send_sems.at[k - 1],
            recv_sem=recv_sems.at[k - 1], device_id=(px, py, pc), device_id_type=_MESH)
        arrival = pltpu.make_async_remote_copy(
            src_ref=g_ref.at[peer], dst_ref=out_ref.at[peer], send_sem=send_sems.at[k - 1],
            recv_sem=recv_sems.at[k - 1], device_id=(px, py, pc), device_id_type=_MESH)
        copies.append((send, arrival))

    def start():
        mine.start()
        for send, _ in copies:
            send.start()

    def finish():
        for _, arrival in copies:
            arrival.wait_recv()
        for send, _ in copies:
            send.wait_send()
        mine.wait()

    return start, lambda: None, finish


_STAGES = {"gather": _gather_stages, "exchange": _exchange_stages}
SEMS_PER_TRANSFER = 3


def _rider_shapes(kind, arrays):
    outs = [_sds((N_DEV,) + a.shape if kind == "gather" else a.shape, a.dtype) for a in arrays]
    scratch = []
    for _ in arrays:
        scratch += [pltpu.SemaphoreType.DMA((7,)), pltpu.SemaphoreType.DMA((7,)), pltpu.SemaphoreType.DMA]
    return outs, scratch


def _ride(kind, in_refs, out_refs, sems, grid):
    stages = [_STAGES[kind](a, o, *sems[SEMS_PER_TRANSFER * n:SEMS_PER_TRANSFER * (n + 1)])
              for n, (a, o) in enumerate(zip(in_refs, out_refs))]
    if not stages:
        return lambda: None
    step, steps = pl.program_id(0) * grid[1] + pl.program_id(1), grid[0] * grid[1]

    def run(stage):
        for st in stages:
            st[stage]()

    pl.when(step == 0)(lambda: run(0))
    pl.when(step == (3 * steps) // 4)(lambda: run(1))
    return lambda: pl.when(step == steps - 1)(lambda: run(2))


def _transfer(kind, arrays, name):
    outs, scratch = _rider_shapes(kind, arrays)
    n = len(arrays)

    def body(*refs):
        stages = [_STAGES[kind](refs[t], refs[n + t], *refs[2 * n + SEMS_PER_TRANSFER * t:2 * n + SEMS_PER_TRANSFER * (t + 1)])
                  for t in range(n)]
        for stage in range(3):
            for st in stages:
                st[stage]()

    return _call(body, name=name, out_shape=outs, in_specs=[_ANY] * n, out_specs=[_ANY] * n, scratch_shapes=scratch)(*arrays)


def _all_gather(x, name):
    return _transfer("gather", [x], name)[0]


def _adamw(parts, w, m, v, name):
    n_l = len(parts)
    _, r, c = parts[0].shape
    tr = r if r <= 256 else 256
    assert r % tr == 0 and w.shape == (n_l * r, c), (name, parts[0].shape, w.shape)
    steps = r // tr

    def body(*refs):
        p_refs, (w_ref, m_ref, v_ref) = refs[:n_l], refs[n_l:n_l + 3]
        g_ref, d_ref, nm_ref, nv_ref, sum_ref = refs[n_l + 3:]
        for layer in range(n_l):
            @pl.when(pl.program_id(0) == layer)
            def _(p_ref=p_refs[layer]):
                acc = p_ref[0].astype(F32)
                for dev in range(1, N_DEV):
                    acc = acc + p_ref[dev].astype(F32)
                sum_ref[...] = acc

        g = sum_ref[...]
        nm = ADAM_B1 * m_ref[...] + (1.0 - ADAM_B1) * g
        nv = ADAM_B2 * v_ref[...] + (1.0 - ADAM_B2) * (g * g)
        m_hat = nm / (1.0 - ADAM_B1 ** ADAM_STEP)
        v_hat = nv / (1.0 - ADAM_B2 ** ADAM_STEP)
        g_ref[...] = g
        nm_ref[...] = nm
        nv_ref[...] = nv
        d_ref[...] = -ADAM_LR * (m_hat / (jnp.sqrt(v_hat) + ADAM_EPS) + ADAM_WD * w_ref[...])

    tile = pl.BlockSpec((tr, c), lambda l, i: (l * steps + i, 0))

    def part(layer):
        return pl.BlockSpec((N_DEV, tr, c), lambda l, i: (0, jnp.where(l == layer, i, 0), 0))

    return _call(
        body, name=name, grid=(n_l, steps),
        in_specs=[part(layer) for layer in range(n_l)] + [tile, tile, tile], out_specs=[tile] * 4,
        out_shape=[_sds((n_l * r, c), F32)] * 4, scratch_shapes=[pltpu.VMEM((tr, c), F32)],
        compiler_params=_cparams("arbitrary", "arbitrary"),
    )(*parts, w, m, v)


def _shard_2d(w):
    return w.astype(BF16).reshape(w.shape[0] * w.shape[1], w.shape[2])


def _full_cols(g, w):
    l, k, n = w.shape
    g = g.reshape(N_DEV, l, k, n)
    return [jnp.transpose(g[:, i], (1, 0, 2)).reshape(k, N_DEV * n) for i in range(l)]


def _full_rows(g, w):
    l, k, n = w.shape
    g = g.reshape(N_DEV, l, k, n)
    return [g[:, i].reshape(N_DEV * k, n) for i in range(l)]


def _parts_cols(grads):
    k, n8 = grads[0].shape
    g = jnp.stack(grads).reshape(len(grads), k, N_DEV, n8 // N_DEV)
    return jnp.transpose(g, (2, 0, 1, 3)).reshape(N_DEV, len(grads) * k, n8 // N_DEV)


def _parts_rows(grads):
    k8, n = grads[0].shape
    g = jnp.stack(grads).reshape(len(grads), N_DEV, k8 // N_DEV, n)
    return jnp.transpose(g, (1, 0, 2, 3)).reshape(N_DEV, len(grads) * (k8 // N_DEV), n)


def _pad_rows(a, rows):
    return jnp.concatenate([a, jnp.zeros((rows - a.shape[0], a.shape[1]), a.dtype)], axis=0)


def kernel(x, norm_gains, sb_w_qkv, sb_q_gain, sb_k_gain, sb_w_o, hg_w_in, hg_lb_logits, hg_norm_gain, hg_w_o, mlp_w1, mlp_w2, loss_target, m_norm_gains, m_sb_w_qkv, m_sb_q_gain, m_sb_k_gain, m_sb_w_o, m_hg_w_in, m_hg_lb_logits, m_hg_norm_gain, m_hg_w_o, m_mlp_w1, m_mlp_w2, v_norm_gains, v_sb_w_qkv, v_sb_q_gain, v_sb_k_gain, v_sb_w_o, v_hg_w_in, v_hg_lb_logits, v_hg_norm_gain, v_hg_w_o, v_mlp_w1, v_mlp_w2):
    depth, _, d_loc = norm_gains.shape
    n_sb, n_hg = sb_w_qkv.shape[0], hg_w_in.shape[0]
    xs = x[0]
    target = loss_target[0]
    s, d = xs.shape
    me = 4 * lax.axis_index("x") + 2 * lax.axis_index("y") + lax.axis_index("c")

    assert n_sb >= 1
    w_qkv, w_o = [None] * n_sb, [None] * n_sb
    w_in, w_ho = [None] * n_hg, [None] * n_hg
    w_1s, w_2s = [None] * depth, [None] * depth

    def mixer_weights(layer):
        j = layer // 2
        if layer % 2 == 0:
            return [(w_qkv, j, sb_w_qkv[j:j + 1], _full_cols), (w_o, j, sb_w_o[j:j + 1], _full_rows)]
        return [(w_in, j, hg_w_in[j:j + 1], _full_cols), (w_ho, j, hg_w_o[j:j + 1], _full_rows)]
    n_gain_rows = 2 * depth
    small_rows = -(-(n_gain_rows + n_hg) // 8) * 8
    small = _pad_rows(jnp.concatenate([norm_gains.reshape(n_gain_rows, d_loc), hg_lb_logits], axis=0), small_rows)
    got_qkv, small = _transfer("gather", [_shard_2d(sb_w_qkv[:1]), small], "gather_first")
    w_qkv[0] = _full_cols(got_qkv, sb_w_qkv[:1])[0]
    gains_full = jnp.transpose(small[:, :n_gain_rows], (1, 0, 2)).reshape(depth, 2, 1, d)
    logits_full = jnp.transpose(small[:, n_gain_rows:n_gain_rows + n_hg], (1, 0, 2)).reshape(n_hg, d)
    lower = _lower_bounds(logits_full, "lower_bounds")

    saved = []
    cur = xs
    h = _rmsnorm(cur, gains_full[0, 0], "norm_mix_0")
    for layer in range(depth):
        j = layer // 2
        ahead = [wt for nxt in (layer + 1, layer + 2) if nxt < depth and layer % 2 == 0 for wt in mixer_weights(nxt)]
        riders = [_shard_2d(mlp_w1[layer:layer + 1]), _shard_2d(mlp_w2[layer:layer + 1])]
        riders += [_shard_2d(shard) for _, _, shard, _ in ahead]
        if layer % 2 == 0:
            if layer == 0:
                qkv, (got_o,) = _matmul("nn", h, w_qkv[j], f"qkv_{layer}", [F32],
                                        riders=("gather", [_shard_2d(sb_w_o[:1])]))
                w_o[0] = _full_rows(got_o, sb_w_o[:1])[0]
            else:
                qkv = _matmul("nn", h, w_qkv[j], f"qkv_{layer}", [F32])
            qk_gains = _qk_gain_table(sb_q_gain[j], sb_k_gain[j], d)
            qkvn = _qknorm(qkv, qk_gains, f"qknorm_{layer}")
            o, runs, got = _sb_attention_fwd(qkvn, f"sb_fwd_{layer}", riders)
            mix, mixed, w_out = (qkv, qk_gains, qkvn, o, runs), o, w_o[j]
        else:
            proj = _matmul("nn", h, w_in[j], f"hg_in_{layer}", [F32])
            og, o, states, got = _hgrn2_fwd(proj, lower[j:j + 1], hg_norm_gain[j:j + 1], f"hg_fwd_{layer}", riders)
            mix, mixed, w_out = (proj, og, o, states), og, w_ho[j]
        w_1s[layer] = w_1 = _full_cols(got[0], mlp_w1[layer:layer + 1])[0]
        w_2s[layer] = w_2 = _full_rows(got[1], mlp_w2[layer:layer + 1])[0]
        for (dest, idx, shard, full), g in zip(ahead, got[2:]):
            dest[idx] = full(g, shard)[0]
        x1, h2 = _matmul("nn", mixed, w_out, f"mix_out_{layer}", [F32, BF16], _ep_add_norm, [cur],
                         rows=[gains_full[layer, 1]])
        a, u = _matmul("nn", h2, w_1, f"mlp_up_{layer}", [BF16, BF16], _ep_relu2)
        saved.append((cur, h, mix, x1, h2, a, u))
        if layer == depth - 1:
            cur = _matmul("nn", u, w_2, f"mlp_down_{layer}", [F32], _ep_add, [x1])
        else:
            cur, h = _matmul("nn", u, w_2, f"mlp_down_{layer}", [F32, BF16], _ep_add_norm, [x1],
                             rows=[gains_full[layer + 1, 0]])

    loss_tile, dx, dxb = _loss_head(cur, target, "loss_head")
    loss = lax.psum(loss_tile[0, 0], AXES)

    d_gains = [[None, None] for _ in range(depth)]
    d_qk, d_lb, d_hgain = [None] * n_sb, [None] * n_hg, [None] * n_hg
    received = {"sb_w_qkv": [None] * n_sb, "sb_w_o": [None] * n_sb, "hg_w_in": [None] * n_hg,
                "hg_w_o": [None] * n_hg, "mlp_w1": [None] * depth, "mlp_w2": [None] * depth}
    pending = []

    def settle(got):
        for (wname, idx, _), arrived in zip(pending, got):
            received[wname][idx] = arrived
        pending.clear()

    for layer in reversed(range(depth)):
        j = layer // 2
        x0, h, mix, x1, h2, a, u = saved[layer]
        pending.append(("mlp_w2", layer, _parts_rows([_matmul("tn", u, dxb, f"d_mlp_w2_{layer}", [BF16])])))
        da = _matmul("nt", dxb, w_2s[layer], f"d_mlp_act_{layer}", [BF16], _ep_relu2_bwd, [a])
        pending.append(("mlp_w1", layer, _parts_cols([_matmul("tn", h2, da, f"d_mlp_w1_{layer}", [BF16])])))
        dx, dxb, d_gains[layer][1] = _matmul("nt", da, w_1s[layer], f"d_mlp_in_{layer}", [F32, BF16], _ep_norm_bwd,
                                             [x1, dx], rows=[gains_full[layer, 1]], n_sums=1)
        if layer % 2 == 0:
            qkv, qk_gains, qkvn, o, runs = mix
            pending.append(("sb_w_o", j, _parts_rows([_matmul("tn", o, dxb, f"d_sb_w_o_{layer}", [BF16])])))
            do = _matmul("nt", dxb, w_o[j], f"d_sb_o_{layer}", [F32])
            dq, dk, dv, got = _sb_attention_bwd(qkvn, do, runs, f"sb_bwd_{layer}", [p for _, _, p in pending])
            settle(got)
            dqkv, d_qk[j] = _qknorm_bwd(qkv, dq, dk, dv, qk_gains, f"d_qknorm_{layer}")
            pending.append(("sb_w_qkv", j, _parts_cols([_matmul("tn", h, dqkv, f"d_sb_w_qkv_{layer}", [BF16])])))
            tail = [p for _, _, p in pending] if layer == 0 else []
            res = _matmul("nt", dqkv, w_qkv[j], f"d_sb_in_{layer}", [F32, BF16], _ep_norm_bwd, [x0, dx],
                          rows=[gains_full[layer, 0]], n_sums=1, riders=("exchange", tail) if tail else None)
            dx, dxb, d_gains[layer][0] = res[:3]
            if tail:
                settle(res[3])
        else:
            proj, og, o, states = mix
            pending.append(("hg_w_o", j, _parts_rows([_matmul("tn", og, dxb, f"d_hg_w_o_{layer}", [BF16])])))
            dog = _matmul("nt", dxb, w_ho[j], f"d_hg_o_{layer}", [F32])
            dq, df, di, dg, d_lb[j], d_hgain[j], got = _hgrn2_bwd(
                proj, lower[j:j + 1], hg_norm_gain[j:j + 1], o, states, dog, f"hg_bwd_{layer}",
                [p for _, _, p in pending])
            settle(got)
            dproj = jnp.concatenate([dq, df, di, dg], axis=1)
            pending.append(("hg_w_in", j, _parts_cols([_matmul("tn", h, dproj, f"d_hg_w_in_{layer}", [BF16])])))
            dx, dxb, d_gains[layer][0] = _matmul("nt", dproj, w_in[j], f"d_hg_in_{layer}", [F32, BF16], _ep_norm_bwd,
                                                 [x0, dx], rows=[gains_full[layer, 0]], n_sums=1)
    grad_x = dx[None]
    if pending:
        settle(_transfer("exchange", [p for _, _, p in pending], "exchange_tail"))

    def update(wname, w, m, v):
        shape = w.shape
        flat = (shape[0] * shape[1], shape[2])
        res = _adamw(received[wname], w.reshape(flat), m.reshape(flat), v.reshape(flat), "adamw_" + wname)
        return [r.reshape(shape) for r in res]

    big = {
        "sb_w_qkv": update("sb_w_qkv", sb_w_qkv, m_sb_w_qkv, v_sb_w_qkv),
        "sb_w_o": update("sb_w_o", sb_w_o, m_sb_w_o, v_sb_w_o),
        "hg_w_in": update("hg_w_in", hg_w_in, m_hg_w_in, v_hg_w_in),
        "hg_w_o": update("hg_w_o", hg_w_o, m_hg_w_o, v_hg_w_o),
        "mlp_w1": update("mlp_w1", mlp_w1, m_mlp_w1, v_mlp_w1),
        "mlp_w2": update("mlp_w2", mlp_w2, m_mlp_w2, v_mlp_w2),
    }

    d_gain_rows = jnp.concatenate([d_gains[l][t] for l in range(depth) for t in range(2)], axis=0)
    d_lb_rows = jnp.concatenate(d_lb, axis=0)
    def fold(t):
        return jnp.sum(t.reshape(d // SB_HEAD_DIM, SB_HEAD_DIM), axis=0, keepdims=True)
    d_qg = jnp.concatenate([fold(d_qk[i][0]) for i in range(n_sb)], axis=0) * SB_SCALE
    d_kg = jnp.concatenate([fold(d_qk[i][1]) for i in range(n_sb)], axis=0)
    d_hg = jnp.concatenate([jnp.sum(d_hgain[i], axis=0) for i in range(n_hg)], axis=0)
    per_row = d // LANES
    packed = jnp.concatenate([
        d_gain_rows.reshape(n_gain_rows * per_row, LANES), d_lb_rows.reshape(n_hg * per_row, LANES),
        jnp.concatenate([d_qg, d_kg], axis=1), d_hg], axis=0)
    n_packed = packed.shape[0]
    packed = _pad_rows(packed, -(-n_packed // 8) * 8)
    everyone = _all_gather(packed, "gather_small_grads")
    o_lb = n_gain_rows * per_row
    o_qk = o_lb + n_hg * per_row
    o_hg = o_qk + n_sb

    def mine_of(rows, count):
        return lax.dynamic_slice_in_dim(rows.reshape(N_DEV, count, per_row, LANES), me, 1, axis=2)[:, :, 0]

    d_logits_full = _lower_bounds_bwd(logits_full, everyone[:, o_lb:o_qk].reshape(N_DEV, n_hg, d), "lower_bounds_bwd")
    d_logits_mine = lax.dynamic_slice_in_dim(d_logits_full.reshape(n_hg, per_row, LANES), me, 1, axis=1)[:, 0]
    zeros7 = jnp.zeros((N_DEV - 1, n_hg, LANES), F32)
    small_parts = jnp.concatenate([
        mine_of(everyone[:, :o_lb], n_gain_rows),
        jnp.concatenate([d_logits_mine[None], zeros7], axis=0),
        everyone[:, o_qk:o_hg], everyone[:, o_hg:o_hg + n_hg]], axis=1)
    rows_small = small_parts.shape[1]
    pad_to = -(-rows_small // 8) * 8
    small_parts = jnp.concatenate([small_parts, jnp.zeros((N_DEV, pad_to - rows_small, LANES), F32)], axis=1)

    def pack_small(ng, qg, kg, lbl, hgn):
        return _pad_rows(jnp.concatenate([
            ng.reshape(n_gain_rows, d_loc), lbl, jnp.concatenate([qg, kg], axis=1), hgn], axis=0), pad_to)

    res = _adamw([small_parts],
                 pack_small(norm_gains, sb_q_gain, sb_k_gain, hg_lb_logits, hg_norm_gain),
                 pack_small(m_norm_gains, m_sb_q_gain, m_sb_k_gain, m_hg_lb_logits, m_hg_norm_gain),
                 pack_small(v_norm_gains, v_sb_q_gain, v_sb_k_gain, v_hg_lb_logits, v_hg_norm_gain), "adamw_small")

    def unpack_small(t):
        o1 = n_gain_rows
        o2 = o1 + n_hg
        o3 = o2 + n_sb
        return {"norm_gains": t[:o1].reshape(depth, 2, d_loc), "hg_lb_logits": t[o1:o2],
                "sb_q_gain": t[o2:o3, :SB_HEAD_DIM], "sb_k_gain": t[o2:o3, SB_HEAD_DIM:],
                "hg_norm_gain": t[o3:o3 + n_hg]}

    small_out = [unpack_small(t) for t in res]
    order = ["norm_gains", "sb_w_qkv", "sb_q_gain", "sb_k_gain", "sb_w_o", "hg_w_in", "hg_lb_logits",
             "hg_norm_gain", "hg_w_o", "mlp_w1", "mlp_w2"]
    outs = [loss, grad_x]
    for kind in range(4):
        outs += [big[n][kind] if n in big else small_out[kind][n] for n in order]
    return tuple(outs)
```

```python
import functools
import math

import numpy as np
import jax
import jax.numpy as jnp
from jax import lax
from jax.experimental import pallas as pl
from jax.experimental.pallas import tpu as pltpu

F32 = jnp.float32
BF16 = jnp.bfloat16
EPS = 1e-6
SB_HEAD_DIM = 64
HG_DIM = 128
LANES = 128
N_DEV = 8
AXES = ("x", "y", "c")
VMEM_LIMIT_BYTES = 48 * 1024 * 1024
MATMUL_VMEM_BUDGET = 40 * 1024 * 1024
SB_SCALE = 1.0 / math.sqrt(SB_HEAD_DIM)
ATT_BLOCK = 256
SB_DEAD = 104.0
SB_NEVER = -1e30
HG_CHUNK = 64
HG_STEP_CHUNKS = 4
HG_STEP_HEADS = 4
ADAM_LR, ADAM_B1, ADAM_B2, ADAM_EPS, ADAM_WD, ADAM_STEP = 0.001, 0.9, 0.999, 1e-08, 0.01, 10


def _call(body, **kw):
    return pl.pallas_call(body, **kw)


def _sds(shape, dtype):
    return jax.ShapeDtypeStruct(tuple(shape), dtype)


def _cparams(*sem):
    return pltpu.CompilerParams(dimension_semantics=sem or None, vmem_limit_bytes=VMEM_LIMIT_BYTES)


def _split_bf16(x):
    hi = x.astype(BF16)
    lo = (x - hi.astype(F32)).astype(BF16)
    return hi, lo


def _dot(a, b, dims):
    return lax.dot_general(a, b, (dims, ((), ())), preferred_element_type=F32)


NN = ((1,), (0,))
NT = ((1,), (1,))
TN = ((0,), (0,))


def _dot2(x, m, dims):
    hi, lo = _split_bf16(x)
    return _dot(hi, m, dims) + _dot(lo, m, dims)


def _mdot2(m, x, dims):
    hi, lo = _split_bf16(x)
    return _dot(m, hi, dims) + _dot(m, lo, dims)


def _matmul_tiles(m, n, k, a_dtype, b_dtype, io_dtypes):
    tm, tn = min(m, 1024), min(n, 1024)

    def need(tm, tn):
        blocks = tm * k * jnp.dtype(a_dtype).itemsize + tn * k * jnp.dtype(b_dtype).itemsize
        blocks += sum(tm * tn * jnp.dtype(dt).itemsize for dt in io_dtypes)
        return 2 * blocks + 2 * tm * tn * 4

    while need(tm, tn) > MATMUL_VMEM_BUDGET:
        if tm > 256:
            tm //= 2
        else:
            tn //= 2
    return tm, tn


def _matmul(kind, a, b, name, out_dtypes, epilogue=None, extras=(), rows=(), n_sums=0, riders=None):
    if kind == "nn":
        (m, k), n = a.shape, b.shape[1]
    elif kind == "nt":
        (m, k), n = a.shape, b.shape[0]
    else:
        (k, m), n = a.shape, b.shape[1]
    tm, tn = _matmul_tiles(m, n, k, a.dtype, b.dtype, list(out_dtypes) + [e.dtype for e in extras])
    assert m % tm == 0 and n % tn == 0 and (n == tn or not n_sums), (name, a.shape, b.shape)
    a_spec = pl.BlockSpec((k, tm), lambda i, j: (0, i)) if kind == "tn" else pl.BlockSpec((tm, k), lambda i, j: (i, 0))
    b_spec = pl.BlockSpec((tn, k), lambda i, j: (j, 0)) if kind == "nt" else pl.BlockSpec((k, tn), lambda i, j: (0, j))
    o_spec = pl.BlockSpec((tm, tn), lambda i, j: (i, j))
    r_spec = pl.BlockSpec((1, tn), lambda i, j: (0, j))
    dims = {"nn": NN, "nt": NT, "tn": TN}[kind]
    n_ex, n_rows, n_out = len(extras), len(rows), len(out_dtypes)
    ride_kind, ride_arrays = riders if riders else ("gather", [])
    n_r = len(ride_arrays)
    ride_shapes, ride_scratch = _rider_shapes(ride_kind, ride_arrays)
    grid = (m // tm, n // tn)

    def body(*refs):
        a_ref, b_ref = refs[:2]
        n_in = 2 + n_ex + n_rows
        ins = refs[2:n_in]
        outs = refs[n_in + n_r:n_in + n_r + n_out + n_sums]
        ride_end = _ride(ride_kind, refs[n_in:n_in + n_r], refs[n_in + n_r + n_out + n_sums:n_in + 2 * n_r + n_out + n_sums],
                         refs[n_in + 2 * n_r + n_out + n_sums:], grid)
        acc = _dot(a_ref[...].astype(BF16), b_ref[...].astype(BF16), dims)
        res = epilogue(acc, *[e[...] for e in ins]) if epilogue is not None else (acc,)
        for o_ref, r in zip(outs[:n_out], res):
            o_ref[...] = r.astype(o_ref.dtype)
        if n_sums:
            @pl.when(pl.program_id(0) == 0)
            def _():
                for o_ref in outs[n_out:]:
                    o_ref[...] = jnp.zeros_like(o_ref)

            for o_ref, r in zip(outs[n_out:], res[n_out:]):
                o_ref[...] += r
        ride_end()

    out = _call(
        body, name=name, grid=grid,
        in_specs=[a_spec, b_spec] + [o_spec] * n_ex + [r_spec] * n_rows + [_ANY] * n_r,
        out_specs=[o_spec] * n_out + [r_spec] * n_sums + [_ANY] * n_r,
        out_shape=[_sds((m, n), dt) for dt in out_dtypes] + [_sds((1, n), F32)] * n_sums + ride_shapes,
        scratch_shapes=ride_scratch,
        compiler_params=_cparams("arbitrary", "arbitrary") if (n_sums or n_r) else _cparams("parallel", "parallel"),
    )(a, b, *extras, *rows, *ride_arrays)
    if n_r:
        return tuple(out[:n_out + n_sums]) + (list(out[n_out + n_sums:]),)
    return out if len(out) > 1 else out[0]


def _ep_add(acc, res):
    return (acc + res,)


def _ep_add_norm(acc, res, gain):
    x = acc + res
    r = lax.rsqrt(jnp.mean(x * x, axis=-1, keepdims=True) + EPS)
    return x, x * r * gain


def _ep_norm_bwd(dh, x, dres, gain):
    r = lax.rsqrt(jnp.mean(x * x, axis=-1, keepdims=True) + EPS)
    xr = x * r
    t = dh * gain
    dx = dres + r * (t - xr * jnp.mean(t * xr, axis=-1, keepdims=True))
    return dx, dx, jnp.sum(dh * xr, axis=0, keepdims=True)


def _ep_relu2(acc):
    r = jnp.maximum(acc, 0.0)
    return acc, r * r


def _ep_relu2_bwd(acc, a):
    return (acc * (2.0 * jnp.maximum(a.astype(F32), 0.0)),)


def _rmsnorm(x, g, name):
    s, d = x.shape
    tm = min(s, 512)

    def body(x_ref, g_ref, h_ref):
        xv = x_ref[...]
        r = lax.rsqrt(jnp.mean(xv * xv, axis=-1, keepdims=True) + EPS)
        h_ref[...] = (xv * r * g_ref[...]).astype(BF16)

    return _call(
        body, name=name, grid=(s // tm,),
        in_specs=[pl.BlockSpec((tm, d), lambda i: (i, 0)), pl.BlockSpec((1, d), lambda i: (0, 0))],
        out_specs=pl.BlockSpec((tm, d), lambda i: (i, 0)),
        out_shape=_sds((s, d), BF16), compiler_params=_cparams("parallel"),
    )(x, g)


def _loss_head(y, target, name):
    s, d = y.shape
    tm = min(s, 512)

    def body(y_ref, t_ref, loss_ref, dy_ref, dyb_ref):
        err = y_ref[...] - t_ref[...]
        dy = err * (1.0 / d)
        dy_ref[...] = dy
        dyb_ref[...] = dy.astype(BF16)

        @pl.when(pl.program_id(0) == 0)
        def _():
            loss_ref[...] = jnp.zeros_like(loss_ref)

        part = 0.5 * jnp.sum(jnp.mean(err * err, axis=-1, keepdims=True), axis=0, keepdims=True)
        loss_ref[...] += part

    row = pl.BlockSpec((tm, d), lambda i: (i, 0))
    return _call(
        body, name=name, grid=(s // tm,), in_specs=[row, row],
        out_specs=[pl.BlockSpec((8, LANES), lambda i: (0, 0)), row, row],
        out_shape=[_sds((8, LANES), F32), _sds((s, d), F32), _sds((s, d), BF16)],
        compiler_params=_cparams("arbitrary"),
    )(y, target)


def _head_lane_mask():
    lane = lax.broadcasted_iota(jnp.int32, (1, LANES), 1)
    return lane < SB_HEAD_DIM


def _pair_mean(t, first):
    del first
    r = lax.broadcasted_iota(jnp.int32, (LANES, LANES), 0) // SB_HEAD_DIM
    c = lax.broadcasted_iota(jnp.int32, (LANES, LANES), 1) // SB_HEAD_DIM
    same_half = jnp.where(r == c, 1.0, 0.0).astype(BF16)
    return _dot2(t, same_half, NN) * (1.0 / SB_HEAD_DIM)


def _pair_rms(xv, first):
    return lax.rsqrt(_pair_mean(xv * xv, first) + EPS)


def _qk_gain_table(q_gain, k_gain, d):
    reps = d // SB_HEAD_DIM
    return jnp.stack([jnp.tile(q_gain, reps) * SB_SCALE, jnp.tile(k_gain, reps), jnp.ones((d,), F32)])[:, None, :]


QKNORM_ROWS = 256


def _qknorm(qkv, gains, name):
    s, d3 = qkv.shape
    d = d3 // 3
    tm = min(s, QKNORM_ROWS)

    def body(x_ref, g_ref, o_ref):
        first = _head_lane_mask()
        is_v = pl.program_id(0) == 2
        for c in range(d // LANES):
            cols = slice(c * LANES, (c + 1) * LANES)
            xv = x_ref[:, cols]
            normed = xv * _pair_rms(xv, first) * g_ref[0, :, cols]
            o_ref[:, cols] = jnp.where(is_v, xv, normed).astype(BF16)

    tile = pl.BlockSpec((tm, d), lambda c, i: (i, c))
    return _call(
        body, name=name, grid=(3, s // tm),
        in_specs=[tile, pl.BlockSpec((1, 1, d), lambda c, i: (c, 0, 0))], out_specs=tile,
        out_shape=_sds((s, d3), BF16), compiler_params=_cparams("parallel", "parallel"),
    )(qkv, gains)


def _ep_qknorm(acc, gain):
    first = _head_lane_mask()
    is_v = pl.program_id(1) == 2
    cols = []
    for c in range(acc.shape[1] // LANES):
        xv = acc[:, c * LANES:(c + 1) * LANES]
        cols.append(jnp.where(is_v, xv, xv * _pair_rms(xv, first) * gain[:, c * LANES:(c + 1) * LANES]))
    return acc, jnp.concatenate(cols, axis=1)


def _qknorm_bwd(qkv, dq, dk, dv, gains, name):
    s, d3 = qkv.shape
    d = d3 // 3
    tm = min(s, QKNORM_ROWS)

    def body(x_ref, dq_ref, dk_ref, dv_ref, g_ref, dx_ref, dg_ref):
        c, i = pl.program_id(0), pl.program_id(1)

        @pl.when(i == 0)
        def _():
            dg_ref[...] = jnp.zeros_like(dg_ref)

        first = _head_lane_mask()
        for col in range(d // LANES):
            cols = slice(col * LANES, (col + 1) * LANES)
            xv = x_ref[:, cols]
            dy = jnp.where(c == 0, dq_ref[:, cols], jnp.where(c == 1, dk_ref[:, cols], dv_ref[:, cols]))
            r = _pair_rms(xv, first)
            xr = xv * r
            t = dy * g_ref[0, :, cols]
            dx = r * (t - xr * _pair_mean(t * xr, first))
            dx_ref[:, cols] = jnp.where(c == 2, dy, dx).astype(BF16)
            dg_ref[0, :, cols] += jnp.sum(dy * xr, axis=0, keepdims=True)

    tile = pl.BlockSpec((tm, d), lambda c, i: (i, c))
    vec = pl.BlockSpec((1, 1, d), lambda c, i: (c, 0, 0))

    def part(kind):
        return pl.BlockSpec((tm, d), lambda c, i: (jnp.where(c == kind, i, 0), 0))

    return _call(
        body, name=name, grid=(3, s // tm), in_specs=[tile, part(0), part(1), part(2), vec], out_specs=[tile, vec],
        out_shape=[_sds((s, d3), BF16), _sds((3, 1, d), F32)],
        compiler_params=_cparams("arbitrary", "arbitrary"),
    )(qkv, dq, dk, dv, gains)


def _softplus_parts(z):
    sp = jnp.maximum(z, 0.0) + jnp.log(1.0 + jnp.exp(-jnp.abs(z)))
    return sp, z - sp


def _diag_causal(tb):
    return lax.broadcasted_iota(jnp.int32, (tb, tb), 1) < lax.broadcasted_iota(jnp.int32, (tb, tb), 0)


def _later_keys(tb):
    r = lax.broadcasted_iota(jnp.int32, (tb, tb), 0)
    c = lax.broadcasted_iota(jnp.int32, (tb, tb), 1)
    return jnp.where(r > c, 1.0, 0.0).astype(BF16)


def _sb_scores(qa, kj, causal):
    sp, logsig = _softplus_parts(_dot(qa, kj, NT))
    return (-sp if causal is None else jnp.where(causal, -sp, 0.0)), logsig


def _sb_weights(stay, logsig, run, later, causal):
    w = jnp.exp(logsig + _dot2(stay, later, NN) + run)
    return w if causal is None else jnp.where(causal, w, 0.0)


def _sb_attention_fwd(qkvn, name, riders=()):
    s, d3 = qkvn.shape
    d = d3 // 3
    pairs, tb = d // LANES, min(ATT_BLOCK, s)
    nb = s // tb
    assert nb <= LANES
    n_r = len(riders)
    ride_shapes, ride_scratch = _rider_shapes("gather", riders)

    def body(*refs):
        q_ref, k_ref, v_ref = refs[:3]
        o_ref, runs_ref = refs[3 + n_r:5 + n_r]
        ride_end = _ride("gather", refs[3:3 + n_r], refs[5 + n_r:5 + 2 * n_r], refs[5 + 2 * n_r:], (pairs, nb))
        i = pl.program_id(1)
        first = _head_lane_mask()
        lane = lax.broadcasted_iota(jnp.int32, (1, LANES), 1)
        later = _later_keys(tb)
        q2 = q_ref[...]
        qs = (jnp.where(first, q2, jnp.zeros_like(q2)), jnp.where(first, jnp.zeros_like(q2), q2))

        def tiles(js, carry, causals):
            kvs = []
            for j in js:
                rows = pl.ds(pl.multiple_of(j * tb, tb), tb)
                kvs.append((k_ref[rows, :], v_ref[rows, :]))
            scores = [[_sb_scores(qs[h], kj, causals[t]) for h in range(2)] for t, (kj, _) in enumerate(kvs)]
            run = [carry[0][0], carry[1][0]]
            acc = [carry[0][1], carry[1][1]]
            runs = [carry[0][2], carry[1][2]]
            weights = []
            for t, j in enumerate(js):
                weights.append([_sb_weights(*scores[t][h], run[h], later, causals[t]) for h in range(2)])
                for h in range(2):
                    runs[h] = jnp.where(lane == j, run[h], runs[h])
                    run[h] = run[h] + jnp.sum(scores[t][h][0], axis=-1, keepdims=True)
            for t, (_, vj) in enumerate(kvs):
                for h in range(2):
                    acc[h] = acc[h] + _dot(weights[t][h].astype(BF16), vj, NN)
            return tuple((run[h], acc[h], runs[h]) for h in range(2))

        def alive(carry):
            return jnp.maximum(jnp.max(carry[0][0]), jnp.max(carry[1][0])) >= -SB_DEAD

        never = jnp.full((tb, LANES), SB_NEVER, F32)
        zero = (jnp.zeros((tb, 1), F32), jnp.zeros((tb, LANES), F32), never)
        diag = _diag_causal(tb)
        carry = lax.cond(i == 0, lambda: tiles([i], (zero, zero), [diag]),
                         lambda: tiles([i, i - 1], (zero, zero), [diag, None]))
        left = jnp.maximum(i - 1, 0)
        _, carry = lax.while_loop(
            lambda st: (st[0] < left // 2) & alive(st[1]),
            lambda st: (st[0] + 1, tiles([i - 2 - 2 * st[0], i - 3 - 2 * st[0]], st[1], [None, None])),
            (jnp.int32(0), carry))
        carry = lax.cond((left % 2 == 1) & alive(carry), lambda c: tiles([0], c, [None]), lambda c: c, carry)
        o_ref[...] = jnp.where(first, carry[0][1], carry[1][1])
        runs_ref[0] = carry[0][2]
        runs_ref[1] = carry[1][2]
        ride_end()

    out = _call(
        body, name=name, grid=(pairs, nb),
        in_specs=[pl.BlockSpec((tb, LANES), lambda p, i: (i, p)),
                  pl.BlockSpec((s, LANES), lambda p, i: (0, pairs + p)),
                  pl.BlockSpec((s, LANES), lambda p, i: (0, 2 * pairs + p))] + [_ANY] * n_r,
        out_specs=[pl.BlockSpec((tb, LANES), lambda p, i: (i, p)),
                   pl.BlockSpec((2, tb, LANES), lambda p, i: (p * nb + i, 0, 0))] + [_ANY] * n_r,
        out_shape=[_sds((s, d), F32), _sds((pairs * nb * 2, tb, LANES), F32)] + ride_shapes,
        scratch_shapes=ride_scratch,
        compiler_params=_cparams("arbitrary", "arbitrary"),
    )(qkvn, qkvn, qkvn, *riders)
    return out[0], out[1], list(out[2:])


def _sb_attention_bwd(qkvn, do, runs, name, riders=()):
    s, d3 = qkvn.shape
    d = d3 // 3
    pairs, tb = d // LANES, min(ATT_BLOCK, s)
    nb = s // tb
    n_r = len(riders)
    ride_shapes, ride_scratch = _rider_shapes("exchange", riders)

    def body(*refs):
        q_ref, k_ref, v_ref, do_ref, runs_ref = refs[:5]
        dq_ref, dk_ref, dv_ref = refs[5 + n_r:8 + n_r]
        ride_end = _ride("exchange", refs[5:5 + n_r], refs[8 + n_r:8 + 2 * n_r], refs[8 + 2 * n_r:], (pairs, nb))
        i = pl.program_id(1)

        @pl.when(i == 0)
        def _():
            dk_ref[...] = jnp.zeros_like(dk_ref)
            dv_ref[...] = jnp.zeros_like(dv_ref)

        first = _head_lane_mask()
        lane = lax.broadcasted_iota(jnp.int32, (1, LANES), 1)
        later = _later_keys(tb)
        q2, do2 = q_ref[...], do_ref[...].astype(BF16)
        zq = jnp.zeros_like(q2)
        qs = (jnp.where(first, q2, zq), jnp.where(first, zq, q2))
        dos = (jnp.where(first, do2, zq), jnp.where(first, zq, do2))

        def tiles(js, carry, causals):
            rows, kv = [], []
            for j in js:
                r = pl.ds(pl.multiple_of(j * tb, tb), tb)
                kj, vj = k_ref[r, :], v_ref[r, :]
                zk = jnp.zeros_like(kj)
                rows.append(r)
                kv.append([(jnp.where(first, kj, zk), jnp.where(first, vj, zk)),
                           (jnp.where(first, zk, kj), jnp.where(first, zk, vj))])
            pairs_th = [(t, h) for t in range(len(js)) for h in range(2)]
            scores = {(t, h): _sb_scores(qs[h], kv[t][h][0], causals[t]) for t, h in pairs_th}
            w, g = {}, {}
            for t, h in pairs_th:
                run = jnp.sum(jnp.where(lane == js[t], runs_ref[h], 0.0), axis=-1, keepdims=True)
                w[t, h] = _sb_weights(*scores[t, h], run, later, causals[t])
                g[t, h] = w[t, h] * _dot(dos[h], kv[t][h][1], NT)
            gsum = [carry[0], carry[1]]
            dz = {}
            for t, h in pairs_th:
                before = _dot(g[t, h].astype(BF16), later, NT) + gsum[h]
                gsum[h] = gsum[h] + jnp.sum(g[t, h], axis=-1, keepdims=True)
                sig = jnp.exp(scores[t, h][1])
                d = g[t, h] * (1.0 - sig) - before * sig
                dz[t, h] = (d if causals[t] is None else jnp.where(causals[t], d, 0.0)).astype(BF16)
            dq = carry[2]
            for t, h in pairs_th:
                dq = dq + _dot(dz[t, h], kv[t][h][0], NN)
            for t in range(len(js)):
                dk_ref[rows[t], :] += _dot(dz[t, 0], qs[0], TN) + _dot(dz[t, 1], qs[1], TN)
                dv_ref[rows[t], :] += _dot(w[t, 0].astype(BF16), dos[0], TN) + _dot(w[t, 1].astype(BF16), dos[1], TN)
            return gsum[0], gsum[1], dq

        left = jnp.maximum(i - 1, 0)
        reach = jnp.max(jnp.maximum(runs_ref[0], runs_ref[1]), axis=0, keepdims=True)
        start = jnp.sum(jnp.where((reach < -SB_DEAD) & (lane < left), 1.0, 0.0)).astype(jnp.int32)
        carry = (jnp.zeros((tb, 1), F32), jnp.zeros((tb, 1), F32), jnp.zeros((tb, LANES), F32))
        live = left - start
        carry = lax.fori_loop(
            0, live // 2, lambda jj, c: tiles([start + 2 * jj, start + 2 * jj + 1], c, [None, None]), carry)
        carry = lax.cond(live % 2 == 1, lambda c: tiles([left - 1], c, [None]), lambda c: c, carry)
        diag = _diag_causal(tb)
        carry = lax.cond(i == 0, lambda c: tiles([i], c, [diag]), lambda c: tiles([i - 1, i], c, [None, diag]), carry)
        dq_ref[...] = carry[2]
        ride_end()

    q_spec = pl.BlockSpec((tb, LANES), lambda p, i: (i, p))
    out = _call(
        body, name=name, grid=(pairs, nb),
        in_specs=[q_spec,
                  pl.BlockSpec((s, LANES), lambda p, i: (0, pairs + p)),
                  pl.BlockSpec((s, LANES), lambda p, i: (0, 2 * pairs + p)),
                  q_spec,
                  pl.BlockSpec((2, tb, LANES), lambda p, i: (p * nb + i, 0, 0))] + [_ANY] * n_r,
        out_specs=[q_spec, pl.BlockSpec((s, LANES), lambda p, i: (0, p)),
                   pl.BlockSpec((s, LANES), lambda p, i: (0, p))] + [_ANY] * n_r,
        out_shape=[_sds((s, d), F32)] * 3 + ride_shapes,
        scratch_shapes=ride_scratch,
        compiler_params=_cparams("arbitrary", "arbitrary"),
    )(qkvn, qkvn, qkvn, do, runs, *riders)
    return out[0], out[1], out[2], list(out[3:])


def _hg_tables(c):
    t = np.arange(c)[:, None]
    j = np.arange(c)[None, :]
    sums = [j <= t, j > t]
    masks = []
    m = c // 2
    while m >= 1:
        pos, base = t % (2 * m), t - t % (2 * m)
        sums.append((pos >= m) & (j >= base + m) & (j <= t))
        sums.append((pos < m) & (j > t) & (j <= base + m - 1))
        masks.append((t // (2 * m) == j // (2 * m)) & (t % (2 * m) >= m) & (j % (2 * m) < m))
        m //= 2
    return (jnp.asarray(np.concatenate(sums, 0), BF16), jnp.asarray(np.stack(masks), F32), len(masks))


def _hg_gates(qr, fr, lb):
    sq = jax.nn.sigmoid(qr)
    sg = jax.nn.sigmoid(fr)
    forget = lb + (1.0 - lb) * sg
    return sq, qr * sq, sg, forget, jnp.log(forget), (1.0 - lb) * (1.0 - sg)


def _hg_scores(q, k, x, masks_ref, c, levels):
    eye = (lax.broadcasted_iota(jnp.int32, (c, c), 0) == lax.broadcasted_iota(jnp.int32, (c, c), 1)).astype(F32)
    scores = eye * jnp.sum(q * k, axis=-1, keepdims=True)
    ops = []
    for l in range(levels):
        qm = (q * x[(2 + 2 * l) * c:(3 + 2 * l) * c]).astype(BF16)
        km = (k * x[(3 + 2 * l) * c:(4 + 2 * l) * c]).astype(BF16)
        scores = scores + masks_ref[l] * _dot(qm, km, NT)
        ops.append((qm, km))
    return scores, eye, ops


def _hg_layout(s, d):
    heads, c = d // HG_DIM, min(HG_CHUNK, s)
    nsub = min(HG_STEP_CHUNKS, s // c)
    hp = HG_STEP_HEADS if heads % HG_STEP_HEADS == 0 else 1
    return heads, c, nsub, hp


def _hg_specs(s, d, reverse):
    heads, c, nsub, hp = _hg_layout(s, d)
    rows, width, groups, n_steps = c * nsub, hp * HG_DIM, heads // hp, s // (c * nsub)

    def step(si):
        return n_steps - 1 - si if reverse else si
    proj = [pl.BlockSpec((rows, width), functools.partial(lambda g, si, part: (step(si), part * groups + g), part=p))
            for p in range(4)]
    tile = pl.BlockSpec((rows, width), lambda g, si: (step(si), g))
    lb = pl.BlockSpec((1, width), lambda g, si: (0, g))
    gain = pl.BlockSpec((1, HG_DIM), lambda g, si: (0, 0))
    state = pl.BlockSpec((hp, nsub, HG_DIM, HG_DIM), lambda g, si: (g, step(si), 0, 0))
    return proj, tile, lb, gain, state, (groups, n_steps)


def _hgrn2_fwd(proj, lb, gain, name, riders=()):
    s, d4 = proj.shape
    d = d4 // 4
    heads, c, nsub, hp = _hg_layout(s, d)
    sums, masks, levels = _hg_tables(c)
    n_r = len(riders)
    ride_shapes, ride_scratch = _rider_shapes("gather", riders)
    pspecs, tile, lbs, gs, state, grid = _hg_specs(s, d, False)

    def body(*refs):
        qr_ref, fr_ref, ir_ref, gr_ref, lb_ref, gain_ref, sums_ref, masks_ref = refs[:8]
        og_ref, o_ref, states_ref = refs[8 + n_r:11 + n_r]
        st_ref = refs[11 + 2 * n_r]
        ride_end = _ride("gather", refs[8:8 + n_r], refs[11 + n_r:11 + 2 * n_r], refs[12 + 2 * n_r:], grid)

        @pl.when(pl.program_id(1) == 0)
        def _():
            st_ref[...] = jnp.zeros_like(st_ref)

        lbv, gainv = lb_ref[...], gain_ref[...]
        units = [(ci, hh) for ci in range(nsub) for hh in range(hp)]

        def lanes(hh):
            return slice(hh * HG_DIM, (hh + 1) * HG_DIM)

        pre = []
        for ci in range(nsub):
            rows = slice(ci * c, (ci + 1) * c)
            _, q, _, _, lf, k = _hg_gates(qr_ref[rows, :], fr_ref[rows, :], lbv)
            pre.append((q, k, jnp.exp(_mdot2(sums_ref[...], lf, NN))))
        scores, qh, vb, update = {}, {}, {}, {}
        for ci, hh in units:
            q, k, x = (a[:, lanes(hh)] for a in pre[ci])
            scores[ci, hh] = _hg_scores(q, k, x, masks_ref, c, levels)[0].astype(BF16)
            qh[ci, hh] = (q * x[0:c]).astype(BF16)
            vb[ci, hh] = ir_ref[ci * c:(ci + 1) * c, lanes(hh)].astype(BF16)
            update[ci, hh] = _dot(vb[ci, hh], (k * x[c:2 * c]).astype(BF16), TN)
        intra = {u: _dot(scores[u], vb[u], NN) for u in units}
        for hh in range(hp):
            st = st_ref[hh]
            for ci in range(nsub):
                rows = slice(ci * c, (ci + 1) * c)
                states_ref[hh, ci] = st
                o = _dot(qh[ci, hh], st.astype(BF16), NT) + intra[ci, hh]
                st = st * pre[ci][2][c - 1:c, lanes(hh)] + update[ci, hh]
                o_ref[rows, lanes(hh)] = o
                r = lax.rsqrt(jnp.mean(o * o, axis=-1, keepdims=True) + EPS)
                og_ref[rows, lanes(hh)] = (o * r * gainv * jax.nn.sigmoid(gr_ref[rows, lanes(hh)])).astype(BF16)
            st_ref[hh] = st
        ride_end()

    const = [pl.BlockSpec(sums.shape, lambda g, si: (0, 0)), pl.BlockSpec(masks.shape, lambda g, si: (0, 0, 0))]
    out = _call(
        body, name=name, grid=grid, in_specs=pspecs + [lbs, gs] + const + [_ANY] * n_r,
        out_specs=[tile, tile, state] + [_ANY] * n_r,
        out_shape=[_sds((s, d), BF16), _sds((s, d), F32), _sds((heads, s // c, HG_DIM, HG_DIM), F32)] + ride_shapes,
        scratch_shapes=[pltpu.VMEM((hp, HG_DIM, HG_DIM), F32)] + ride_scratch,
        compiler_params=_cparams("arbitrary", "arbitrary"),
    )(proj, proj, proj, proj, lb, gain, sums, masks, *riders)
    return out[0], out[1], out[2], list(out[3:])


def _hgrn2_bwd(proj, lb, gain, o, states, dog, name, riders=()):
    s, d4 = proj.shape
    d = d4 // 4
    heads, c, nsub, hp = _hg_layout(s, d)
    sums, masks, levels = _hg_tables(c)
    n_r = len(riders)
    ride_shapes, ride_scratch = _rider_shapes("exchange", riders)
    pspecs, tile, lbs, gs, state, grid = _hg_specs(s, d, True)

    def body(*refs):
        (qr_ref, fr_ref, ir_ref, gr_ref, lb_ref, gain_ref, sums_ref, masks_ref, o_ref, states_ref, dog_ref) = refs[:11]
        dq_ref, df_ref, di_ref, dg_ref, dlb_ref, dgain_ref = refs[11 + n_r:17 + n_r]
        dst_ref = refs[17 + 2 * n_r]
        ride_end = _ride("exchange", refs[11:11 + n_r], refs[17 + n_r:17 + 2 * n_r], refs[18 + 2 * n_r:], grid)

        @pl.when(pl.program_id(1) == 0)
        def _():
            dst_ref[...] = jnp.zeros_like(dst_ref)
            dlb_ref[...] = jnp.zeros_like(dlb_ref)
            dgain_ref[...] = jnp.zeros_like(dgain_ref)

        lbv, gainv = lb_ref[...], gain_ref[...]
        units = [(ci, hh) for ci in range(nsub) for hh in range(hp)]

        def lanes(hh):
            return slice(hh * HG_DIM, (hh + 1) * HG_DIM)

        def rows_of(ci):
            return slice(ci * c, (ci + 1) * c)

        pre = []
        for ci in range(nsub):
            qr = qr_ref[rows_of(ci), :]
            sq, q, sg, forget, lf, k = _hg_gates(qr, fr_ref[rows_of(ci), :], lbv)
            pre.append(dict(qr=qr, sq=sq, q=q, sg=sg, forget=forget, k=k, x=jnp.exp(_mdot2(sums_ref[...], lf, NN))))

        dob, vb, sc, qh_f, kh_f, feed = {}, {}, {}, {}, {}, {}
        dgain = [jnp.zeros((1, HG_DIM), F32) for _ in range(hp)]
        for ci, hh in units:
            rows, ln = rows_of(ci), lanes(hh)
            ov, gate = o_ref[rows, ln], jax.nn.sigmoid(gr_ref[rows, ln])
            r = lax.rsqrt(jnp.mean(ov * ov, axis=-1, keepdims=True) + EPS)
            orr = ov * r
            dogv = dog_ref[rows, ln]
            dg_ref[rows, ln] = (dogv * orr * gainv * gate * (1.0 - gate)).astype(BF16)
            don = dogv * gate
            dgain[hh] = dgain[hh] + jnp.sum(don * orr, axis=0, keepdims=True)
            t = don * gainv
            dob[ci, hh] = (r * (t - orr * jnp.mean(t * orr, axis=-1, keepdims=True))).astype(BF16)
            q, k, x = (pre[ci][n][:, ln] for n in ("q", "k", "x"))
            vb[ci, hh] = ir_ref[rows, ln].astype(BF16)
            sc[ci, hh] = _hg_scores(q, k, x, masks_ref, c, levels)
            qh_f[ci, hh], kh_f[ci, hh] = q * x[0:c], k * x[c:2 * c]
            feed[ci, hh] = _dot(dob[ci, hh], qh_f[ci, hh].astype(BF16), TN)

        dsts = {}
        for hh in range(hp):
            dst = dst_ref[hh]
            for ci in reversed(range(nsub)):
                dsts[ci, hh] = dst
                dst = dst * pre[ci]["x"][c - 1:c, lanes(hh)] + feed[ci, hh]
            dst_ref[hh] = dst

        dlb = [jnp.zeros((1, HG_DIM), F32) for _ in range(hp)]
        part = {(ci, hh): {n: v[:, lanes(hh)] for n, v in pre[ci].items()} for ci, hh in units}
        dscores, decay_grad, dq, dk, dexp = {}, {}, {}, {}, {}
        for u in units:
            ci, hh = u
            x, (scores, eye, _) = part[u]["x"], sc[u]
            st, dst = states_ref[hh, ci], dsts[u]
            dstb = dst.astype(BF16)
            dscores[u] = _dot(dob[u], vb[u], NT)
            di_ref[rows_of(ci), lanes(hh)] = (_dot(scores.astype(BF16), dob[u], TN)
                                              + _dot(kh_f[u].astype(BF16), dstb, NT)).astype(BF16)
            dqh = _dot(dob[u], st.astype(BF16), NN)
            dkh = _dot(vb[u], dstb, NN)
            decay_grad[u] = x[c - 1:c] * jnp.sum(dst * st, axis=0, keepdims=True)
            ddiag = jnp.sum(eye * dscores[u], axis=-1, keepdims=True)
            dq[u] = dqh * x[0:c] + ddiag * part[u]["k"]
            dk[u] = dkh * x[c:2 * c] + ddiag * part[u]["q"]
            dexp[u] = [dqh * qh_f[u], dkh * kh_f[u]]
        for l in range(levels):
            for u in units:
                q, k, x = part[u]["q"], part[u]["k"], part[u]["x"]
                qm, km = sc[u][2][l]
                dsm = (masks_ref[l] * dscores[u]).astype(BF16)
                dqm, dkm = _dot(dsm, km, NN), _dot(dsm, qm, TN)
                xq, xk = x[(2 + 2 * l) * c:(3 + 2 * l) * c], x[(3 + 2 * l) * c:(4 + 2 * l) * c]
                dq[u] = dq[u] + dqm * xq
                dk[u] = dk[u] + dkm * xk
                dexp[u] += [dqm * (q * xq), dkm * (k * xk)]
        for u in units:
            ci, hh = u
            rows, ln, p = rows_of(ci), lanes(hh), part[u]
            dlf = _mdot2(sums_ref[...], jnp.concatenate(dexp[u], axis=0), TN) + decay_grad[u]
            dforget = dlf / p["forget"] - dk[u]
            dlb[hh] = dlb[hh] + jnp.sum(dforget * (1.0 - p["sg"]), axis=0, keepdims=True)
            df_ref[rows, ln] = (dforget * (1.0 - lbv[:, ln]) * p["sg"] * (1.0 - p["sg"])).astype(BF16)
            dq_ref[rows, ln] = (dq[u] * p["sq"] * (1.0 + p["qr"] * (1.0 - p["sq"]))).astype(BF16)
        for hh in range(hp):
            dlb_ref[:, lanes(hh)] += dlb[hh]
            dgain_ref[hh] += dgain[hh]
        ride_end()

    const = [pl.BlockSpec(sums.shape, lambda g, si: (0, 0)), pl.BlockSpec(masks.shape, lambda g, si: (0, 0, 0))]
    out = _call(
        body, name=name, grid=grid, in_specs=pspecs + [lbs, gs] + const + [tile, state, tile] + [_ANY] * n_r,
        out_specs=[tile, tile, tile, tile, lbs, pl.BlockSpec((hp, 1, HG_DIM), lambda g, si: (g, 0, 0))] + [_ANY] * n_r,
        out_shape=[_sds((s, d), BF16)] * 4 + [_sds((1, d), F32), _sds((heads, 1, HG_DIM), F32)] + ride_shapes,
        scratch_shapes=[pltpu.VMEM((hp, HG_DIM, HG_DIM), F32)] + ride_scratch,
        compiler_params=_cparams("arbitrary", "arbitrary"),
    )(proj, proj, proj, proj, lb, gain, sums, masks, o, states, dog, *riders)
    return tuple(out[:6]) + (list(out[6:]),)


def _lower_bounds(logits, name):
    n, d = logits.shape

    def body(l_ref, lb_ref):
        lv = l_ref[...]
        e = jnp.exp(lv - jnp.max(lv, axis=0, keepdims=True))
        p = e / jnp.sum(e, axis=0, keepdims=True)
        run = jnp.zeros((1, d), F32)
        for j in range(n):
            if j > 0:
                run = run + p[j:j + 1]
            lb_ref[j:j + 1, :] = run

    return _call(body, name=name, out_shape=_sds((n, d), F32))(logits)


def _lower_bounds_bwd(logits, dlb_parts, name):
    n, d = logits.shape

    def body(l_ref, dlb_ref, dl_ref):
        lv, dv = l_ref[...], dlb_ref[0]
        for dev in range(1, N_DEV):
            dv = dv + dlb_ref[dev]
        e = jnp.exp(lv - jnp.max(lv, axis=0, keepdims=True))
        p = e / jnp.sum(e, axis=0, keepdims=True)
        run = jnp.zeros((1, d), F32)
        dps = [None] * n
        for j in range(n - 1, 0, -1):
            run = run + dv[j:j + 1]
            dps[j] = run
        dps[0] = jnp.zeros((1, d), F32)
        inner = jnp.zeros((1, d), F32)
        for j in range(n):
            inner = inner + p[j:j + 1] * dps[j]
        for j in range(n):
            dl_ref[j:j + 1, :] = p[j:j + 1] * (dps[j] - inner)

    return _call(body, name=name, out_shape=_sds((n, d), F32))(logits, dlb_parts)


_ANY = pl.BlockSpec(memory_space=pl.ANY)
_MESH = pl.DeviceIdType.MESH


def _gather_stages(x_ref, out_ref, send_sems, recv_sems, local_sem):
    mx, my, mc = lax.axis_index("x"), lax.axis_index("y"), lax.axis_index("c")
    me, sibling = (mx, my, mc), (mx, my, 1 - mc)
    chips = [(1 - mx, my), (mx, 1 - my), (1 - mx, 1 - my)]

    def slot(px, py, pc):
        return out_ref.at[4 * px + 2 * py + pc]

    def copy(k, block, to, src=None):
        return pltpu.make_async_remote_copy(
            src_ref=slot(*block) if src is None else src, dst_ref=slot(*block),
            send_sem=send_sems.at[k], recv_sem=recv_sems.at[k], device_id=to, device_id_type=_MESH)

    mine = pltpu.make_async_copy(x_ref, slot(*me), local_sem)
    first = [copy(0, me, sibling, src=x_ref)] + [copy(1 + j, me, (*chip, mc), src=x_ref) for j, chip in enumerate(chips)]
    passed = [copy(4 + j, (*chip, mc), sibling) for j, chip in enumerate(chips)]

    def start():
        mine.start()
        for cp in first:
            cp.start()

    def middle():
        for j, chip in enumerate(chips):
            copy(1 + j, (*chip, mc), me).wait_recv()
            passed[j].start()

    def finish():
        copy(0, sibling, me).wait_recv()
        for j, chip in enumerate(chips):
            copy(4 + j, (*chip, 1 - mc), me).wait_recv()
        for cp in first + passed:
            cp.wait_send()
        mine.wait()

    return start, middle, finish


def _exchange_stages(g_ref, out_ref, send_sems, recv_sems, local_sem):
    mx, my, mc = lax.axis_index("x"), lax.axis_index("y"), lax.axis_index("c")
    me = 4 * mx + 2 * my + mc
    mine = pltpu.make_async_copy(g_ref.at[me], out_ref.at[me], local_sem)
    copies = []
    for k in range(1, N_DEV):
        px, py, pc = mx ^ (k >> 2), my ^ ((k >> 1) & 1), mc ^ (k & 1)
        peer = 4 * px + 2 * py + pc
        send = pltpu.make_async_remote_copy(
            src_ref=g_ref.at[peer], dst_ref=out_ref.at[me], send_sem=send_sems.at[k - 1],
            recv_sem=recv_sems.at[k - 1], device_id=(px, py, pc), device_id_type=_MESH)
        arrival = pltpu.make_async_remote_copy(
            src_ref=g_ref.at[peer], dst_ref=out_ref.at[peer], send_sem=send_sems.at[k - 1],
            recv_sem=recv_sems.at[k - 1], device_id=(px, py, pc), device_id_type=_MESH)
        copies.append((send, arrival))

    def start():
        mine.start()
        for send, _ in copies:
            send.start()

    def finish():
        for _, arrival in copies:
            arrival.wait_recv()
        for send, _ in copies:
            send.wait_send()
        mine.wait()

    return start, lambda: None, finish


_STAGES = {"gather": _gather_stages, "exchange": _exchange_stages}
SEMS_PER_TRANSFER = 3


def _rider_shapes(kind, arrays):
    outs = [_sds((N_DEV,) + a.shape if kind == "gather" else a.shape, a.dtype) for a in arrays]
    scratch = []
    for _ in arrays:
        scratch += [pltpu.SemaphoreType.DMA((7,)), pltpu.SemaphoreType.DMA((7,)), pltpu.SemaphoreType.DMA]
    return outs, scratch


def _ride(kind, in_refs, out_refs, sems, grid):
    stages = [_STAGES[kind](a, o, *sems[SEMS_PER_TRANSFER * n:SEMS_PER_TRANSFER * (n + 1)])
              for n, (a, o) in enumerate(zip(in_refs, out_refs))]
    if not stages:
        return lambda: None
    step, steps = pl.program_id(0) * grid[1] + pl.program_id(1), grid[0] * grid[1]

    def run(stage):
        for st in stages:
            st[stage]()

    pl.when(step == 0)(lambda: run(0))
    pl.when(step == (3 * steps) // 4)(lambda: run(1))
    return lambda: pl.when(step == steps - 1)(lambda: run(2))


def _transfer(kind, arrays, name):
    outs, scratch = _rider_shapes(kind, arrays)
    n = len(arrays)

    def body(*refs):
        stages = [_STAGES[kind](refs[t], refs[n + t], *refs[2 * n + SEMS_PER_TRANSFER * t:2 * n + SEMS_PER_TRANSFER * (t + 1)])
                  for t in range(n)]
        for stage in range(3):
            for st in stages:
                st[stage]()

    return _call(body, name=name, out_shape=outs, in_specs=[_ANY] * n, out_specs=[_ANY] * n, scratch_shapes=scratch)(*arrays)


def _all_gather(x, name):
    return _transfer("gather", [x], name)[0]


def _adamw(parts, w, m, v, name):
    n_l = len(parts)
    _, r, c = parts[0].shape
    tr = r if r <= 256 else 256
    assert r % tr == 0 and w.shape == (n_l * r, c), (name, parts[0].shape, w.shape)
    steps = r // tr

    def body(*refs):
        p_refs, (w_ref, m_ref, v_ref) = refs[:n_l], refs[n_l:n_l + 3]
        g_ref, d_ref, nm_ref, nv_ref, sum_ref = refs[n_l + 3:]
        for layer in range(n_l):
            @pl.when(pl.program_id(0) == layer)
            def _(p_ref=p_refs[layer]):
                acc = p_ref[0].astype(F32)
                for dev in range(1, N_DEV):
                    acc = acc + p_ref[dev].astype(F32)
                sum_ref[...] = acc

        g = sum_ref[...]
        nm = ADAM_B1 * m_ref[...] + (1.0 - ADAM_B1) * g
        nv = ADAM_B2 * v_ref[...] + (1.0 - ADAM_B2) * (g * g)
        m_hat = nm / (1.0 - ADAM_B1 ** ADAM_STEP)
        v_hat = nv / (1.0 - ADAM_B2 ** ADAM_STEP)
        g_ref[...] = g
        nm_ref[...] = nm
        nv_ref[...] = nv
        d_ref[...] = -ADAM_LR * (m_hat / (jnp.sqrt(v_hat) + ADAM_EPS) + ADAM_WD * w_ref[...])

    tile = pl.BlockSpec((tr, c), lambda l, i: (l * steps + i, 0))

    def part(layer):
        return pl.BlockSpec((N_DEV, tr, c), lambda l, i: (0, jnp.where(l == layer, i, 0), 0))

    return _call(
        body, name=name, grid=(n_l, steps),
        in_specs=[part(layer) for layer in range(n_l)] + [tile, tile, tile], out_specs=[tile] * 4,
        out_shape=[_sds((n_l * r, c), F32)] * 4, scratch_shapes=[pltpu.VMEM((tr, c), F32)],
        compiler_params=_cparams("arbitrary", "arbitrary"),
    )(*parts, w, m, v)


def _shard_2d(w):
    return w.astype(BF16).reshape(w.shape[0] * w.shape[1], w.shape[2])


def _full_cols(g, w):
    l, k, n = w.shape
    g = g.reshape(N_DEV, l, k, n)
    return [jnp.transpose(g[:, i], (1, 0, 2)).reshape(k, N_DEV * n) for i in range(l)]


def _full_rows(g, w):
    l, k, n = w.shape
    g = g.reshape(N_DEV, l, k, n)
    return [g[:, i].reshape(N_DEV * k, n) for i in range(l)]


def _parts_cols(grads):
    k, n8 = grads[0].shape
    g = jnp.stack(grads).reshape(len(grads), k, N_DEV, n8 // N_DEV)
    return jnp.transpose(g, (2, 0, 1, 3)).reshape(N_DEV, len(grads) * k, n8 // N_DEV)


def _parts_rows(grads):
    k8, n = grads[0].shape
    g = jnp.stack(grads).reshape(len(grads), N_DEV, k8 // N_DEV, n)
    return jnp.transpose(g, (1, 0, 2, 3)).reshape(N_DEV, len(grads) * (k8 // N_DEV), n)


def _pad_rows(a, rows):
    return jnp.concatenate([a, jnp.zeros((rows - a.shape[0], a.shape[1]), a.dtype)], axis=0)


def kernel(x, norm_gains, sb_w_qkv, sb_q_gain, sb_k_gain, sb_w_o, hg_w_in, hg_lb_logits, hg_norm_gain, hg_w_o, mlp_w1, mlp_w2, loss_target, m_norm_gains, m_sb_w_qkv, m_sb_q_gain, m_sb_k_gain, m_sb_w_o, m_hg_w_in, m_hg_lb_logits, m_hg_norm_gain, m_hg_w_o, m_mlp_w1, m_mlp_w2, v_norm_gains, v_sb_w_qkv, v_sb_q_gain, v_sb_k_gain, v_sb_w_o, v_hg_w_in, v_hg_lb_logits, v_hg_norm_gain, v_hg_w_o, v_mlp_w1, v_mlp_w2):
    depth, _, d_loc = norm_gains.shape
    n_sb, n_hg = sb_w_qkv.shape[0], hg_w_in.shape[0]
    xs = x[0]
    target = loss_target[0]
    s, d = xs.shape
    me = 4 * lax.axis_index("x") + 2 * lax.axis_index("y") + lax.axis_index("c")

    assert n_sb >= 1
    w_qkv, w_o = [None] * n_sb, [None] * n_sb
    w_in, w_ho = [None] * n_hg, [None] * n_hg
    w_1s, w_2s = [None] * depth, [None] * depth

    def mixer_weights(layer):
        j = layer // 2
        if layer % 2 == 0:
            return [(w_qkv, j, sb_w_qkv[j:j + 1], _full_cols), (w_o, j, sb_w_o[j:j + 1], _full_rows)]
        return [(w_in, j, hg_w_in[j:j + 1], _full_cols), (w_ho, j, hg_w_o[j:j + 1], _full_rows)]
    n_gain_rows = 2 * depth
    small_rows = -(-(n_gain_rows + n_hg) // 8) * 8
    small = _pad_rows(jnp.concatenate([norm_gains.reshape(n_gain_rows, d_loc), hg_lb_logits], axis=0), small_rows)
    got_qkv, small = _transfer("gather", [_shard_2d(sb_w_qkv[:1]), small], "gather_first")
    w_qkv[0] = _full_cols(got_qkv, sb_w_qkv[:1])[0]
    gains_full = jnp.transpose(small[:, :n_gain_rows], (1, 0, 2)).reshape(depth, 2, 1, d)
    logits_full = jnp.transpose(small[:, n_gain_rows:n_gain_rows + n_hg], (1, 0, 2)).reshape(n_hg, d)
    lower = _lower_bounds(logits_full, "lower_bounds")

    saved = []
    cur = xs
    h = _rmsnorm(cur, gains_full[0, 0], "norm_mix_0")
    for layer in range(depth):
        j = layer // 2
        ahead = [wt for nxt in (layer + 1, layer + 2) if nxt < depth and layer % 2 == 0 for wt in mixer_weights(nxt)]
        riders = [_shard_2d(mlp_w1[layer:layer + 1]), _shard_2d(mlp_w2[layer:layer + 1])]
        riders += [_shard_2d(shard) for _, _, shard, _ in ahead]
        if layer % 2 == 0:
            qk_gains = _qk_gain_table(sb_q_gain[j], sb_k_gain[j], d)
            assert _matmul_tiles(s, 3 * d, d, BF16, BF16, [F32, BF16])[1] == d
            gain_row = qk_gains.reshape(1, 3 * d)
            if layer == 0:
                qkv, qkvn, (got_o,) = _matmul("nn", h, w_qkv[j], f"qkv_{layer}", [F32, BF16], _ep_qknorm,
                                              rows=[gain_row], riders=("gather", [_shard_2d(sb_w_o[:1])]))
                w_o[0] = _full_rows(got_o, sb_w_o[:1])[0]
            else:
                qkv, qkvn = _matmul("nn", h, w_qkv[j], f"qkv_{layer}", [F32, BF16], _ep_qknorm, rows=[gain_row])
            o, runs, got = _sb_attention_fwd(qkvn, f"sb_fwd_{layer}", riders)
            mix, mixed, w_out = (qkv, qk_gains, qkvn, o, runs), o, w_o[j]
        else:
            proj = _matmul("nn", h, w_in[j], f"hg_in_{layer}", [F32])
            og, o, states, got = _hgrn2_fwd(proj, lower[j:j + 1], hg_norm_gain[j:j + 1], f"hg_fwd_{layer}", riders)
            mix, mixed, w_out = (proj, og, o, states), og, w_ho[j]
        w_1s[layer] = w_1 = _full_cols(got[0], mlp_w1[layer:layer + 1])[0]
        w_2s[layer] = w_2 = _full_rows(got[1], mlp_w2[layer:layer + 1])[0]
        for (dest, idx, shard, full), g in zip(ahead, got[2:]):
            dest[idx] = full(g, shard)[0]
        x1, h2 = _matmul("nn", mixed, w_out, f"mix_out_{layer}", [F32, BF16], _ep_add_norm, [cur],
                         rows=[gains_full[layer, 1]])
        a, u = _matmul("nn", h2, w_1, f"mlp_up_{layer}", [BF16, BF16], _ep_relu2)
        saved.append((cur, h, mix, x1, h2, a, u))
        if layer == depth - 1:
            cur = _matmul("nn", u, w_2, f"mlp_down_{layer}", [F32], _ep_add, [x1])
        else:
            cur, h = _matmul("nn", u, w_2, f"mlp_down_{layer}", [F32, BF16], _ep_add_norm, [x1],
                             rows=[gains_full[layer + 1, 0]])

    loss_tile, dx, dxb = _loss_head(cur, target, "loss_head")
    loss = lax.psum(loss_tile[0, 0], AXES)

    d_gains = [[None, None] for _ in range(depth)]
    d_qk, d_lb, d_hgain = [None] * n_sb, [None] * n_hg, [None] * n_hg
    received = {"sb_w_qkv": [None] * n_sb, "sb_w_o": [None] * n_sb, "hg_w_in": [None] * n_hg,
                "hg_w_o": [None] * n_hg, "mlp_w1": [None] * depth, "mlp_w2": [None] * depth}
    pending = []

    def settle(got):
        for (wname, idx, _), arrived in zip(pending, got):
            received[wname][idx] = arrived
        pending.clear()

    for layer in reversed(range(depth)):
        j = layer // 2
        x0, h, mix, x1, h2, a, u = saved[layer]
        pending.append(("mlp_w2", layer, _parts_rows([_matmul("tn", u, dxb, f"d_mlp_w2_{layer}", [BF16])])))
        da = _matmul("nt", dxb, w_2s[layer], f"d_mlp_act_{layer}", [BF16], _ep_relu2_bwd, [a])
        pending.append(("mlp_w1", layer, _parts_cols([_matmul("tn", h2, da, f"d_mlp_w1_{layer}", [BF16])])))
        dx, dxb, d_gains[layer][1] = _matmul("nt", da, w_1s[layer], f"d_mlp_in_{layer}", [F32, BF16], _ep_norm_bwd,
                                             [x1, dx], rows=[gains_full[layer, 1]], n_sums=1)
        if layer % 2 == 0:
            qkv, qk_gains, qkvn, o, runs = mix
            pending.append(("sb_w_o", j, _parts_rows([_matmul("tn", o, dxb, f"d_sb_w_o_{layer}", [BF16])])))
            do = _matmul("nt", dxb, w_o[j], f"d_sb_o_{layer}", [F32])
            dq, dk, dv, got = _sb_attention_bwd(qkvn, do, runs, f"sb_bwd_{layer}", [p for _, _, p in pending])
            settle(got)
            dqkv, d_qk[j] = _qknorm_bwd(qkv, dq, dk, dv, qk_gains, f"d_qknorm_{layer}")
            pending.append(("sb_w_qkv", j, _parts_cols([_matmul("tn", h, dqkv, f"d_sb_w_qkv_{layer}", [BF16])])))
            tail = [p for _, _, p in pending] if layer == 0 else []
            res = _matmul("nt", dqkv, w_qkv[j], f"d_sb_in_{layer}", [F32, BF16], _ep_norm_bwd, [x0, dx],
                          rows=[gains_full[layer, 0]], n_sums=1, riders=("exchange", tail) if tail else None)
            dx, dxb, d_gains[layer][0] = res[:3]
            if tail:
                settle(res[3])
        else:
            proj, og, o, states = mix
            pending.append(("hg_w_o", j, _parts_rows([_matmul("tn", og, dxb, f"d_hg_w_o_{layer}", [BF16])])))
            dog = _matmul("nt", dxb, w_ho[j], f"d_hg_o_{layer}", [F32])
            dq, df, di, dg, d_lb[j], d_hgain[j], got = _hgrn2_bwd(
                proj, lower[j:j + 1], hg_norm_gain[j:j + 1], o, states, dog, f"hg_bwd_{layer}",
                [p for _, _, p in pending])
            settle(got)
            dproj = jnp.concatenate([dq, df, di, dg], axis=1)
            pending.append(("hg_w_in", j, _parts_cols([_matmul("tn", h, dproj, f"d_hg_w_in_{layer}", [BF16])])))
            dx, dxb, d_gains[layer][0] = _matmul("nt", dproj, w_in[j], f"d_hg_in_{layer}", [F32, BF16], _ep_norm_bwd,
                                                 [x0, dx], rows=[gains_full[layer, 0]], n_sums=1)
    grad_x = dx[None]
    if pending:
        settle(_transfer("exchange", [p for _, _, p in pending], "exchange_tail"))

    def update(wname, w, m, v):
        shape = w.shape
        flat = (shape[0] * shape[1], shape[2])
        res = _adamw(received[wname], w.reshape(flat), m.reshape(flat), v.reshape(flat), "adamw_" + wname)
        return [r.reshape(shape) for r in res]

    big = {
        "sb_w_qkv": update("sb_w_qkv", sb_w_qkv, m_sb_w_qkv, v_sb_w_qkv),
        "sb_w_o": update("sb_w_o", sb_w_o, m_sb_w_o, v_sb_w_o),
        "hg_w_in": update("hg_w_in", hg_w_in, m_hg_w_in, v_hg_w_in),
        "hg_w_o": update("hg_w_o", hg_w_o, m_hg_w_o, v_hg_w_o),
        "mlp_w1": update("mlp_w1", mlp_w1, m_mlp_w1, v_mlp_w1),
        "mlp_w2": update("mlp_w2", mlp_w2, m_mlp_w2, v_mlp_w2),
    }

    d_gain_rows = jnp.concatenate([d_gains[l][t] for l in range(depth) for t in range(2)], axis=0)
    d_lb_rows = jnp.concatenate(d_lb, axis=0)
    def fold(t):
        return jnp.sum(t.reshape(d // SB_HEAD_DIM, SB_HEAD_DIM), axis=0, keepdims=True)
    d_qg = jnp.concatenate([fold(d_qk[i][0]) for i in range(n_sb)], axis=0) * SB_SCALE
    d_kg = jnp.concatenate([fold(d_qk[i][1]) for i in range(n_sb)], axis=0)
    d_hg = jnp.concatenate([jnp.sum(d_hgain[i], axis=0) for i in range(n_hg)], axis=0)
    per_row = d // LANES
    packed = jnp.concatenate([
        d_gain_rows.reshape(n_gain_rows * per_row, LANES), d_lb_rows.reshape(n_hg * per_row, LANES),
        jnp.concatenate([d_qg, d_kg], axis=1), d_hg], axis=0)
    n_packed = packed.shape[0]
    packed = _pad_rows(packed, -(-n_packed // 8) * 8)
    everyone = _all_gather(packed, "gather_small_grads")
    o_lb = n_gain_rows * per_row
    o_qk = o_lb + n_hg * per_row
    o_hg = o_qk + n_sb

    def mine_of(rows, count):
        return lax.dynamic_slice_in_dim(rows.reshape(N_DEV, count, per_row, LANES), me, 1, axis=2)[:, :, 0]

    d_logits_full = _lower_bounds_bwd(logits_full, everyone[:, o_lb:o_qk].reshape(N_DEV, n_hg, d), "lower_bounds_bwd")
    d_logits_mine = lax.dynamic_slice_in_dim(d_logits_full.reshape(n_hg, per_row, LANES), me, 1, axis=1)[:, 0]
    zeros7 = jnp.zeros((N_DEV - 1, n_hg, LANES), F32)
    small_parts = jnp.concatenate([
        mine_of(everyone[:, :o_lb], n_gain_rows),
        jnp.concatenate([d_logits_mine[None], zeros7], axis=0),
        everyone[:, o_qk:o_hg], everyone[:, o_hg:o_hg + n_hg]], axis=1)
    rows_small = small_parts.shape[1]
    pad_to = -(-rows_small // 8) * 8
    small_parts = jnp.concatenate([small_parts, jnp.zeros((N_DEV, pad_to - rows_small, LANES), F32)], axis=1)

    def pack_small(ng, qg, kg, lbl, hgn):
        return _pad_rows(jnp.concatenate([
            ng.reshape(n_gain_rows, d_loc), lbl, jnp.concatenate([qg, kg], axis=1), hgn], axis=0), pad_to)

    res = _adamw([small_parts],
                 pack_small(norm_gains, sb_q_gain, sb_k_gain, hg_lb_logits, hg_norm_gain),
                 pack_small(m_norm_gains, m_sb_q_gain, m_sb_k_gain, m_hg_lb_logits, m_hg_norm_gain),
                 pack_small(v_norm_gains, v_sb_q_gain, v_sb_k_gain, v_hg_lb_logits, v_hg_norm_gain), "adamw_small")

    def unpack_small(t):
        o1 = n_gain_rows
        o2 = o1 + n_hg
        o3 = o2 + n_sb
        return {"norm_gains": t[:o1].reshape(depth, 2, d_loc), "hg_lb_logits": t[o1:o2],
                "sb_q_gain": t[o2:o3, :SB_HEAD_DIM], "sb_k_gain": t[o2:o3, SB_HEAD_DIM:],
                "hg_norm_gain": t[o3:o3 + n_hg]}

    small_out = [unpack_small(t) for t in res]
    order = ["norm_gains", "sb_w_qkv", "sb_q_gain", "sb_k_gain", "sb_w_o", "hg_w_in", "hg_lb_logits",
             "hg_norm_gain", "hg_w_o", "mlp_w1", "mlp_w2"]
    outs = [loss, grad_x]
    for kind in range(4):
        outs += [big[n][kind] if n in big else small_out[kind][n] for n in order]
    return tuple(outs)
```
